```python
import math
import jax, jax.numpy as jnp
from jax import lax
import numpy as np

D_MODEL = 2048
BATCH = 2
SEQ = 4096
DEPTH = 2

MLA_HEADS = 8
MLA_Q_LORA = 512
MLA_KV_LORA = 512
MLA_NOPE_DIM = 128
MLA_ROPE_DIM = 64
MLA_V_DIM = 128
DSA_HEADS = 8
DSA_HEAD_DIM = 128
DSA_ROPE_DIM = DSA_HEAD_DIM // 4
IDX_HEADS = 16
IDX_DIM = 64
IDX_ROPE_DIM = IDX_DIM // 4
DSA_MAX_TOPK = 256
ROPE_THETA = 500000.0
IN_SPLIT_WIDTHS = (MLA_Q_LORA, MLA_KV_LORA, MLA_ROPE_DIM,
                   DSA_HEADS * DSA_HEAD_DIM, DSA_HEADS * DSA_HEAD_DIM, DSA_HEADS * DSA_HEAD_DIM,
                   IDX_HEADS * IDX_DIM, IDX_DIM, IDX_HEADS,
                   D_MODEL, D_MODEL)
IN_WIDTH = sum(IN_SPLIT_WIDTHS)
N_GROUPS = 8
EXPERTS_PER_GROUP = 8
N_EXPERTS = N_GROUPS * EXPERTS_PER_GROUP
TOPK_IN_GROUP = 2
D_EXPERT = 512
Q_BLOCK = 128
MOE_ROW_BLOCK = 128
DEEPNORM_ALPHA = (2 * DEPTH) ** 0.25
DEEPNORM_BETA = (8 * DEPTH) ** -0.25
LN_EPS = 1e-5
RMS_EPS = 1e-6

kernel_name = "hybrid_mla_dsa_hmoe_deepnorm"


def _layer_norm(x, g, b):
    xf = x.astype(jnp.float32)
    mu = jnp.mean(xf, axis=-1, keepdims=True)
    var = jnp.mean(jnp.square(xf - mu), axis=-1, keepdims=True)
    y = (xf - mu) * lax.rsqrt(var + LN_EPS) * g.astype(jnp.float32) + b.astype(jnp.float32)
    return y.astype(x.dtype)


def _rms_norm(x, g):
    xf = x.astype(jnp.float32)
    y = xf * lax.rsqrt(jnp.mean(jnp.square(xf), axis=-1, keepdims=True) + RMS_EPS) * g.astype(jnp.float32)
    return y.astype(x.dtype)


def _rope_tables(positions, rot_dim):
    inv = ROPE_THETA ** (-jnp.arange(0, rot_dim, 2, dtype=jnp.float32) / rot_dim)
    ang = positions.astype(jnp.float32)[..., None] * inv
    return jnp.cos(ang)[:, :, None, :], jnp.sin(ang)[:, :, None, :]


def _apply_rope(x, cos, sin):
    half = cos.shape[-1]
    rot = 2 * half
    c = cos.astype(x.dtype)
    s = sin.astype(x.dtype)
    x1 = x[..., :half]
    x2 = x[..., half:rot]
    return jnp.concatenate([x1 * c - x2 * s, x2 * c + x1 * s, x[..., rot:]], axis=-1)


def _split_points():
    pts, acc = [], 0
    for w in IN_SPLIT_WIDTHS[:-1]:
        acc += w
        pts.append(acc)
    return pts


def _to_query_blocks(a):
    B, S = a.shape[0], a.shape[1]
    return jnp.swapaxes(a.reshape((B, S // Q_BLOCK, Q_BLOCK) + a.shape[2:]), 0, 1)


def _from_query_blocks(o):
    o = jnp.swapaxes(o, 0, 1)
    B, nb, qb, H, d = o.shape
    return o.reshape(B, nb * qb, H * d)


def _causal_block_attention(q, k, v, positions, scale):
    def one(args):
        qi, pi = args
        s = jnp.einsum('bqhd,bkhd->bhqk', qi, k).astype(jnp.float32) * scale
        mask = positions[:, None, None, :] <= pi[:, None, :, None]
        p = jax.nn.softmax(jnp.where(mask, s, -jnp.inf), axis=-1).astype(v.dtype)
        return jnp.einsum('bhqk,bkhd->bqhd', p, v)
    o = lax.map(one, (_to_query_blocks(q), _to_query_blocks(positions)))
    return _from_query_blocks(o)


def _gather_rows(arr, idx):
    return jax.vmap(lambda a, i: a[i])(arr, idx)


def _indexed_sparse_attention(q, k, v, q_idx, k_idx, w_idx, positions, scale):
    n_keys = k.shape[1]
    top_k = min(DSA_MAX_TOPK, n_keys // 4)

    def one(args):
        qi, qii, wi, pi = args
        rel = jax.nn.relu(jnp.einsum('bqhd,bsd->bqhs', qii, k_idx).astype(jnp.float32))
        score = jnp.einsum('bqhs,bqh->bqs', rel, wi.astype(jnp.float32))
        causal = positions[:, None, :] <= pi[:, :, None]
        score = jnp.where(causal, score, -jnp.inf)
        _, idx = lax.top_k(score, top_k)
        k_sel = _gather_rows(k, idx)
        v_sel = _gather_rows(v, idx)
        p_sel = _gather_rows(positions, idx)
        s = jnp.einsum('bqhd,bqkhd->bhqk', qi, k_sel).astype(jnp.float32) * scale
        valid = (p_sel <= pi[:, :, None])[:, None, :, :]
        p = jax.nn.softmax(jnp.where(valid, s, -jnp.inf), axis=-1).astype(v.dtype)
        return jnp.einsum('bhqk,bqkhd->bqhd', p, v_sel)

    o = lax.map(one, (_to_query_blocks(q), _to_query_blocks(q_idx),
                      _to_query_blocks(w_idx), _to_query_blocks(positions)))
    return _from_query_blocks(o)


def _hybrid_mixer(h, positions, ropes, w_in, g_q_lora, w_q_b, g_kv_lora, w_kv_b, w_o_a, w_o_b, w_out):
    B, S, _ = h.shape
    (cos_mla, sin_mla), (cos_dsa, sin_dsa), (cos_idx, sin_idx) = ropes
    proj = h @ w_in
    (c_q, c_kv, k_pe, q_b, k_b, v_b, q_idx, k_idx, w_idx, gate_a, gate_b) = jnp.split(
        proj, _split_points(), axis=-1)

    q_a = (_rms_norm(c_q, g_q_lora) @ w_q_b).reshape(B, S, MLA_HEADS, MLA_NOPE_DIM + MLA_ROPE_DIM)
    q_nope, q_pe = q_a[..., :MLA_NOPE_DIM], q_a[..., MLA_NOPE_DIM:]
    q_pe = _apply_rope(q_pe, cos_mla, sin_mla)
    kv = (_rms_norm(c_kv, g_kv_lora) @ w_kv_b).reshape(B, S, MLA_HEADS, MLA_NOPE_DIM + MLA_V_DIM)
    k_nope, v_a = kv[..., :MLA_NOPE_DIM], kv[..., MLA_NOPE_DIM:]
    k_pe = _apply_rope(k_pe[:, :, None, :], cos_mla, sin_mla)
    k_a = jnp.concatenate([k_nope, jnp.broadcast_to(k_pe, (B, S, MLA_HEADS, MLA_ROPE_DIM))], axis=-1)
    q_a = jnp.concatenate([q_nope, q_pe], axis=-1)
    o_a = _causal_block_attention(q_a, k_a, v_a, positions,
                                  (MLA_NOPE_DIM + MLA_ROPE_DIM) ** -0.5)

    q_b = _apply_rope(q_b.reshape(B, S, DSA_HEADS, DSA_HEAD_DIM), cos_dsa, sin_dsa)
    k_b = _apply_rope(k_b.reshape(B, S, DSA_HEADS, DSA_HEAD_DIM), cos_dsa, sin_dsa)
    v_b = v_b.reshape(B, S, DSA_HEADS, DSA_HEAD_DIM)
    q_idx = _apply_rope(q_idx.reshape(B, S, IDX_HEADS, IDX_DIM), cos_idx, sin_idx)
    k_idx = _apply_rope(k_idx[:, :, None, :], cos_idx, sin_idx)[:, :, 0, :]
    w_idx = w_idx * ((IDX_HEADS ** -0.5) * (IDX_DIM ** -0.5))
    o_b = _indexed_sparse_attention(q_b, k_b, v_b, q_idx, k_idx, w_idx, positions,
                                    DSA_HEAD_DIM ** -0.5)

    y = jax.nn.sigmoid(gate_a) * (o_a @ w_o_a) + jax.nn.sigmoid(gate_b) * (o_b @ w_o_b)
    return y @ w_out


def _hierarchical_moe(h, w_group, b_group, w_router, b_router, w_gate, w_up, w_down):
    B, S, D = h.shape
    t = h.reshape(-1, D)
    T = t.shape[0]
    g_prob = jax.nn.softmax((t @ w_group + b_group).astype(jnp.float32), axis=-1)
    g_p, g_idx = lax.top_k(g_prob, 1)
    e_logits = (t @ w_router + b_router).astype(jnp.float32).reshape(T, N_GROUPS, EXPERTS_PER_GROUP)
    in_group = jnp.take_along_axis(e_logits, g_idx[:, :, None], axis=1)[:, 0]
    e_p, e_local = lax.top_k(jax.nn.softmax(in_group, axis=-1), TOPK_IN_GROUP)
    e_p = e_p / jnp.sum(e_p, axis=-1, keepdims=True)
    weights = g_p * e_p
    expert_ids = g_idx * EXPERTS_PER_GROUP + e_local

    A = T * TOPK_IN_GROUP
    flat_e = expert_ids.reshape(-1).astype(jnp.int32)
    flat_tok = jnp.repeat(jnp.arange(T, dtype=jnp.int32), TOPK_IN_GROUP)
    flat_w = weights.reshape(-1)
    order = jnp.argsort(flat_e)
    se = flat_e[order]
    counts = jnp.bincount(flat_e, length=N_EXPERTS)
    starts = jnp.cumsum(counts) - counts
    padded = ((counts + MOE_ROW_BLOCK - 1) // MOE_ROW_BLOCK) * MOE_ROW_BLOCK
    pends = jnp.cumsum(padded)
    pstarts = pends - padded
    dest = pstarts[se] + (jnp.arange(A, dtype=jnp.int32) - starts[se])
    n_blocks = -(-A // MOE_ROW_BLOCK) + N_EXPERTS
    P = n_blocks * MOE_ROW_BLOCK
    row_tok = jnp.zeros((P,), jnp.int32).at[dest].set(flat_tok[order])
    row_w = jnp.zeros((P,), jnp.float32).at[dest].set(flat_w[order])
    block_e = jnp.minimum(jnp.searchsorted(pends, jnp.arange(n_blocks) * MOE_ROW_BLOCK, side='right'),
                          N_EXPERTS - 1).astype(jnp.int32)

    def expert_block(args):
        tok, w, e = args
        xb = t[tok]
        hid = jax.nn.silu(xb @ w_gate[e]) * (xb @ w_up[e])
        return ((hid @ w_down[e]) * w[:, None]).astype(t.dtype)

    ys = lax.map(expert_block, (row_tok.reshape(n_blocks, MOE_ROW_BLOCK),
                                row_w.reshape(n_blocks, MOE_ROW_BLOCK), block_e))
    out = jnp.zeros((T, D), t.dtype).at[row_tok].add(ys.reshape(P, D))
    return out.reshape(B, S, D)


def setup_inputs(seed: int = 0) -> dict:
    key = jax.random.key(seed)
    ks = jax.random.split(key, 21)
    f32 = jnp.float32
    n = lambda k, shape, scale: jax.random.normal(k, shape, f32) * scale
    x = jax.random.normal(ks[0], (BATCH, SEQ, D_MODEL), f32)
    offsets = jax.random.randint(ks[1], (BATCH, 1), 0, 2048, dtype=jnp.int32)
    positions = (offsets + jnp.arange(SEQ, dtype=jnp.int32)[None, :]).astype(jnp.int32)
    return {
        "x": x,
        "positions": positions,
        "w_in": n(ks[2], (DEPTH, D_MODEL, IN_WIDTH), D_MODEL ** -0.5),
        "g_q_lora": 1.0 + n(ks[3], (DEPTH, MLA_Q_LORA), 0.02),
        "w_q_b": n(ks[4], (DEPTH, MLA_Q_LORA, MLA_HEADS * (MLA_NOPE_DIM + MLA_ROPE_DIM)), MLA_Q_LORA ** -0.5),
        "g_kv_lora": 1.0 + n(ks[5], (DEPTH, MLA_KV_LORA), 0.02),
        "w_kv_b": n(ks[6], (DEPTH, MLA_KV_LORA, MLA_HEADS * (MLA_NOPE_DIM + MLA_V_DIM)), MLA_KV_LORA ** -0.5),
        "w_o_a": n(ks[7], (DEPTH, MLA_HEADS * MLA_V_DIM, D_MODEL), (MLA_HEADS * MLA_V_DIM) ** -0.5),
        "w_o_b": n(ks[8], (DEPTH, DSA_HEADS * DSA_HEAD_DIM, D_MODEL), (DSA_HEADS * DSA_HEAD_DIM) ** -0.5),
        "w_out": n(ks[9], (DEPTH, D_MODEL, D_MODEL), DEEPNORM_BETA * D_MODEL ** -0.5),
        "ln1_g": 1.0 + n(ks[10], (DEPTH, D_MODEL), 0.02),
        "ln1_b": n(ks[11], (DEPTH, D_MODEL), 0.02),
        "w_group": n(ks[12], (DEPTH, D_MODEL, N_GROUPS), D_MODEL ** -0.5),
        "b_group": n(ks[13], (DEPTH, N_GROUPS), 0.01),
        "w_router": n(ks[14], (DEPTH, D_MODEL, N_EXPERTS), D_MODEL ** -0.5),
        "b_router": n(ks[15], (DEPTH, N_EXPERTS), 0.01),
        "w_e_gate": n(ks[16], (DEPTH, N_EXPERTS, D_MODEL, D_EXPERT), D_MODEL ** -0.5),
        "w_e_up": n(ks[17], (DEPTH, N_EXPERTS, D_MODEL, D_EXPERT), D_MODEL ** -0.5),
        "w_e_down": n(ks[18], (DEPTH, N_EXPERTS, D_EXPERT, D_MODEL), DEEPNORM_BETA * D_EXPERT ** -0.5),
        "ln2_g": 1.0 + n(ks[19], (DEPTH, D_MODEL), 0.02),
        "ln2_b": n(ks[20], (DEPTH, D_MODEL), 0.02),
    }


def reference(x, positions, w_in, g_q_lora, w_q_b, g_kv_lora, w_kv_b, w_o_a, w_o_b, w_out,
              ln1_g, ln1_b, w_group, b_group, w_router, b_router, w_e_gate, w_e_up, w_e_down,
              ln2_g, ln2_b):
    ropes = (_rope_tables(positions, MLA_ROPE_DIM),
             _rope_tables(positions, DSA_ROPE_DIM),
             _rope_tables(positions, IDX_ROPE_DIM))
    h = x
    for l in range(DEPTH):
        mix = _hybrid_mixer(h, positions, ropes, w_in[l], g_q_lora[l], w_q_b[l], g_kv_lora[l],
                            w_kv_b[l], w_o_a[l], w_o_b[l], w_out[l])
        h = _layer_norm(DEEPNORM_ALPHA * h + mix, ln1_g[l], ln1_b[l])
        ff = _hierarchical_moe(h, w_group[l], b_group[l], w_router[l], b_router[l],
                               w_e_gate[l], w_e_up[l], w_e_down[l])
        h = _layer_norm(DEEPNORM_ALPHA * h + ff, ln2_g[l], ln2_b[l])
    return h
```

```python
import functools
import math

import jax
import jax.numpy as jnp
from jax import lax
from jax.experimental import pallas as pl
from jax.experimental.pallas import tpu as pltpu

F32 = jnp.float32
BF16 = jnp.bfloat16
I32 = jnp.int32

MLA_HEADS = 8
MLA_Q_LORA = 512
MLA_KV_LORA = 512
MLA_NOPE_DIM = 128
MLA_ROPE_DIM = 64
MLA_V_DIM = 128
MLA_QK_DIM = MLA_NOPE_DIM + MLA_ROPE_DIM
DSA_HEADS = 8
DSA_HEAD_DIM = 128
DSA_ROPE_DIM = DSA_HEAD_DIM // 4
IDX_HEADS = 16
IDX_DIM = 64
IDX_ROPE_DIM = IDX_DIM // 4
DSA_MAX_TOPK = 256
ROPE_THETA = 500000.0
N_GROUPS = 8
EXPERTS_PER_GROUP = 8
N_EXPERTS = N_GROUPS * EXPERTS_PER_GROUP
D_EXPERT = 512
MOE_ROW_BLOCK = 128
LN_EPS = 1e-5
RMS_EPS = 1e-6

LANES = 128
NEG_BIG = -1e30
INT_MIN = -(2 ** 31)
VMEM_LIMIT = 56 * 1024 * 1024


def _params(*sem):
    return pltpu.CompilerParams(dimension_semantics=sem, vmem_limit_bytes=VMEM_LIMIT)


def _resident(shape):
    nd = len(shape)
    return pl.BlockSpec(shape, lambda *_: (0,) * nd, pipeline_mode=pl.Buffered(1))


def _dot(a, b):
    return jnp.dot(a, b, preferred_element_type=F32)


def _dot_nt(a, b):
    return lax.dot_general(a, b, (((1,), (1,)), ((), ())), preferred_element_type=F32)


def _rope128(x, c, sa, sb, half):
    return x * c + pltpu.roll(x, LANES - half, 1) * sa + pltpu.roll(x, half, 1) * sb


def _rms(x, g):
    return x * lax.rsqrt(jnp.mean(x * x, axis=-1, keepdims=True) + RMS_EPS) * g


def _layer_norm(x, g, b):
    mu = jnp.mean(x, axis=-1, keepdims=True)
    xc = x - mu
    var = jnp.mean(xc * xc, axis=-1, keepdims=True)
    return xc * lax.rsqrt(var + LN_EPS) * g + b


def _mla_prep_kernel(h_ref, win_ref, gq_ref, gkv_ref, wqb_ref, wkvb_ref, c_ref, sa_ref, sb_ref,
                     q_ref, k_ref, v_ref):
    h = h_ref[...]
    p = _dot(h, win_ref[...])
    qn = _rms(p[:, :MLA_Q_LORA], gq_ref[...]).astype(BF16)
    kvn = _rms(p[:, MLA_Q_LORA:MLA_Q_LORA + MLA_KV_LORA], gkv_ref[...]).astype(BF16)
    q = _dot(qn, wqb_ref[...])
    kv = _dot(kvn, wkvb_ref[...])
    c, sa, sb = c_ref[...], sa_ref[...], sb_ref[...]
    half = MLA_ROPE_DIM // 2
    kpe = _rope128(p[:, MLA_Q_LORA + MLA_KV_LORA:], c, sa, sb, half)[:, :MLA_ROPE_DIM].astype(BF16)
    pe0 = MLA_HEADS * MLA_NOPE_DIM
    for hh in range(MLA_HEADS):
        q_ref[0, hh, :, 0:MLA_NOPE_DIM] = q[:, 128 * hh:128 * hh + 128].astype(BF16)
        k_ref[0, hh, :, 0:MLA_NOPE_DIM] = kv[:, 256 * hh:256 * hh + 128].astype(BF16)
        k_ref[0, hh, :, MLA_NOPE_DIM:MLA_QK_DIM] = kpe
        v_ref[:, 128 * hh:128 * hh + 128] = kv[:, 256 * hh + 128:256 * hh + 256].astype(BF16)
    for s in range(MLA_HEADS // 2):
        slab = _rope128(q[:, pe0 + 128 * s:pe0 + 128 * s + 128], c, sa, sb, half).astype(BF16)
        q_ref[0, 2 * s, :, MLA_NOPE_DIM:MLA_QK_DIM] = slab[:, :MLA_ROPE_DIM]
        q_ref[0, 2 * s + 1, :, MLA_NOPE_DIM:MLA_QK_DIM] = slab[:, MLA_ROPE_DIM:]


def _dsa_prep_kernel(h_ref, w_ref, cd_ref, sad_ref, sbd_ref, ci_ref, sai_ref, sbi_ref,
                     qb_ref, kb_ref, vb_ref, qi_ref, ki_ref, wi_ref, *, w_idx_scale):
    h = h_ref[...]
    hd = DSA_HEADS * DSA_HEAD_DIM
    cd, sad, sbd = cd_ref[...], sad_ref[...], sbd_ref[...]
    ci, sai, sbi = ci_ref[...], sai_ref[...], sbi_ref[...]
    hd_half, hi_half = DSA_ROPE_DIM // 2, IDX_ROPE_DIM // 2
    q = _dot(h, w_ref[:, 0:hd])
    for hh in range(DSA_HEADS):
        qb_ref[:, 128 * hh:128 * hh + 128] = _rope128(q[:, 128 * hh:128 * hh + 128], cd, sad, sbd, hd_half).astype(BF16)
    k = _dot(h, w_ref[:, hd:2 * hd])
    for hh in range(DSA_HEADS):
        kb_ref[:, 128 * hh:128 * hh + 128] = _rope128(k[:, 128 * hh:128 * hh + 128], cd, sad, sbd, hd_half).astype(BF16)
    vb_ref[...] = _dot(h, w_ref[:, 2 * hd:3 * hd]).astype(BF16)
    qi = _dot(h, w_ref[:, 3 * hd:3 * hd + IDX_HEADS * IDX_DIM])
    for s in range(IDX_HEADS // 2):
        slab = _rope128(qi[:, 128 * s:128 * s + 128], ci, sai, sbi, hi_half).astype(BF16)
        qi_ref[0, 2 * s] = slab[:, :IDX_DIM]
        qi_ref[0, 2 * s + 1] = slab[:, IDX_DIM:]
    o = 3 * hd + IDX_HEADS * IDX_DIM
    last = _dot(h, w_ref[:, o:o + 2 * LANES])
    ki_ref[...] = _rope128(last[:, :LANES], ci, sai, sbi, hi_half)[:, :IDX_DIM].astype(BF16)
    wi_ref[...] = last[:, LANES:LANES + IDX_HEADS] * w_idx_scale


def _mla_attn_kernel(q_ref, k_ref, v_ref, pq_ref, pk_ref, o_ref, m_sc, l_sc, acc_sc, *, scale, tile):
    qi = pl.program_id(2)
    q = q_ref[0, 0]
    m_sc[...] = jnp.full(m_sc.shape, -jnp.inf, F32)
    l_sc[...] = jnp.zeros(l_sc.shape, F32)
    acc_sc[...] = jnp.zeros(acc_sc.shape, F32)

    def chunk(c, masked):
        ks = pl.ds(pl.multiple_of(c * tile, tile), tile)
        s = _dot_nt(q, k_ref[0, 0, ks, :]) * scale
        if masked:
            s = jnp.where(pk_ref[0, c] <= pq_ref[0], s, NEG_BIG)
        m_old = m_sc[...]
        m_new = jnp.maximum(m_old, jnp.max(s, axis=-1, keepdims=True))
        p = jnp.exp(s - m_new)
        alpha = jnp.exp(m_old - m_new)
        l_sc[...] = alpha * l_sc[...] + jnp.sum(p, axis=-1, keepdims=True)
        acc_sc[...] = alpha * acc_sc[...] + _dot(p.astype(BF16), v_ref[0, ks, :])
        m_sc[...] = m_new

    def body(c, carry):
        chunk(c, False)
        return carry

    lax.fori_loop(0, qi, body, 0)
    chunk(qi, True)
    o_ref[0] = (acc_sc[...] / l_sc[...]).astype(o_ref.dtype)


def _idx_select_kernel(q_ref, k_ref, w_ref, pq_ref, pk_ref, o_ref, key_sc, *, tq, tk, nkc, top_k):
    qi = pl.program_id(1)
    nk = (qi * tq + tq + tk - 1) // tk
    w = w_ref[0]
    pq = pq_ref[0]

    def score_chunk(c, carry):
        kc = k_ref[0, pl.ds(pl.multiple_of(c * tk, tk), tk), :]
        acc = jnp.zeros((tq, tk), F32)
        for hh in range(IDX_HEADS):
            acc = acc + w[:, hh:hh + 1] * jnp.maximum(_dot_nt(q_ref[0, hh], kc), 0.0)
        bits = pltpu.bitcast(acc, I32)
        key = jnp.where(bits < 0, bits ^ jnp.int32(0x7FFFFFFF), bits)
        key = jnp.where(acc == 0.0, 0, key)
        key_sc[c] = jnp.where(pk_ref[0, c] <= pq, key, INT_MIN)
        return carry

    lax.fori_loop(0, nk, score_chunk, 0)

    def count(pred):
        def body(c, part):
            m = pred(key_sc[c], c).astype(I32)
            for j in range(tk // LANES):
                part = part + m[:, LANES * j:LANES * j + LANES]
            return part
        part = lax.fori_loop(0, nk, body, jnp.zeros((tq, LANES), I32))
        return jnp.sum(part, axis=1, keepdims=True)

    def bit_step(i, base):
        cand = base ^ (jnp.int32(1) << (31 - i))
        cnt = count(lambda key, c: key >= cand)
        return jnp.where(cnt >= top_k, cand, base)

    thr = lax.fori_loop(0, 32, bit_step, jnp.full((tq, 1), INT_MIN, I32))
    n_gt = count(lambda key, c: key > thr)
    n_eq = count(lambda key, c: key == thr)
    need = top_k - n_gt
    tie = jnp.logical_and(n_eq > need, thr != INT_MIN)

    def key_index(c):
        return c * tk + lax.broadcasted_iota(I32, (tq, tk), 1)

    def tie_search():
        def step(i, p):
            cand = p | (jnp.int32(1) << (14 - i))
            cnt = count(lambda key, c: jnp.logical_and(key == thr, key_index(c) < cand))
            return jnp.where(cnt <= need, cand, p)
        return lax.fori_loop(0, 15, step, jnp.zeros((tq, 1), I32))

    p_cut = lax.cond(jnp.max(tie.astype(I32)) > 0, tie_search,
                     lambda: jnp.full((tq, 1), 2 ** 30, I32))

    def out_chunk(c, carry):
        key = key_sc[c]
        sel = jnp.logical_or(key > thr, jnp.logical_and(key == thr, key_index(c) < p_cut))
        sel = jnp.logical_and(sel, key != INT_MIN)
        o_ref[0, c] = jnp.where(sel, 0.0, NEG_BIG).astype(o_ref.dtype)
        return carry

    lax.fori_loop(0, nk, out_chunk, 0)

    def fill_chunk(c, carry):
        o_ref[0, c] = jnp.full((tq, tk), NEG_BIG, o_ref.dtype)
        return carry

    lax.fori_loop(nk, nkc, fill_chunk, 0)


def _dsa_attn_kernel(q_ref, k_ref, v_ref, b_ref, o_ref, m_sc, l_sc, acc_sc, *, scale, tq, tk):
    qi = pl.program_id(1)
    nk = (qi * tq + tq + tk - 1) // tk
    m_sc[...] = jnp.full(m_sc.shape, NEG_BIG, F32)
    l_sc[...] = jnp.zeros(l_sc.shape, F32)
    acc_sc[...] = jnp.zeros(acc_sc.shape, F32)

    def chunk(c, carry):
        bias = b_ref[0, c].astype(F32)
        ks = pl.ds(pl.multiple_of(c * tk, tk), tk)
        for hh in range(DSA_HEADS):
            hs = slice(DSA_HEAD_DIM * hh, DSA_HEAD_DIM * (hh + 1))
            s = _dot_nt(q_ref[0, :, hs], k_ref[0, ks, hs]) * scale + bias
            m_old = m_sc[hh]
            m_new = jnp.maximum(m_old, jnp.max(s, axis=-1, keepdims=True))
            p = jnp.exp(s - m_new)
            alpha = jnp.exp(m_old - m_new)
            l_sc[hh] = alpha * l_sc[hh] + jnp.sum(p, axis=-1, keepdims=True)
            acc_sc[:, hs] = alpha * acc_sc[:, hs] + _dot(p.astype(BF16), v_ref[0, ks, hs])
            m_sc[hh] = m_new
        return carry

    lax.fori_loop(0, nk, chunk, 0)
    for hh in range(DSA_HEADS):
        hs = slice(DSA_HEAD_DIM * hh, DSA_HEAD_DIM * (hh + 1))
        o_ref[0, :, hs] = (acc_sc[:, hs] / l_sc[hh]).astype(o_ref.dtype)


def _merge_kernel(h_ref, oa_ref, ob_ref, wga_ref, wgb_ref, woa_ref, wob_ref, y_ref):
    h = h_ref[...]
    ya = jax.nn.sigmoid(_dot(h, wga_ref[...])) * _dot(oa_ref[...], woa_ref[...])
    yb = jax.nn.sigmoid(_dot(h, wgb_ref[...])) * _dot(ob_ref[...], wob_ref[...])
    y_ref[...] = (ya + yb).astype(y_ref.dtype)


def _out_ln_kernel(y_ref, h_ref, wout_ref, g_ref, b_ref, wr_ref, br_ref, o_ref, lg_ref, *, alpha):
    mix = _dot(y_ref[...], wout_ref[...])
    h1 = _layer_norm(alpha * h_ref[...] + mix, g_ref[...], b_ref[...])
    o_ref[...] = h1
    lg_ref[...] = jnp.dot(h1, wr_ref[...], preferred_element_type=F32,
                          precision=lax.Precision.HIGHEST) + br_ref[...]


def _route_kernel(lg_ref, meta_ref, wgt_ref, cnt_ref, carry_sc, *, tr):
    @pl.when(pl.program_id(0) == 0)
    def _():
        carry_sc[...] = jnp.zeros(carry_sc.shape, F32)

    lg = lg_ref[...]
    lane = lax.broadcasted_iota(I32, (tr, LANES), 1)
    gl = jnp.where(lane < N_GROUPS, lg, -jnp.inf)
    gmax = jnp.max(gl, axis=1, keepdims=True)
    g_idx = jnp.min(jnp.where(gl == gmax, lane, LANES), axis=1, keepdims=True)
    g_p = 1.0 / jnp.sum(jnp.exp(gl - gmax), axis=1, keepdims=True)
    lo = N_GROUPS + EXPERTS_PER_GROUP * g_idx
    in_group = jnp.logical_and(lane >= lo, lane < lo + EXPERTS_PER_GROUP)
    el = jnp.where(in_group, lg, -jnp.inf)
    emax = jnp.max(el, axis=1, keepdims=True)
    e1 = jnp.min(jnp.where(el == emax, lane, LANES), axis=1, keepdims=True)
    den = jnp.sum(jnp.exp(el - emax), axis=1, keepdims=True)
    el2 = jnp.where(lane == e1, -jnp.inf, el)
    emax2 = jnp.max(el2, axis=1, keepdims=True)
    e2 = jnp.min(jnp.where(el2 == emax2, lane, LANES), axis=1, keepdims=True)
    p1 = 1.0 / den
    p2 = jnp.exp(emax2 - emax) / den
    w1 = g_p * (p1 / (p1 + p2))
    w2 = g_p * (p2 / (p1 + p2))

    is1 = lane == e1
    is2 = lane == e2
    onehot = jnp.logical_or(is1, is2).astype(BF16)
    r = lax.broadcasted_iota(I32, (tr, tr), 0)
    cidx = lax.broadcasted_iota(I32, (tr, tr), 1)
    lower = (cidx < r).astype(BF16)
    before = _dot(lower, onehot) + carry_sc[...]
    rank1 = jnp.sum(jnp.where(is1, before, 0.0), axis=1, keepdims=True)
    rank2 = jnp.sum(jnp.where(is2, before, 0.0), axis=1, keepdims=True)
    carry_sc[...] = carry_sc[...] + jnp.sum(onehot.astype(F32), axis=0, keepdims=True)

    meta = jnp.where(lane == 0, e1 - N_GROUPS,
                     jnp.where(lane == 1, e2 - N_GROUPS,
                               jnp.where(lane == 2, rank1.astype(I32),
                                         jnp.where(lane == 3, rank2.astype(I32), 0))))
    meta_ref[...] = meta
    wgt_ref[...] = jnp.where(lane == 0, w1, jnp.where(lane == 1, w2, 0.0))
    cnt_ref[...] = jnp.broadcast_to(carry_sc[...], cnt_ref.shape)


def _row_copy(src, src_row, dst, dst_row, sem):
    return pltpu.make_async_copy(src.at[pl.ds(src_row, 1)], dst.at[pl.ds(dst_row, 1)], sem)


def _dispatch_kernel(dest_ref, h_hbm, xs_in, xs_out, sem, *, tt):
    del xs_in
    base = pl.program_id(0) * tt

    def issue(t, carry):
        tok = base + t
        _row_copy(h_hbm, tok, xs_out, dest_ref[2 * tok], sem).start()
        _row_copy(h_hbm, tok, xs_out, dest_ref[2 * tok + 1], sem).start()
        return carry

    lax.fori_loop(0, tt, issue, 0)
    pltpu.make_async_copy(h_hbm.at[pl.ds(0, 2 * tt)], xs_out.at[pl.ds(0, 2 * tt)], sem).wait()


def _experts_kernel(be_ref, nu_ref, xs_ref, wg_ref, wu_ref, wd_ref, o_ref, wg_sc, wu_sc, wd_sc):
    i = pl.program_id(0)
    e = be_ref[i]
    prev = be_ref[jnp.maximum(i - 1, 0)]

    @pl.when(jnp.logical_or(i == 0, e != prev))
    def _():
        wg_sc[...] = wg_ref[0].astype(BF16)
        wu_sc[...] = wu_ref[0].astype(BF16)
        wd_sc[...] = wd_ref[0].astype(BF16)

    @pl.when(i < nu_ref[0])
    def _():
        x = xs_ref[...].astype(BF16)
        hid = jax.nn.silu(_dot(x, wg_sc[...])) * _dot(x, wu_sc[...])
        o_ref[...] = _dot(hid.astype(BF16), wd_sc[...])

    @pl.when(i >= nu_ref[0])
    def _():
        o_ref[...] = jnp.zeros(o_ref.shape, o_ref.dtype)


def _combine_kernel(dest_ref, h_ref, wgt_ref, g_ref, b_ref, ys_hbm, o_ref, ob_ref, ybuf, sem, *, tc, alpha):
    base = pl.program_id(0) * tc

    def issue(t, carry):
        tok = base + t
        _row_copy(ys_hbm, dest_ref[2 * tok], ybuf, t, sem).start()
        _row_copy(ys_hbm, dest_ref[2 * tok + 1], ybuf, tc + t, sem).start()
        return carry

    lax.fori_loop(0, tc, issue, 0)
    pltpu.make_async_copy(ys_hbm.at[pl.ds(0, 2 * tc)], ybuf, sem).wait()
    wgt = wgt_ref[...]
    ff = ybuf[0:tc, :] * wgt[:, 0:1] + ybuf[tc:2 * tc, :] * wgt[:, 1:2]
    h2 = _layer_norm(alpha * h_ref[...] + ff, g_ref[...], b_ref[...])
    o_ref[...] = h2
    ob_ref[...] = h2.astype(BF16)


def _tiles(seq):
    big = seq >= 2048
    return dict(
        rows=256 if big else 128,
        mla=512 if big else 128,
        idx_q=128 if big else 64,
        dsa_q=256 if big else 128,
        key=512 if big else 128,
        route=512 if big else 128,
        moe_tok=512 if big else 128,
        comb=128,
    )


def _rope_tables(positions, rot_dim, head_dim):
    half = rot_dim // 2
    inv = ROPE_THETA ** (-jnp.arange(0, rot_dim, 2, dtype=F32) / rot_dim)
    ang = positions.astype(F32).reshape(-1, 1) * inv
    cos, sin = jnp.cos(ang), jnp.sin(ang)
    t = cos.shape[0]
    rest = head_dim - rot_dim
    ones, zeros, zh = jnp.ones((t, rest), F32), jnp.zeros((t, rest), F32), jnp.zeros((t, half), F32)
    rep = LANES // head_dim
    c = jnp.tile(jnp.concatenate([cos, cos, ones], axis=1), (1, rep))
    sa = jnp.tile(jnp.concatenate([-sin, zh, zeros], axis=1), (1, rep))
    sb = jnp.tile(jnp.concatenate([zh, sin, zeros], axis=1), (1, rep))
    return c, sa, sb


def _layer(h, h_bf, positions, tabs, lw, alpha, B, S):
    T, D = h.shape
    tl = _tiles(S)
    tm = tl["rows"]
    nrow = T // tm
    spb = S // tm
    (c_m, sa_m, sb_m), (c_d, sa_d, sb_d), (c_i, sa_i, sb_i) = tabs
    row = lambda w: pl.BlockSpec((tm, w), lambda i: (i, 0))
    head_major = lambda nh, w: pl.BlockSpec((1, nh, tm, w), lambda i: (i // spb, 0, i % spb, 0))

    q_a, k_a, v_a = pl.pallas_call(
        _mla_prep_kernel,
        grid=(nrow,),
        in_specs=[row(D), _resident(lw["w_mla_in"].shape), _resident((1, MLA_Q_LORA)), _resident((1, MLA_KV_LORA)),
                  _resident(lw["w_q_b"].shape), _resident(lw["w_kv_b"].shape), row(LANES), row(LANES), row(LANES)],
        out_specs=[head_major(MLA_HEADS, MLA_QK_DIM), head_major(MLA_HEADS, MLA_QK_DIM), row(MLA_HEADS * MLA_V_DIM)],
        out_shape=[jax.ShapeDtypeStruct((B, MLA_HEADS, S, MLA_QK_DIM), BF16),
                   jax.ShapeDtypeStruct((B, MLA_HEADS, S, MLA_QK_DIM), BF16),
                   jax.ShapeDtypeStruct((T, MLA_HEADS * MLA_V_DIM), BF16)],
        compiler_params=_params("parallel"),
    )(h_bf, lw["w_mla_in"], lw["g_q"], lw["g_kv"], lw["w_q_b"], lw["w_kv_b"], c_m, sa_m, sb_m)

    hd = DSA_HEADS * DSA_HEAD_DIM
    q_b, k_b, v_b, q_idx, k_idx, w_idx = pl.pallas_call(
        functools.partial(_dsa_prep_kernel, w_idx_scale=(IDX_HEADS ** -0.5) * (IDX_DIM ** -0.5)),
        grid=(nrow,),
        in_specs=[row(D), _resident(lw["w_dsa_in"].shape)] + [row(LANES)] * 6,
        out_specs=[row(hd), row(hd), row(hd), head_major(IDX_HEADS, IDX_DIM), row(IDX_DIM), row(IDX_HEADS)],
        out_shape=[jax.ShapeDtypeStruct((T, hd), BF16)] * 3 + [
            jax.ShapeDtypeStruct((B, IDX_HEADS, S, IDX_DIM), BF16),
            jax.ShapeDtypeStruct((T, IDX_DIM), BF16),
            jax.ShapeDtypeStruct((T, IDX_HEADS), F32)],
        compiler_params=_params("parallel"),
    )(h_bf, lw["w_dsa_in"], c_d, sa_d, sb_d, c_i, sa_i, sb_i)

    ta = tl["mla"]
    pos_q = positions.reshape(B, S, 1)
    pos_k_mla = positions.reshape(B, S // ta, 1, ta)
    o_a = pl.pallas_call(
        functools.partial(_mla_attn_kernel, scale=MLA_QK_DIM ** -0.5, tile=ta),
        grid=(B, MLA_HEADS, S // ta),
        in_specs=[pl.BlockSpec((1, 1, ta, MLA_QK_DIM), lambda b, hh, i: (b, hh, i, 0)),
                  pl.BlockSpec((1, 1, S, MLA_QK_DIM), lambda b, hh, i: (b, hh, 0, 0)),
                  pl.BlockSpec((1, S, MLA_V_DIM), lambda b, hh, i: (b, 0, hh)),
                  pl.BlockSpec((1, ta, 1), lambda b, hh, i: (b, i, 0)),
                  pl.BlockSpec((1, S // ta, 1, ta), lambda b, hh, i: (b, 0, 0, 0))],
        out_specs=pl.BlockSpec((1, ta, MLA_V_DIM), lambda b, hh, i: (b, i, hh)),
        out_shape=jax.ShapeDtypeStruct((B, S, MLA_HEADS * MLA_V_DIM), BF16),
        scratch_shapes=[pltpu.VMEM((ta, 1), F32), pltpu.VMEM((ta, 1), F32), pltpu.VMEM((ta, MLA_V_DIM), F32)],
        compiler_params=_params("parallel", "parallel", "arbitrary"),
    )(q_a, k_a, v_a.reshape(B, S, -1), pos_q, pos_k_mla)

    tk = tl["key"]
    nkc = S // tk
    tqi = tl["idx_q"]
    top_k = min(DSA_MAX_TOPK, S // 4)
    pos_k = positions.reshape(B, nkc, 1, tk)
    sel_bias = pl.pallas_call(
        functools.partial(_idx_select_kernel, tq=tqi, tk=tk, nkc=nkc, top_k=top_k),
        grid=(B, S // tqi),
        in_specs=[pl.BlockSpec((1, IDX_HEADS, tqi, IDX_DIM), lambda b, i: (b, 0, i, 0)),
                  pl.BlockSpec((1, S, IDX_DIM), lambda b, i: (b, 0, 0)),
                  pl.BlockSpec((1, tqi, IDX_HEADS), lambda b, i: (b, i, 0)),
                  pl.BlockSpec((1, tqi, 1), lambda b, i: (b, i, 0)),
                  pl.BlockSpec((1, nkc, 1, tk), lambda b, i: (b, 0, 0, 0))],
        out_specs=pl.BlockSpec((1, nkc, tqi, tk), lambda b, i: (b, 0, i, 0)),
        out_shape=jax.ShapeDtypeStruct((B, nkc, S, tk), BF16),
        scratch_shapes=[pltpu.VMEM((nkc, tqi, tk), I32)],
        compiler_params=_params("parallel", "arbitrary"),
    )(q_idx, k_idx.reshape(B, S, IDX_DIM), w_idx.reshape(B, S, IDX_HEADS), pos_q, pos_k)

    tqd = tl["dsa_q"]
    o_b = pl.pallas_call(
        functools.partial(_dsa_attn_kernel, scale=DSA_HEAD_DIM ** -0.5, tq=tqd, tk=tk),
        grid=(B, S // tqd),
        in_specs=[pl.BlockSpec((1, tqd, hd), lambda b, i: (b, i, 0)),
                  pl.BlockSpec((1, S, hd), lambda b, i: (b, 0, 0), pipeline_mode=pl.Buffered(1)),
                  pl.BlockSpec((1, S, hd), lambda b, i: (b, 0, 0), pipeline_mode=pl.Buffered(1)),
                  pl.BlockSpec((1, nkc, tqd, tk), lambda b, i: (b, 0, i, 0))],
        out_specs=pl.BlockSpec((1, tqd, hd), lambda b, i: (b, i, 0)),
        out_shape=jax.ShapeDtypeStruct((B, S, hd), BF16),
        scratch_shapes=[pltpu.VMEM((DSA_HEADS, tqd, 1), F32), pltpu.VMEM((DSA_HEADS, tqd, 1), F32),
                        pltpu.VMEM((tqd, hd), F32)],
        compiler_params=_params("parallel", "arbitrary"),
    )(q_b.reshape(B, S, hd), k_b.reshape(B, S, hd), v_b.reshape(B, S, hd), sel_bias)

    y = pl.pallas_call(
        _merge_kernel,
        grid=(nrow,),
        in_specs=[row(D), row(MLA_HEADS * MLA_V_DIM), row(hd),
                  _resident((D, D)), _resident((D, D)), _resident((MLA_HEADS * MLA_V_DIM, D)), _resident((hd, D))],
        out_specs=row(D),
        out_shape=jax.ShapeDtypeStruct((T, D), BF16),
        compiler_params=_params("parallel"),
    )(h_bf, o_a.reshape(T, -1), o_b.reshape(T, hd), lw["w_gate_a"], lw["w_gate_b"], lw["w_o_a"], lw["w_o_b"])

    h1, logits = pl.pallas_call(
        functools.partial(_out_ln_kernel, alpha=alpha),
        grid=(nrow,),
        in_specs=[row(D), row(D), _resident((D, D)), _resident((1, D)), _resident((1, D)),
                  _resident((D, LANES)), _resident((1, LANES))],
        out_specs=[row(D), row(LANES)],
        out_shape=[jax.ShapeDtypeStruct((T, D), F32), jax.ShapeDtypeStruct((T, LANES), F32)],
        compiler_params=_params("parallel"),
    )(y, h, lw["w_out"], lw["ln1_g"], lw["ln1_b"], lw["w_route"], lw["b_route"])

    tr = tl["route"]
    meta, wgt, cnt = pl.pallas_call(
        functools.partial(_route_kernel, tr=tr),
        grid=(T // tr,),
        in_specs=[pl.BlockSpec((tr, LANES), lambda i: (i, 0))],
        out_specs=[pl.BlockSpec((tr, LANES), lambda i: (i, 0)), pl.BlockSpec((tr, LANES), lambda i: (i, 0)),
                   pl.BlockSpec((8, LANES), lambda i: (0, 0))],
        out_shape=[jax.ShapeDtypeStruct((T, LANES), I32), jax.ShapeDtypeStruct((T, LANES), F32),
                   jax.ShapeDtypeStruct((8, LANES), F32)],
        scratch_shapes=[pltpu.VMEM((1, LANES), F32)],
        compiler_params=_params("arbitrary"),
    )(logits)

    rb = MOE_ROW_BLOCK
    counts = cnt[0, N_GROUPS:N_GROUPS + N_EXPERTS].astype(I32)
    padded = ((counts + rb - 1) // rb) * rb
    pends = jnp.cumsum(padded)
    pstarts = pends - padded
    n_blocks = -(-(2 * T) // rb) + N_EXPERTS
    P = n_blocks * rb
    dest = (pstarts[meta[:, 0:2]] + meta[:, 2:4]).reshape(-1).astype(I32)
    block_e = jnp.minimum(jnp.searchsorted(pends, jnp.arange(n_blocks, dtype=I32) * rb, side="right"),
                          N_EXPERTS - 1).astype(I32)
    n_used = (pends[-1] // rb).astype(I32).reshape(1)

    tt = tl["moe_tok"]
    xs = pl.pallas_call(
        functools.partial(_dispatch_kernel, tt=tt),
        grid_spec=pltpu.PrefetchScalarGridSpec(
            num_scalar_prefetch=1, grid=(T // tt,),
            in_specs=[pl.BlockSpec(memory_space=pl.ANY), pl.BlockSpec(memory_space=pl.ANY)],
            out_specs=pl.BlockSpec(memory_space=pl.ANY),
            scratch_shapes=[pltpu.SemaphoreType.DMA(())]),
        out_shape=jax.ShapeDtypeStruct((P, D), F32),
        input_output_aliases={2: 0},
        compiler_params=_params("arbitrary"),
    )(dest, h1, jnp.zeros((P, D), F32))

    F = D_EXPERT
    ys = pl.pallas_call(
        _experts_kernel,
        grid_spec=pltpu.PrefetchScalarGridSpec(
            num_scalar_prefetch=2, grid=(n_blocks,),
            in_specs=[pl.BlockSpec((rb, D), lambda i, be, nu: (i, 0)),
                      pl.BlockSpec((1, D, F), lambda i, be, nu: (be[i], 0, 0)),
                      pl.BlockSpec((1, D, F), lambda i, be, nu: (be[i], 0, 0)),
                      pl.BlockSpec((1, F, D), lambda i, be, nu: (be[i], 0, 0))],
            out_specs=pl.BlockSpec((rb, D), lambda i, be, nu: (i, 0)),
            scratch_shapes=[pltpu.VMEM((D, F), BF16), pltpu.VMEM((D, F), BF16), pltpu.VMEM((F, D), BF16)]),
        out_shape=jax.ShapeDtypeStruct((P, D), F32),
        compiler_params=_params("arbitrary"),
    )(block_e, n_used, xs, lw["w_e_gate"], lw["w_e_up"], lw["w_e_down"])

    tc = tl["comb"]
    h2, h2_bf = pl.pallas_call(
        functools.partial(_combine_kernel, tc=tc, alpha=alpha),
        grid_spec=pltpu.PrefetchScalarGridSpec(
            num_scalar_prefetch=1, grid=(T // tc,),
            in_specs=[pl.BlockSpec((tc, D), lambda i, d: (i, 0)),
                      pl.BlockSpec((tc, LANES), lambda i, d: (i, 0)),
                      pl.BlockSpec((1, D), lambda i, d: (0, 0)),
                      pl.BlockSpec((1, D), lambda i, d: (0, 0)),
                      pl.BlockSpec(memory_space=pl.ANY)],
            out_specs=[pl.BlockSpec((tc, D), lambda i, d: (i, 0)), pl.BlockSpec((tc, D), lambda i, d: (i, 0))],
            scratch_shapes=[pltpu.VMEM((2 * tc, D), F32), pltpu.SemaphoreType.DMA(())]),
        out_shape=[jax.ShapeDtypeStruct((T, D), F32), jax.ShapeDtypeStruct((T, D), BF16)],
        compiler_params=_params("arbitrary"),
    )(dest, h1, wgt, lw["ln2_g"], lw["ln2_b"], ys)
    return h2, h2_bf


def _layer_weights(l, w_in, g_q_lora, w_q_b, g_kv_lora, w_kv_b, w_o_a, w_o_b, w_out, ln1_g, ln1_b,
                   w_group, b_group, w_router, b_router, w_e_gate, w_e_up, w_e_down, ln2_g, ln2_b):
    D = w_in.shape[1]
    wi = w_in[l]
    o_kpe = MLA_Q_LORA + MLA_KV_LORA
    o_dsa = o_kpe + MLA_ROPE_DIM
    hd = DSA_HEADS * DSA_HEAD_DIM
    o_kidx = o_dsa + 3 * hd + IDX_HEADS * IDX_DIM
    o_widx = o_kidx + IDX_DIM
    o_gate = o_widx + IDX_HEADS
    kpe = wi[:, o_kpe:o_dsa]
    kidx = wi[:, o_kidx:o_widx]
    w_mla_in = jnp.concatenate([wi[:, :o_kpe], kpe, kpe], axis=1).astype(BF16)
    w_dsa_in = jnp.concatenate([wi[:, o_dsa:o_kidx], kidx, kidx, wi[:, o_widx:o_gate],
                                jnp.zeros((D, LANES - IDX_HEADS), F32)], axis=1).astype(BF16)
    wq = w_q_b[l].reshape(MLA_Q_LORA, MLA_HEADS, MLA_QK_DIM)
    wq = jnp.concatenate([wq[:, :, :MLA_NOPE_DIM].reshape(MLA_Q_LORA, -1),
                          wq[:, :, MLA_NOPE_DIM:].reshape(MLA_Q_LORA, -1)], axis=1).astype(BF16)
    w_route = jnp.concatenate([w_group[l], w_router[l],
                               jnp.zeros((D, LANES - N_GROUPS - N_EXPERTS), F32)], axis=1)
    b_route = jnp.concatenate([b_group[l], b_router[l],
                               jnp.zeros((LANES - N_GROUPS - N_EXPERTS,), F32)]).reshape(1, LANES)
    return dict(
        w_mla_in=w_mla_in, w_dsa_in=w_dsa_in,
        w_gate_a=wi[:, o_gate:o_gate + D].astype(BF16), w_gate_b=wi[:, o_gate + D:].astype(BF16),
        g_q=g_q_lora[l].reshape(1, -1), g_kv=g_kv_lora[l].reshape(1, -1),
        w_q_b=wq, w_kv_b=w_kv_b[l].astype(BF16),
        w_o_a=w_o_a[l].astype(BF16), w_o_b=w_o_b[l].astype(BF16), w_out=w_out[l].astype(BF16),
        ln1_g=ln1_g[l].reshape(1, -1), ln1_b=ln1_b[l].reshape(1, -1),
        w_route=w_route, b_route=b_route,
        w_e_gate=w_e_gate[l], w_e_up=w_e_up[l], w_e_down=w_e_down[l],
        ln2_g=ln2_g[l].reshape(1, -1), ln2_b=ln2_b[l].reshape(1, -1),
    )


def kernel(x, positions, w_in, g_q_lora, w_q_b, g_kv_lora, w_kv_b, w_o_a, w_o_b, w_out, ln1_g, ln1_b,
           w_group, b_group, w_router, b_router, w_e_gate, w_e_up, w_e_down, ln2_g, ln2_b):
    B, S, D = x.shape
    depth = w_in.shape[0]
    alpha = (2 * depth) ** 0.25
    tabs = (_rope_tables(positions, MLA_ROPE_DIM, MLA_ROPE_DIM),
            _rope_tables(positions, DSA_ROPE_DIM, DSA_HEAD_DIM),
            _rope_tables(positions, IDX_ROPE_DIM, IDX_DIM))
    h = x.reshape(B * S, D)
    h_bf = h.astype(BF16)
    for l in range(depth):
        lw = _layer_weights(l, w_in, g_q_lora, w_q_b, g_kv_lora, w_kv_b, w_o_a, w_o_b, w_out, ln1_g, ln1_b,
                            w_group, b_group, w_router, b_router, w_e_gate, w_e_up, w_e_down, ln2_g, ln2_b)
        h, h_bf = _layer(h, h_bf, positions, tabs, lw, alpha, B, S)
    return h.reshape(B, S, D)
```

```python
import functools
import math

import jax
import jax.numpy as jnp
from jax import lax
from jax.experimental import pallas as pl
from jax.experimental.pallas import tpu as pltpu

F32 = jnp.float32
BF16 = jnp.bfloat16
I32 = jnp.int32

MLA_HEADS = 8
MLA_Q_LORA = 512
MLA_KV_LORA = 512
MLA_NOPE_DIM = 128
MLA_ROPE_DIM = 64
MLA_V_DIM = 128
MLA_QK_DIM = MLA_NOPE_DIM + MLA_ROPE_DIM
DSA_HEADS = 8
DSA_HEAD_DIM = 128
DSA_ROPE_DIM = DSA_HEAD_DIM // 4
IDX_HEADS = 16
IDX_DIM = 64
IDX_ROPE_DIM = IDX_DIM // 4
DSA_MAX_TOPK = 256
ROPE_THETA = 500000.0
N_GROUPS = 8
EXPERTS_PER_GROUP = 8
N_EXPERTS = N_GROUPS * EXPERTS_PER_GROUP
D_EXPERT = 512
MOE_ROW_BLOCK = 128
LN_EPS = 1e-5
RMS_EPS = 1e-6

LANES = 128
NEG_BIG = -1e30
INT_MIN = -(2 ** 31)
VMEM_LIMIT = 56 * 1024 * 1024


def _params(*sem):
    return pltpu.CompilerParams(dimension_semantics=sem, vmem_limit_bytes=VMEM_LIMIT)


def _resident(shape):
    nd = len(shape)
    return pl.BlockSpec(shape, lambda *_: (0,) * nd, pipeline_mode=pl.Buffered(1))


def _dot(a, b):
    return jnp.dot(a, b, preferred_element_type=F32)


def _dot_nt(a, b):
    return lax.dot_general(a, b, (((1,), (1,)), ((), ())), preferred_element_type=F32)


def _rope128(x, c, sa, sb, half):
    return x * c + pltpu.roll(x, LANES - half, 1) * sa + pltpu.roll(x, half, 1) * sb


def _rms(x, g):
    return x * lax.rsqrt(jnp.mean(x * x, axis=-1, keepdims=True) + RMS_EPS) * g


def _layer_norm(x, g, b):
    mu = jnp.mean(x, axis=-1, keepdims=True)
    xc = x - mu
    var = jnp.mean(xc * xc, axis=-1, keepdims=True)
    return xc * lax.rsqrt(var + LN_EPS) * g + b


def _mla_prep_kernel(h_ref, win_ref, gq_ref, gkv_ref, wqb_ref, wkvb_ref, c_ref, sa_ref, sb_ref,
                     q_ref, k_ref, v_ref):
    h = h_ref[...]
    p = _dot(h, win_ref[...])
    qn = _rms(p[:, :MLA_Q_LORA], gq_ref[...]).astype(BF16)
    kvn = _rms(p[:, MLA_Q_LORA:MLA_Q_LORA + MLA_KV_LORA], gkv_ref[...]).astype(BF16)
    q = _dot(qn, wqb_ref[...])
    kv = _dot(kvn, wkvb_ref[...])
    c, sa, sb = c_ref[...], sa_ref[...], sb_ref[...]
    half = MLA_ROPE_DIM // 2
    kpe = _rope128(p[:, MLA_Q_LORA + MLA_KV_LORA:], c, sa, sb, half)[:, :MLA_ROPE_DIM].astype(BF16)
    pe0 = MLA_HEADS * MLA_NOPE_DIM
    for hh in range(MLA_HEADS):
        q_ref[0, hh, :, 0:MLA_NOPE_DIM] = q[:, 128 * hh:128 * hh + 128].astype(BF16)
        k_ref[0, hh, :, 0:MLA_NOPE_DIM] = kv[:, 256 * hh:256 * hh + 128].astype(BF16)
        k_ref[0, hh, :, MLA_NOPE_DIM:MLA_QK_DIM] = kpe
        v_ref[:, 128 * hh:128 * hh + 128] = kv[:, 256 * hh + 128:256 * hh + 256].astype(BF16)
    for s in range(MLA_HEADS // 2):
        slab = _rope128(q[:, pe0 + 128 * s:pe0 + 128 * s + 128], c, sa, sb, half).astype(BF16)
        q_ref[0, 2 * s, :, MLA_NOPE_DIM:MLA_QK_DIM] = slab[:, :MLA_ROPE_DIM]
        q_ref[0, 2 * s + 1, :, MLA_NOPE_DIM:MLA_QK_DIM] = slab[:, MLA_ROPE_DIM:]


def _dsa_prep_kernel(h_ref, w_ref, cd_ref, sad_ref, sbd_ref, ci_ref, sai_ref, sbi_ref,
                     qb_ref, kb_ref, vb_ref, qi_ref, ki_ref, wi_ref, *, w_idx_scale):
    h = h_ref[...]
    hd = DSA_HEADS * DSA_HEAD_DIM
    cd, sad, sbd = cd_ref[...], sad_ref[...], sbd_ref[...]
    ci, sai, sbi = ci_ref[...], sai_ref[...], sbi_ref[...]
    hd_half, hi_half = DSA_ROPE_DIM // 2, IDX_ROPE_DIM // 2
    q = _dot(h, w_ref[:, 0:hd])
    for hh in range(DSA_HEADS):
        qb_ref[:, 128 * hh:128 * hh + 128] = _rope128(q[:, 128 * hh:128 * hh + 128], cd, sad, sbd, hd_half).astype(BF16)
    k = _dot(h, w_ref[:, hd:2 * hd])
    for hh in range(DSA_HEADS):
        kb_ref[:, 128 * hh:128 * hh + 128] = _rope128(k[:, 128 * hh:128 * hh + 128], cd, sad, sbd, hd_half).astype(BF16)
    vb_ref[...] = _dot(h, w_ref[:, 2 * hd:3 * hd]).astype(BF16)
    qi = _dot(h, w_ref[:, 3 * hd:3 * hd + IDX_HEADS * IDX_DIM])
    for s in range(IDX_HEADS // 2):
        slab = _rope128(qi[:, 128 * s:128 * s + 128], ci, sai, sbi, hi_half).astype(BF16)
        qi_ref[0, 2 * s] = slab[:, :IDX_DIM]
        qi_ref[0, 2 * s + 1] = slab[:, IDX_DIM:]
    o = 3 * hd + IDX_HEADS * IDX_DIM
    last = _dot(h, w_ref[:, o:o + 2 * LANES])
    ki_ref[...] = _rope128(last[:, :LANES], ci, sai, sbi, hi_half)[:, :IDX_DIM].astype(BF16)
    wi_ref[...] = last[:, LANES:LANES + IDX_HEADS] * w_idx_scale


def _mla_attn_kernel(q_ref, k_ref, v_ref, pq_ref, pk_ref, o_ref, m_sc, l_sc, acc_sc, *, scale, tile):
    qi = pl.program_id(2)
    q = q_ref[0, 0]
    m_sc[...] = jnp.full(m_sc.shape, -jnp.inf, F32)
    l_sc[...] = jnp.zeros(l_sc.shape, F32)
    acc_sc[...] = jnp.zeros(acc_sc.shape, F32)

    def chunk(c, masked):
        ks = pl.ds(pl.multiple_of(c * tile, tile), tile)
        s = _dot_nt(q, k_ref[0, 0, ks, :]) * scale
        if masked:
            s = jnp.where(pk_ref[0, c] <= pq_ref[0], s, NEG_BIG)
        m_old = m_sc[...]
        m_new = jnp.maximum(m_old, jnp.max(s, axis=-1, keepdims=True))
        p = jnp.exp(s - m_new)
        alpha = jnp.exp(m_old - m_new)
        l_sc[...] = alpha * l_sc[...] + jnp.sum(p, axis=-1, keepdims=True)
        acc_sc[...] = alpha * acc_sc[...] + _dot(p.astype(BF16), v_ref[0, ks, :])
        m_sc[...] = m_new

    def body(c, carry):
        chunk(c, False)
        return carry

    lax.fori_loop(0, qi, body, 0)
    chunk(qi, True)
    o_ref[0] = (acc_sc[...] / l_sc[...]).astype(o_ref.dtype)


def _idx_select_kernel(q_ref, k_ref, w_ref, pq_ref, pk_ref, o_ref, key_sc, *, tq, tk, nkc, top_k):
    qi = pl.program_id(1)
    nk = (qi * tq + tq + tk - 1) // tk
    w = w_ref[0]
    pq = pq_ref[0]

    def score_chunk(c, carry):
        kc = k_ref[0, pl.ds(pl.multiple_of(c * tk, tk), tk), :]
        acc = jnp.zeros((tq, tk), F32)
        for hh in range(IDX_HEADS):
            acc = acc + w[:, hh:hh + 1] * jnp.maximum(_dot_nt(q_ref[0, hh], kc), 0.0)
        bits = pltpu.bitcast(acc, I32)
        key = jnp.where(bits < 0, bits ^ jnp.int32(0x7FFFFFFF), bits)
        key = jnp.where(acc == 0.0, 0, key)
        key_sc[c] = jnp.where(pk_ref[0, c] <= pq, key, INT_MIN)
        return carry

    lax.fori_loop(0, nk, score_chunk, 0)

    def count(pred):
        def body(c, part):
            m = pred(key_sc[c], c).astype(I32)
            for j in range(tk // LANES):
                part = part + m[:, LANES * j:LANES * j + LANES]
            return part
        part = lax.fori_loop(0, nk, body, jnp.zeros((tq, LANES), I32))
        return jnp.sum(part, axis=1, keepdims=True)

    def bit_step(i, base):
        cand = base ^ (jnp.int32(1) << (31 - i))
        cnt = count(lambda key, c: key >= cand)
        return jnp.where(cnt >= top_k, cand, base)

    thr = lax.fori_loop(0, 32, bit_step, jnp.full((tq, 1), INT_MIN, I32))
    n_gt = count(lambda key, c: key > thr)
    n_eq = count(lambda key, c: key == thr)
    need = top_k - n_gt
    tie = jnp.logical_and(n_eq > need, thr != INT_MIN)

    def key_index(c):
        return c * tk + lax.broadcasted_iota(I32, (tq, tk), 1)

    def tie_search():
        def step(i, p):
            cand = p | (jnp.int32(1) << (14 - i))
            cnt = count(lambda key, c: jnp.logical_and(key == thr, key_index(c) < cand))
            return jnp.where(cnt <= need, cand, p)
        return lax.fori_loop(0, 15, step, jnp.zeros((tq, 1), I32))

    p_cut = lax.cond(jnp.max(tie.astype(I32)) > 0, tie_search,
                     lambda: jnp.full((tq, 1), 2 ** 30, I32))

    def out_chunk(c, carry):
        key = key_sc[c]
        sel = jnp.logical_or(key > thr, jnp.logical_and(key == thr, key_index(c) < p_cut))
        sel = jnp.logical_and(sel, key != INT_MIN)
        o_ref[0, c] = jnp.where(sel, 0.0, NEG_BIG).astype(o_ref.dtype)
        return carry

    lax.fori_loop(0, nk, out_chunk, 0)

    def fill_chunk(c, carry):
        o_ref[0, c] = jnp.full((tq, tk), NEG_BIG, o_ref.dtype)
        return carry

    lax.fori_loop(nk, nkc, fill_chunk, 0)


def _dsa_attn_kernel(q_ref, k_ref, v_ref, b_ref, o_ref, m_sc, l_sc, acc_sc, *, scale, tq, tk):
    qi = pl.program_id(1)
    nk = (qi * tq + tq + tk - 1) // tk
    m_sc[...] = jnp.full(m_sc.shape, NEG_BIG, F32)
    l_sc[...] = jnp.zeros(l_sc.shape, F32)
    acc_sc[...] = jnp.zeros(acc_sc.shape, F32)

    def chunk(c, carry):
        bias = b_ref[0, c].astype(F32)
        ks = pl.ds(pl.multiple_of(c * tk, tk), tk)
        for hh in range(DSA_HEADS):
            hs = slice(DSA_HEAD_DIM * hh, DSA_HEAD_DIM * (hh + 1))
            s = _dot_nt(q_ref[0, :, hs], k_ref[0, ks, hs]) * scale + bias
            m_old = m_sc[hh]
            m_new = jnp.maximum(m_old, jnp.max(s, axis=-1, keepdims=True))
            p = jnp.exp(s - m_new)
            alpha = jnp.exp(m_old - m_new)
            l_sc[hh] = alpha * l_sc[hh] + jnp.sum(p, axis=-1, keepdims=True)
            acc_sc[:, hs] = alpha * acc_sc[:, hs] + _dot(p.astype(BF16), v_ref[0, ks, hs])
            m_sc[hh] = m_new
        return carry

    lax.fori_loop(0, nk, chunk, 0)
    for hh in range(DSA_HEADS):
        hs = slice(DSA_HEAD_DIM * hh, DSA_HEAD_DIM * (hh + 1))
        o_ref[0, :, hs] = (acc_sc[:, hs] / l_sc[hh]).astype(o_ref.dtype)


def _merge_kernel(h_ref, oa_ref, ob_ref, wga_ref, wgb_ref, woa_ref, wob_ref, y_ref):
    h = h_ref[...]
    ya = jax.nn.sigmoid(_dot(h, wga_ref[...])) * _dot(oa_ref[...], woa_ref[...])
    yb = jax.nn.sigmoid(_dot(h, wgb_ref[...])) * _dot(ob_ref[...], wob_ref[...])
    y_ref[...] = (ya + yb).astype(y_ref.dtype)


def _out_ln_kernel(y_ref, h_ref, wout_ref, g_ref, b_ref, wr_ref, br_ref, o_ref, lg_ref, *, alpha):
    mix = _dot(y_ref[...], wout_ref[...])
    h1 = _layer_norm(alpha * h_ref[...] + mix, g_ref[...], b_ref[...])
    o_ref[...] = h1
    lg_ref[...] = jnp.dot(h1, wr_ref[...], preferred_element_type=F32,
                          precision=lax.Precision.HIGHEST) + br_ref[...]


def _route_kernel(lg_ref, meta_ref, wgt_ref, cnt_ref, carry_sc, *, tr):
    @pl.when(pl.program_id(0) == 0)
    def _():
        carry_sc[...] = jnp.zeros(carry_sc.shape, F32)

    lg = lg_ref[...]
    lane = lax.broadcasted_iota(I32, (tr, LANES), 1)
    gl = jnp.where(lane < N_GROUPS, lg, -jnp.inf)
    gmax = jnp.max(gl, axis=1, keepdims=True)
    g_idx = jnp.min(jnp.where(gl == gmax, lane, LANES), axis=1, keepdims=True)
    g_p = 1.0 / jnp.sum(jnp.exp(gl - gmax), axis=1, keepdims=True)
    lo = N_GROUPS + EXPERTS_PER_GROUP * g_idx
    in_group = jnp.logical_and(lane >= lo, lane < lo + EXPERTS_PER_GROUP)
    el = jnp.where(in_group, lg, -jnp.inf)
    emax = jnp.max(el, axis=1, keepdims=True)
    e1 = jnp.min(jnp.where(el == emax, lane, LANES), axis=1, keepdims=True)
    den = jnp.sum(jnp.exp(el - emax), axis=1, keepdims=True)
    el2 = jnp.where(lane == e1, -jnp.inf, el)
    emax2 = jnp.max(el2, axis=1, keepdims=True)
    e2 = jnp.min(jnp.where(el2 == emax2, lane, LANES), axis=1, keepdims=True)
    p1 = 1.0 / den
    p2 = jnp.exp(emax2 - emax) / den
    w1 = g_p * (p1 / (p1 + p2))
    w2 = g_p * (p2 / (p1 + p2))

    is1 = lane == e1
    is2 = lane == e2
    onehot = jnp.logical_or(is1, is2).astype(BF16)
    r = lax.broadcasted_iota(I32, (tr, tr), 0)
    cidx = lax.broadcasted_iota(I32, (tr, tr), 1)
    lower = (cidx < r).astype(BF16)
    before = _dot(lower, onehot) + carry_sc[...]
    rank1 = jnp.sum(jnp.where(is1, before, 0.0), axis=1, keepdims=True)
    rank2 = jnp.sum(jnp.where(is2, before, 0.0), axis=1, keepdims=True)
    carry_sc[...] = carry_sc[...] + jnp.sum(onehot.astype(F32), axis=0, keepdims=True)

    meta = jnp.where(lane == 0, e1 - N_GROUPS,
                     jnp.where(lane == 1, e2 - N_GROUPS,
                               jnp.where(lane == 2, rank1.astype(I32),
                                         jnp.where(lane == 3, rank2.astype(I32), 0))))
    meta_ref[...] = meta
    wgt_ref[...] = jnp.where(lane == 0, w1, jnp.where(lane == 1, w2, 0.0))
    cnt_ref[...] = jnp.broadcast_to(carry_sc[...], cnt_ref.shape)


def _row_copy(src, src_row, dst, dst_row, sem):
    return pltpu.make_async_copy(src.at[pl.ds(src_row, 1)], dst.at[pl.ds(dst_row, 1)], sem)


def _dispatch_kernel(dest_ref, h_ref, xs_in, xs_out, sem, *, tt):
    del xs_in
    base = pl.program_id(0) * tt

    def issue(t, carry):
        tok = base + t
        _row_copy(h_ref, t, xs_out, dest_ref[2 * tok], sem).start()
        _row_copy(h_ref, t, xs_out, dest_ref[2 * tok + 1], sem).start()
        return carry

    lax.fori_loop(0, tt, issue, 0)
    for _ in range(2):
        pltpu.make_async_copy(h_ref, xs_out.at[pl.ds(0, tt)], sem).wait()


def _experts_kernel(be_ref, nu_ref, xs_ref, wg_ref, wu_ref, wd_ref, o_ref, wg_sc, wu_sc, wd_sc):
    i = pl.program_id(0)
    e = be_ref[i]
    prev = be_ref[jnp.maximum(i - 1, 0)]

    @pl.when(jnp.logical_or(i == 0, e != prev))
    def _():
        wg_sc[...] = wg_ref[0, 0].astype(BF16)
        wu_sc[...] = wu_ref[0, 0].astype(BF16)
        wd_sc[...] = wd_ref[0, 0].astype(BF16)

    @pl.when(i < nu_ref[0])
    def _():
        x = xs_ref[...].astype(BF16)
        hid = jax.nn.silu(_dot(x, wg_sc[...])) * _dot(x, wu_sc[...])
        o_ref[...] = _dot(hid.astype(BF16), wd_sc[...])

    @pl.when(i >= nu_ref[0])
    def _():
        o_ref[...] = jnp.zeros(o_ref.shape, o_ref.dtype)


def _combine_kernel(dest_ref, h_ref, wgt_ref, g_ref, b_ref, ys_hbm, o_ref, ob_ref, ybuf, sem, *, tc, alpha):
    base = pl.program_id(0) * tc

    def issue(t, carry):
        tok = base + t
        _row_copy(ys_hbm, dest_ref[2 * tok], ybuf, t, sem).start()
        _row_copy(ys_hbm, dest_ref[2 * tok + 1], ybuf, tc + t, sem).start()
        return carry

    lax.fori_loop(0, tc, issue, 0)
    pltpu.make_async_copy(ys_hbm.at[pl.ds(0, 2 * tc)], ybuf, sem).wait()
    wgt = wgt_ref[...]
    ff = ybuf[0:tc, :] * wgt[:, 0:1] + ybuf[tc:2 * tc, :] * wgt[:, 1:2]
    h2 = _layer_norm(alpha * h_ref[...] + ff, g_ref[...], b_ref[...])
    o_ref[...] = h2
    ob_ref[...] = h2.astype(BF16)


def _tiles(seq):
    big = seq >= 2048
    return dict(
        rows=256 if big else 128,
        mla=512 if big else 128,
        idx_q=128 if big else 64,
        dsa_q=256 if big else 128,
        key=512 if big else 128,
        route=512 if big else 128,
        moe_tok=512 if big else 128,
        comb=128,
    )


def _rope_tables(positions, rot_dim, head_dim):
    half = rot_dim // 2
    inv = ROPE_THETA ** (-jnp.arange(0, rot_dim, 2, dtype=F32) / rot_dim)
    ang = positions.astype(F32).reshape(-1, 1) * inv
    cos, sin = jnp.cos(ang), jnp.sin(ang)
    t = cos.shape[0]
    rest = head_dim - rot_dim
    ones, zeros, zh = jnp.ones((t, rest), F32), jnp.zeros((t, rest), F32), jnp.zeros((t, half), F32)
    rep = LANES // head_dim
    c = jnp.tile(jnp.concatenate([cos, cos, ones], axis=1), (1, rep))
    sa = jnp.tile(jnp.concatenate([-sin, zh, zeros], axis=1), (1, rep))
    sb = jnp.tile(jnp.concatenate([zh, sin, zeros], axis=1), (1, rep))
    return c, sa, sb


def _layer(layer, h, h_bf, positions, tabs, lw, alpha, B, S):
    T, D = h.shape
    tl = _tiles(S)
    tm = tl["rows"]
    nrow = T // tm
    spb = S // tm
    (c_m, sa_m, sb_m), (c_d, sa_d, sb_d), (c_i, sa_i, sb_i) = tabs
    row = lambda w: pl.BlockSpec((tm, w), lambda i: (i, 0))
    head_major = lambda nh, w: pl.BlockSpec((1, nh, tm, w), lambda i: (i // spb, 0, i % spb, 0))

    q_a, k_a, v_a = pl.pallas_call(
        _mla_prep_kernel,
        grid=(nrow,),
        in_specs=[row(D), _resident(lw["w_mla_in"].shape), _resident((1, MLA_Q_LORA)), _resident((1, MLA_KV_LORA)),
                  _resident(lw["w_q_b"].shape), _resident(lw["w_kv_b"].shape), row(LANES), row(LANES), row(LANES)],
        out_specs=[head_major(MLA_HEADS, MLA_QK_DIM), head_major(MLA_HEADS, MLA_QK_DIM), row(MLA_HEADS * MLA_V_DIM)],
        out_shape=[jax.ShapeDtypeStruct((B, MLA_HEADS, S, MLA_QK_DIM), BF16),
                   jax.ShapeDtypeStruct((B, MLA_HEADS, S, MLA_QK_DIM), BF16),
                   jax.ShapeDtypeStruct((T, MLA_HEADS * MLA_V_DIM), BF16)],
        compiler_params=_params("parallel"),
    )(h_bf, lw["w_mla_in"], lw["g_q"], lw["g_kv"], lw["w_q_b"], lw["w_kv_b"], c_m, sa_m, sb_m)

    hd = DSA_HEADS * DSA_HEAD_DIM
    q_b, k_b, v_b, q_idx, k_idx, w_idx = pl.pallas_call(
        functools.partial(_dsa_prep_kernel, w_idx_scale=(IDX_HEADS ** -0.5) * (IDX_DIM ** -0.5)),
        grid=(nrow,),
        in_specs=[row(D), _resident(lw["w_dsa_in"].shape)] + [row(LANES)] * 6,
        out_specs=[row(hd), row(hd), row(hd), head_major(IDX_HEADS, IDX_DIM), row(IDX_DIM), row(IDX_HEADS)],
        out_shape=[jax.ShapeDtypeStruct((T, hd), BF16)] * 3 + [
            jax.ShapeDtypeStruct((B, IDX_HEADS, S, IDX_DIM), BF16),
            jax.ShapeDtypeStruct((T, IDX_DIM), BF16),
            jax.ShapeDtypeStruct((T, IDX_HEADS), F32)],
        compiler_params=_params("parallel"),
    )(h_bf, lw["w_dsa_in"], c_d, sa_d, sb_d, c_i, sa_i, sb_i)

    ta = tl["mla"]
    pos_q = positions.reshape(B, S, 1)
    pos_k_mla = positions.reshape(B, S // ta, 1, ta)
    o_a = pl.pallas_call(
        functools.partial(_mla_attn_kernel, scale=MLA_QK_DIM ** -0.5, tile=ta),
        grid=(B, MLA_HEADS, S // ta),
        in_specs=[pl.BlockSpec((1, 1, ta, MLA_QK_DIM), lambda b, hh, i: (b, hh, i, 0)),
                  pl.BlockSpec((1, 1, S, MLA_QK_DIM), lambda b, hh, i: (b, hh, 0, 0)),
                  pl.BlockSpec((1, S, MLA_V_DIM), lambda b, hh, i: (b, 0, hh)),
                  pl.BlockSpec((1, ta, 1), lambda b, hh, i: (b, i, 0)),
                  pl.BlockSpec((1, S // ta, 1, ta), lambda b, hh, i: (b, 0, 0, 0))],
        out_specs=pl.BlockSpec((1, ta, MLA_V_DIM), lambda b, hh, i: (b, i, hh)),
        out_shape=jax.ShapeDtypeStruct((B, S, MLA_HEADS * MLA_V_DIM), BF16),
        scratch_shapes=[pltpu.VMEM((ta, 1), F32), pltpu.VMEM((ta, 1), F32), pltpu.VMEM((ta, MLA_V_DIM), F32)],
        compiler_params=_params("parallel", "parallel", "arbitrary"),
    )(q_a, k_a, v_a.reshape(B, S, -1), pos_q, pos_k_mla)

    tk = tl["key"]
    nkc = S // tk
    tqi = tl["idx_q"]
    top_k = min(DSA_MAX_TOPK, S // 4)
    pos_k = positions.reshape(B, nkc, 1, tk)
    sel_bias = pl.pallas_call(
        functools.partial(_idx_select_kernel, tq=tqi, tk=tk, nkc=nkc, top_k=top_k),
        grid=(B, S // tqi),
        in_specs=[pl.BlockSpec((1, IDX_HEADS, tqi, IDX_DIM), lambda b, i: (b, 0, i, 0)),
                  pl.BlockSpec((1, S, IDX_DIM), lambda b, i: (b, 0, 0)),
                  pl.BlockSpec((1, tqi, IDX_HEADS), lambda b, i: (b, i, 0)),
                  pl.BlockSpec((1, tqi, 1), lambda b, i: (b, i, 0)),
                  pl.BlockSpec((1, nkc, 1, tk), lambda b, i: (b, 0, 0, 0))],
        out_specs=pl.BlockSpec((1, nkc, tqi, tk), lambda b, i: (b, 0, i, 0)),
        out_shape=jax.ShapeDtypeStruct((B, nkc, S, tk), BF16),
        scratch_shapes=[pltpu.VMEM((nkc, tqi, tk), I32)],
        compiler_params=_params("parallel", "arbitrary"),
    )(q_idx, k_idx.reshape(B, S, IDX_DIM), w_idx.reshape(B, S, IDX_HEADS), pos_q, pos_k)

    tqd = tl["dsa_q"]
    o_b = pl.pallas_call(
        functools.partial(_dsa_attn_kernel, scale=DSA_HEAD_DIM ** -0.5, tq=tqd, tk=tk),
        grid=(B, S // tqd),
        in_specs=[pl.BlockSpec((1, tqd, hd), lambda b, i: (b, i, 0)),
                  pl.BlockSpec((1, S, hd), lambda b, i: (b, 0, 0), pipeline_mode=pl.Buffered(1)),
                  pl.BlockSpec((1, S, hd), lambda b, i: (b, 0, 0), pipeline_mode=pl.Buffered(1)),
                  pl.BlockSpec((1, nkc, tqd, tk), lambda b, i: (b, 0, i, 0))],
        out_specs=pl.BlockSpec((1, tqd, hd), lambda b, i: (b, i, 0)),
        out_shape=jax.ShapeDtypeStruct((B, S, hd), BF16),
        scratch_shapes=[pltpu.VMEM((DSA_HEADS, tqd, 1), F32), pltpu.VMEM((DSA_HEADS, tqd, 1), F32),
                        pltpu.VMEM((tqd, hd), F32)],
        compiler_params=_params("parallel", "arbitrary"),
    )(q_b.reshape(B, S, hd), k_b.reshape(B, S, hd), v_b.reshape(B, S, hd), sel_bias)

    y = pl.pallas_call(
        _merge_kernel,
        grid=(nrow,),
        in_specs=[row(D), row(MLA_HEADS * MLA_V_DIM), row(hd),
                  _resident((D, D)), _resident((D, D)), _resident((MLA_HEADS * MLA_V_DIM, D)), _resident((hd, D))],
        out_specs=row(D),
        out_shape=jax.ShapeDtypeStruct((T, D), BF16),
        compiler_params=_params("parallel"),
    )(h_bf, o_a.reshape(T, -1), o_b.reshape(T, hd), lw["w_gate_a"], lw["w_gate_b"], lw["w_o_a"], lw["w_o_b"])

    h1, logits = pl.pallas_call(
        functools.partial(_out_ln_kernel, alpha=alpha),
        grid=(nrow,),
        in_specs=[row(D), row(D), _resident((D, D)), _resident((1, D)), _resident((1, D)),
                  _resident((D, LANES)), _resident((1, LANES))],
        out_specs=[row(D), row(LANES)],
        out_shape=[jax.ShapeDtypeStruct((T, D), F32), jax.ShapeDtypeStruct((T, LANES), F32)],
        compiler_params=_params("parallel"),
    )(y, h, lw["w_out"], lw["ln1_g"], lw["ln1_b"], lw["w_route"], lw["b_route"])

    tr = tl["route"]
    meta, wgt, cnt = pl.pallas_call(
        functools.partial(_route_kernel, tr=tr),
        grid=(T // tr,),
        in_specs=[pl.BlockSpec((tr, LANES), lambda i: (i, 0))],
        out_specs=[pl.BlockSpec((tr, LANES), lambda i: (i, 0)), pl.BlockSpec((tr, LANES), lambda i: (i, 0)),
                   pl.BlockSpec((8, LANES), lambda i: (0, 0))],
        out_shape=[jax.ShapeDtypeStruct((T, LANES), I32), jax.ShapeDtypeStruct((T, LANES), F32),
                   jax.ShapeDtypeStruct((8, LANES), F32)],
        scratch_shapes=[pltpu.VMEM((1, LANES), F32)],
        compiler_params=_params("arbitrary"),
    )(logits)

    rb = MOE_ROW_BLOCK
    counts = cnt[0, N_GROUPS:N_GROUPS + N_EXPERTS].astype(I32)
    padded = ((counts + rb - 1) // rb) * rb
    pends = jnp.cumsum(padded)
    pstarts = pends - padded
    n_blocks = -(-(2 * T) // rb) + N_EXPERTS
    P = n_blocks * rb
    dest = (pstarts[meta[:, 0:2]] + meta[:, 2:4]).reshape(-1).astype(I32)
    block_e = jnp.minimum(jnp.searchsorted(pends, jnp.arange(n_blocks, dtype=I32) * rb, side="right"),
                          N_EXPERTS - 1).astype(I32)
    n_used = (pends[-1] // rb).astype(I32).reshape(1)

    tt = tl["moe_tok"]
    xs = pl.pallas_call(
        functools.partial(_dispatch_kernel, tt=tt),
        grid_spec=pltpu.PrefetchScalarGridSpec(
            num_scalar_prefetch=1, grid=(T // tt,),
            in_specs=[pl.BlockSpec((tt, D), lambda i, d: (i, 0)), pl.BlockSpec(memory_space=pl.ANY)],
            out_specs=pl.BlockSpec(memory_space=pl.ANY),
            scratch_shapes=[pltpu.SemaphoreType.DMA(())]),
        out_shape=jax.ShapeDtypeStruct((P, D), F32),
        input_output_aliases={2: 0},
        compiler_params=_params("arbitrary"),
    )(dest, h1, jnp.zeros((P, D), F32))

    F = D_EXPERT
    ys = pl.pallas_call(
        _experts_kernel,
        grid_spec=pltpu.PrefetchScalarGridSpec(
            num_scalar_prefetch=2, grid=(n_blocks,),
            in_specs=[pl.BlockSpec((rb, D), lambda i, be, nu: (i, 0)),
                      pl.BlockSpec((1, 1, D, F), lambda i, be, nu: (layer, be[i], 0, 0)),
                      pl.BlockSpec((1, 1, D, F), lambda i, be, nu: (layer, be[i], 0, 0)),
                      pl.BlockSpec((1, 1, F, D), lambda i, be, nu: (layer, be[i], 0, 0))],
            out_specs=pl.BlockSpec((rb, D), lambda i, be, nu: (i, 0)),
            scratch_shapes=[pltpu.VMEM((D, F), BF16), pltpu.VMEM((D, F), BF16), pltpu.VMEM((F, D), BF16)]),
        out_shape=jax.ShapeDtypeStruct((P, D), F32),
        compiler_params=_params("arbitrary"),
    )(block_e, n_used, xs, lw["w_e_gate"], lw["w_e_up"], lw["w_e_down"])

    tc = tl["comb"]
    h2, h2_bf = pl.pallas_call(
        functools.partial(_combine_kernel, tc=tc, alpha=alpha),
        grid_spec=pltpu.PrefetchScalarGridSpec(
            num_scalar_prefetch=1, grid=(T // tc,),
            in_specs=[pl.BlockSpec((tc, D), lambda i, d: (i, 0)),
                      pl.BlockSpec((tc, LANES), lambda i, d: (i, 0)),
                      pl.BlockSpec((1, D), lambda i, d: (0, 0)),
                      pl.BlockSpec((1, D), lambda i, d: (0, 0)),
                      pl.BlockSpec(memory_space=pl.ANY)],
            out_specs=[pl.BlockSpec((tc, D), lambda i, d: (i, 0)), pl.BlockSpec((tc, D), lambda i, d: (i, 0))],
            scratch_shapes=[pltpu.VMEM((2 * tc, D), F32), pltpu.SemaphoreType.DMA(())]),
        out_shape=[jax.ShapeDtypeStruct((T, D), F32), jax.ShapeDtypeStruct((T, D), BF16)],
        compiler_params=_params("arbitrary"),
    )(dest, h1, wgt, lw["ln2_g"], lw["ln2_b"], ys)
    return h2, h2_bf


def _layer_weights(l, w_in, g_q_lora, w_q_b, g_kv_lora, w_kv_b, w_o_a, w_o_b, w_out, ln1_g, ln1_b,
                   w_group, b_group, w_router, b_router, w_e_gate, w_e_up, w_e_down, ln2_g, ln2_b):
    D = w_in.shape[1]
    wi = w_in[l]
    o_kpe = MLA_Q_LORA + MLA_KV_LORA
    o_dsa = o_kpe + MLA_ROPE_DIM
    hd = DSA_HEADS * DSA_HEAD_DIM
    o_kidx = o_dsa + 3 * hd + IDX_HEADS * IDX_DIM
    o_widx = o_kidx + IDX_DIM
    o_gate = o_widx + IDX_HEADS
    kpe = wi[:, o_kpe:o_dsa]
    kidx = wi[:, o_kidx:o_widx]
    w_mla_in = jnp.concatenate([wi[:, :o_kpe], kpe, kpe], axis=1).astype(BF16)
    w_dsa_in = jnp.concatenate([wi[:, o_dsa:o_kidx], kidx, kidx, wi[:, o_widx:o_gate],
                                jnp.zeros((D, LANES - IDX_HEADS), F32)], axis=1).astype(BF16)
    wq = w_q_b[l].reshape(MLA_Q_LORA, MLA_HEADS, MLA_QK_DIM)
    wq = jnp.concatenate([wq[:, :, :MLA_NOPE_DIM].reshape(MLA_Q_LORA, -1),
                          wq[:, :, MLA_NOPE_DIM:].reshape(MLA_Q_LORA, -1)], axis=1).astype(BF16)
    w_route = jnp.concatenate([w_group[l], w_router[l],
                               jnp.zeros((D, LANES - N_GROUPS - N_EXPERTS), F32)], axis=1)
    b_route = jnp.concatenate([b_group[l], b_router[l],
                               jnp.zeros((LANES - N_GROUPS - N_EXPERTS,), F32)]).reshape(1, LANES)
    return dict(
        w_mla_in=w_mla_in, w_dsa_in=w_dsa_in,
        w_gate_a=wi[:, o_gate:o_gate + D].astype(BF16), w_gate_b=wi[:, o_gate + D:].astype(BF16),
        g_q=g_q_lora[l].reshape(1, -1), g_kv=g_kv_lora[l].reshape(1, -1),
        w_q_b=wq, w_kv_b=w_kv_b[l].astype(BF16),
        w_o_a=w_o_a[l].astype(BF16), w_o_b=w_o_b[l].astype(BF16), w_out=w_out[l].astype(BF16),
        ln1_g=ln1_g[l].reshape(1, -1), ln1_b=ln1_b[l].reshape(1, -1),
        w_route=w_route, b_route=b_route,
        w_e_gate=w_e_gate, w_e_up=w_e_up, w_e_down=w_e_down,
        ln2_g=ln2_g[l].reshape(1, -1), ln2_b=ln2_b[l].reshape(1, -1),
    )


def kernel(x, positions, w_in, g_q_lora, w_q_b, g_kv_lora, w_kv_b, w_o_a, w_o_b, w_out, ln1_g, ln1_b,
           w_group, b_group, w_router, b_router, w_e_gate, w_e_up, w_e_down, ln2_g, ln2_b):
    B, S, D = x.shape
    depth = w_in.shape[0]
    alpha = (2 * depth) ** 0.25
    tabs = (_rope_tables(positions, MLA_ROPE_DIM, MLA_ROPE_DIM),
            _rope_tables(positions, DSA_ROPE_DIM, DSA_HEAD_DIM),
            _rope_tables(positions, IDX_ROPE_DIM, IDX_DIM))
    h = x.reshape(B * S, D)
    h_bf = h.astype(BF16)
    for l in range(depth):
        lw = _layer_weights(l, w_in, g_q_lora, w_q_b, g_kv_lora, w_kv_b, w_o_a, w_o_b, w_out, ln1_g, ln1_b,
                            w_group, b_group, w_router, b_router, w_e_gate, w_e_up, w_e_down, ln2_g, ln2_b)
        h, h_bf = _layer(l, h, h_bf, positions, tabs, lw, alpha, B, S)
    return h.reshape(B, S, D)
```

```python
import functools
import math

import jax
import jax.numpy as jnp
from jax import lax
from jax.experimental import pallas as pl
from jax.experimental.pallas import tpu as pltpu

F32 = jnp.float32
BF16 = jnp.bfloat16
I32 = jnp.int32

MLA_HEADS = 8
MLA_Q_LORA = 512
MLA_KV_LORA = 512
MLA_NOPE_DIM = 128
MLA_ROPE_DIM = 64
MLA_V_DIM = 128
MLA_QK_DIM = MLA_NOPE_DIM + MLA_ROPE_DIM
DSA_HEADS = 8
DSA_HEAD_DIM = 128
DSA_ROPE_DIM = DSA_HEAD_DIM // 4
IDX_HEADS = 16
IDX_DIM = 64
IDX_ROPE_DIM = IDX_DIM // 4
DSA_MAX_TOPK = 256
ROPE_THETA = 500000.0
N_GROUPS = 8
EXPERTS_PER_GROUP = 8
N_EXPERTS = N_GROUPS * EXPERTS_PER_GROUP
D_EXPERT = 512
MOE_ROW_BLOCK = 128
LN_EPS = 1e-5
RMS_EPS = 1e-6

LANES = 128
NEG_BIG = -1e30
INT_MIN = -(2 ** 31)
VMEM_LIMIT = 56 * 1024 * 1024


def _params(*sem):
    return pltpu.CompilerParams(dimension_semantics=sem, vmem_limit_bytes=VMEM_LIMIT)


def _resident(shape):
    nd = len(shape)
    return pl.BlockSpec(shape, lambda *_: (0,) * nd, pipeline_mode=pl.Buffered(1))


def _dot(a, b):
    return jnp.dot(a, b, preferred_element_type=F32)


def _dot_nt(a, b):
    return lax.dot_general(a, b, (((1,), (1,)), ((), ())), preferred_element_type=F32)


def _rope128(x, c, sa, sb, half):
    return x * c + pltpu.roll(x, LANES - half, 1) * sa + pltpu.roll(x, half, 1) * sb


def _rms(x, g):
    return x * lax.rsqrt(jnp.mean(x * x, axis=-1, keepdims=True) + RMS_EPS) * g


def _layer_norm(x, g, b):
    mu = jnp.mean(x, axis=-1, keepdims=True)
    xc = x - mu
    var = jnp.mean(xc * xc, axis=-1, keepdims=True)
    return xc * lax.rsqrt(var + LN_EPS) * g + b


def _mla_prep_kernel(h_ref, win_ref, gq_ref, gkv_ref, wqb_ref, wkvb_ref, c_ref, sa_ref, sb_ref,
                     q_ref, k_ref, v_ref, *, q_scale):
    h = h_ref[...]
    p = _dot(h, win_ref[...])
    qn = _rms(p[:, :MLA_Q_LORA], gq_ref[...]).astype(BF16)
    kvn = _rms(p[:, MLA_Q_LORA:MLA_Q_LORA + MLA_KV_LORA], gkv_ref[...]).astype(BF16)
    q = _dot(qn, wqb_ref[...]) * q_scale
    kv = _dot(kvn, wkvb_ref[...])
    c, sa, sb = c_ref[...], sa_ref[...], sb_ref[...]
    half = MLA_ROPE_DIM // 2
    kpe = _rope128(p[:, MLA_Q_LORA + MLA_KV_LORA:], c, sa, sb, half)[:, :MLA_ROPE_DIM].astype(BF16)
    pe0 = MLA_HEADS * MLA_NOPE_DIM
    for hh in range(MLA_HEADS):
        q_ref[0, hh, :, 0:MLA_NOPE_DIM] = q[:, 128 * hh:128 * hh + 128].astype(BF16)
        k_ref[0, hh, :, 0:MLA_NOPE_DIM] = kv[:, 256 * hh:256 * hh + 128].astype(BF16)
        k_ref[0, hh, :, MLA_NOPE_DIM:MLA_QK_DIM] = kpe
        v_ref[:, 256 * hh:256 * hh + 128] = kv[:, 256 * hh + 128:256 * hh + 256].astype(BF16)
        v_ref[:, 256 * hh + 128:256 * hh + 256] = jnp.ones((kv.shape[0], LANES), BF16)
    for s in range(MLA_HEADS // 2):
        slab = _rope128(q[:, pe0 + 128 * s:pe0 + 128 * s + 128], c, sa, sb, half).astype(BF16)
        q_ref[0, 2 * s, :, MLA_NOPE_DIM:MLA_QK_DIM] = slab[:, :MLA_ROPE_DIM]
        q_ref[0, 2 * s + 1, :, MLA_NOPE_DIM:MLA_QK_DIM] = slab[:, MLA_ROPE_DIM:]


def _dsa_prep_kernel(h_ref, w_ref, cd_ref, sad_ref, sbd_ref, ci_ref, sai_ref, sbi_ref,
                     qb_ref, kb_ref, vb_ref, qi_ref, ki_ref, wi_ref, *, w_idx_scale, q_scale):
    h = h_ref[...]
    hd = DSA_HEADS * DSA_HEAD_DIM
    cd, sad, sbd = cd_ref[...], sad_ref[...], sbd_ref[...]
    ci, sai, sbi = ci_ref[...], sai_ref[...], sbi_ref[...]
    hd_half, hi_half = DSA_ROPE_DIM // 2, IDX_ROPE_DIM // 2
    q = _dot(h, w_ref[:, 0:hd]) * q_scale
    for hh in range(DSA_HEADS):
        qb_ref[:, 128 * hh:128 * hh + 128] = _rope128(q[:, 128 * hh:128 * hh + 128], cd, sad, sbd, hd_half).astype(BF16)
    k = _dot(h, w_ref[:, hd:2 * hd])
    for hh in range(DSA_HEADS):
        kb_ref[:, 128 * hh:128 * hh + 128] = _rope128(k[:, 128 * hh:128 * hh + 128], cd, sad, sbd, hd_half).astype(BF16)
    v = _dot(h, w_ref[:, 2 * hd:3 * hd]).astype(BF16)
    for hh in range(DSA_HEADS):
        vb_ref[:, 256 * hh:256 * hh + 128] = v[:, 128 * hh:128 * hh + 128]
        vb_ref[:, 256 * hh + 128:256 * hh + 256] = jnp.ones((v.shape[0], LANES), BF16)
    qi = _dot(h, w_ref[:, 3 * hd:3 * hd + IDX_HEADS * IDX_DIM])
    for s in range(IDX_HEADS // 2):
        slab = _rope128(qi[:, 128 * s:128 * s + 128], ci, sai, sbi, hi_half).astype(BF16)
        qi_ref[0, 2 * s] = slab[:, :IDX_DIM]
        qi_ref[0, 2 * s + 1] = slab[:, IDX_DIM:]
    o = 3 * hd + IDX_HEADS * IDX_DIM
    last = _dot(h, w_ref[:, o:o + 2 * LANES])
    ki_ref[...] = _rope128(last[:, :LANES], ci, sai, sbi, hi_half)[:, :IDX_DIM].astype(BF16)
    wi_ref[...] = last[:, LANES:LANES + IDX_HEADS] * w_idx_scale


def _flash_update(s, v_ones, m_ref, acc_ref):
    tk = s.shape[1]
    m_prev = m_ref[...]
    m_next = jnp.maximum(m_prev, jnp.max(s, axis=1, keepdims=True))
    p = jnp.exp2(s - jnp.tile(m_next, (1, tk // LANES)))
    alpha = jnp.exp2(m_prev - m_next)
    acc_ref[...] = acc_ref[...] * jnp.tile(alpha, (1, 2)) + _dot(p.astype(BF16), v_ones)
    m_ref[...] = m_next


def _mla_attn_kernel(q_ref, k_ref, v_ref, pq_ref, pk_ref, o_ref, m_sc, acc_sc, *, tile, heads):
    qi = pl.program_id(2)
    m_sc[...] = jnp.full(m_sc.shape, -jnp.inf, F32)
    acc_sc[...] = jnp.zeros(acc_sc.shape, F32)

    def chunk(c, masked):
        ks = pl.ds(pl.multiple_of(c * tile, tile), tile)
        if masked:
            visible = pk_ref[0, c] <= pq_ref[0]
        for hh in range(heads):
            s = _dot_nt(q_ref[0, hh], k_ref[0, hh, ks, :])
            if masked:
                s = jnp.where(visible, s, NEG_BIG)
            _flash_update(s, v_ref[0, ks, 2 * MLA_V_DIM * hh:2 * MLA_V_DIM * (hh + 1)], m_sc.at[hh], acc_sc.at[hh])

    def body(c, carry):
        chunk(c, False)
        return carry

    lax.fori_loop(0, qi, body, 0)
    chunk(qi, True)
    for hh in range(heads):
        o_ref[0, :, MLA_V_DIM * hh:MLA_V_DIM * (hh + 1)] = (
            acc_sc[hh, :, :MLA_V_DIM] / acc_sc[hh, :, MLA_V_DIM:]).astype(o_ref.dtype)


def _idx_select_kernel(q_ref, k_ref, w_ref, pq_ref, pk_ref, o_ref, key_sc, *, tq, tk, nkc, top_k):
    qi = pl.program_id(1)
    nk = (qi * tq + tq + tk - 1) // tk
    w = w_ref[0]
    pq = pq_ref[0]

    def score_chunk(c, carry):
        kc = k_ref[0, pl.ds(pl.multiple_of(c * tk, tk), tk), :]
        acc = jnp.zeros((tq, tk), F32)
        for hh in range(IDX_HEADS):
            acc = acc + w[:, hh:hh + 1] * jnp.maximum(_dot_nt(q_ref[0, hh], kc), 0.0)
        bits = pltpu.bitcast(acc, I32)
        key = jnp.where(bits < 0, bits ^ jnp.int32(0x7FFFFFFF), bits)
        key = jnp.where(acc == 0.0, 0, key)
        key_sc[c] = jnp.where(pk_ref[0, c] <= pq, key, INT_MIN)
        return carry

    lax.fori_loop(0, nk, score_chunk, 0)

    def count(pred):
        def body(c, part):
            m = pred(key_sc[c], c).astype(I32)
            for j in range(tk // LANES):
                part = part + m[:, LANES * j:LANES * j + LANES]
            return part
        part = lax.fori_loop(0, nk, body, jnp.zeros((tq, LANES), I32))
        return jnp.sum(part, axis=1, keepdims=True)

    def bit_step(i, base):
        cand = base ^ (jnp.int32(1) << (31 - i))
        cnt = count(lambda key, c: key >= cand)
        return jnp.where(cnt >= top_k, cand, base)

    thr = lax.fori_loop(0, 32, bit_step, jnp.full((tq, 1), INT_MIN, I32))
    n_gt = count(lambda key, c: key > thr)
    n_eq = count(lambda key, c: key == thr)
    need = top_k - n_gt
    tie = jnp.logical_and(n_eq > need, thr != INT_MIN)

    def key_index(c):
        return c * tk + lax.broadcasted_iota(I32, (tq, tk), 1)

    def tie_search():
        def step(i, p):
            cand = p | (jnp.int32(1) << (14 - i))
            cnt = count(lambda key, c: jnp.logical_and(key == thr, key_index(c) < cand))
            return jnp.where(cnt <= need, cand, p)
        return lax.fori_loop(0, 15, step, jnp.zeros((tq, 1), I32))

    p_cut = lax.cond(jnp.max(tie.astype(I32)) > 0, tie_search,
                     lambda: jnp.full((tq, 1), 2 ** 30, I32))

    def out_chunk(c, carry):
        key = key_sc[c]
        sel = jnp.logical_or(key > thr, jnp.logical_and(key == thr, key_index(c) < p_cut))
        sel = jnp.logical_and(sel, key != INT_MIN)
        o_ref[0, c] = jnp.where(sel, 0.0, NEG_BIG).astype(o_ref.dtype)
        return carry

    lax.fori_loop(0, nk, out_chunk, 0)

    def fill_chunk(c, carry):
        o_ref[0, c] = jnp.full((tq, tk), NEG_BIG, o_ref.dtype)
        return carry

    lax.fori_loop(nk, nkc, fill_chunk, 0)


def _dsa_attn_kernel(q_ref, k_ref, v_ref, b_ref, o_ref, m_sc, acc_sc, *, tq, tk):
    qi = pl.program_id(1)
    nk = (qi * tq + tq + tk - 1) // tk
    m_sc[...] = jnp.full(m_sc.shape, NEG_BIG, F32)
    acc_sc[...] = jnp.zeros(acc_sc.shape, F32)

    def chunk(c, carry):
        bias = b_ref[0, c].astype(F32)
        ks = pl.ds(pl.multiple_of(c * tk, tk), tk)
        for hh in range(DSA_HEADS):
            hs = slice(DSA_HEAD_DIM * hh, DSA_HEAD_DIM * (hh + 1))
            vs = slice(2 * DSA_HEAD_DIM * hh, 2 * DSA_HEAD_DIM * (hh + 1))
            s = _dot_nt(q_ref[0, :, hs], k_ref[0, ks, hs]) + bias
            _flash_update(s, v_ref[0, ks, vs], m_sc.at[hh], acc_sc.at[hh])
        return carry

    lax.fori_loop(0, nk, chunk, 0)
    for hh in range(DSA_HEADS):
        hs = slice(DSA_HEAD_DIM * hh, DSA_HEAD_DIM * (hh + 1))
        o_ref[0, :, hs] = (acc_sc[hh, :, :DSA_HEAD_DIM] / acc_sc[hh, :, DSA_HEAD_DIM:]).astype(o_ref.dtype)


def _merge_kernel(h_ref, oa_ref, ob_ref, wga_ref, wgb_ref, woa_ref, wob_ref, y_ref):
    h = h_ref[...]
    ya = jax.nn.sigmoid(_dot(h, wga_ref[...])) * _dot(oa_ref[...], woa_ref[...])
    yb = jax.nn.sigmoid(_dot(h, wgb_ref[...])) * _dot(ob_ref[...], wob_ref[...])
    y_ref[...] = (ya + yb).astype(y_ref.dtype)


def _out_ln_kernel(y_ref, h_ref, wout_ref, g_ref, b_ref, wr_ref, br_ref, o_ref, lg_ref, *, alpha):
    mix = _dot(y_ref[...], wout_ref[...])
    h1 = _layer_norm(alpha * h_ref[...] + mix, g_ref[...], b_ref[...])
    o_ref[...] = h1
    lg_ref[...] = jnp.dot(h1, wr_ref[...], preferred_element_type=F32,
                          precision=lax.Precision.HIGHEST) + br_ref[...]


def _route_kernel(lg_ref, meta_ref, wgt_ref, cnt_ref, carry_sc, *, tr):
    @pl.when(pl.program_id(0) == 0)
    def _():
        carry_sc[...] = jnp.zeros(carry_sc.shape, F32)

    lg = lg_ref[...]
    lane = lax.broadcasted_iota(I32, (tr, LANES), 1)
    gl = jnp.where(lane < N_GROUPS, lg, -jnp.inf)
    gmax = jnp.max(gl, axis=1, keepdims=True)
    g_idx = jnp.min(jnp.where(gl == gmax, lane, LANES), axis=1, keepdims=True)
    g_p = 1.0 / jnp.sum(jnp.exp(gl - gmax), axis=1, keepdims=True)
    lo = N_GROUPS + EXPERTS_PER_GROUP * g_idx
    in_group = jnp.logical_and(lane >= lo, lane < lo + EXPERTS_PER_GROUP)
    el = jnp.where(in_group, lg, -jnp.inf)
    emax = jnp.max(el, axis=1, keepdims=True)
    e1 = jnp.min(jnp.where(el == emax, lane, LANES), axis=1, keepdims=True)
    den = jnp.sum(jnp.exp(el - emax), axis=1, keepdims=True)
    el2 = jnp.where(lane == e1, -jnp.inf, el)
    emax2 = jnp.max(el2, axis=1, keepdims=True)
    e2 = jnp.min(jnp.where(el2 == emax2, lane, LANES), axis=1, keepdims=True)
    p1 = 1.0 / den
    p2 = jnp.exp(emax2 - emax) / den
    w1 = g_p * (p1 / (p1 + p2))
    w2 = g_p * (p2 / (p1 + p2))

    is1 = lane == e1
    is2 = lane == e2
    onehot = jnp.logical_or(is1, is2).astype(BF16)
    r = lax.broadcasted_iota(I32, (tr, tr), 0)
    cidx = lax.broadcasted_iota(I32, (tr, tr), 1)
    lower = (cidx < r).astype(BF16)
    before = _dot(lower, onehot) + carry_sc[...]
    rank1 = jnp.sum(jnp.where(is1, before, 0.0), axis=1, keepdims=True)
    rank2 = jnp.sum(jnp.where(is2, before, 0.0), axis=1, keepdims=True)
    carry_sc[...] = carry_sc[...] + jnp.sum(onehot.astype(F32), axis=0, keepdims=True)

    meta = jnp.where(lane == 0, e1 - N_GROUPS,
                     jnp.where(lane == 1, e2 - N_GROUPS,
                               jnp.where(lane == 2, rank1.astype(I32),
                                         jnp.where(lane == 3, rank2.astype(I32), 0))))
    meta_ref[...] = meta
    wgt_ref[...] = jnp.where(lane == 0, w1, jnp.where(lane == 1, w2, 0.0))
    cnt_ref[...] = jnp.broadcast_to(carry_sc[...], cnt_ref.shape)


def _row_copy(src, src_row, dst, dst_row, sem):
    return pltpu.make_async_copy(src.at[pl.ds(src_row, 1)], dst.at[pl.ds(dst_row, 1)], sem)


def _dispatch_kernel(dest_ref, h_ref, xs_in, xs_out, sem, *, tt):
    del xs_in
    base = pl.program_id(0) * tt

    def issue(t, carry):
        tok = base + t
        _row_copy(h_ref, t, xs_out, dest_ref[2 * tok], sem).start()
        _row_copy(h_ref, t, xs_out, dest_ref[2 * tok + 1], sem).start()
        return carry

    lax.fori_loop(0, tt, issue, 0, unroll=8)
    for _ in range(2):
        pltpu.make_async_copy(h_ref, xs_out.at[pl.ds(0, tt)], sem).wait()


def _experts_kernel(be_ref, nu_ref, xs_ref, wg_ref, wu_ref, wd_ref, o_ref, wg_sc, wu_sc, wd_sc):
    i = pl.program_id(0)
    e = be_ref[i]
    prev = be_ref[jnp.maximum(i - 1, 0)]

    @pl.when(jnp.logical_or(i == 0, e != prev))
    def _():
        wg_sc[...] = wg_ref[0, 0].astype(BF16)
        wu_sc[...] = wu_ref[0, 0].astype(BF16)
        wd_sc[...] = wd_ref[0, 0].astype(BF16)

    @pl.when(i < nu_ref[0])
    def _():
        x = xs_ref[...].astype(BF16)
        hid = jax.nn.silu(_dot(x, wg_sc[...])) * _dot(x, wu_sc[...])
        o_ref[...] = _dot(hid.astype(BF16), wd_sc[...])

    @pl.when(i >= nu_ref[0])
    def _():
        o_ref[...] = jnp.zeros(o_ref.shape, o_ref.dtype)


def _combine_kernel(dest_ref, h_ref, wgt_ref, g_ref, b_ref, ys_hbm, o_ref, ob_ref, ybuf, sem, *, tc, alpha):
    base = pl.program_id(0) * tc

    def issue(t, carry):
        tok = base + t
        _row_copy(ys_hbm, dest_ref[2 * tok], ybuf, t, sem).start()
        _row_copy(ys_hbm, dest_ref[2 * tok + 1], ybuf, tc + t, sem).start()
        return carry

    lax.fori_loop(0, tc, issue, 0, unroll=8)
    pltpu.make_async_copy(ys_hbm.at[pl.ds(0, 2 * tc)], ybuf, sem).wait()
    wgt = wgt_ref[...]
    ff = ybuf[0:tc, :] * wgt[:, 0:1] + ybuf[tc:2 * tc, :] * wgt[:, 1:2]
    h2 = _layer_norm(alpha * h_ref[...] + ff, g_ref[...], b_ref[...])
    o_ref[...] = h2
    ob_ref[...] = h2.astype(BF16)


def _tiles(seq):
    big = seq >= 2048
    return dict(
        rows=256 if big else 128,
        mla=512 if big else 128,
        mla_heads=4,
        idx_q=256 if big else 64,
        dsa_q=256 if big else 128,
        key=512 if big else 128,
        route=512 if big else 128,
        moe_tok=512 if big else 128,
        comb=128,
    )


def _rope_tables(positions, rot_dim, head_dim):
    half = rot_dim // 2
    inv = ROPE_THETA ** (-jnp.arange(0, rot_dim, 2, dtype=F32) / rot_dim)
    ang = positions.astype(F32).reshape(-1, 1) * inv
    cos, sin = jnp.cos(ang), jnp.sin(ang)
    t = cos.shape[0]
    rest = head_dim - rot_dim
    ones, zeros, zh = jnp.ones((t, rest), F32), jnp.zeros((t, rest), F32), jnp.zeros((t, half), F32)
    rep = LANES // head_dim
    c = jnp.tile(jnp.concatenate([cos, cos, ones], axis=1), (1, rep))
    sa = jnp.tile(jnp.concatenate([-sin, zh, zeros], axis=1), (1, rep))
    sb = jnp.tile(jnp.concatenate([zh, sin, zeros], axis=1), (1, rep))
    return c, sa, sb


def _layer(layer, h, h_bf, positions, tabs, lw, alpha, B, S):
    T, D = h.shape
    tl = _tiles(S)
    tm = tl["rows"]
    nrow = T // tm
    spb = S // tm
    (c_m, sa_m, sb_m), (c_d, sa_d, sb_d), (c_i, sa_i, sb_i) = tabs
    row = lambda w: pl.BlockSpec((tm, w), lambda i: (i, 0))
    head_major = lambda nh, w: pl.BlockSpec((1, nh, tm, w), lambda i: (i // spb, 0, i % spb, 0))

    log2e = math.log2(math.e)
    q_a, k_a, v_a = pl.pallas_call(
        functools.partial(_mla_prep_kernel, q_scale=log2e * MLA_QK_DIM ** -0.5),
        grid=(nrow,),
        in_specs=[row(D), _resident(lw["w_mla_in"].shape), _resident((1, MLA_Q_LORA)), _resident((1, MLA_KV_LORA)),
                  _resident(lw["w_q_b"].shape), _resident(lw["w_kv_b"].shape), row(LANES), row(LANES), row(LANES)],
        out_specs=[head_major(MLA_HEADS, MLA_QK_DIM), head_major(MLA_HEADS, MLA_QK_DIM), row(2 * MLA_HEADS * MLA_V_DIM)],
        out_shape=[jax.ShapeDtypeStruct((B, MLA_HEADS, S, MLA_QK_DIM), BF16),
                   jax.ShapeDtypeStruct((B, MLA_HEADS, S, MLA_QK_DIM), BF16),
                   jax.ShapeDtypeStruct((T, 2 * MLA_HEADS * MLA_V_DIM), BF16)],
        compiler_params=_params("parallel"),
    )(h_bf, lw["w_mla_in"], lw["g_q"], lw["g_kv"], lw["w_q_b"], lw["w_kv_b"], c_m, sa_m, sb_m)

    hd = DSA_HEADS * DSA_HEAD_DIM
    q_b, k_b, v_b, q_idx, k_idx, w_idx = pl.pallas_call(
        functools.partial(_dsa_prep_kernel, w_idx_scale=(IDX_HEADS ** -0.5) * (IDX_DIM ** -0.5),
                          q_scale=log2e * DSA_HEAD_DIM ** -0.5),
        grid=(nrow,),
        in_specs=[row(D), _resident(lw["w_dsa_in"].shape)] + [row(LANES)] * 6,
        out_specs=[row(hd), row(hd), row(2 * hd), head_major(IDX_HEADS, IDX_DIM), row(IDX_DIM), row(IDX_HEADS)],
        out_shape=[jax.ShapeDtypeStruct((T, hd), BF16)] * 2 + [jax.ShapeDtypeStruct((T, 2 * hd), BF16)] + [
            jax.ShapeDtypeStruct((B, IDX_HEADS, S, IDX_DIM), BF16),
            jax.ShapeDtypeStruct((T, IDX_DIM), BF16),
            jax.ShapeDtypeStruct((T, IDX_HEADS), F32)],
        compiler_params=_params("parallel"),
    )(h_bf, lw["w_dsa_in"], c_d, sa_d, sb_d, c_i, sa_i, sb_i)

    ta = tl["mla"]
    pos_q = positions.reshape(B, S, 1)
    pos_k_mla = positions.reshape(B, S // ta, 1, ta)
    hp = tl["mla_heads"]
    o_a = pl.pallas_call(
        functools.partial(_mla_attn_kernel, tile=ta, heads=hp),
        grid=(B, MLA_HEADS // hp, S // ta),
        in_specs=[pl.BlockSpec((1, hp, ta, MLA_QK_DIM), lambda b, g, i: (b, g, i, 0)),
                  pl.BlockSpec((1, hp, S, MLA_QK_DIM), lambda b, g, i: (b, g, 0, 0)),
                  pl.BlockSpec((1, S, 2 * MLA_V_DIM * hp), lambda b, g, i: (b, 0, g)),
                  pl.BlockSpec((1, ta, 1), lambda b, g, i: (b, i, 0)),
                  pl.BlockSpec((1, S // ta, 1, ta), lambda b, g, i: (b, 0, 0, 0))],
        out_specs=pl.BlockSpec((1, ta, MLA_V_DIM * hp), lambda b, g, i: (b, i, g)),
        out_shape=jax.ShapeDtypeStruct((B, S, MLA_HEADS * MLA_V_DIM), BF16),
        scratch_shapes=[pltpu.VMEM((hp, ta, LANES), F32), pltpu.VMEM((hp, ta, 2 * MLA_V_DIM), F32)],
        compiler_params=_params("parallel", "parallel", "arbitrary"),
    )(q_a, k_a, v_a.reshape(B, S, -1), pos_q, pos_k_mla)

    tk = tl["key"]
    nkc = S // tk
    tqi = tl["idx_q"]
    top_k = min(DSA_MAX_TOPK, S // 4)
    pos_k = positions.reshape(B, nkc, 1, tk)
    sel_bias = pl.pallas_call(
        functools.partial(_idx_select_kernel, tq=tqi, tk=tk, nkc=nkc, top_k=top_k),
        grid=(B, S // tqi),
        in_specs=[pl.BlockSpec((1, IDX_HEADS, tqi, IDX_DIM), lambda b, i: (b, 0, i, 0)),
                  pl.BlockSpec((1, S, IDX_DIM), lambda b, i: (b, 0, 0)),
                  pl.BlockSpec((1, tqi, IDX_HEADS), lambda b, i: (b, i, 0)),
                  pl.BlockSpec((1, tqi, 1), lambda b, i: (b, i, 0)),
                  pl.BlockSpec((1, nkc, 1, tk), lambda b, i: (b, 0, 0, 0))],
        out_specs=pl.BlockSpec((1, nkc, tqi, tk), lambda b, i: (b, 0, i, 0)),
        out_shape=jax.ShapeDtypeStruct((B, nkc, S, tk), BF16),
        scratch_shapes=[pltpu.VMEM((nkc, tqi, tk), I32)],
        compiler_params=_params("parallel", "arbitrary"),
    )(q_idx, k_idx.reshape(B, S, IDX_DIM), w_idx.reshape(B, S, IDX_HEADS), pos_q, pos_k)

    tqd = tl["dsa_q"]
    o_b = pl.pallas_call(
        functools.partial(_dsa_attn_kernel, tq=tqd, tk=tk),
        grid=(B, S // tqd),
        in_specs=[pl.BlockSpec((1, tqd, hd), lambda b, i: (b, i, 0)),
                  pl.BlockSpec((1, S, hd), lambda b, i: (b, 0, 0), pipeline_mode=pl.Buffered(1)),
                  pl.BlockSpec((1, S, 2 * hd), lambda b, i: (b, 0, 0), pipeline_mode=pl.Buffered(1)),
                  pl.BlockSpec((1, nkc, tqd, tk), lambda b, i: (b, 0, i, 0))],
        out_specs=pl.BlockSpec((1, tqd, hd), lambda b, i: (b, i, 0)),
        out_shape=jax.ShapeDtypeStruct((B, S, hd), BF16),
        scratch_shapes=[pltpu.VMEM((DSA_HEADS, tqd, LANES), F32), pltpu.VMEM((DSA_HEADS, tqd, 2 * DSA_HEAD_DIM), F32)],
        compiler_params=_params("parallel", "arbitrary"),
    )(q_b.reshape(B, S, hd), k_b.reshape(B, S, hd), v_b.reshape(B, S, 2 * hd), sel_bias)

    y = pl.pallas_call(
        _merge_kernel,
        grid=(nrow,),
        in_specs=[row(D), row(MLA_HEADS * MLA_V_DIM), row(hd),
                  _resident((D, D)), _resident((D, D)), _resident((MLA_HEADS * MLA_V_DIM, D)), _resident((hd, D))],
        out_specs=row(D),
        out_shape=jax.ShapeDtypeStruct((T, D), BF16),
        compiler_params=_params("parallel"),
    )(h_bf, o_a.reshape(T, -1), o_b.reshape(T, hd), lw["w_gate_a"], lw["w_gate_b"], lw["w_o_a"], lw["w_o_b"])

    h1, logits = pl.pallas_call(
        functools.partial(_out_ln_kernel, alpha=alpha),
        grid=(nrow,),
        in_specs=[row(D), row(D), _resident((D, D)), _resident((1, D)), _resident((1, D)),
                  _resident((D, LANES)), _resident((1, LANES))],
        out_specs=[row(D), row(LANES)],
        out_shape=[jax.ShapeDtypeStruct((T, D), F32), jax.ShapeDtypeStruct((T, LANES), F32)],
        compiler_params=_params("parallel"),
    )(y, h, lw["w_out"], lw["ln1_g"], lw["ln1_b"], lw["w_route"], lw["b_route"])

    tr = tl["route"]
    meta, wgt, cnt = pl.pallas_call(
        functools.partial(_route_kernel, tr=tr),
        grid=(T // tr,),
        in_specs=[pl.BlockSpec((tr, LANES), lambda i: (i, 0))],
        out_specs=[pl.BlockSpec((tr, LANES), lambda i: (i, 0)), pl.BlockSpec((tr, LANES), lambda i: (i, 0)),
                   pl.BlockSpec((8, LANES), lambda i: (0, 0))],
        out_shape=[jax.ShapeDtypeStruct((T, LANES), I32), jax.ShapeDtypeStruct((T, LANES), F32),
                   jax.ShapeDtypeStruct((8, LANES), F32)],
        scratch_shapes=[pltpu.VMEM((1, LANES), F32)],
        compiler_params=_params("arbitrary"),
    )(logits)

    rb = MOE_ROW_BLOCK
    counts = cnt[0, N_GROUPS:N_GROUPS + N_EXPERTS].astype(I32)
    padded = ((counts + rb - 1) // rb) * rb
    pends = jnp.cumsum(padded)
    pstarts = pends - padded
    n_blocks = -(-(2 * T) // rb) + N_EXPERTS
    P = n_blocks * rb
    dest = (pstarts[meta[:, 0:2]] + meta[:, 2:4]).reshape(-1).astype(I32)
    block_e = jnp.minimum(jnp.searchsorted(pends, jnp.arange(n_blocks, dtype=I32) * rb, side="right"),
                          N_EXPERTS - 1).astype(I32)
    n_used = (pends[-1] // rb).astype(I32).reshape(1)

    tt = tl["moe_tok"]
    xs = pl.pallas_call(
        functools.partial(_dispatch_kernel, tt=tt),
        grid_spec=pltpu.PrefetchScalarGridSpec(
            num_scalar_prefetch=1, grid=(T // tt,),
            in_specs=[pl.BlockSpec((tt, D), lambda i, d: (i, 0)), pl.BlockSpec(memory_space=pl.ANY)],
            out_specs=pl.BlockSpec(memory_space=pl.ANY),
            scratch_shapes=[pltpu.SemaphoreType.DMA(())]),
        out_shape=jax.ShapeDtypeStruct((P, D), F32),
        input_output_aliases={2: 0},
        compiler_params=_params("arbitrary"),
    )(dest, h1, jnp.zeros((P, D), F32))

    F = D_EXPERT
    ys = pl.pallas_call(
        _experts_kernel,
        grid_spec=pltpu.PrefetchScalarGridSpec(
            num_scalar_prefetch=2, grid=(n_blocks,),
            in_specs=[pl.BlockSpec((rb, D), lambda i, be, nu: (i, 0)),
                      pl.BlockSpec((1, 1, D, F), lambda i, be, nu: (layer, be[i], 0, 0)),
                      pl.BlockSpec((1, 1, D, F), lambda i, be, nu: (layer, be[i], 0, 0)),
                      pl.BlockSpec((1, 1, F, D), lambda i, be, nu: (layer, be[i], 0, 0))],
            out_specs=pl.BlockSpec((rb, D), lambda i, be, nu: (i, 0)),
            scratch_shapes=[pltpu.VMEM((D, F), BF16), pltpu.VMEM((D, F), BF16), pltpu.VMEM((F, D), BF16)]),
        out_shape=jax.ShapeDtypeStruct((P, D), F32),
        compiler_params=_params("arbitrary"),
    )(block_e, n_used, xs, lw["w_e_gate"], lw["w_e_up"], lw["w_e_down"])

    tc = tl["comb"]
    h2, h2_bf = pl.pallas_call(
        functools.partial(_combine_kernel, tc=tc, alpha=alpha),
        grid_spec=pltpu.PrefetchScalarGridSpec(
            num_scalar_prefetch=1, grid=(T // tc,),
            in_specs=[pl.BlockSpec((tc, D), lambda i, d: (i, 0)),
                      pl.BlockSpec((tc, LANES), lambda i, d: (i, 0)),
                      pl.BlockSpec((1, D), lambda i, d: (0, 0)),
                      pl.BlockSpec((1, D), lambda i, d: (0, 0)),
                      pl.BlockSpec(memory_space=pl.ANY)],
            out_specs=[pl.BlockSpec((tc, D), lambda i, d: (i, 0)), pl.BlockSpec((tc, D), lambda i, d: (i, 0))],
            scratch_shapes=[pltpu.VMEM((2 * tc, D), F32), pltpu.SemaphoreType.DMA(())]),
        out_shape=[jax.ShapeDtypeStruct((T, D), F32), jax.ShapeDtypeStruct((T, D), BF16)],
        compiler_params=_params("arbitrary"),
    )(dest, h1, wgt, lw["ln2_g"], lw["ln2_b"], ys)
    return h2, h2_bf


def _layer_weights(l, w_in, g_q_lora, w_q_b, g_kv_lora, w_kv_b, w_o_a, w_o_b, w_out, ln1_g, ln1_b,
                   w_group, b_group, w_router, b_router, w_e_gate, w_e_up, w_e_down, ln2_g, ln2_b):
    D = w_in.shape[1]
    wi = w_in[l]
    o_kpe = MLA_Q_LORA + MLA_KV_LORA
    o_dsa = o_kpe + MLA_ROPE_DIM
    hd = DSA_HEADS * DSA_HEAD_DIM
    o_kidx = o_dsa + 3 * hd + IDX_HEADS * IDX_DIM
    o_widx = o_kidx + IDX_DIM
    o_gate = o_widx + IDX_HEADS
    kpe = wi[:, o_kpe:o_dsa]
    kidx = wi[:, o_kidx:o_widx]
    w_mla_in = jnp.concatenate([wi[:, :o_kpe], kpe, kpe], axis=1).astype(BF16)
    w_dsa_in = jnp.concatenate([wi[:, o_dsa:o_kidx], kidx, kidx, wi[:, o_widx:o_gate],
                                jnp.zeros((D, LANES - IDX_HEADS), F32)], axis=1).astype(BF16)
    wq = w_q_b[l].reshape(MLA_Q_LORA, MLA_HEADS, MLA_QK_DIM)
    wq = jnp.concatenate([wq[:, :, :MLA_NOPE_DIM].reshape(MLA_Q_LORA, -1),
                          wq[:, :, MLA_NOPE_DIM:].reshape(MLA_Q_LORA, -1)], axis=1).astype(BF16)
    w_route = jnp.concatenate([w_group[l], w_router[l],
                               jnp.zeros((D, LANES - N_GROUPS - N_EXPERTS), F32)], axis=1)
    b_route = jnp.concatenate([b_group[l], b_router[l],
                               jnp.zeros((LANES - N_GROUPS - N_EXPERTS,), F32)]).reshape(1, LANES)
    return dict(
        w_mla_in=w_mla_in, w_dsa_in=w_dsa_in,
        w_gate_a=wi[:, o_gate:o_gate + D].astype(BF16), w_gate_b=wi[:, o_gate + D:].astype(BF16),
        g_q=g_q_lora[l].reshape(1, -1), g_kv=g_kv_lora[l].reshape(1, -1),
        w_q_b=wq, w_kv_b=w_kv_b[l].astype(BF16),
        w_o_a=w_o_a[l].astype(BF16), w_o_b=w_o_b[l].astype(BF16), w_out=w_out[l].astype(BF16),
        ln1_g=ln1_g[l].reshape(1, -1), ln1_b=ln1_b[l].reshape(1, -1),
        w_route=w_route, b_route=b_route,
        w_e_gate=w_e_gate, w_e_up=w_e_up, w_e_down=w_e_down,
        ln2_g=ln2_g[l].reshape(1, -1), ln2_b=ln2_b[l].reshape(1, -1),
    )


def kernel(x, positions, w_in, g_q_lora, w_q_b, g_kv_lora, w_kv_b, w_o_a, w_o_b, w_out, ln1_g, ln1_b,
           w_group, b_group, w_router, b_router, w_e_gate, w_e_up, w_e_down, ln2_g, ln2_b):
    B, S, D = x.shape
    depth = w_in.shape[0]
    alpha = (2 * depth) ** 0.25
    tabs = (_rope_tables(positions, MLA_ROPE_DIM, MLA_ROPE_DIM),
            _rope_tables(positions, DSA_ROPE_DIM, DSA_HEAD_DIM),
            _rope_tables(positions, IDX_ROPE_DIM, IDX_DIM))
    h = x.reshape(B * S, D)
    h_bf = h.astype(BF16)
    for l in range(depth):
        lw = _layer_weights(l, w_in, g_q_lora, w_q_b, g_kv_lora, w_kv_b, w_o_a, w_o_b, w_out, ln1_g, ln1_b,
                            w_group, b_group, w_router, b_router, w_e_gate, w_e_up, w_e_down, ln2_g, ln2_b)
        h, h_bf = _layer(l, h, h_bf, positions, tabs, lw, alpha, B, S)
    return h.reshape(B, S, D)
```

```python
import functools
import math

import jax
import jax.numpy as jnp
from jax import lax
from jax.experimental import pallas as pl
from jax.experimental.pallas import tpu as pltpu

F32 = jnp.float32
BF16 = jnp.bfloat16
I32 = jnp.int32

MLA_HEADS = 8
MLA_Q_LORA = 512
MLA_KV_LORA = 512
MLA_NOPE_DIM = 128
MLA_ROPE_DIM = 64
MLA_V_DIM = 128
MLA_QK_DIM = MLA_NOPE_DIM + MLA_ROPE_DIM
DSA_HEADS = 8
DSA_HEAD_DIM = 128
DSA_ROPE_DIM = DSA_HEAD_DIM // 4
IDX_HEADS = 16
IDX_DIM = 64
IDX_ROPE_DIM = IDX_DIM // 4
DSA_MAX_TOPK = 256
ROPE_THETA = 500000.0
N_GROUPS = 8
EXPERTS_PER_GROUP = 8
N_EXPERTS = N_GROUPS * EXPERTS_PER_GROUP
D_EXPERT = 512
MOE_ROW_BLOCK = 128
LN_EPS = 1e-5
RMS_EPS = 1e-6

LANES = 128
NEG_BIG = -1e30
INT_MIN = -(2 ** 31)
VMEM_LIMIT = 56 * 1024 * 1024


def _params(*sem):
    return pltpu.CompilerParams(dimension_semantics=sem, vmem_limit_bytes=VMEM_LIMIT)


def _resident(shape):
    nd = len(shape)
    return pl.BlockSpec(shape, lambda *_: (0,) * nd, pipeline_mode=pl.Buffered(1))


def _dot(a, b):
    return jnp.dot(a, b, preferred_element_type=F32)


def _dot_nt(a, b):
    return lax.dot_general(a, b, (((1,), (1,)), ((), ())), preferred_element_type=F32)


def _rope128(x, c, sa, sb, half):
    return x * c + pltpu.roll(x, LANES - half, 1) * sa + pltpu.roll(x, half, 1) * sb


def _rms(x, g):
    return x * lax.rsqrt(jnp.mean(x * x, axis=-1, keepdims=True) + RMS_EPS) * g


def _layer_norm(x, g, b):
    mu = jnp.mean(x, axis=-1, keepdims=True)
    xc = x - mu
    var = jnp.mean(xc * xc, axis=-1, keepdims=True)
    return xc * lax.rsqrt(var + LN_EPS) * g + b


def _mla_prep_kernel(h_ref, win_ref, gq_ref, gkv_ref, wqb_ref, wkvb_ref, c_ref, sa_ref, sb_ref,
                     q_ref, k_ref, v_ref, *, q_scale):
    h = h_ref[...]
    p = _dot(h, win_ref[...])
    qn = _rms(p[:, :MLA_Q_LORA], gq_ref[...]).astype(BF16)
    kvn = _rms(p[:, MLA_Q_LORA:MLA_Q_LORA + MLA_KV_LORA], gkv_ref[...]).astype(BF16)
    q = _dot(qn, wqb_ref[...]) * q_scale
    kv = _dot(kvn, wkvb_ref[...])
    c, sa, sb = c_ref[...], sa_ref[...], sb_ref[...]
    half = MLA_ROPE_DIM // 2
    kpe = _rope128(p[:, MLA_Q_LORA + MLA_KV_LORA:], c, sa, sb, half)[:, :MLA_ROPE_DIM].astype(BF16)
    pe0 = MLA_HEADS * MLA_NOPE_DIM
    for hh in range(MLA_HEADS):
        q_ref[0, hh, :, 0:MLA_NOPE_DIM] = q[:, 128 * hh:128 * hh + 128].astype(BF16)
        k_ref[0, hh, :, 0:MLA_NOPE_DIM] = kv[:, 256 * hh:256 * hh + 128].astype(BF16)
        k_ref[0, hh, :, MLA_NOPE_DIM:MLA_QK_DIM] = kpe
        v_ref[:, 256 * hh:256 * hh + 128] = kv[:, 256 * hh + 128:256 * hh + 256].astype(BF16)
        v_ref[:, 256 * hh + 128:256 * hh + 256] = jnp.ones((kv.shape[0], LANES), BF16)
    for s in range(MLA_HEADS // 2):
        slab = _rope128(q[:, pe0 + 128 * s:pe0 + 128 * s + 128], c, sa, sb, half).astype(BF16)
        q_ref[0, 2 * s, :, MLA_NOPE_DIM:MLA_QK_DIM] = slab[:, :MLA_ROPE_DIM]
        q_ref[0, 2 * s + 1, :, MLA_NOPE_DIM:MLA_QK_DIM] = slab[:, MLA_ROPE_DIM:]


def _dsa_prep_kernel(h_ref, w_ref, cd_ref, sad_ref, sbd_ref, ci_ref, sai_ref, sbi_ref,
                     qb_ref, kb_ref, vb_ref, qi_ref, ki_ref, wi_ref, *, w_idx_scale, q_scale):
    h = h_ref[...]
    hd = DSA_HEADS * DSA_HEAD_DIM
    cd, sad, sbd = cd_ref[...], sad_ref[...], sbd_ref[...]
    ci, sai, sbi = ci_ref[...], sai_ref[...], sbi_ref[...]
    hd_half, hi_half = DSA_ROPE_DIM // 2, IDX_ROPE_DIM // 2
    q = _dot(h, w_ref[:, 0:hd]) * q_scale
    for hh in range(DSA_HEADS):
        qb_ref[:, 128 * hh:128 * hh + 128] = _rope128(q[:, 128 * hh:128 * hh + 128], cd, sad, sbd, hd_half).astype(BF16)
    k = _dot(h, w_ref[:, hd:2 * hd])
    for hh in range(DSA_HEADS):
        kb_ref[:, 128 * hh:128 * hh + 128] = _rope128(k[:, 128 * hh:128 * hh + 128], cd, sad, sbd, hd_half).astype(BF16)
    v = _dot(h, w_ref[:, 2 * hd:3 * hd]).astype(BF16)
    for hh in range(DSA_HEADS):
        vb_ref[:, 256 * hh:256 * hh + 128] = v[:, 128 * hh:128 * hh + 128]
        vb_ref[:, 256 * hh + 128:256 * hh + 256] = jnp.ones((v.shape[0], LANES), BF16)
    qi = _dot(h, w_ref[:, 3 * hd:3 * hd + IDX_HEADS * IDX_DIM])
    for s in range(IDX_HEADS // 2):
        slab = _rope128(qi[:, 128 * s:128 * s + 128], ci, sai, sbi, hi_half).astype(BF16)
        qi_ref[0, 2 * s] = slab[:, :IDX_DIM]
        qi_ref[0, 2 * s + 1] = slab[:, IDX_DIM:]
    o = 3 * hd + IDX_HEADS * IDX_DIM
    last = _dot(h, w_ref[:, o:o + 2 * LANES])
    ki_ref[...] = _rope128(last[:, :LANES], ci, sai, sbi, hi_half)[:, :IDX_DIM].astype(BF16)
    wi_ref[0] = (last[:, LANES:] * w_idx_scale).T[:IDX_HEADS, :]


def _flash_update(s, v_ones, m_ref, acc_ref):
    tk = s.shape[1]
    m_prev = m_ref[...]
    m_next = jnp.maximum(m_prev, jnp.max(s, axis=1, keepdims=True))
    p = jnp.exp2(s - jnp.tile(m_next, (1, tk // LANES)))
    alpha = jnp.exp2(m_prev - m_next)
    acc_ref[...] = acc_ref[...] * jnp.tile(alpha, (1, 2)) + _dot(p.astype(BF16), v_ones)
    m_ref[...] = m_next


def _mla_attn_kernel(q_ref, k_ref, v_ref, pq_ref, pk_ref, o_ref, m_sc, acc_sc, *, tile, heads):
    qi = pl.program_id(2)
    m_sc[...] = jnp.full(m_sc.shape, -jnp.inf, F32)
    acc_sc[...] = jnp.zeros(acc_sc.shape, F32)

    def chunk(c, masked):
        ks = pl.ds(pl.multiple_of(c * tile, tile), tile)
        if masked:
            visible = pk_ref[0, c] <= pq_ref[0]
        for hh in range(heads):
            s = _dot_nt(q_ref[0, hh], k_ref[0, hh, ks, :])
            if masked:
                s = jnp.where(visible, s, NEG_BIG)
            _flash_update(s, v_ref[0, ks, 2 * MLA_V_DIM * hh:2 * MLA_V_DIM * (hh + 1)], m_sc.at[hh], acc_sc.at[hh])

    def body(c, carry):
        chunk(c, False)
        return carry

    lax.fori_loop(0, qi, body, 0)
    chunk(qi, True)
    for hh in range(heads):
        o_ref[0, :, MLA_V_DIM * hh:MLA_V_DIM * (hh + 1)] = (
            acc_sc[hh, :, :MLA_V_DIM] / acc_sc[hh, :, MLA_V_DIM:]).astype(o_ref.dtype)


def _idx_select_kernel(q_ref, k_ref, wt_ref, pq_ref, pk_ref, o_ref, key_sc, *, tq, tk, nkc, top_k):
    qi = pl.program_id(1)
    nk = (qi * tq + tq + tk - 1) // tk
    wt = wt_ref[0]
    pq = pq_ref[0]

    def score_chunk(c, carry):
        kc = k_ref[0, pl.ds(pl.multiple_of(c * tk, tk), tk), :]
        acc = jnp.zeros((tk, tq), F32)
        for hh in range(IDX_HEADS):
            acc = acc + wt[hh:hh + 1, :] * jnp.maximum(_dot_nt(kc, q_ref[0, hh]), 0.0)
        bits = pltpu.bitcast(acc, I32)
        key = jnp.where(bits < 0, bits ^ jnp.int32(0x7FFFFFFF), bits)
        key = jnp.where(acc == 0.0, 0, key)
        key_sc[c] = jnp.where(pk_ref[0, c] <= pq, key, INT_MIN)
        return carry

    lax.fori_loop(0, nk, score_chunk, 0)

    def count(pred):
        def body(c, part):
            m = pred(key_sc[c], c).astype(I32)
            return part + jnp.sum(m.reshape(tk // 8, 8, tq), axis=0)
        part = lax.fori_loop(0, nk, body, jnp.zeros((8, tq), I32))
        return jnp.sum(part, axis=0, keepdims=True)

    def bit_step(i, base):
        cand = base ^ (jnp.int32(1) << (31 - i))
        cnt = count(lambda key, c: key >= cand)
        return jnp.where(cnt >= top_k, cand, base)

    thr = lax.fori_loop(0, 32, bit_step, jnp.full((1, tq), INT_MIN, I32))
    n_gt = count(lambda key, c: key > thr)
    n_eq = count(lambda key, c: key == thr)
    need = top_k - n_gt
    tie = jnp.logical_and(n_eq > need, thr != INT_MIN)

    def key_index(c):
        return c * tk + lax.broadcasted_iota(I32, (tk, tq), 0)

    def tie_search():
        def step(i, p):
            cand = p | (jnp.int32(1) << (14 - i))
            cnt = count(lambda key, c: jnp.logical_and(key == thr, key_index(c) < cand))
            return jnp.where(cnt <= need, cand, p)
        return lax.fori_loop(0, 15, step, jnp.zeros((1, tq), I32))

    p_cut = lax.cond(jnp.max(tie.astype(I32)) > 0, tie_search,
                     lambda: jnp.full((1, tq), 2 ** 30, I32))

    def out_chunk(c, carry):
        key = key_sc[c]
        sel = jnp.logical_or(key > thr, jnp.logical_and(key == thr, key_index(c) < p_cut))
        sel = jnp.logical_and(sel, key != INT_MIN)
        o_ref[0, c] = jnp.where(sel, 0.0, NEG_BIG).T.astype(o_ref.dtype)
        return carry

    lax.fori_loop(0, nk, out_chunk, 0)

    def fill_chunk(c, carry):
        o_ref[0, c] = jnp.full((tq, tk), NEG_BIG, o_ref.dtype)
        return carry

    lax.fori_loop(nk, nkc, fill_chunk, 0)


def _dsa_attn_kernel(q_ref, k_ref, v_ref, b_ref, o_ref, m_sc, acc_sc, *, tq, tk):
    qi = pl.program_id(1)
    nk = (qi * tq + tq + tk - 1) // tk
    m_sc[...] = jnp.full(m_sc.shape, NEG_BIG, F32)
    acc_sc[...] = jnp.zeros(acc_sc.shape, F32)

    def chunk(c, carry):
        bias = b_ref[0, c].astype(F32)
        ks = pl.ds(pl.multiple_of(c * tk, tk), tk)
        for hh in range(DSA_HEADS):
            hs = slice(DSA_HEAD_DIM * hh, DSA_HEAD_DIM * (hh + 1))
            vs = slice(2 * DSA_HEAD_DIM * hh, 2 * DSA_HEAD_DIM * (hh + 1))
            s = _dot_nt(q_ref[0, :, hs], k_ref[0, ks, hs]) + bias
            _flash_update(s, v_ref[0, ks, vs], m_sc.at[hh], acc_sc.at[hh])
        return carry

    lax.fori_loop(0, nk, chunk, 0)
    for hh in range(DSA_HEADS):
        hs = slice(DSA_HEAD_DIM * hh, DSA_HEAD_DIM * (hh + 1))
        o_ref[0, :, hs] = (acc_sc[hh, :, :DSA_HEAD_DIM] / acc_sc[hh, :, DSA_HEAD_DIM:]).astype(o_ref.dtype)


def _merge_kernel(h_ref, oa_ref, ob_ref, wga_ref, wgb_ref, woa_ref, wob_ref, y_ref):
    h = h_ref[...]
    ya = jax.nn.sigmoid(_dot(h, wga_ref[...])) * _dot(oa_ref[...], woa_ref[...])
    yb = jax.nn.sigmoid(_dot(h, wgb_ref[...])) * _dot(ob_ref[...], wob_ref[...])
    y_ref[...] = (ya + yb).astype(y_ref.dtype)


def _out_ln_kernel(y_ref, h_ref, wout_ref, g_ref, b_ref, wr_ref, br_ref, o_ref, lg_ref, *, alpha):
    sub = 128
    for r in range(y_ref.shape[0] // sub):
        rows = slice(r * sub, (r + 1) * sub)
        mix = _dot(y_ref[rows, :], wout_ref[...])
        h1 = _layer_norm(alpha * h_ref[rows, :] + mix, g_ref[...], b_ref[...])
        o_ref[rows, :] = h1
        lg_ref[rows, :] = jnp.dot(h1, wr_ref[...], preferred_element_type=F32,
                                  precision=lax.Precision.HIGHEST) + br_ref[...]


def _route_kernel(lg_ref, meta_ref, wgt_ref, cnt_ref, carry_sc, *, tr):
    @pl.when(pl.program_id(0) == 0)
    def _():
        carry_sc[...] = jnp.zeros(carry_sc.shape, F32)

    lg = lg_ref[...]
    lane = lax.broadcasted_iota(I32, (tr, LANES), 1)
    gl = jnp.where(lane < N_GROUPS, lg, -jnp.inf)
    gmax = jnp.max(gl, axis=1, keepdims=True)
    g_idx = jnp.min(jnp.where(gl == gmax, lane, LANES), axis=1, keepdims=True)
    g_p = 1.0 / jnp.sum(jnp.exp(gl - gmax), axis=1, keepdims=True)
    lo = N_GROUPS + EXPERTS_PER_GROUP * g_idx
    in_group = jnp.logical_and(lane >= lo, lane < lo + EXPERTS_PER_GROUP)
    el = jnp.where(in_group, lg, -jnp.inf)
    emax = jnp.max(el, axis=1, keepdims=True)
    e1 = jnp.min(jnp.where(el == emax, lane, LANES), axis=1, keepdims=True)
    den = jnp.sum(jnp.exp(el - emax), axis=1, keepdims=True)
    el2 = jnp.where(lane == e1, -jnp.inf, el)
    emax2 = jnp.max(el2, axis=1, keepdims=True)
    e2 = jnp.min(jnp.where(el2 == emax2, lane, LANES), axis=1, keepdims=True)
    p1 = 1.0 / den
    p2 = jnp.exp(emax2 - emax) / den
    w1 = g_p * (p1 / (p1 + p2))
    w2 = g_p * (p2 / (p1 + p2))

    is1 = lane == e1
    is2 = lane == e2
    onehot = jnp.logical_or(is1, is2).astype(BF16)
    r = lax.broadcasted_iota(I32, (tr, tr), 0)
    cidx = lax.broadcasted_iota(I32, (tr, tr), 1)
    lower = (cidx < r).astype(BF16)
    before = _dot(lower, onehot) + carry_sc[...]
    rank1 = jnp.sum(jnp.where(is1, before, 0.0), axis=1, keepdims=True)
    rank2 = jnp.sum(jnp.where(is2, before, 0.0), axis=1, keepdims=True)
    carry_sc[...] = carry_sc[...] + jnp.sum(onehot.astype(F32), axis=0, keepdims=True)

    meta = jnp.where(lane == 0, e1 - N_GROUPS,
                     jnp.where(lane == 1, e2 - N_GROUPS,
                               jnp.where(lane == 2, rank1.astype(I32),
                                         jnp.where(lane == 3, rank2.astype(I32), 0))))
    meta_ref[...] = meta
    wgt_ref[...] = jnp.where(lane == 0, w1, jnp.where(lane == 1, w2, 0.0))
    cnt_ref[...] = jnp.broadcast_to(carry_sc[...], cnt_ref.shape)


def _row_copy(src, src_row, dst, dst_row, sem):
    return pltpu.make_async_copy(src.at[pl.ds(src_row, 1)], dst.at[pl.ds(dst_row, 1)], sem)


def _dispatch_kernel(dest_ref, h_ref, xs_in, xs_out, sem, *, tt):
    del xs_in
    base = pl.program_id(0) * tt

    def issue(t, carry):
        tok = base + t
        _row_copy(h_ref, t, xs_out, dest_ref[2 * tok], sem).start()
        _row_copy(h_ref, t, xs_out, dest_ref[2 * tok + 1], sem).start()
        return carry

    lax.fori_loop(0, tt, issue, 0, unroll=8)
    for _ in range(2):
        pltpu.make_async_copy(h_ref, xs_out.at[pl.ds(0, tt)], sem).wait()


def _experts_kernel(be_ref, grp_ref, nxt_ref, nu_ref, xs_ref, wg_hbm, wu_hbm, wd_hbm, o_ref,
                    wg_buf, wu_buf, wd_buf, wg_sc, wu_sc, wd_sc, sem, *, layer):
    i = pl.program_id(0)
    e = be_ref[i]
    used = i < nu_ref[0]
    first = jnp.logical_and(used, jnp.logical_or(i == 0, e != be_ref[jnp.maximum(i - 1, 0)]))
    slot = grp_ref[i] % 2

    def copies(expert, s):
        return (pltpu.make_async_copy(wg_hbm.at[layer, expert], wg_buf.at[s], sem.at[s, 0]),
                pltpu.make_async_copy(wu_hbm.at[layer, expert], wu_buf.at[s], sem.at[s, 1]),
                pltpu.make_async_copy(wd_hbm.at[layer, expert], wd_buf.at[s], sem.at[s, 2]))

    @pl.when(i == 0)
    def _():
        for cp in copies(e, slot):
            cp.start()

    @pl.when(first)
    def _():
        for cp in copies(e, slot):
            cp.wait()
        nxt = nxt_ref[i]

        @pl.when(nxt >= 0)
        def _():
            for cp in copies(nxt, 1 - slot):
                cp.start()

        wg_sc[...] = wg_buf[slot].astype(BF16)
        wu_sc[...] = wu_buf[slot].astype(BF16)
        wd_sc[...] = wd_buf[slot].astype(BF16)

    @pl.when(used)
    def _():
        x = xs_ref[...].astype(BF16)
        hid = jax.nn.silu(_dot(x, wg_sc[...])) * _dot(x, wu_sc[...])
        o_ref[...] = _dot(hid.astype(BF16), wd_sc[...])

    @pl.when(jnp.logical_not(used))
    def _():
        o_ref[...] = jnp.zeros(o_ref.shape, o_ref.dtype)


def _combine_kernel(dest_ref, h_ref, wgt_ref, g_ref, b_ref, ys_hbm, o_ref, ob_ref, ybuf, sem, *, tc, alpha):
    base = pl.program_id(0) * tc

    def issue(t, carry):
        tok = base + t
        _row_copy(ys_hbm, dest_ref[2 * tok], ybuf, t, sem).start()
        _row_copy(ys_hbm, dest_ref[2 * tok + 1], ybuf, tc + t, sem).start()
        return carry

    lax.fori_loop(0, tc, issue, 0, unroll=8)
    pltpu.make_async_copy(ys_hbm.at[pl.ds(0, 2 * tc)], ybuf, sem).wait()
    wgt = wgt_ref[...]
    ff = ybuf[0:tc, :] * wgt[:, 0:1] + ybuf[tc:2 * tc, :] * wgt[:, 1:2]
    h2 = _layer_norm(alpha * h_ref[...] + ff, g_ref[...], b_ref[...])
    o_ref[...] = h2
    ob_ref[...] = h2.astype(BF16)


def _tiles(seq):
    big = seq >= 2048
    return dict(
        rows=256 if big else 128,
        mla=512 if big else 128,
        mla_heads=4,
        idx_q=256 if big else 64,
        dsa_q=256 if big else 128,
        key=512 if big else 128,
        route=512 if big else 128,
        moe_tok=512 if big else 128,
        comb=128,
    )


def _rope_tables(positions, rot_dim, head_dim):
    half = rot_dim // 2
    inv = ROPE_THETA ** (-jnp.arange(0, rot_dim, 2, dtype=F32) / rot_dim)
    ang = positions.astype(F32).reshape(-1, 1) * inv
    cos, sin = jnp.cos(ang), jnp.sin(ang)
    t = cos.shape[0]
    rest = head_dim - rot_dim
    ones, zeros, zh = jnp.ones((t, rest), F32), jnp.zeros((t, rest), F32), jnp.zeros((t, half), F32)
    rep = LANES // head_dim
    c = jnp.tile(jnp.concatenate([cos, cos, ones], axis=1), (1, rep))
    sa = jnp.tile(jnp.concatenate([-sin, zh, zeros], axis=1), (1, rep))
    sb = jnp.tile(jnp.concatenate([zh, sin, zeros], axis=1), (1, rep))
    return c, sa, sb


def _layer(layer, h, h_bf, positions, tabs, lw, alpha, B, S):
    T, D = h.shape
    tl = _tiles(S)
    tm = tl["rows"]
    nrow = T // tm
    spb = S // tm
    (c_m, sa_m, sb_m), (c_d, sa_d, sb_d), (c_i, sa_i, sb_i) = tabs
    row = lambda w: pl.BlockSpec((tm, w), lambda i: (i, 0))
    head_major = lambda nh, w: pl.BlockSpec((1, nh, tm, w), lambda i: (i // spb, 0, i % spb, 0))

    log2e = math.log2(math.e)
    q_a, k_a, v_a = pl.pallas_call(
        functools.partial(_mla_prep_kernel, q_scale=log2e * MLA_QK_DIM ** -0.5),
        grid=(nrow,),
        in_specs=[row(D), _resident(lw["w_mla_in"].shape), _resident((1, MLA_Q_LORA)), _resident((1, MLA_KV_LORA)),
                  _resident(lw["w_q_b"].shape), _resident(lw["w_kv_b"].shape), row(LANES), row(LANES), row(LANES)],
        out_specs=[head_major(MLA_HEADS, MLA_QK_DIM), head_major(MLA_HEADS, MLA_QK_DIM), row(2 * MLA_HEADS * MLA_V_DIM)],
        out_shape=[jax.ShapeDtypeStruct((B, MLA_HEADS, S, MLA_QK_DIM), BF16),
                   jax.ShapeDtypeStruct((B, MLA_HEADS, S, MLA_QK_DIM), BF16),
                   jax.ShapeDtypeStruct((T, 2 * MLA_HEADS * MLA_V_DIM), BF16)],
        compiler_params=_params("parallel"),
    )(h_bf, lw["w_mla_in"], lw["g_q"], lw["g_kv"], lw["w_q_b"], lw["w_kv_b"], c_m, sa_m, sb_m)

    hd = DSA_HEADS * DSA_HEAD_DIM
    q_b, k_b, v_b, q_idx, k_idx, w_idx = pl.pallas_call(
        functools.partial(_dsa_prep_kernel, w_idx_scale=(IDX_HEADS ** -0.5) * (IDX_DIM ** -0.5),
                          q_scale=log2e * DSA_HEAD_DIM ** -0.5),
        grid=(nrow,),
        in_specs=[row(D), _resident(lw["w_dsa_in"].shape)] + [row(LANES)] * 6,
        out_specs=[row(hd), row(hd), row(2 * hd), head_major(IDX_HEADS, IDX_DIM), row(IDX_DIM),
                   pl.BlockSpec((1, IDX_HEADS, tm), lambda i: (i // spb, 0, i % spb))],
        out_shape=[jax.ShapeDtypeStruct((T, hd), BF16)] * 2 + [jax.ShapeDtypeStruct((T, 2 * hd), BF16)] + [
            jax.ShapeDtypeStruct((B, IDX_HEADS, S, IDX_DIM), BF16),
            jax.ShapeDtypeStruct((T, IDX_DIM), BF16),
            jax.ShapeDtypeStruct((B, IDX_HEADS, S), F32)],
        compiler_params=_params("parallel"),
    )(h_bf, lw["w_dsa_in"], c_d, sa_d, sb_d, c_i, sa_i, sb_i)

    ta = tl["mla"]
    pos_q = positions.reshape(B, S, 1)
    pos_k_mla = positions.reshape(B, S // ta, 1, ta)
    hp = tl["mla_heads"]
    o_a = pl.pallas_call(
        functools.partial(_mla_attn_kernel, tile=ta, heads=hp),
        grid=(B, MLA_HEADS // hp, S // ta),
        in_specs=[pl.BlockSpec((1, hp, ta, MLA_QK_DIM), lambda b, g, i: (b, g, i, 0)),
                  pl.BlockSpec((1, hp, S, MLA_QK_DIM), lambda b, g, i: (b, g, 0, 0)),
                  pl.BlockSpec((1, S, 2 * MLA_V_DIM * hp), lambda b, g, i: (b, 0, g)),
                  pl.BlockSpec((1, ta, 1), lambda b, g, i: (b, i, 0)),
                  pl.BlockSpec((1, S // ta, 1, ta), lambda b, g, i: (b, 0, 0, 0))],
        out_specs=pl.BlockSpec((1, ta, MLA_V_DIM * hp), lambda b, g, i: (b, i, g)),
        out_shape=jax.ShapeDtypeStruct((B, S, MLA_HEADS * MLA_V_DIM), BF16),
        scratch_shapes=[pltpu.VMEM((hp, ta, LANES), F32), pltpu.VMEM((hp, ta, 2 * MLA_V_DIM), F32)],
        compiler_params=_params("parallel", "parallel", "arbitrary"),
    )(q_a, k_a, v_a.reshape(B, S, -1), pos_q, pos_k_mla)

    tk = tl["key"]
    nkc = S // tk
    tqi = tl["idx_q"]
    top_k = min(DSA_MAX_TOPK, S // 4)
    sel_bias = pl.pallas_call(
        functools.partial(_idx_select_kernel, tq=tqi, tk=tk, nkc=nkc, top_k=top_k),
        grid=(B, S // tqi),
        in_specs=[pl.BlockSpec((1, IDX_HEADS, tqi, IDX_DIM), lambda b, i: (b, 0, i, 0)),
                  pl.BlockSpec((1, S, IDX_DIM), lambda b, i: (b, 0, 0)),
                  pl.BlockSpec((1, IDX_HEADS, tqi), lambda b, i: (b, 0, i)),
                  pl.BlockSpec((1, 1, tqi), lambda b, i: (b, 0, i)),
                  pl.BlockSpec((1, nkc, tk, 1), lambda b, i: (b, 0, 0, 0))],
        out_specs=pl.BlockSpec((1, nkc, tqi, tk), lambda b, i: (b, 0, i, 0)),
        out_shape=jax.ShapeDtypeStruct((B, nkc, S, tk), BF16),
        scratch_shapes=[pltpu.VMEM((nkc, tk, tqi), I32)],
        compiler_params=_params("parallel", "arbitrary"),
    )(q_idx, k_idx.reshape(B, S, IDX_DIM), w_idx, positions.reshape(B, 1, S), positions.reshape(B, nkc, tk, 1))

    tqd = tl["dsa_q"]
    o_b = pl.pallas_call(
        functools.partial(_dsa_attn_kernel, tq=tqd, tk=tk),
        grid=(B, S // tqd),
        in_specs=[pl.BlockSpec((1, tqd, hd), lambda b, i: (b, i, 0)),
                  pl.BlockSpec((1, S, hd), lambda b, i: (b, 0, 0), pipeline_mode=pl.Buffered(1)),
                  pl.BlockSpec((1, S, 2 * hd), lambda b, i: (b, 0, 0), pipeline_mode=pl.Buffered(1)),
                  pl.BlockSpec((1, nkc, tqd, tk), lambda b, i: (b, 0, i, 0))],
        out_specs=pl.BlockSpec((1, tqd, hd), lambda b, i: (b, i, 0)),
        out_shape=jax.ShapeDtypeStruct((B, S, hd), BF16),
        scratch_shapes=[pltpu.VMEM((DSA_HEADS, tqd, LANES), F32), pltpu.VMEM((DSA_HEADS, tqd, 2 * DSA_HEAD_DIM), F32)],
        compiler_params=_params("parallel", "arbitrary"),
    )(q_b.reshape(B, S, hd), k_b.reshape(B, S, hd), v_b.reshape(B, S, 2 * hd), sel_bias)

    y = pl.pallas_call(
        _merge_kernel,
        grid=(nrow,),
        in_specs=[row(D), row(MLA_HEADS * MLA_V_DIM), row(hd),
                  _resident((D, D)), _resident((D, D)), _resident((MLA_HEADS * MLA_V_DIM, D)), _resident((hd, D))],
        out_specs=row(D),
        out_shape=jax.ShapeDtypeStruct((T, D), BF16),
        compiler_params=_params("parallel"),
    )(h_bf, o_a.reshape(T, -1), o_b.reshape(T, hd), lw["w_gate_a"], lw["w_gate_b"], lw["w_o_a"], lw["w_o_b"])

    h1, logits = pl.pallas_call(
        functools.partial(_out_ln_kernel, alpha=alpha),
        grid=(nrow,),
        in_specs=[row(D), row(D), _resident((D, D)), _resident((1, D)), _resident((1, D)),
                  _resident((D, LANES)), _resident((1, LANES))],
        out_specs=[row(D), row(LANES)],
        out_shape=[jax.ShapeDtypeStruct((T, D), F32), jax.ShapeDtypeStruct((T, LANES), F32)],
        compiler_params=_params("parallel"),
    )(y, h, lw["w_out"], lw["ln1_g"], lw["ln1_b"], lw["w_route"], lw["b_route"])

    tr = tl["route"]
    meta, wgt, cnt = pl.pallas_call(
        functools.partial(_route_kernel, tr=tr),
        grid=(T // tr,),
        in_specs=[pl.BlockSpec((tr, LANES), lambda i: (i, 0))],
        out_specs=[pl.BlockSpec((tr, LANES), lambda i: (i, 0)), pl.BlockSpec((tr, LANES), lambda i: (i, 0)),
                   pl.BlockSpec((8, LANES), lambda i: (0, 0))],
        out_shape=[jax.ShapeDtypeStruct((T, LANES), I32), jax.ShapeDtypeStruct((T, LANES), F32),
                   jax.ShapeDtypeStruct((8, LANES), F32)],
        scratch_shapes=[pltpu.VMEM((1, LANES), F32)],
        compiler_params=_params("arbitrary"),
    )(logits)

    rb = MOE_ROW_BLOCK
    counts = cnt[0, N_GROUPS:N_GROUPS + N_EXPERTS].astype(I32)
    padded = ((counts + rb - 1) // rb) * rb
    pends = jnp.cumsum(padded)
    pstarts = pends - padded
    n_blocks = -(-(2 * T) // rb) + N_EXPERTS
    P = n_blocks * rb
    dest = (pstarts[meta[:, 0:2]] + meta[:, 2:4]).reshape(-1).astype(I32)
    block_e = jnp.minimum(jnp.searchsorted(pends, jnp.arange(n_blocks, dtype=I32) * rb, side="right"),
                          N_EXPERTS - 1).astype(I32)
    n_used = (pends[-1] // rb).astype(I32).reshape(1)

    tt = tl["moe_tok"]
    xs = pl.pallas_call(
        functools.partial(_dispatch_kernel, tt=tt),
        grid_spec=pltpu.PrefetchScalarGridSpec(
            num_scalar_prefetch=1, grid=(T // tt,),
            in_specs=[pl.BlockSpec((tt, D), lambda i, d: (i, 0)), pl.BlockSpec(memory_space=pl.ANY)],
            out_specs=pl.BlockSpec(memory_space=pl.ANY),
            scratch_shapes=[pltpu.SemaphoreType.DMA(())]),
        out_shape=jax.ShapeDtypeStruct((P, D), F32),
        input_output_aliases={2: 0},
        compiler_params=_params("arbitrary"),
    )(dest, h1, jnp.zeros((P, D), F32))

    F = D_EXPERT
    grp = jnp.cumsum(jnp.concatenate([jnp.zeros((1,), I32), (block_e[1:] != block_e[:-1]).astype(I32)]))
    ids = jnp.where(counts > 0, jnp.arange(N_EXPERTS, dtype=I32), N_EXPERTS)
    at_or_after = lax.cummin(ids, axis=0, reverse=True)
    after = jnp.concatenate([at_or_after[1:], jnp.full((1,), N_EXPERTS, I32)])
    nxt = jnp.where(after < N_EXPERTS, after, -1)[block_e].astype(I32)
    any_spec = pl.BlockSpec(memory_space=pl.ANY)
    ys = pl.pallas_call(
        functools.partial(_experts_kernel, layer=layer),
        grid_spec=pltpu.PrefetchScalarGridSpec(
            num_scalar_prefetch=4, grid=(n_blocks,),
            in_specs=[pl.BlockSpec((rb, D), lambda i, *_: (i, 0)), any_spec, any_spec, any_spec],
            out_specs=pl.BlockSpec((rb, D), lambda i, *_: (i, 0)),
            scratch_shapes=[pltpu.VMEM((2, D, F), F32), pltpu.VMEM((2, D, F), F32), pltpu.VMEM((2, F, D), F32),
                            pltpu.VMEM((D, F), BF16), pltpu.VMEM((D, F), BF16), pltpu.VMEM((F, D), BF16),
                            pltpu.SemaphoreType.DMA((2, 3))]),
        out_shape=jax.ShapeDtypeStruct((P, D), F32),
        compiler_params=_params("arbitrary"),
    )(block_e, grp.astype(I32), nxt, n_used, xs, lw["w_e_gate"], lw["w_e_up"], lw["w_e_down"])

    tc = tl["comb"]
    h2, h2_bf = pl.pallas_call(
        functools.partial(_combine_kernel, tc=tc, alpha=alpha),
        grid_spec=pltpu.PrefetchScalarGridSpec(
            num_scalar_prefetch=1, grid=(T // tc,),
            in_specs=[pl.BlockSpec((tc, D), lambda i, d: (i, 0)),
                      pl.BlockSpec((tc, LANES), lambda i, d: (i, 0)),
                      pl.BlockSpec((1, D), lambda i, d: (0, 0)),
                      pl.BlockSpec((1, D), lambda i, d: (0, 0)),
                      pl.BlockSpec(memory_space=pl.ANY)],
            out_specs=[pl.BlockSpec((tc, D), lambda i, d: (i, 0)), pl.BlockSpec((tc, D), lambda i, d: (i, 0))],
            scratch_shapes=[pltpu.VMEM((2 * tc, D), F32), pltpu.SemaphoreType.DMA(())]),
        out_shape=[jax.ShapeDtypeStruct((T, D), F32), jax.ShapeDtypeStruct((T, D), BF16)],
        compiler_params=_params("arbitrary"),
    )(dest, h1, wgt, lw["ln2_g"], lw["ln2_b"], ys)
    return h2, h2_bf


def _layer_weights(l, w_in, g_q_lora, w_q_b, g_kv_lora, w_kv_b, w_o_a, w_o_b, w_out, ln1_g, ln1_b,
                   w_group, b_group, w_router, b_router, w_e_gate, w_e_up, w_e_down, ln2_g, ln2_b):
    D = w_in.shape[1]
    wi = w_in[l]
    o_kpe = MLA_Q_LORA + MLA_KV_LORA
    o_dsa = o_kpe + MLA_ROPE_DIM
    hd = DSA_HEADS * DSA_HEAD_DIM
    o_kidx = o_dsa + 3 * hd + IDX_HEADS * IDX_DIM
    o_widx = o_kidx + IDX_DIM
    o_gate = o_widx + IDX_HEADS
    kpe = wi[:, o_kpe:o_dsa]
    kidx = wi[:, o_kidx:o_widx]
    w_mla_in = jnp.concatenate([wi[:, :o_kpe], kpe, kpe], axis=1).astype(BF16)
    w_dsa_in = jnp.concatenate([wi[:, o_dsa:o_kidx], kidx, kidx, wi[:, o_widx:o_gate],
                                jnp.zeros((D, LANES - IDX_HEADS), F32)], axis=1).astype(BF16)
    wq = w_q_b[l].reshape(MLA_Q_LORA, MLA_HEADS, MLA_QK_DIM)
    wq = jnp.concatenate([wq[:, :, :MLA_NOPE_DIM].reshape(MLA_Q_LORA, -1),
                          wq[:, :, MLA_NOPE_DIM:].reshape(MLA_Q_LORA, -1)], axis=1).astype(BF16)
    w_route = jnp.concatenate([w_group[l], w_router[l],
                               jnp.zeros((D, LANES - N_GROUPS - N_EXPERTS), F32)], axis=1)
    b_route = jnp.concatenate([b_group[l], b_router[l],
                               jnp.zeros((LANES - N_GROUPS - N_EXPERTS,), F32)]).reshape(1, LANES)
    return dict(
        w_mla_in=w_mla_in, w_dsa_in=w_dsa_in,
        w_gate_a=wi[:, o_gate:o_gate + D].astype(BF16), w_gate_b=wi[:, o_gate + D:].astype(BF16),
        g_q=g_q_lora[l].reshape(1, -1), g_kv=g_kv_lora[l].reshape(1, -1),
        w_q_b=wq, w_kv_b=w_kv_b[l].astype(BF16),
        w_o_a=w_o_a[l].astype(BF16), w_o_b=w_o_b[l].astype(BF16), w_out=w_out[l].astype(BF16),
        ln1_g=ln1_g[l].reshape(1, -1), ln1_b=ln1_b[l].reshape(1, -1),
        w_route=w_route, b_route=b_route,
        w_e_gate=w_e_gate, w_e_up=w_e_up, w_e_down=w_e_down,
        ln2_g=ln2_g[l].reshape(1, -1), ln2_b=ln2_b[l].reshape(1, -1),
    )


def kernel(x, positions, w_in, g_q_lora, w_q_b, g_kv_lora, w_kv_b, w_o_a, w_o_b, w_out, ln1_g, ln1_b,
           w_group, b_group, w_router, b_router, w_e_gate, w_e_up, w_e_down, ln2_g, ln2_b):
    B, S, D = x.shape
    depth = w_in.shape[0]
    alpha = (2 * depth) ** 0.25
    tabs = (_rope_tables(positions, MLA_ROPE_DIM, MLA_ROPE_DIM),
            _rope_tables(positions, DSA_ROPE_DIM, DSA_HEAD_DIM),
            _rope_tables(positions, IDX_ROPE_DIM, IDX_DIM))
    h = x.reshape(B * S, D)
    h_bf = h.astype(BF16)
    for l in range(depth):
        lw = _layer_weights(l, w_in, g_q_lora, w_q_b, g_kv_lora, w_kv_b, w_o_a, w_o_b, w_out, ln1_g, ln1_b,
                            w_group, b_group, w_router, b_router, w_e_gate, w_e_up, w_e_down, ln2_g, ln2_b)
        h, h_bf = _layer(l, h, h_bf, positions, tabs, lw, alpha, B, S)
    return h.reshape(B, S, D)
```

```python
import functools
import math

import jax
import jax.numpy as jnp
from jax import lax
from jax.experimental import pallas as pl
from jax.experimental.pallas import tpu as pltpu

F32 = jnp.float32
BF16 = jnp.bfloat16
I32 = jnp.int32

MLA_HEADS = 8
MLA_Q_LORA = 512
MLA_KV_LORA = 512
MLA_NOPE_DIM = 128
MLA_ROPE_DIM = 64
MLA_V_DIM = 128
MLA_QK_DIM = MLA_NOPE_DIM + MLA_ROPE_DIM
DSA_HEADS = 8
DSA_HEAD_DIM = 128
DSA_ROPE_DIM = DSA_HEAD_DIM // 4
IDX_HEADS = 16
IDX_DIM = 64
IDX_ROPE_DIM = IDX_DIM // 4
DSA_MAX_TOPK = 256
ROPE_THETA = 500000.0
N_GROUPS = 8
EXPERTS_PER_GROUP = 8
N_EXPERTS = N_GROUPS * EXPERTS_PER_GROUP
D_EXPERT = 512
MOE_ROW_BLOCK = 128
LN_EPS = 1e-5
RMS_EPS = 1e-6

LANES = 128
NEG_BIG = -1e30
INT_MIN = -(2 ** 31)
VMEM_LIMIT = 56 * 1024 * 1024
WEIGHT_DMA_CHUNKS = 4


def _params(*sem):
    return pltpu.CompilerParams(dimension_semantics=sem, vmem_limit_bytes=VMEM_LIMIT)


def _resident(shape):
    nd = len(shape)
    return pl.BlockSpec(shape, lambda *_: (0,) * nd, pipeline_mode=pl.Buffered(1))


def _dot(a, b):
    return jnp.dot(a, b, preferred_element_type=F32)


def _dot_nt(a, b):
    return lax.dot_general(a, b, (((1,), (1,)), ((), ())), preferred_element_type=F32)


def _rope128(x, c, sa, sb, half):
    return x * c + pltpu.roll(x, LANES - half, 1) * sa + pltpu.roll(x, half, 1) * sb


def _rms(x, g):
    return x * lax.rsqrt(jnp.mean(x * x, axis=-1, keepdims=True) + RMS_EPS) * g


def _layer_norm(x, g, b):
    mu = jnp.mean(x, axis=-1, keepdims=True)
    xc = x - mu
    var = jnp.mean(xc * xc, axis=-1, keepdims=True)
    return xc * lax.rsqrt(var + LN_EPS) * g + b


def _mla_prep_kernel(h_ref, win_ref, gq_ref, gkv_ref, wqb_ref, wkvb_ref, c_ref, sa_ref, sb_ref,
                     q_ref, k_ref, v_ref, *, q_scale):
    h = h_ref[...]
    p = _dot(h, win_ref[...])
    qn = _rms(p[:, :MLA_Q_LORA], gq_ref[...]).astype(BF16)
    kvn = _rms(p[:, MLA_Q_LORA:MLA_Q_LORA + MLA_KV_LORA], gkv_ref[...]).astype(BF16)
    q = _dot(qn, wqb_ref[...]) * q_scale
    kv = _dot(kvn, wkvb_ref[...])
    c, sa, sb = c_ref[...], sa_ref[...], sb_ref[...]
    half = MLA_ROPE_DIM // 2
    kpe = _rope128(p[:, MLA_Q_LORA + MLA_KV_LORA:], c, sa, sb, half)[:, :MLA_ROPE_DIM].astype(BF16)
    pe0 = MLA_HEADS * MLA_NOPE_DIM
    for hh in range(MLA_HEADS):
        q_ref[0, hh, :, 0:MLA_NOPE_DIM] = q[:, 128 * hh:128 * hh + 128].astype(BF16)
        k_ref[0, hh, :, 0:MLA_NOPE_DIM] = kv[:, 256 * hh:256 * hh + 128].astype(BF16)
        k_ref[0, hh, :, MLA_NOPE_DIM:MLA_QK_DIM] = kpe
        v_ref[:, 256 * hh:256 * hh + 128] = kv[:, 256 * hh + 128:256 * hh + 256].astype(BF16)
        v_ref[:, 256 * hh + 128:256 * hh + 256] = jnp.ones((kv.shape[0], LANES), BF16)
    for s in range(MLA_HEADS // 2):
        slab = _rope128(q[:, pe0 + 128 * s:pe0 + 128 * s + 128], c, sa, sb, half).astype(BF16)
        q_ref[0, 2 * s, :, MLA_NOPE_DIM:MLA_QK_DIM] = slab[:, :MLA_ROPE_DIM]
        q_ref[0, 2 * s + 1, :, MLA_NOPE_DIM:MLA_QK_DIM] = slab[:, MLA_ROPE_DIM:]


def _dsa_prep_kernel(h_ref, w_ref, cd_ref, sad_ref, sbd_ref, ci_ref, sai_ref, sbi_ref,
                     qb_ref, kb_ref, vb_ref, qi_ref, ki_ref, wi_ref, *, w_idx_scale, q_scale):
    h = h_ref[...]
    hd = DSA_HEADS * DSA_HEAD_DIM
    cd, sad, sbd = cd_ref[...], sad_ref[...], sbd_ref[...]
    ci, sai, sbi = ci_ref[...], sai_ref[...], sbi_ref[...]
    hd_half, hi_half = DSA_ROPE_DIM // 2, IDX_ROPE_DIM // 2
    q = _dot(h, w_ref[:, 0:hd]) * q_scale
    for hh in range(DSA_HEADS):
        qb_ref[:, 128 * hh:128 * hh + 128] = _rope128(q[:, 128 * hh:128 * hh + 128], cd, sad, sbd, hd_half).astype(BF16)
    k = _dot(h, w_ref[:, hd:2 * hd])
    for hh in range(DSA_HEADS):
        kb_ref[:, 128 * hh:128 * hh + 128] = _rope128(k[:, 128 * hh:128 * hh + 128], cd, sad, sbd, hd_half).astype(BF16)
    v = _dot(h, w_ref[:, 2 * hd:3 * hd]).astype(BF16)
    for hh in range(DSA_HEADS):
        vb_ref[:, 256 * hh:256 * hh + 128] = v[:, 128 * hh:128 * hh + 128]
        vb_ref[:, 256 * hh + 128:256 * hh + 256] = jnp.ones((v.shape[0], LANES), BF16)
    qi = _dot(h, w_ref[:, 3 * hd:3 * hd + IDX_HEADS * IDX_DIM])
    for s in range(IDX_HEADS // 2):
        slab = _rope128(qi[:, 128 * s:128 * s + 128], ci, sai, sbi, hi_half).astype(BF16)
        qi_ref[0, 2 * s] = slab[:, :IDX_DIM]
        qi_ref[0, 2 * s + 1] = slab[:, IDX_DIM:]
    o = 3 * hd + IDX_HEADS * IDX_DIM
    last = _dot(h, w_ref[:, o:o + 2 * LANES])
    ki_ref[...] = _rope128(last[:, :LANES], ci, sai, sbi, hi_half)[:, :IDX_DIM].astype(BF16)
    wi_ref[0] = (last[:, LANES:] * w_idx_scale).T[:IDX_HEADS, :]


def _flash_update(s, v_ones, m_ref, acc_ref):
    tk = s.shape[1]
    m_prev = m_ref[...]
    m_next = jnp.maximum(m_prev, jnp.max(s, axis=1, keepdims=True))
    p = jnp.exp2(s - jnp.tile(m_next, (1, tk // LANES)))
    alpha = jnp.exp2(m_prev - m_next)
    acc_ref[...] = acc_ref[...] * jnp.tile(alpha, (1, 2)) + _dot(p.astype(BF16), v_ones)
    m_ref[...] = m_next


def _mla_attn_kernel(q_ref, k_ref, v_ref, pq_ref, pk_ref, o_ref, m_sc, acc_sc, *, tile, heads):
    qi = pl.program_id(2)
    m_sc[...] = jnp.full(m_sc.shape, -jnp.inf, F32)
    acc_sc[...] = jnp.zeros(acc_sc.shape, F32)

    def chunk(c, masked):
        ks = pl.ds(pl.multiple_of(c * tile, tile), tile)
        if masked:
            visible = pk_ref[0, c] <= pq_ref[0]
        for hh in range(heads):
            s = _dot_nt(q_ref[0, hh], k_ref[0, hh, ks, :])
            if masked:
                s = jnp.where(visible, s, NEG_BIG)
            _flash_update(s, v_ref[0, ks, 2 * MLA_V_DIM * hh:2 * MLA_V_DIM * (hh + 1)], m_sc.at[hh], acc_sc.at[hh])

    def body(c, carry):
        chunk(c, False)
        return carry

    lax.fori_loop(0, qi, body, 0)
    chunk(qi, True)
    for hh in range(heads):
        o_ref[0, :, MLA_V_DIM * hh:MLA_V_DIM * (hh + 1)] = (
            acc_sc[hh, :, :MLA_V_DIM] / acc_sc[hh, :, MLA_V_DIM:]).astype(o_ref.dtype)


def _idx_select_kernel(q_ref, k_ref, wt_ref, pq_ref, pk_ref, o_ref, key_sc, *, tq, tk, nkc, top_k):
    qi = pl.program_id(1)
    nk = (qi * tq + tq + tk - 1) // tk
    wt = wt_ref[0]
    pq = pq_ref[0]

    def score_chunk(c, carry):
        kc = k_ref[0, pl.ds(pl.multiple_of(c * tk, tk), tk), :]
        acc = jnp.zeros((tk, tq), F32)
        for hh in range(IDX_HEADS):
            acc = acc + wt[hh:hh + 1, :] * jnp.maximum(_dot_nt(kc, q_ref[0, hh]), 0.0)
        bits = pltpu.bitcast(acc, I32)
        key = jnp.where(bits < 0, bits ^ jnp.int32(0x7FFFFFFF), bits)
        key = jnp.where(acc == 0.0, 0, key)
        key_sc[c] = jnp.where(pk_ref[0, c] <= pq, key, INT_MIN)
        return carry

    lax.fori_loop(0, nk, score_chunk, 0)

    def count(pred):
        def body(c, part):
            m = pred(key_sc[c], c).astype(I32)
            return part + jnp.sum(m.reshape(tk // 8, 8, tq), axis=0)
        part = lax.fori_loop(0, nk, body, jnp.zeros((8, tq), I32))
        return jnp.sum(part, axis=0, keepdims=True)

    def bit_step(i, base):
        cand = base ^ (jnp.int32(1) << (31 - i))
        cnt = count(lambda key, c: key >= cand)
        return jnp.where(cnt >= top_k, cand, base)

    thr = lax.fori_loop(0, 32, bit_step, jnp.full((1, tq), INT_MIN, I32))
    n_gt = count(lambda key, c: key > thr)
    n_eq = count(lambda key, c: key == thr)
    need = top_k - n_gt
    tie = jnp.logical_and(n_eq > need, thr != INT_MIN)

    def key_index(c):
        return c * tk + lax.broadcasted_iota(I32, (tk, tq), 0)

    def tie_search():
        def step(i, p):
            cand = p | (jnp.int32(1) << (14 - i))
            cnt = count(lambda key, c: jnp.logical_and(key == thr, key_index(c) < cand))
            return jnp.where(cnt <= need, cand, p)
        return lax.fori_loop(0, 15, step, jnp.zeros((1, tq), I32))

    p_cut = lax.cond(jnp.max(tie.astype(I32)) > 0, tie_search,
                     lambda: jnp.full((1, tq), 2 ** 30, I32))

    def out_chunk(c, carry):
        key = key_sc[c]
        sel = jnp.logical_or(key > thr, jnp.logical_and(key == thr, key_index(c) < p_cut))
        sel = jnp.logical_and(sel, key != INT_MIN)
        o_ref[0, c] = jnp.where(sel, 0.0, NEG_BIG).T.astype(o_ref.dtype)
        return carry

    lax.fori_loop(0, nk, out_chunk, 0)

    def fill_chunk(c, carry):
        o_ref[0, c] = jnp.full((tq, tk), NEG_BIG, o_ref.dtype)
        return carry

    lax.fori_loop(nk, nkc, fill_chunk, 0)


def _dsa_attn_kernel(q_ref, k_ref, v_ref, b_ref, o_ref, m_sc, acc_sc, *, tq, tk):
    qi = pl.program_id(1)
    nk = (qi * tq + tq + tk - 1) // tk
    m_sc[...] = jnp.full(m_sc.shape, NEG_BIG, F32)
    acc_sc[...] = jnp.zeros(acc_sc.shape, F32)

    def chunk(c, carry):
        bias = b_ref[0, c].astype(F32)
        ks = pl.ds(pl.multiple_of(c * tk, tk), tk)
        for hh in range(DSA_HEADS):
            hs = slice(DSA_HEAD_DIM * hh, DSA_HEAD_DIM * (hh + 1))
            vs = slice(2 * DSA_HEAD_DIM * hh, 2 * DSA_HEAD_DIM * (hh + 1))
            s = _dot_nt(q_ref[0, :, hs], k_ref[0, ks, hs]) + bias
            _flash_update(s, v_ref[0, ks, vs], m_sc.at[hh], acc_sc.at[hh])
        return carry

    lax.fori_loop(0, nk, chunk, 0)
    for hh in range(DSA_HEADS):
        hs = slice(DSA_HEAD_DIM * hh, DSA_HEAD_DIM * (hh + 1))
        o_ref[0, :, hs] = (acc_sc[hh, :, :DSA_HEAD_DIM] / acc_sc[hh, :, DSA_HEAD_DIM:]).astype(o_ref.dtype)


def _merge_kernel(h_ref, oa_ref, ob_ref, wga_ref, wgb_ref, woa_ref, wob_ref, y_ref):
    h = h_ref[...]
    ya = jax.nn.sigmoid(_dot(h, wga_ref[...])) * _dot(oa_ref[...], woa_ref[...])
    yb = jax.nn.sigmoid(_dot(h, wgb_ref[...])) * _dot(ob_ref[...], wob_ref[...])
    y_ref[...] = (ya + yb).astype(y_ref.dtype)


def _out_ln_kernel(y_ref, h_ref, wout_ref, g_ref, b_ref, wr_ref, br_ref, o_ref, lg_ref, *, alpha):
    sub = 128
    for r in range(y_ref.shape[0] // sub):
        rows = slice(r * sub, (r + 1) * sub)
        mix = _dot(y_ref[rows, :], wout_ref[...])
        h1 = _layer_norm(alpha * h_ref[rows, :] + mix, g_ref[...], b_ref[...])
        o_ref[rows, :] = h1
        lg_ref[rows, :] = jnp.dot(h1, wr_ref[...], preferred_element_type=F32,
                                  precision=lax.Precision.HIGHEST) + br_ref[...]


def _route_kernel(lg_ref, meta_ref, wgt_ref, cnt_ref, carry_sc, *, tr):
    @pl.when(pl.program_id(0) == 0)
    def _():
        carry_sc[...] = jnp.zeros(carry_sc.shape, F32)

    lg = lg_ref[...]
    lane = lax.broadcasted_iota(I32, (tr, LANES), 1)
    gl = jnp.where(lane < N_GROUPS, lg, -jnp.inf)
    gmax = jnp.max(gl, axis=1, keepdims=True)
    g_idx = jnp.min(jnp.where(gl == gmax, lane, LANES), axis=1, keepdims=True)
    g_p = 1.0 / jnp.sum(jnp.exp(gl - gmax), axis=1, keepdims=True)
    lo = N_GROUPS + EXPERTS_PER_GROUP * g_idx
    in_group = jnp.logical_and(lane >= lo, lane < lo + EXPERTS_PER_GROUP)
    el = jnp.where(in_group, lg, -jnp.inf)
    emax = jnp.max(el, axis=1, keepdims=True)
    e1 = jnp.min(jnp.where(el == emax, lane, LANES), axis=1, keepdims=True)
    den = jnp.sum(jnp.exp(el - emax), axis=1, keepdims=True)
    el2 = jnp.where(lane == e1, -jnp.inf, el)
    emax2 = jnp.max(el2, axis=1, keepdims=True)
    e2 = jnp.min(jnp.where(el2 == emax2, lane, LANES), axis=1, keepdims=True)
    p1 = 1.0 / den
    p2 = jnp.exp(emax2 - emax) / den
    w1 = g_p * (p1 / (p1 + p2))
    w2 = g_p * (p2 / (p1 + p2))

    is1 = lane == e1
    is2 = lane == e2
    onehot = jnp.logical_or(is1, is2).astype(BF16)
    r = lax.broadcasted_iota(I32, (tr, tr), 0)
    cidx = lax.broadcasted_iota(I32, (tr, tr), 1)
    lower = (cidx < r).astype(BF16)
    before = _dot(lower, onehot) + carry_sc[...]
    rank1 = jnp.sum(jnp.where(is1, before, 0.0), axis=1, keepdims=True)
    rank2 = jnp.sum(jnp.where(is2, before, 0.0), axis=1, keepdims=True)
    carry_sc[...] = carry_sc[...] + jnp.sum(onehot.astype(F32), axis=0, keepdims=True)

    meta = jnp.where(lane == 0, e1 - N_GROUPS,
                     jnp.where(lane == 1, e2 - N_GROUPS,
                               jnp.where(lane == 2, rank1.astype(I32),
                                         jnp.where(lane == 3, rank2.astype(I32), 0))))
    meta_ref[...] = meta
    wgt_ref[...] = jnp.where(lane == 0, w1, jnp.where(lane == 1, w2, 0.0))
    cnt_ref[...] = jnp.broadcast_to(carry_sc[...], cnt_ref.shape)


def _row_copy(src, src_row, dst, dst_row, sem):
    return pltpu.make_async_copy(src.at[pl.ds(src_row, 1)], dst.at[pl.ds(dst_row, 1)], sem)


def _dest_kernel(meta_ref, ps_ref, o_ref):
    meta = meta_ref[...]
    ps = ps_ref[...]
    lane = lax.broadcasted_iota(I32, meta.shape, 1)
    d1 = jnp.sum(jnp.where(lane == meta[:, 0:1], ps, 0), axis=1, keepdims=True) + meta[:, 2:3]
    d2 = jnp.sum(jnp.where(lane == meta[:, 1:2], ps, 0), axis=1, keepdims=True) + meta[:, 3:4]
    o_ref[...] = jnp.where(lane == 0, d1, jnp.where(lane == 1, d2, 0))


def _dispatch_kernel(dest_ref, zblk_ref, nz_ref, h_ref, xs_out, zero_sc, sem, zsem, *, tt):
    base = pl.program_id(0) * tt
    rb = zero_sc.shape[0]

    @pl.when(pl.program_id(0) == 0)
    def _():
        zero_sc[...] = jnp.zeros(zero_sc.shape, zero_sc.dtype)

        def fill(j, carry):
            start = pl.multiple_of(zblk_ref[j] * rb, rb)
            pltpu.make_async_copy(zero_sc, xs_out.at[pl.ds(start, rb)], zsem).start()
            return carry

        def drain(j, carry):
            pltpu.make_async_copy(zero_sc, xs_out.at[pl.ds(0, rb)], zsem).wait()
            return carry

        lax.fori_loop(0, nz_ref[0], fill, 0)
        lax.fori_loop(0, nz_ref[0], drain, 0)

    def issue(t, carry):
        tok = base + t
        _row_copy(h_ref, t, xs_out, dest_ref[2 * tok], sem).start()
        _row_copy(h_ref, t, xs_out, dest_ref[2 * tok + 1], sem).start()
        return carry

    lax.fori_loop(0, tt, issue, 0, unroll=8)
    for _ in range(2):
        pltpu.make_async_copy(h_ref, xs_out.at[pl.ds(0, tt)], sem).wait()


def _experts_kernel(be_ref, grp_ref, nxt_ref, nu_ref, xs_ref, wg_hbm, wu_hbm, wd_hbm, o_ref,
                    wg_buf, wu_buf, wd_buf, wg_sc, wu_sc, wd_sc, sem, *, layer):
    i = pl.program_id(0)
    e = be_ref[i]
    used = i < nu_ref[0]
    first = jnp.logical_and(used, jnp.logical_or(i == 0, e != be_ref[jnp.maximum(i - 1, 0)]))
    slot = grp_ref[i] % 2

    def copies(expert, s):
        out = []
        for j, (hbm, buf) in enumerate(((wg_hbm, wg_buf), (wu_hbm, wu_buf), (wd_hbm, wd_buf))):
            rows = buf.shape[1] // WEIGHT_DMA_CHUNKS
            for c in range(WEIGHT_DMA_CHUNKS):
                rs = pl.ds(c * rows, rows)
                out.append(pltpu.make_async_copy(hbm.at[layer, expert, rs], buf.at[s, rs], sem.at[s, j]))
        return out

    @pl.when(i == 0)
    def _():
        for cp in copies(e, slot):
            cp.start()

    @pl.when(first)
    def _():
        for cp in copies(e, slot):
            cp.wait()
        nxt = nxt_ref[i]

        @pl.when(nxt >= 0)
        def _():
            for cp in copies(nxt, 1 - slot):
                cp.start()

        wg_sc[...] = wg_buf[slot].astype(BF16)
        wu_sc[...] = wu_buf[slot].astype(BF16)
        wd_sc[...] = wd_buf[slot].astype(BF16)

    @pl.when(used)
    def _():
        x = xs_ref[...].astype(BF16)
        hid = jax.nn.silu(_dot(x, wg_sc[...])) * _dot(x, wu_sc[...])
        o_ref[...] = _dot(hid.astype(BF16), wd_sc[...])

    @pl.when(jnp.logical_not(used))
    def _():
        o_ref[...] = jnp.zeros(o_ref.shape, o_ref.dtype)


def _combine_kernel(dest_ref, h_ref, wgt_ref, g_ref, b_ref, ys_hbm, o_ref, ob_ref, ybuf, sem, *, tc, alpha):
    i = pl.program_id(0)

    def gather(tile, slot):
        def issue(t, carry):
            tok = tile * tc + t
            _row_copy(ys_hbm, dest_ref[2 * tok], ybuf.at[slot], t, sem.at[slot]).start()
            _row_copy(ys_hbm, dest_ref[2 * tok + 1], ybuf.at[slot], tc + t, sem.at[slot]).start()
            return carry
        lax.fori_loop(0, tc, issue, 0, unroll=8)

    @pl.when(i == 0)
    def _():
        gather(0, 0)

    @pl.when(i + 1 < pl.num_programs(0))
    def _():
        gather(i + 1, (i + 1) % 2)

    slot = i % 2
    pltpu.make_async_copy(ys_hbm.at[pl.ds(0, 2 * tc)], ybuf.at[slot], sem.at[slot]).wait()
    wgt = wgt_ref[...]
    ff = ybuf[slot, 0:tc, :] * wgt[:, 0:1] + ybuf[slot, tc:2 * tc, :] * wgt[:, 1:2]
    h2 = _layer_norm(alpha * h_ref[...] + ff, g_ref[...], b_ref[...])
    o_ref[...] = h2
    ob_ref[...] = h2.astype(BF16)


def _tiles(seq):
    big = seq >= 2048
    return dict(
        rows=256 if big else 128,
        mla=512 if big else 128,
        mla_heads=4,
        idx_q=256 if big else 64,
        dsa_q=256 if big else 128,
        key=512 if big else 128,
        route=512 if big else 128,
        moe_tok=512 if big else 128,
        comb=128,
    )


def _rope_tables(positions, rot_dim, head_dim):
    half = rot_dim // 2
    inv = ROPE_THETA ** (-jnp.arange(0, rot_dim, 2, dtype=F32) / rot_dim)
    ang = positions.astype(F32).reshape(-1, 1) * inv
    cos, sin = jnp.cos(ang), jnp.sin(ang)
    t = cos.shape[0]
    rest = head_dim - rot_dim
    ones, zeros, zh = jnp.ones((t, rest), F32), jnp.zeros((t, rest), F32), jnp.zeros((t, half), F32)
    rep = LANES // head_dim
    c = jnp.tile(jnp.concatenate([cos, cos, ones], axis=1), (1, rep))
    sa = jnp.tile(jnp.concatenate([-sin, zh, zeros], axis=1), (1, rep))
    sb = jnp.tile(jnp.concatenate([zh, sin, zeros], axis=1), (1, rep))
    return c, sa, sb


def _layer(layer, h, h_bf, positions, tabs, lw, alpha, B, S):
    T, D = h.shape
    tl = _tiles(S)
    tm = tl["rows"]
    nrow = T // tm
    spb = S // tm
    (c_m, sa_m, sb_m), (c_d, sa_d, sb_d), (c_i, sa_i, sb_i) = tabs
    row = lambda w: pl.BlockSpec((tm, w), lambda i: (i, 0))
    head_major = lambda nh, w: pl.BlockSpec((1, nh, tm, w), lambda i: (i // spb, 0, i % spb, 0))

    log2e = math.log2(math.e)
    q_a, k_a, v_a = pl.pallas_call(
        functools.partial(_mla_prep_kernel, q_scale=log2e * MLA_QK_DIM ** -0.5),
        grid=(nrow,),
        in_specs=[row(D), _resident(lw["w_mla_in"].shape), _resident((1, MLA_Q_LORA)), _resident((1, MLA_KV_LORA)),
                  _resident(lw["w_q_b"].shape), _resident(lw["w_kv_b"].shape), row(LANES), row(LANES), row(LANES)],
        out_specs=[head_major(MLA_HEADS, MLA_QK_DIM), head_major(MLA_HEADS, MLA_QK_DIM), row(2 * MLA_HEADS * MLA_V_DIM)],
        out_shape=[jax.ShapeDtypeStruct((B, MLA_HEADS, S, MLA_QK_DIM), BF16),
                   jax.ShapeDtypeStruct((B, MLA_HEADS, S, MLA_QK_DIM), BF16),
                   jax.ShapeDtypeStruct((T, 2 * MLA_HEADS * MLA_V_DIM), BF16)],
        compiler_params=_params("parallel"),
    )(h_bf, lw["w_mla_in"], lw["g_q"], lw["g_kv"], lw["w_q_b"], lw["w_kv_b"], c_m, sa_m, sb_m)

    hd = DSA_HEADS * DSA_HEAD_DIM
    q_b, k_b, v_b, q_idx, k_idx, w_idx = pl.pallas_call(
        functools.partial(_dsa_prep_kernel, w_idx_scale=(IDX_HEADS ** -0.5) * (IDX_DIM ** -0.5),
                          q_scale=log2e * DSA_HEAD_DIM ** -0.5),
        grid=(nrow,),
        in_specs=[row(D), _resident(lw["w_dsa_in"].shape)] + [row(LANES)] * 6,
        out_specs=[row(hd), row(hd), row(2 * hd), head_major(IDX_HEADS, IDX_DIM), row(IDX_DIM),
                   pl.BlockSpec((1, IDX_HEADS, tm), lambda i: (i // spb, 0, i % spb))],
        out_shape=[jax.ShapeDtypeStruct((T, hd), BF16)] * 2 + [jax.ShapeDtypeStruct((T, 2 * hd), BF16)] + [
            jax.ShapeDtypeStruct((B, IDX_HEADS, S, IDX_DIM), BF16),
            jax.ShapeDtypeStruct((T, IDX_DIM), BF16),
            jax.ShapeDtypeStruct((B, IDX_HEADS, S), F32)],
        compiler_params=_params("parallel"),
    )(h_bf, lw["w_dsa_in"], c_d, sa_d, sb_d, c_i, sa_i, sb_i)

    ta = tl["mla"]
    pos_q = positions.reshape(B, S, 1)
    pos_k_mla = positions.reshape(B, S // ta, 1, ta)
    hp = tl["mla_heads"]
    o_a = pl.pallas_call(
        functools.partial(_mla_attn_kernel, tile=ta, heads=hp),
        grid=(B, MLA_HEADS // hp, S // ta),
        in_specs=[pl.BlockSpec((1, hp, ta, MLA_QK_DIM), lambda b, g, i: (b, g, i, 0)),
                  pl.BlockSpec((1, hp, S, MLA_QK_DIM), lambda b, g, i: (b, g, 0, 0)),
                  pl.BlockSpec((1, S, 2 * MLA_V_DIM * hp), lambda b, g, i: (b, 0, g)),
                  pl.BlockSpec((1, ta, 1), lambda b, g, i: (b, i, 0)),
                  pl.BlockSpec((1, S // ta, 1, ta), lambda b, g, i: (b, 0, 0, 0))],
        out_specs=pl.BlockSpec((1, ta, MLA_V_DIM * hp), lambda b, g, i: (b, i, g)),
        out_shape=jax.ShapeDtypeStruct((B, S, MLA_HEADS * MLA_V_DIM), BF16),
        scratch_shapes=[pltpu.VMEM((hp, ta, LANES), F32), pltpu.VMEM((hp, ta, 2 * MLA_V_DIM), F32)],
        compiler_params=_params("parallel", "parallel", "arbitrary"),
    )(q_a, k_a, v_a.reshape(B, S, -1), pos_q, pos_k_mla)

    tk = tl["key"]
    nkc = S // tk
    tqi = tl["idx_q"]
    top_k = min(DSA_MAX_TOPK, S // 4)
    sel_bias = pl.pallas_call(
        functools.partial(_idx_select_kernel, tq=tqi, tk=tk, nkc=nkc, top_k=top_k),
        grid=(B, S // tqi),
        in_specs=[pl.BlockSpec((1, IDX_HEADS, tqi, IDX_DIM), lambda b, i: (b, 0, i, 0)),
                  pl.BlockSpec((1, S, IDX_DIM), lambda b, i: (b, 0, 0)),
                  pl.BlockSpec((1, IDX_HEADS, tqi), lambda b, i: (b, 0, i)),
                  pl.BlockSpec((1, 1, tqi), lambda b, i: (b, 0, i)),
                  pl.BlockSpec((1, nkc, tk, 1), lambda b, i: (b, 0, 0, 0))],
        out_specs=pl.BlockSpec((1, nkc, tqi, tk), lambda b, i: (b, 0, i, 0)),
        out_shape=jax.ShapeDtypeStruct((B, nkc, S, tk), BF16),
        scratch_shapes=[pltpu.VMEM((nkc, tk, tqi), I32)],
        compiler_params=_params("parallel", "arbitrary"),
    )(q_idx, k_idx.reshape(B, S, IDX_DIM), w_idx, positions.reshape(B, 1, S), positions.reshape(B, nkc, tk, 1))

    tqd = tl["dsa_q"]
    o_b = pl.pallas_call(
        functools.partial(_dsa_attn_kernel, tq=tqd, tk=tk),
        grid=(B, S // tqd),
        in_specs=[pl.BlockSpec((1, tqd, hd), lambda b, i: (b, i, 0)),
                  pl.BlockSpec((1, S, hd), lambda b, i: (b, 0, 0), pipeline_mode=pl.Buffered(1)),
                  pl.BlockSpec((1, S, 2 * hd), lambda b, i: (b, 0, 0), pipeline_mode=pl.Buffered(1)),
                  pl.BlockSpec((1, nkc, tqd, tk), lambda b, i: (b, 0, i, 0))],
        out_specs=pl.BlockSpec((1, tqd, hd), lambda b, i: (b, i, 0)),
        out_shape=jax.ShapeDtypeStruct((B, S, hd), BF16),
        scratch_shapes=[pltpu.VMEM((DSA_HEADS, tqd, LANES), F32), pltpu.VMEM((DSA_HEADS, tqd, 2 * DSA_HEAD_DIM), F32)],
        compiler_params=_params("parallel", "arbitrary"),
    )(q_b.reshape(B, S, hd), k_b.reshape(B, S, hd), v_b.reshape(B, S, 2 * hd), sel_bias)

    y = pl.pallas_call(
        _merge_kernel,
        grid=(nrow,),
        in_specs=[row(D), row(MLA_HEADS * MLA_V_DIM), row(hd),
                  _resident((D, D)), _resident((D, D)), _resident((MLA_HEADS * MLA_V_DIM, D)), _resident((hd, D))],
        out_specs=row(D),
        out_shape=jax.ShapeDtypeStruct((T, D), BF16),
        compiler_params=_params("parallel"),
    )(h_bf, o_a.reshape(T, -1), o_b.reshape(T, hd), lw["w_gate_a"], lw["w_gate_b"], lw["w_o_a"], lw["w_o_b"])

    h1, logits = pl.pallas_call(
        functools.partial(_out_ln_kernel, alpha=alpha),
        grid=(nrow,),
        in_specs=[row(D), row(D), _resident((D, D)), _resident((1, D)), _resident((1, D)),
                  _resident((D, LANES)), _resident((1, LANES))],
        out_specs=[row(D), row(LANES)],
        out_shape=[jax.ShapeDtypeStruct((T, D), F32), jax.ShapeDtypeStruct((T, LANES), F32)],
        compiler_params=_params("parallel"),
    )(y, h, lw["w_out"], lw["ln1_g"], lw["ln1_b"], lw["w_route"], lw["b_route"])

    tr = tl["route"]
    meta, wgt, cnt = pl.pallas_call(
        functools.partial(_route_kernel, tr=tr),
        grid=(T // tr,),
        in_specs=[pl.BlockSpec((tr, LANES), lambda i: (i, 0))],
        out_specs=[pl.BlockSpec((tr, LANES), lambda i: (i, 0)), pl.BlockSpec((tr, LANES), lambda i: (i, 0)),
                   pl.BlockSpec((8, LANES), lambda i: (0, 0))],
        out_shape=[jax.ShapeDtypeStruct((T, LANES), I32), jax.ShapeDtypeStruct((T, LANES), F32),
                   jax.ShapeDtypeStruct((8, LANES), F32)],
        scratch_shapes=[pltpu.VMEM((1, LANES), F32)],
        compiler_params=_params("arbitrary"),
    )(logits)

    rb = MOE_ROW_BLOCK
    counts = cnt[0, N_GROUPS:N_GROUPS + N_EXPERTS].astype(I32)
    padded = ((counts + rb - 1) // rb) * rb
    pends = jnp.cumsum(padded)
    pstarts = pends - padded
    n_blocks = -(-(2 * T) // rb) + N_EXPERTS
    P = n_blocks * rb
    block_e = jnp.minimum(jnp.searchsorted(pends, jnp.arange(n_blocks, dtype=I32) * rb, side="right"),
                          N_EXPERTS - 1).astype(I32)
    n_used = (pends[-1] // rb).astype(I32).reshape(1)
    ps_lanes = jnp.zeros((1, LANES), I32).at[0, :N_EXPERTS].set(pstarts.astype(I32))
    dest = pl.pallas_call(
        _dest_kernel,
        grid=(T // tr,),
        in_specs=[pl.BlockSpec((tr, LANES), lambda i: (i, 0)), pl.BlockSpec((1, LANES), lambda i: (0, 0))],
        out_specs=pl.BlockSpec((tr, LANES), lambda i: (i, 0)),
        out_shape=jax.ShapeDtypeStruct((T, LANES), I32),
        compiler_params=_params("parallel"),
    )(meta, ps_lanes)[:, 0:2].reshape(-1)

    blk = jnp.arange(n_blocks, dtype=I32)
    last_of_expert = jnp.concatenate([block_e[1:] != block_e[:-1], jnp.ones((1,), bool)])
    needs_zero = jnp.logical_or(blk >= n_used[0], jnp.logical_or(last_of_expert, blk == n_used[0] - 1))
    zero_blocks = jnp.nonzero(needs_zero, size=n_blocks, fill_value=0)[0].astype(I32)
    n_zero = jnp.sum(needs_zero).astype(I32).reshape(1)
    tt = tl["moe_tok"]
    xs = pl.pallas_call(
        functools.partial(_dispatch_kernel, tt=tt),
        grid_spec=pltpu.PrefetchScalarGridSpec(
            num_scalar_prefetch=3, grid=(T // tt,),
            in_specs=[pl.BlockSpec((tt, D), lambda i, *_: (i, 0))],
            out_specs=pl.BlockSpec(memory_space=pl.ANY),
            scratch_shapes=[pltpu.VMEM((rb, D), F32), pltpu.SemaphoreType.DMA(()), pltpu.SemaphoreType.DMA(())]),
        out_shape=jax.ShapeDtypeStruct((P, D), F32),
        compiler_params=_params("arbitrary"),
    )(dest, zero_blocks, n_zero, h1)

    F = D_EXPERT
    grp = jnp.cumsum(jnp.concatenate([jnp.zeros((1,), I32), (block_e[1:] != block_e[:-1]).astype(I32)]))
    ids = jnp.where(counts > 0, jnp.arange(N_EXPERTS, dtype=I32), N_EXPERTS)
    at_or_after = lax.cummin(ids, axis=0, reverse=True)
    after = jnp.concatenate([at_or_after[1:], jnp.full((1,), N_EXPERTS, I32)])
    nxt = jnp.where(after < N_EXPERTS, after, -1)[block_e].astype(I32)
    any_spec = pl.BlockSpec(memory_space=pl.ANY)
    ys = pl.pallas_call(
        functools.partial(_experts_kernel, layer=layer),
        grid_spec=pltpu.PrefetchScalarGridSpec(
            num_scalar_prefetch=4, grid=(n_blocks,),
            in_specs=[pl.BlockSpec((rb, D), lambda i, be, gr, nx, nu: (jnp.minimum(i, nu[0] - 1), 0)),
                      any_spec, any_spec, any_spec],
            out_specs=pl.BlockSpec((rb, D), lambda i, *_: (i, 0)),
            scratch_shapes=[pltpu.VMEM((2, D, F), F32), pltpu.VMEM((2, D, F), F32), pltpu.VMEM((2, F, D), F32),
                            pltpu.VMEM((D, F), BF16), pltpu.VMEM((D, F), BF16), pltpu.VMEM((F, D), BF16),
                            pltpu.SemaphoreType.DMA((2, 3))]),
        out_shape=jax.ShapeDtypeStruct((P, D), F32),
        compiler_params=_params("arbitrary"),
    )(block_e, grp.astype(I32), nxt, n_used, xs, lw["w_e_gate"], lw["w_e_up"], lw["w_e_down"])

    tc = tl["comb"]
    h2, h2_bf = pl.pallas_call(
        functools.partial(_combine_kernel, tc=tc, alpha=alpha),
        grid_spec=pltpu.PrefetchScalarGridSpec(
            num_scalar_prefetch=1, grid=(T // tc,),
            in_specs=[pl.BlockSpec((tc, D), lambda i, d: (i, 0)),
                      pl.BlockSpec((tc, LANES), lambda i, d: (i, 0)),
                      pl.BlockSpec((1, D), lambda i, d: (0, 0)),
                      pl.BlockSpec((1, D), lambda i, d: (0, 0)),
                      pl.BlockSpec(memory_space=pl.ANY)],
            out_specs=[pl.BlockSpec((tc, D), lambda i, d: (i, 0)), pl.BlockSpec((tc, D), lambda i, d: (i, 0))],
            scratch_shapes=[pltpu.VMEM((2, 2 * tc, D), F32), pltpu.SemaphoreType.DMA((2,))]),
        out_shape=[jax.ShapeDtypeStruct((T, D), F32), jax.ShapeDtypeStruct((T, D), BF16)],
        compiler_params=_params("arbitrary"),
    )(dest, h1, wgt, lw["ln2_g"], lw["ln2_b"], ys)
    return h2, h2_bf


def _layer_weights(l, w_in, g_q_lora, w_q_b, g_kv_lora, w_kv_b, w_o_a, w_o_b, w_out, ln1_g, ln1_b,
                   w_group, b_group, w_router, b_router, w_e_gate, w_e_up, w_e_down, ln2_g, ln2_b):
    D = w_in.shape[1]
    wi = w_in[l]
    o_kpe = MLA_Q_LORA + MLA_KV_LORA
    o_dsa = o_kpe + MLA_ROPE_DIM
    hd = DSA_HEADS * DSA_HEAD_DIM
    o_kidx = o_dsa + 3 * hd + IDX_HEADS * IDX_DIM
    o_widx = o_kidx + IDX_DIM
    o_gate = o_widx + IDX_HEADS
    kpe = wi[:, o_kpe:o_dsa]
    kidx = wi[:, o_kidx:o_widx]
    w_mla_in = jnp.concatenate([wi[:, :o_kpe], kpe, kpe], axis=1).astype(BF16)
    w_dsa_in = jnp.concatenate([wi[:, o_dsa:o_kidx], kidx, kidx, wi[:, o_widx:o_gate],
                                jnp.zeros((D, LANES - IDX_HEADS), F32)], axis=1).astype(BF16)
    wq = w_q_b[l].reshape(MLA_Q_LORA, MLA_HEADS, MLA_QK_DIM)
    wq = jnp.concatenate([wq[:, :, :MLA_NOPE_DIM].reshape(MLA_Q_LORA, -1),
                          wq[:, :, MLA_NOPE_DIM:].reshape(MLA_Q_LORA, -1)], axis=1).astype(BF16)
    w_route = jnp.concatenate([w_group[l], w_router[l],
                               jnp.zeros((D, LANES - N_GROUPS - N_EXPERTS), F32)], axis=1)
    b_route = jnp.concatenate([b_group[l], b_router[l],
                               jnp.zeros((LANES - N_GROUPS - N_EXPERTS,), F32)]).reshape(1, LANES)
    return dict(
        w_mla_in=w_mla_in, w_dsa_in=w_dsa_in,
        w_gate_a=wi[:, o_gate:o_gate + D].astype(BF16), w_gate_b=wi[:, o_gate + D:].astype(BF16),
        g_q=g_q_lora[l].reshape(1, -1), g_kv=g_kv_lora[l].reshape(1, -1),
        w_q_b=wq, w_kv_b=w_kv_b[l].astype(BF16),
        w_o_a=w_o_a[l].astype(BF16), w_o_b=w_o_b[l].astype(BF16), w_out=w_out[l].astype(BF16),
        ln1_g=ln1_g[l].reshape(1, -1), ln1_b=ln1_b[l].reshape(1, -1),
        w_route=w_route, b_route=b_route,
        w_e_gate=w_e_gate, w_e_up=w_e_up, w_e_down=w_e_down,
        ln2_g=ln2_g[l].reshape(1, -1), ln2_b=ln2_b[l].reshape(1, -1),
    )


def kernel(x, positions, w_in, g_q_lora, w_q_b, g_kv_lora, w_kv_b, w_o_a, w_o_b, w_out, ln1_g, ln1_b,
           w_group, b_group, w_router, b_router, w_e_gate, w_e_up, w_e_down, ln2_g, ln2_b):
    B, S, D = x.shape
    depth = w_in.shape[0]
    alpha = (2 * depth) ** 0.25
    tabs = (_rope_tables(positions, MLA_ROPE_DIM, MLA_ROPE_DIM),
            _rope_tables(positions, DSA_ROPE_DIM, DSA_HEAD_DIM),
            _rope_tables(positions, IDX_ROPE_DIM, IDX_DIM))
    h = x.reshape(B * S, D)
    h_bf = h.astype(BF16)
    for l in range(depth):
        lw = _layer_weights(l, w_in, g_q_lora, w_q_b, g_kv_lora, w_kv_b, w_o_a, w_o_b, w_out, ln1_g, ln1_b,
                            w_group, b_group, w_router, b_router, w_e_gate, w_e_up, w_e_down, ln2_g, ln2_b)
        h, h_bf = _layer(l, h, h_bf, positions, tabs, lw, alpha, B, S)
    return h.reshape(B, S, D)
```

```python
import functools
import math

import jax
import jax.numpy as jnp
from jax import lax
from jax.experimental import pallas as pl
from jax.experimental.pallas import tpu as pltpu

F32 = jnp.float32
BF16 = jnp.bfloat16
I32 = jnp.int32

MLA_HEADS = 8
MLA_Q_LORA = 512
MLA_KV_LORA = 512
MLA_NOPE_DIM = 128
MLA_ROPE_DIM = 64
MLA_V_DIM = 128
MLA_QK_DIM = MLA_NOPE_DIM + MLA_ROPE_DIM
DSA_HEADS = 8
DSA_HEAD_DIM = 128
DSA_ROPE_DIM = DSA_HEAD_DIM // 4
IDX_HEADS = 16
IDX_DIM = 64
IDX_ROPE_DIM = IDX_DIM // 4
DSA_MAX_TOPK = 256
ROPE_THETA = 500000.0
N_GROUPS = 8
EXPERTS_PER_GROUP = 8
N_EXPERTS = N_GROUPS * EXPERTS_PER_GROUP
D_EXPERT = 512
MOE_ROW_BLOCK = 128
LN_EPS = 1e-5
RMS_EPS = 1e-6

LANES = 128
NEG_BIG = -1e30
INT_MIN = -(2 ** 31)
VMEM_LIMIT = 56 * 1024 * 1024
WEIGHT_DMA_CHUNKS = 4


def _params(*sem):
    return pltpu.CompilerParams(dimension_semantics=sem, vmem_limit_bytes=VMEM_LIMIT)


def _resident(shape):
    nd = len(shape)
    return pl.BlockSpec(shape, lambda *_: (0,) * nd, pipeline_mode=pl.Buffered(1))


def _dot(a, b):
    return jnp.dot(a, b, preferred_element_type=F32)


def _dot_nt(a, b):
    return lax.dot_general(a, b, (((1,), (1,)), ((), ())), preferred_element_type=F32)


def _rope128(x, c, sa, sb, half):
    return x * c + pltpu.roll(x, LANES - half, 1) * sa + pltpu.roll(x, half, 1) * sb


def _rms(x, g):
    return x * lax.rsqrt(jnp.mean(x * x, axis=-1, keepdims=True) + RMS_EPS) * g


def _layer_norm(x, g, b):
    mu = jnp.mean(x, axis=-1, keepdims=True)
    xc = x - mu
    var = jnp.mean(xc * xc, axis=-1, keepdims=True)
    return xc * lax.rsqrt(var + LN_EPS) * g + b


def _mla_prep_kernel(h_ref, win_ref, gq_ref, gkv_ref, wqb_ref, wkvb_ref, c_ref, sa_ref, sb_ref,
                     q_ref, k_ref, v_ref, *, q_scale):
    h = h_ref[...]
    p = _dot(h, win_ref[...])
    qn = _rms(p[:, :MLA_Q_LORA], gq_ref[...]).astype(BF16)
    kvn = _rms(p[:, MLA_Q_LORA:MLA_Q_LORA + MLA_KV_LORA], gkv_ref[...]).astype(BF16)
    q = _dot(qn, wqb_ref[...]) * q_scale
    kv = _dot(kvn, wkvb_ref[...])
    c, sa, sb = c_ref[...], sa_ref[...], sb_ref[...]
    half = MLA_ROPE_DIM // 2
    kpe = _rope128(p[:, MLA_Q_LORA + MLA_KV_LORA:], c, sa, sb, half)[:, :MLA_ROPE_DIM].astype(BF16)
    pe0 = MLA_HEADS * MLA_NOPE_DIM
    for hh in range(MLA_HEADS):
        q_ref[0, hh, :, 0:MLA_NOPE_DIM] = q[:, 128 * hh:128 * hh + 128].astype(BF16)
        k_ref[0, hh, :, 0:MLA_NOPE_DIM] = kv[:, 256 * hh:256 * hh + 128].astype(BF16)
        k_ref[0, hh, :, MLA_NOPE_DIM:MLA_QK_DIM] = kpe
        v_ref[:, 256 * hh:256 * hh + 128] = kv[:, 256 * hh + 128:256 * hh + 256].astype(BF16)
        v_ref[:, 256 * hh + 128:256 * hh + 256] = jnp.ones((kv.shape[0], LANES), BF16)
    for s in range(MLA_HEADS // 2):
        slab = _rope128(q[:, pe0 + 128 * s:pe0 + 128 * s + 128], c, sa, sb, half).astype(BF16)
        q_ref[0, 2 * s, :, MLA_NOPE_DIM:MLA_QK_DIM] = slab[:, :MLA_ROPE_DIM]
        q_ref[0, 2 * s + 1, :, MLA_NOPE_DIM:MLA_QK_DIM] = slab[:, MLA_ROPE_DIM:]


def _dsa_prep_kernel(h_ref, w_ref, cd_ref, sad_ref, sbd_ref, ci_ref, sai_ref, sbi_ref,
                     qb_ref, kb_ref, vb_ref, qi_ref, ki_ref, wi_ref, *, w_idx_scale, q_scale):
    h = h_ref[...]
    hd = DSA_HEADS * DSA_HEAD_DIM
    cd, sad, sbd = cd_ref[...], sad_ref[...], sbd_ref[...]
    ci, sai, sbi = ci_ref[...], sai_ref[...], sbi_ref[...]
    hd_half, hi_half = DSA_ROPE_DIM // 2, IDX_ROPE_DIM // 2
    q = _dot(h, w_ref[:, 0:hd]) * q_scale
    for hh in range(DSA_HEADS):
        qb_ref[:, 128 * hh:128 * hh + 128] = _rope128(q[:, 128 * hh:128 * hh + 128], cd, sad, sbd, hd_half).astype(BF16)
    k = _dot(h, w_ref[:, hd:2 * hd])
    for hh in range(DSA_HEADS):
        kb_ref[:, 128 * hh:128 * hh + 128] = _rope128(k[:, 128 * hh:128 * hh + 128], cd, sad, sbd, hd_half).astype(BF16)
    v = _dot(h, w_ref[:, 2 * hd:3 * hd]).astype(BF16)
    for hh in range(DSA_HEADS):
        vb_ref[:, 256 * hh:256 * hh + 128] = v[:, 128 * hh:128 * hh + 128]
        vb_ref[:, 256 * hh + 128:256 * hh + 256] = jnp.ones((v.shape[0], LANES), BF16)
    qi = _dot(h, w_ref[:, 3 * hd:3 * hd + IDX_HEADS * IDX_DIM])
    for s in range(IDX_HEADS // 2):
        slab = _rope128(qi[:, 128 * s:128 * s + 128], ci, sai, sbi, hi_half).astype(BF16)
        qi_ref[0, 2 * s] = slab[:, :IDX_DIM]
        qi_ref[0, 2 * s + 1] = slab[:, IDX_DIM:]
    o = 3 * hd + IDX_HEADS * IDX_DIM
    last = _dot(h, w_ref[:, o:o + 2 * LANES])
    ki_ref[...] = _rope128(last[:, :LANES], ci, sai, sbi, hi_half)[:, :IDX_DIM].astype(BF16)
    wi_ref[0] = (last[:, LANES:] * w_idx_scale).T[:IDX_HEADS, :]


def _flash_update(s, v_ones, m_ref, acc_ref):
    tk = s.shape[1]
    m_prev = m_ref[...]
    m_next = jnp.maximum(m_prev, jnp.max(s, axis=1, keepdims=True))
    p = jnp.exp2(s - jnp.tile(m_next, (1, tk // LANES)))
    alpha = jnp.exp2(m_prev - m_next)
    acc_ref[...] = acc_ref[...] * jnp.tile(alpha, (1, 2)) + _dot(p.astype(BF16), v_ones)
    m_ref[...] = m_next


def _mla_attn_kernel(q_ref, k_ref, v_ref, pq_ref, pk_ref, o_ref, m_sc, acc_sc, *, tile, heads):
    qi = pl.program_id(2)
    m_sc[...] = jnp.full(m_sc.shape, -jnp.inf, F32)
    acc_sc[...] = jnp.zeros(acc_sc.shape, F32)

    def chunk(c, masked):
        ks = pl.ds(pl.multiple_of(c * tile, tile), tile)
        if masked:
            visible = pk_ref[0, c] <= pq_ref[0]
        for hh in range(heads):
            s = _dot_nt(q_ref[0, hh], k_ref[0, hh, ks, :])
            if masked:
                s = jnp.where(visible, s, NEG_BIG)
            _flash_update(s, v_ref[0, ks, 2 * MLA_V_DIM * hh:2 * MLA_V_DIM * (hh + 1)], m_sc.at[hh], acc_sc.at[hh])

    def body(c, carry):
        chunk(c, False)
        return carry

    lax.fori_loop(0, qi, body, 0)
    chunk(qi, True)
    for hh in range(heads):
        o_ref[0, :, MLA_V_DIM * hh:MLA_V_DIM * (hh + 1)] = (
            acc_sc[hh, :, :MLA_V_DIM] / acc_sc[hh, :, MLA_V_DIM:]).astype(o_ref.dtype)


def _idx_select_kernel(q_ref, k_ref, wt_ref, pq_ref, pk_ref, o_ref, key_sc, *, tq, tk, nkc, top_k):
    qi = pl.program_id(1)
    nk = (qi * tq + tq + tk - 1) // tk
    wt = wt_ref[0]
    pq = pq_ref[0]

    def score_chunk(c, carry):
        kc = k_ref[0, pl.ds(pl.multiple_of(c * tk, tk), tk), :]
        acc = jnp.zeros((tk, tq), F32)
        for hh in range(IDX_HEADS):
            acc = acc + wt[hh:hh + 1, :] * jnp.maximum(_dot_nt(kc, q_ref[0, hh]), 0.0)
        bits = pltpu.bitcast(acc, I32)
        key = jnp.where(bits < 0, bits ^ jnp.int32(0x7FFFFFFF), bits)
        key = jnp.where(acc == 0.0, 0, key)
        key_sc[c] = jnp.where(pk_ref[0, c] <= pq, key, INT_MIN)
        return carry

    lax.fori_loop(0, nk, score_chunk, 0)

    def count(pred):
        def body(c, part):
            m = pred(key_sc[c], c).astype(I32)
            return part + jnp.sum(m.reshape(tk // 8, 8, tq), axis=0)
        part = lax.fori_loop(0, nk, body, jnp.zeros((8, tq), I32))
        return jnp.sum(part, axis=0, keepdims=True)

    def bit_step(i, base):
        cand = base ^ (jnp.int32(1) << (31 - i))
        cnt = count(lambda key, c: key >= cand)
        return jnp.where(cnt >= top_k, cand, base)

    thr = lax.fori_loop(0, 32, bit_step, jnp.full((1, tq), INT_MIN, I32))
    n_gt = count(lambda key, c: key > thr)
    n_eq = count(lambda key, c: key == thr)
    need = top_k - n_gt
    tie = jnp.logical_and(n_eq > need, thr != INT_MIN)

    def key_index(c):
        return c * tk + lax.broadcasted_iota(I32, (tk, tq), 0)

    def tie_search():
        def step(i, p):
            cand = p | (jnp.int32(1) << (14 - i))
            cnt = count(lambda key, c: jnp.logical_and(key == thr, key_index(c) < cand))
            return jnp.where(cnt <= need, cand, p)
        return lax.fori_loop(0, 15, step, jnp.zeros((1, tq), I32))

    p_cut = lax.cond(jnp.max(tie.astype(I32)) > 0, tie_search,
                     lambda: jnp.full((1, tq), 2 ** 30, I32))

    def out_chunk(c, carry):
        key = key_sc[c]
        sel = jnp.logical_or(key > thr, jnp.logical_and(key == thr, key_index(c) < p_cut))
        sel = jnp.logical_and(sel, key != INT_MIN)
        o_ref[0, c] = jnp.where(sel, 0.0, NEG_BIG).T.astype(o_ref.dtype)
        return carry

    lax.fori_loop(0, nk, out_chunk, 0)

    def fill_chunk(c, carry):
        o_ref[0, c] = jnp.full((tq, tk), NEG_BIG, o_ref.dtype)
        return carry

    lax.fori_loop(nk, nkc, fill_chunk, 0)


def _dsa_attn_kernel(q_ref, k_ref, v_ref, b_ref, o_ref, m_sc, acc_sc, *, tq, tk):
    qi = pl.program_id(1)
    nk = (qi * tq + tq + tk - 1) // tk
    m_sc[...] = jnp.full(m_sc.shape, NEG_BIG, F32)
    acc_sc[...] = jnp.zeros(acc_sc.shape, F32)

    def chunk(c, carry):
        bias = b_ref[0, c].astype(F32)
        ks = pl.ds(pl.multiple_of(c * tk, tk), tk)
        for hh in range(DSA_HEADS):
            hs = slice(DSA_HEAD_DIM * hh, DSA_HEAD_DIM * (hh + 1))
            vs = slice(2 * DSA_HEAD_DIM * hh, 2 * DSA_HEAD_DIM * (hh + 1))
            s = _dot_nt(q_ref[0, :, hs], k_ref[0, ks, hs]) + bias
            _flash_update(s, v_ref[0, ks, vs], m_sc.at[hh], acc_sc.at[hh])
        return carry

    lax.fori_loop(0, nk, chunk, 0)
    for hh in range(DSA_HEADS):
        hs = slice(DSA_HEAD_DIM * hh, DSA_HEAD_DIM * (hh + 1))
        o_ref[0, :, hs] = (acc_sc[hh, :, :DSA_HEAD_DIM] / acc_sc[hh, :, DSA_HEAD_DIM:]).astype(o_ref.dtype)


def _merge_kernel(h_ref, oa_ref, ob_ref, wga_ref, wgb_ref, woa_ref, wob_ref, y_ref):
    h = h_ref[...]
    ya = jax.nn.sigmoid(_dot(h, wga_ref[...])) * _dot(oa_ref[...], woa_ref[...])
    yb = jax.nn.sigmoid(_dot(h, wgb_ref[...])) * _dot(ob_ref[...], wob_ref[...])
    y_ref[...] = (ya + yb).astype(y_ref.dtype)


def _out_ln_kernel(y_ref, h_ref, wout_ref, g_ref, b_ref, wr_ref, br_ref, o_ref, lg_ref, *, alpha):
    sub = 128
    for r in range(y_ref.shape[0] // sub):
        rows = slice(r * sub, (r + 1) * sub)
        mix = _dot(y_ref[rows, :], wout_ref[...])
        h1 = _layer_norm(alpha * h_ref[rows, :] + mix, g_ref[...], b_ref[...])
        o_ref[rows, :] = h1
        lg_ref[rows, :] = jnp.dot(h1, wr_ref[...], preferred_element_type=F32,
                                  precision=lax.Precision.HIGHEST) + br_ref[...]


def _route_kernel(lg_ref, meta_ref, wgt_ref, cnt_ref, carry_sc, *, tr):
    @pl.when(pl.program_id(0) == 0)
    def _():
        carry_sc[...] = jnp.zeros(carry_sc.shape, F32)

    lg = lg_ref[...]
    lane = lax.broadcasted_iota(I32, (tr, LANES), 1)
    gl = jnp.where(lane < N_GROUPS, lg, -jnp.inf)
    gmax = jnp.max(gl, axis=1, keepdims=True)
    g_idx = jnp.min(jnp.where(gl == gmax, lane, LANES), axis=1, keepdims=True)
    g_p = 1.0 / jnp.sum(jnp.exp(gl - gmax), axis=1, keepdims=True)
    lo = N_GROUPS + EXPERTS_PER_GROUP * g_idx
    in_group = jnp.logical_and(lane >= lo, lane < lo + EXPERTS_PER_GROUP)
    el = jnp.where(in_group, lg, -jnp.inf)
    emax = jnp.max(el, axis=1, keepdims=True)
    e1 = jnp.min(jnp.where(el == emax, lane, LANES), axis=1, keepdims=True)
    den = jnp.sum(jnp.exp(el - emax), axis=1, keepdims=True)
    el2 = jnp.where(lane == e1, -jnp.inf, el)
    emax2 = jnp.max(el2, axis=1, keepdims=True)
    e2 = jnp.min(jnp.where(el2 == emax2, lane, LANES), axis=1, keepdims=True)
    p1 = 1.0 / den
    p2 = jnp.exp(emax2 - emax) / den
    w1 = g_p * (p1 / (p1 + p2))
    w2 = g_p * (p2 / (p1 + p2))

    is1 = lane == e1
    is2 = lane == e2
    onehot = jnp.logical_or(is1, is2).astype(BF16)
    r = lax.broadcasted_iota(I32, (tr, tr), 0)
    cidx = lax.broadcasted_iota(I32, (tr, tr), 1)
    lower = (cidx < r).astype(BF16)
    before = _dot(lower, onehot) + carry_sc[...]
    rank1 = jnp.sum(jnp.where(is1, before, 0.0), axis=1, keepdims=True)
    rank2 = jnp.sum(jnp.where(is2, before, 0.0), axis=1, keepdims=True)
    carry_sc[...] = carry_sc[...] + jnp.sum(onehot.astype(F32), axis=0, keepdims=True)

    meta = jnp.where(lane == 0, e1 - N_GROUPS,
                     jnp.where(lane == 1, e2 - N_GROUPS,
                               jnp.where(lane == 2, rank1.astype(I32),
                                         jnp.where(lane == 3, rank2.astype(I32), 0))))
    meta_ref[...] = meta
    wgt_ref[...] = jnp.where(lane == 0, w1, jnp.where(lane == 1, w2, 0.0))
    cnt_ref[...] = jnp.broadcast_to(carry_sc[...], cnt_ref.shape)


def _row_copy(src, src_row, dst, dst_row, sem):
    return pltpu.make_async_copy(src.at[pl.ds(src_row, 1)], dst.at[pl.ds(dst_row, 1)], sem)


def _dest_kernel(meta_ref, ps_ref, o_ref):
    meta = meta_ref[...]
    ps = ps_ref[...]
    lane = lax.broadcasted_iota(I32, meta.shape, 1)
    d1 = jnp.sum(jnp.where(lane == meta[:, 0:1], ps, 0), axis=1, keepdims=True) + meta[:, 2:3]
    d2 = jnp.sum(jnp.where(lane == meta[:, 1:2], ps, 0), axis=1, keepdims=True) + meta[:, 3:4]
    o_ref[...] = jnp.where(lane == 0, d1, jnp.where(lane == 1, d2, 0))


def _dispatch_kernel(dest_ref, zblk_ref, nz_ref, h_ref, xs_out, zero_sc, sem, zsem, *, tt):
    base = pl.program_id(0) * tt
    rb = zero_sc.shape[0]

    @pl.when(pl.program_id(0) == 0)
    def _():
        zero_sc[...] = jnp.zeros(zero_sc.shape, zero_sc.dtype)

        def fill(j, carry):
            @pl.when(zblk_ref[j] != 0)
            def _():
                pltpu.make_async_copy(zero_sc, xs_out.at[pl.ds(pl.multiple_of(j * rb, rb), rb)], zsem).start()
            return carry

        def drain(j, carry):
            pltpu.make_async_copy(zero_sc, xs_out.at[pl.ds(0, rb)], zsem).wait()
            return carry

        lax.fori_loop(0, zblk_ref.shape[0], fill, 0)
        lax.fori_loop(0, nz_ref[0], drain, 0)

    def issue(t, carry):
        tok = base + t
        _row_copy(h_ref, t, xs_out, dest_ref[2 * tok], sem).start()
        _row_copy(h_ref, t, xs_out, dest_ref[2 * tok + 1], sem).start()
        return carry

    lax.fori_loop(0, tt, issue, 0, unroll=8)
    for _ in range(2):
        pltpu.make_async_copy(h_ref, xs_out.at[pl.ds(0, tt)], sem).wait()


def _experts_kernel(be_ref, grp_ref, nxt_ref, nu_ref, xs_ref, wg_hbm, wu_hbm, wd_hbm, o_ref,
                    wg_buf, wu_buf, wd_buf, wg_sc, wu_sc, wd_sc, sem, *, layer):
    i = pl.program_id(0)
    e = be_ref[i]
    used = i < nu_ref[0]
    first = jnp.logical_and(used, jnp.logical_or(i == 0, e != be_ref[jnp.maximum(i - 1, 0)]))
    slot = grp_ref[i] % 2

    def copies(expert, s):
        out = []
        for j, (hbm, buf) in enumerate(((wg_hbm, wg_buf), (wu_hbm, wu_buf), (wd_hbm, wd_buf))):
            rows = buf.shape[1] // WEIGHT_DMA_CHUNKS
            for c in range(WEIGHT_DMA_CHUNKS):
                rs = pl.ds(c * rows, rows)
                out.append(pltpu.make_async_copy(hbm.at[layer, expert, rs], buf.at[s, rs], sem.at[s, j]))
        return out

    @pl.when(i == 0)
    def _():
        for cp in copies(e, slot):
            cp.start()

    @pl.when(first)
    def _():
        for cp in copies(e, slot):
            cp.wait()
        nxt = nxt_ref[i]

        @pl.when(nxt >= 0)
        def _():
            for cp in copies(nxt, 1 - slot):
                cp.start()

        wg_sc[...] = wg_buf[slot].astype(BF16)
        wu_sc[...] = wu_buf[slot].astype(BF16)
        wd_sc[...] = wd_buf[slot].astype(BF16)

    @pl.when(used)
    def _():
        x = xs_ref[...].astype(BF16)
        hid = jax.nn.silu(_dot(x, wg_sc[...])) * _dot(x, wu_sc[...])
        o_ref[...] = _dot(hid.astype(BF16), wd_sc[...])

    @pl.when(jnp.logical_not(used))
    def _():
        o_ref[...] = jnp.zeros(o_ref.shape, o_ref.dtype)


def _combine_kernel(dest_ref, h_ref, wgt_ref, g_ref, b_ref, ys_hbm, o_ref, ob_ref, ybuf, sem, *, tc, alpha):
    i = pl.program_id(0)

    def gather(tile, slot):
        def issue(t, carry):
            tok = tile * tc + t
            _row_copy(ys_hbm, dest_ref[2 * tok], ybuf.at[slot], t, sem.at[slot]).start()
            _row_copy(ys_hbm, dest_ref[2 * tok + 1], ybuf.at[slot], tc + t, sem.at[slot]).start()
            return carry
        lax.fori_loop(0, tc, issue, 0, unroll=8)

    @pl.when(i == 0)
    def _():
        gather(0, 0)

    @pl.when(i + 1 < pl.num_programs(0))
    def _():
        gather(i + 1, (i + 1) % 2)

    slot = i % 2
    pltpu.make_async_copy(ys_hbm.at[pl.ds(0, 2 * tc)], ybuf.at[slot], sem.at[slot]).wait()
    wgt = wgt_ref[...]
    ff = ybuf[slot, 0:tc, :] * wgt[:, 0:1] + ybuf[slot, tc:2 * tc, :] * wgt[:, 1:2]
    h2 = _layer_norm(alpha * h_ref[...] + ff, g_ref[...], b_ref[...])
    o_ref[...] = h2
    ob_ref[...] = h2.astype(BF16)


def _tiles(seq):
    big = seq >= 2048
    return dict(
        rows=256 if big else 128,
        mla=512 if big else 128,
        mla_heads=8,
        idx_q=256 if big else 64,
        dsa_q=512 if big else 128,
        key=512 if big else 128,
        route=512 if big else 128,
        moe_tok=512 if big else 128,
        comb=128,
    )


def _rope_tables(positions, rot_dim, head_dim):
    half = rot_dim // 2
    inv = ROPE_THETA ** (-jnp.arange(0, rot_dim, 2, dtype=F32) / rot_dim)
    ang = positions.astype(F32).reshape(-1, 1) * inv
    cos, sin = jnp.cos(ang), jnp.sin(ang)
    t = cos.shape[0]
    rest = head_dim - rot_dim
    ones, zeros, zh = jnp.ones((t, rest), F32), jnp.zeros((t, rest), F32), jnp.zeros((t, half), F32)
    rep = LANES // head_dim
    c = jnp.tile(jnp.concatenate([cos, cos, ones], axis=1), (1, rep))
    sa = jnp.tile(jnp.concatenate([-sin, zh, zeros], axis=1), (1, rep))
    sb = jnp.tile(jnp.concatenate([zh, sin, zeros], axis=1), (1, rep))
    return c, sa, sb


def _layer(layer, h, h_bf, positions, tabs, lw, alpha, B, S):
    T, D = h.shape
    tl = _tiles(S)
    tm = tl["rows"]
    nrow = T // tm
    spb = S // tm
    (c_m, sa_m, sb_m), (c_d, sa_d, sb_d), (c_i, sa_i, sb_i) = tabs
    row = lambda w: pl.BlockSpec((tm, w), lambda i: (i, 0))
    head_major = lambda nh, w: pl.BlockSpec((1, nh, tm, w), lambda i: (i // spb, 0, i % spb, 0))

    log2e = math.log2(math.e)
    q_a, k_a, v_a = pl.pallas_call(
        functools.partial(_mla_prep_kernel, q_scale=log2e * MLA_QK_DIM ** -0.5),
        grid=(nrow,),
        in_specs=[row(D), _resident(lw["w_mla_in"].shape), _resident((1, MLA_Q_LORA)), _resident((1, MLA_KV_LORA)),
                  _resident(lw["w_q_b"].shape), _resident(lw["w_kv_b"].shape), row(LANES), row(LANES), row(LANES)],
        out_specs=[head_major(MLA_HEADS, MLA_QK_DIM), head_major(MLA_HEADS, MLA_QK_DIM), row(2 * MLA_HEADS * MLA_V_DIM)],
        out_shape=[jax.ShapeDtypeStruct((B, MLA_HEADS, S, MLA_QK_DIM), BF16),
                   jax.ShapeDtypeStruct((B, MLA_HEADS, S, MLA_QK_DIM), BF16),
                   jax.ShapeDtypeStruct((T, 2 * MLA_HEADS * MLA_V_DIM), BF16)],
        compiler_params=_params("parallel"),
    )(h_bf, lw["w_mla_in"], lw["g_q"], lw["g_kv"], lw["w_q_b"], lw["w_kv_b"], c_m, sa_m, sb_m)

    hd = DSA_HEADS * DSA_HEAD_DIM
    q_b, k_b, v_b, q_idx, k_idx, w_idx = pl.pallas_call(
        functools.partial(_dsa_prep_kernel, w_idx_scale=(IDX_HEADS ** -0.5) * (IDX_DIM ** -0.5),
                          q_scale=log2e * DSA_HEAD_DIM ** -0.5),
        grid=(nrow,),
        in_specs=[row(D), _resident(lw["w_dsa_in"].shape)] + [row(LANES)] * 6,
        out_specs=[row(hd), row(hd), row(2 * hd), head_major(IDX_HEADS, IDX_DIM), row(IDX_DIM),
                   pl.BlockSpec((1, IDX_HEADS, tm), lambda i: (i // spb, 0, i % spb))],
        out_shape=[jax.ShapeDtypeStruct((T, hd), BF16)] * 2 + [jax.ShapeDtypeStruct((T, 2 * hd), BF16)] + [
            jax.ShapeDtypeStruct((B, IDX_HEADS, S, IDX_DIM), BF16),
            jax.ShapeDtypeStruct((T, IDX_DIM), BF16),
            jax.ShapeDtypeStruct((B, IDX_HEADS, S), F32)],
        compiler_params=_params("parallel"),
    )(h_bf, lw["w_dsa_in"], c_d, sa_d, sb_d, c_i, sa_i, sb_i)

    ta = tl["mla"]
    pos_q = positions.reshape(B, S, 1)
    pos_k_mla = positions.reshape(B, S // ta, 1, ta)
    hp = tl["mla_heads"]
    o_a = pl.pallas_call(
        functools.partial(_mla_attn_kernel, tile=ta, heads=hp),
        grid=(B, MLA_HEADS // hp, S // ta),
        in_specs=[pl.BlockSpec((1, hp, ta, MLA_QK_DIM), lambda b, g, i: (b, g, i, 0)),
                  pl.BlockSpec((1, hp, S, MLA_QK_DIM), lambda b, g, i: (b, g, 0, 0), pipeline_mode=pl.Buffered(1)),
                  pl.BlockSpec((1, S, 2 * MLA_V_DIM * hp), lambda b, g, i: (b, 0, g), pipeline_mode=pl.Buffered(1)),
                  pl.BlockSpec((1, ta, 1), lambda b, g, i: (b, i, 0)),
                  pl.BlockSpec((1, S // ta, 1, ta), lambda b, g, i: (b, 0, 0, 0))],
        out_specs=pl.BlockSpec((1, ta, MLA_V_DIM * hp), lambda b, g, i: (b, i, g)),
        out_shape=jax.ShapeDtypeStruct((B, S, MLA_HEADS * MLA_V_DIM), BF16),
        scratch_shapes=[pltpu.VMEM((hp, ta, LANES), F32), pltpu.VMEM((hp, ta, 2 * MLA_V_DIM), F32)],
        compiler_params=_params("parallel", "parallel", "arbitrary"),
    )(q_a, k_a, v_a.reshape(B, S, -1), pos_q, pos_k_mla)

    tk = tl["key"]
    nkc = S // tk
    tqi = tl["idx_q"]
    top_k = min(DSA_MAX_TOPK, S // 4)
    sel_bias = pl.pallas_call(
        functools.partial(_idx_select_kernel, tq=tqi, tk=tk, nkc=nkc, top_k=top_k),
        grid=(B, S // tqi),
        in_specs=[pl.BlockSpec((1, IDX_HEADS, tqi, IDX_DIM), lambda b, i: (b, 0, i, 0)),
                  pl.BlockSpec((1, S, IDX_DIM), lambda b, i: (b, 0, 0)),
                  pl.BlockSpec((1, IDX_HEADS, tqi), lambda b, i: (b, 0, i)),
                  pl.BlockSpec((1, 1, tqi), lambda b, i: (b, 0, i)),
                  pl.BlockSpec((1, nkc, tk, 1), lambda b, i: (b, 0, 0, 0))],
        out_specs=pl.BlockSpec((1, nkc, tqi, tk), lambda b, i: (b, 0, i, 0)),
        out_shape=jax.ShapeDtypeStruct((B, nkc, S, tk), BF16),
        scratch_shapes=[pltpu.VMEM((nkc, tk, tqi), I32)],
        compiler_params=_params("parallel", "arbitrary"),
    )(q_idx, k_idx.reshape(B, S, IDX_DIM), w_idx, positions.reshape(B, 1, S), positions.reshape(B, nkc, tk, 1))

    tqd = tl["dsa_q"]
    o_b = pl.pallas_call(
        functools.partial(_dsa_attn_kernel, tq=tqd, tk=tk),
        grid=(B, S // tqd),
        in_specs=[pl.BlockSpec((1, tqd, hd), lambda b, i: (b, i, 0)),
                  pl.BlockSpec((1, S, hd), lambda b, i: (b, 0, 0), pipeline_mode=pl.Buffered(1)),
                  pl.BlockSpec((1, S, 2 * hd), lambda b, i: (b, 0, 0), pipeline_mode=pl.Buffered(1)),
                  pl.BlockSpec((1, nkc, tqd, tk), lambda b, i: (b, 0, i, 0))],
        out_specs=pl.BlockSpec((1, tqd, hd), lambda b, i: (b, i, 0)),
        out_shape=jax.ShapeDtypeStruct((B, S, hd), BF16),
        scratch_shapes=[pltpu.VMEM((DSA_HEADS, tqd, LANES), F32), pltpu.VMEM((DSA_HEADS, tqd, 2 * DSA_HEAD_DIM), F32)],
        compiler_params=_params("parallel", "arbitrary"),
    )(q_b.reshape(B, S, hd), k_b.reshape(B, S, hd), v_b.reshape(B, S, 2 * hd), sel_bias)

    y = pl.pallas_call(
        _merge_kernel,
        grid=(nrow,),
        in_specs=[row(D), row(MLA_HEADS * MLA_V_DIM), row(hd),
                  _resident((D, D)), _resident((D, D)), _resident((MLA_HEADS * MLA_V_DIM, D)), _resident((hd, D))],
        out_specs=row(D),
        out_shape=jax.ShapeDtypeStruct((T, D), BF16),
        compiler_params=_params("parallel"),
    )(h_bf, o_a.reshape(T, -1), o_b.reshape(T, hd), lw["w_gate_a"], lw["w_gate_b"], lw["w_o_a"], lw["w_o_b"])

    h1, logits = pl.pallas_call(
        functools.partial(_out_ln_kernel, alpha=alpha),
        grid=(nrow,),
        in_specs=[row(D), row(D), _resident((D, D)), _resident((1, D)), _resident((1, D)),
                  _resident((D, LANES)), _resident((1, LANES))],
        out_specs=[row(D), row(LANES)],
        out_shape=[jax.ShapeDtypeStruct((T, D), F32), jax.ShapeDtypeStruct((T, LANES), F32)],
        compiler_params=_params("parallel"),
    )(y, h, lw["w_out"], lw["ln1_g"], lw["ln1_b"], lw["w_route"], lw["b_route"])

    tr = tl["route"]
    meta, wgt, cnt = pl.pallas_call(
        functools.partial(_route_kernel, tr=tr),
        grid=(T // tr,),
        in_specs=[pl.BlockSpec((tr, LANES), lambda i: (i, 0))],
        out_specs=[pl.BlockSpec((tr, LANES), lambda i: (i, 0)), pl.BlockSpec((tr, LANES), lambda i: (i, 0)),
                   pl.BlockSpec((8, LANES), lambda i: (0, 0))],
        out_shape=[jax.ShapeDtypeStruct((T, LANES), I32), jax.ShapeDtypeStruct((T, LANES), F32),
                   jax.ShapeDtypeStruct((8, LANES), F32)],
        scratch_shapes=[pltpu.VMEM((1, LANES), F32)],
        compiler_params=_params("arbitrary"),
    )(logits)

    rb = MOE_ROW_BLOCK
    counts = cnt[0, N_GROUPS:N_GROUPS + N_EXPERTS].astype(I32)
    padded = ((counts + rb - 1) // rb) * rb
    pends = jnp.cumsum(padded)
    pstarts = pends - padded
    n_blocks = -(-(2 * T) // rb) + N_EXPERTS
    P = n_blocks * rb
    blk = jnp.arange(n_blocks, dtype=I32)
    block_e = jnp.minimum(jnp.sum((pends[None, :] <= (blk * rb)[:, None]).astype(I32), axis=1), N_EXPERTS - 1)
    n_used = (pends[-1] // rb).astype(I32).reshape(1)
    ps_lanes = jnp.zeros((1, LANES), I32).at[0, :N_EXPERTS].set(pstarts.astype(I32))
    dest = pl.pallas_call(
        _dest_kernel,
        grid=(T // tr,),
        in_specs=[pl.BlockSpec((tr, LANES), lambda i: (i, 0)), pl.BlockSpec((1, LANES), lambda i: (0, 0))],
        out_specs=pl.BlockSpec((tr, LANES), lambda i: (i, 0)),
        out_shape=jax.ShapeDtypeStruct((T, LANES), I32),
        compiler_params=_params("parallel"),
    )(meta, ps_lanes)[:, 0:2].reshape(-1)

    last_of_expert = jnp.concatenate([block_e[1:] != block_e[:-1], jnp.ones((1,), bool)])
    needs_zero = jnp.logical_or(blk >= n_used[0], jnp.logical_or(last_of_expert, blk == n_used[0] - 1)).astype(I32)
    n_zero = jnp.sum(needs_zero).astype(I32).reshape(1)
    tt = tl["moe_tok"]
    xs = pl.pallas_call(
        functools.partial(_dispatch_kernel, tt=tt),
        grid_spec=pltpu.PrefetchScalarGridSpec(
            num_scalar_prefetch=3, grid=(T // tt,),
            in_specs=[pl.BlockSpec((tt, D), lambda i, *_: (i, 0))],
            out_specs=pl.BlockSpec(memory_space=pl.ANY),
            scratch_shapes=[pltpu.VMEM((rb, D), F32), pltpu.SemaphoreType.DMA(()), pltpu.SemaphoreType.DMA(())]),
        out_shape=jax.ShapeDtypeStruct((P, D), F32),
        compiler_params=_params("arbitrary"),
    )(dest, needs_zero, n_zero, h1)

    F = D_EXPERT
    grp = jnp.cumsum(jnp.concatenate([jnp.zeros((1,), I32), (block_e[1:] != block_e[:-1]).astype(I32)]))
    ids = jnp.where(counts > 0, jnp.arange(N_EXPERTS, dtype=I32), N_EXPERTS)
    at_or_after = lax.cummin(ids, axis=0, reverse=True)
    after = jnp.concatenate([at_or_after[1:], jnp.full((1,), N_EXPERTS, I32)])
    nxt = jnp.where(after < N_EXPERTS, after, -1)[block_e].astype(I32)
    any_spec = pl.BlockSpec(memory_space=pl.ANY)
    ys = pl.pallas_call(
        functools.partial(_experts_kernel, layer=layer),
        grid_spec=pltpu.PrefetchScalarGridSpec(
            num_scalar_prefetch=4, grid=(n_blocks,),
            in_specs=[pl.BlockSpec((rb, D), lambda i, be, gr, nx, nu: (jnp.minimum(i, nu[0] - 1), 0)),
                      any_spec, any_spec, any_spec],
            out_specs=pl.BlockSpec((rb, D), lambda i, *_: (i, 0)),
            scratch_shapes=[pltpu.VMEM((2, D, F), F32), pltpu.VMEM((2, D, F), F32), pltpu.VMEM((2, F, D), F32),
                            pltpu.VMEM((D, F), BF16), pltpu.VMEM((D, F), BF16), pltpu.VMEM((F, D), BF16),
                            pltpu.SemaphoreType.DMA((2, 3))]),
        out_shape=jax.ShapeDtypeStruct((P, D), F32),
        compiler_params=_params("arbitrary"),
    )(block_e, grp.astype(I32), nxt, n_used, xs, lw["w_e_gate"], lw["w_e_up"], lw["w_e_down"])

    tc = tl["comb"]
    h2, h2_bf = pl.pallas_call(
        functools.partial(_combine_kernel, tc=tc, alpha=alpha),
        grid_spec=pltpu.PrefetchScalarGridSpec(
            num_scalar_prefetch=1, grid=(T // tc,),
            in_specs=[pl.BlockSpec((tc, D), lambda i, d: (i, 0)),
                      pl.BlockSpec((tc, LANES), lambda i, d: (i, 0)),
                      pl.BlockSpec((1, D), lambda i, d: (0, 0)),
                      pl.BlockSpec((1, D), lambda i, d: (0, 0)),
                      pl.BlockSpec(memory_space=pl.ANY)],
            out_specs=[pl.BlockSpec((tc, D), lambda i, d: (i, 0)), pl.BlockSpec((tc, D), lambda i, d: (i, 0))],
            scratch_shapes=[pltpu.VMEM((2, 2 * tc, D), F32), pltpu.SemaphoreType.DMA((2,))]),
        out_shape=[jax.ShapeDtypeStruct((T, D), F32), jax.ShapeDtypeStruct((T, D), BF16)],
        compiler_params=_params("arbitrary"),
    )(dest, h1, wgt, lw["ln2_g"], lw["ln2_b"], ys)
    return h2, h2_bf


def _layer_weights(l, w_in, g_q_lora, w_q_b, g_kv_lora, w_kv_b, w_o_a, w_o_b, w_out, ln1_g, ln1_b,
                   w_group, b_group, w_router, b_router, w_e_gate, w_e_up, w_e_down, ln2_g, ln2_b):
    D = w_in.shape[1]
    wi = w_in[l]
    o_kpe = MLA_Q_LORA + MLA_KV_LORA
    o_dsa = o_kpe + MLA_ROPE_DIM
    hd = DSA_HEADS * DSA_HEAD_DIM
    o_kidx = o_dsa + 3 * hd + IDX_HEADS * IDX_DIM
    o_widx = o_kidx + IDX_DIM
    o_gate = o_widx + IDX_HEADS
    kpe = wi[:, o_kpe:o_dsa]
    kidx = wi[:, o_kidx:o_widx]
    w_mla_in = jnp.concatenate([wi[:, :o_kpe], kpe, kpe], axis=1).astype(BF16)
    w_dsa_in = jnp.concatenate([wi[:, o_dsa:o_kidx], kidx, kidx, wi[:, o_widx:o_gate],
                                jnp.zeros((D, LANES - IDX_HEADS), F32)], axis=1).astype(BF16)
    wq = w_q_b[l].reshape(MLA_Q_LORA, MLA_HEADS, MLA_QK_DIM)
    wq = jnp.concatenate([wq[:, :, :MLA_NOPE_DIM].reshape(MLA_Q_LORA, -1),
                          wq[:, :, MLA_NOPE_DIM:].reshape(MLA_Q_LORA, -1)], axis=1).astype(BF16)
    w_route = jnp.concatenate([w_group[l], w_router[l],
                               jnp.zeros((D, LANES - N_GROUPS - N_EXPERTS), F32)], axis=1)
    b_route = jnp.concatenate([b_group[l], b_router[l],
                               jnp.zeros((LANES - N_GROUPS - N_EXPERTS,), F32)]).reshape(1, LANES)
    return dict(
        w_mla_in=w_mla_in, w_dsa_in=w_dsa_in,
        w_gate_a=wi[:, o_gate:o_gate + D].astype(BF16), w_gate_b=wi[:, o_gate + D:].astype(BF16),
        g_q=g_q_lora[l].reshape(1, -1), g_kv=g_kv_lora[l].reshape(1, -1),
        w_q_b=wq, w_kv_b=w_kv_b[l].astype(BF16),
        w_o_a=w_o_a[l].astype(BF16), w_o_b=w_o_b[l].astype(BF16), w_out=w_out[l].astype(BF16),
        ln1_g=ln1_g[l].reshape(1, -1), ln1_b=ln1_b[l].reshape(1, -1),
        w_route=w_route, b_route=b_route,
        w_e_gate=w_e_gate, w_e_up=w_e_up, w_e_down=w_e_down,
        ln2_g=ln2_g[l].reshape(1, -1), ln2_b=ln2_b[l].reshape(1, -1),
    )


def kernel(x, positions, w_in, g_q_lora, w_q_b, g_kv_lora, w_kv_b, w_o_a, w_o_b, w_out, ln1_g, ln1_b,
           w_group, b_group, w_router, b_router, w_e_gate, w_e_up, w_e_down, ln2_g, ln2_b):
    B, S, D = x.shape
    depth = w_in.shape[0]
    alpha = (2 * depth) ** 0.25
    tabs = (_rope_tables(positions, MLA_ROPE_DIM, MLA_ROPE_DIM),
            _rope_tables(positions, DSA_ROPE_DIM, DSA_HEAD_DIM),
            _rope_tables(positions, IDX_ROPE_DIM, IDX_DIM))
    h = x.reshape(B * S, D)
    h_bf = h.astype(BF16)
    for l in range(depth):
        lw = _layer_weights(l, w_in, g_q_lora, w_q_b, g_kv_lora, w_kv_b, w_o_a, w_o_b, w_out, ln1_g, ln1_b,
                            w_group, b_group, w_router, b_router, w_e_gate, w_e_up, w_e_down, ln2_g, ln2_b)
        h, h_bf = _layer(l, h, h_bf, positions, tabs, lw, alpha, B, S)
    return h.reshape(B, S, D)
```

```python
import functools
import math

import jax
import jax.numpy as jnp
from jax import lax
from jax.experimental import pallas as pl
from jax.experimental.pallas import tpu as pltpu

F32 = jnp.float32
BF16 = jnp.bfloat16
I32 = jnp.int32

MLA_HEADS = 8
MLA_Q_LORA = 512
MLA_KV_LORA = 512
MLA_NOPE_DIM = 128
MLA_ROPE_DIM = 64
MLA_V_DIM = 128
MLA_QK_DIM = MLA_NOPE_DIM + MLA_ROPE_DIM
DSA_HEADS = 8
DSA_HEAD_DIM = 128
DSA_ROPE_DIM = DSA_HEAD_DIM // 4
IDX_HEADS = 16
IDX_DIM = 64
IDX_ROPE_DIM = IDX_DIM // 4
DSA_MAX_TOPK = 256
ROPE_THETA = 500000.0
N_GROUPS = 8
EXPERTS_PER_GROUP = 8
N_EXPERTS = N_GROUPS * EXPERTS_PER_GROUP
D_EXPERT = 512
MOE_ROW_BLOCK = 128
LN_EPS = 1e-5
RMS_EPS = 1e-6

LANES = 128
NEG_BIG = -1e30
INT_MIN = -(2 ** 31)
VMEM_LIMIT = 56 * 1024 * 1024
WEIGHT_DMA_CHUNKS = 4


def _params(*sem):
    return pltpu.CompilerParams(dimension_semantics=sem, vmem_limit_bytes=VMEM_LIMIT)


def _resident(shape):
    nd = len(shape)
    return pl.BlockSpec(shape, lambda *_: (0,) * nd, pipeline_mode=pl.Buffered(1))


def _resident_layer(stacked, layer):
    shape = stacked.shape[1:]
    nd = len(shape)
    return pl.BlockSpec((None,) + shape, lambda *_: (layer,) + (0,) * nd, pipeline_mode=pl.Buffered(1))


def _dot(a, b):
    return jnp.dot(a, b, preferred_element_type=F32)


def _dot_nt(a, b):
    return lax.dot_general(a, b, (((1,), (1,)), ((), ())), preferred_element_type=F32)


def _rope128(x, c, sa, sb, half):
    return x * c + pltpu.roll(x, LANES - half, 1) * sa + pltpu.roll(x, half, 1) * sb


def _rms(x, g):
    return x * lax.rsqrt(jnp.mean(x * x, axis=-1, keepdims=True) + RMS_EPS) * g


def _layer_norm(x, g, b):
    mu = jnp.mean(x, axis=-1, keepdims=True)
    xc = x - mu
    var = jnp.mean(xc * xc, axis=-1, keepdims=True)
    return xc * lax.rsqrt(var + LN_EPS) * g + b


_O_KPE = MLA_Q_LORA + MLA_KV_LORA
_O_DSA = _O_KPE + MLA_ROPE_DIM
_O_KIDX = _O_DSA + 3 * DSA_HEADS * DSA_HEAD_DIM + IDX_HEADS * IDX_DIM
_O_WIDX = _O_KIDX + IDX_DIM
_O_GATE = _O_WIDX + IDX_HEADS
_W_MLA_IN = _O_KPE + 2 * MLA_ROPE_DIM
_W_DSA_IN = (_O_KIDX - _O_DSA) + 2 * IDX_DIM + LANES


def _regroup_w_in_kernel(w_ref, mla_ref, dsa_ref, ga_ref, gb_ref, *, d_model):
    mla_ref[:, 0:_O_DSA] = w_ref[:, 0:_O_DSA].astype(BF16)
    mla_ref[:, _O_DSA:_W_MLA_IN] = w_ref[:, _O_KPE:_O_DSA].astype(BF16)
    main = _O_KIDX - _O_DSA
    dsa_ref[:, 0:main] = w_ref[:, _O_DSA:_O_KIDX].astype(BF16)
    kidx = w_ref[:, _O_KIDX:_O_WIDX].astype(BF16)
    dsa_ref[:, main:main + IDX_DIM] = kidx
    dsa_ref[:, main + IDX_DIM:main + 2 * IDX_DIM] = kidx
    dsa_ref[:, main + 2 * IDX_DIM:main + 2 * IDX_DIM + IDX_HEADS] = w_ref[:, _O_WIDX:_O_GATE].astype(BF16)
    dsa_ref[:, main + 2 * IDX_DIM + IDX_HEADS:] = jnp.zeros((w_ref.shape[0], LANES - IDX_HEADS), BF16)
    ga_ref[...] = w_ref[:, _O_GATE:_O_GATE + d_model].astype(BF16)
    gb_ref[...] = w_ref[:, _O_GATE + d_model:_O_GATE + 2 * d_model].astype(BF16)


def _regroup_w_in(w_in):
    depth, d_model, width = w_in.shape
    tr = 256
    out_w = (_W_MLA_IN, _W_DSA_IN, d_model, d_model)
    return pl.pallas_call(
        functools.partial(_regroup_w_in_kernel, d_model=d_model),
        grid=(depth, d_model // tr),
        in_specs=[pl.BlockSpec((None, tr, width), lambda l, i: (l, i, 0))],
        out_specs=[pl.BlockSpec((None, tr, w), lambda l, i: (l, i, 0)) for w in out_w],
        out_shape=[jax.ShapeDtypeStruct((depth, d_model, w), BF16) for w in out_w],
        compiler_params=_params("parallel", "parallel"),
    )(w_in)


def _mla_prep_kernel(h_ref, win_ref, gq_ref, gkv_ref, wqb_ref, wkvb_ref, c_ref, sa_ref, sb_ref,
                     q_ref, k_ref, v_ref, *, q_scale):
    h = h_ref[...]
    p = _dot(h, win_ref[...])
    qn = _rms(p[:, :MLA_Q_LORA], gq_ref[...]).astype(BF16)
    kvn = _rms(p[:, MLA_Q_LORA:MLA_Q_LORA + MLA_KV_LORA], gkv_ref[...]).astype(BF16)
    q = _dot(qn, wqb_ref[...]) * q_scale
    kv = _dot(kvn, wkvb_ref[...])
    c, sa, sb = c_ref[...], sa_ref[...], sb_ref[...]
    half = MLA_ROPE_DIM // 2
    kpe = _rope128(p[:, MLA_Q_LORA + MLA_KV_LORA:], c, sa, sb, half)[:, :MLA_ROPE_DIM].astype(BF16)
    pe0 = MLA_HEADS * MLA_NOPE_DIM
    for hh in range(MLA_HEADS):
        q_ref[0, hh, :, 0:MLA_NOPE_DIM] = q[:, 128 * hh:128 * hh + 128].astype(BF16)
        k_ref[0, hh, :, 0:MLA_NOPE_DIM] = kv[:, 256 * hh:256 * hh + 128].astype(BF16)
        k_ref[0, hh, :, MLA_NOPE_DIM:MLA_QK_DIM] = kpe
        v_ref[:, 256 * hh:256 * hh + 128] = kv[:, 256 * hh + 128:256 * hh + 256].astype(BF16)
        v_ref[:, 256 * hh + 128:256 * hh + 256] = jnp.ones((kv.shape[0], LANES), BF16)
    for s in range(MLA_HEADS // 2):
        slab = _rope128(q[:, pe0 + 128 * s:pe0 + 128 * s + 128], c, sa, sb, half).astype(BF16)
        q_ref[0, 2 * s, :, MLA_NOPE_DIM:MLA_QK_DIM] = slab[:, :MLA_ROPE_DIM]
        q_ref[0, 2 * s + 1, :, MLA_NOPE_DIM:MLA_QK_DIM] = slab[:, MLA_ROPE_DIM:]


def _dsa_prep_kernel(h_ref, w_ref, cd_ref, sad_ref, sbd_ref, ci_ref, sai_ref, sbi_ref,
                     qb_ref, kb_ref, vb_ref, qi_ref, ki_ref, wi_ref, *, w_idx_scale, q_scale):
    h = h_ref[...]
    hd = DSA_HEADS * DSA_HEAD_DIM
    cd, sad, sbd = cd_ref[...], sad_ref[...], sbd_ref[...]
    ci, sai, sbi = ci_ref[...], sai_ref[...], sbi_ref[...]
    hd_half, hi_half = DSA_ROPE_DIM // 2, IDX_ROPE_DIM // 2
    q = _dot(h, w_ref[:, 0:hd]) * q_scale
    for hh in range(DSA_HEADS):
        qb_ref[:, 128 * hh:128 * hh + 128] = _rope128(q[:, 128 * hh:128 * hh + 128], cd, sad, sbd, hd_half).astype(BF16)
    k = _dot(h, w_ref[:, hd:2 * hd])
    for hh in range(DSA_HEADS):
        kb_ref[:, 128 * hh:128 * hh + 128] = _rope128(k[:, 128 * hh:128 * hh + 128], cd, sad, sbd, hd_half).astype(BF16)
    v = _dot(h, w_ref[:, 2 * hd:3 * hd]).astype(BF16)
    for hh in range(DSA_HEADS):
        vb_ref[:, 256 * hh:256 * hh + 128] = v[:, 128 * hh:128 * hh + 128]
        vb_ref[:, 256 * hh + 128:256 * hh + 256] = jnp.ones((v.shape[0], LANES), BF16)
    qi = _dot(h, w_ref[:, 3 * hd:3 * hd + IDX_HEADS * IDX_DIM])
    for s in range(IDX_HEADS // 2):
        slab = _rope128(qi[:, 128 * s:128 * s + 128], ci, sai, sbi, hi_half).astype(BF16)
        qi_ref[0, 2 * s] = slab[:, :IDX_DIM]
        qi_ref[0, 2 * s + 1] = slab[:, IDX_DIM:]
    o = 3 * hd + IDX_HEADS * IDX_DIM
    last = _dot(h, w_ref[:, o:o + 2 * LANES])
    ki_ref[...] = _rope128(last[:, :LANES], ci, sai, sbi, hi_half)[:, :IDX_DIM].astype(BF16)
    wi_ref[0] = (last[:, LANES:] * w_idx_scale).T[:IDX_HEADS, :]


def _flash_update(s, v_ones, m_ref, acc_ref):
    tk = s.shape[1]
    m_prev = m_ref[...]
    m_next = jnp.maximum(m_prev, jnp.max(s, axis=1, keepdims=True))
    p = jnp.exp2(s - jnp.tile(m_next, (1, tk // LANES)))
    alpha = jnp.exp2(m_prev - m_next)
    acc_ref[...] = acc_ref[...] * jnp.tile(alpha, (1, 2)) + _dot(p.astype(BF16), v_ones)
    m_ref[...] = m_next


def _mla_attn_kernel(q_ref, k_ref, v_ref, pq_ref, pk_ref, o_ref, m_sc, acc_sc, *, tile, heads):
    qi = pl.program_id(2)
    m_sc[...] = jnp.full(m_sc.shape, -jnp.inf, F32)
    acc_sc[...] = jnp.zeros(acc_sc.shape, F32)

    def chunk(c, masked):
        ks = pl.ds(pl.multiple_of(c * tile, tile), tile)
        if masked:
            visible = pk_ref[0, c] <= pq_ref[0]
        for hh in range(heads):
            s = _dot_nt(q_ref[0, hh], k_ref[0, hh, ks, :])
            if masked:
                s = jnp.where(visible, s, NEG_BIG)
            _flash_update(s, v_ref[0, ks, 2 * MLA_V_DIM * hh:2 * MLA_V_DIM * (hh + 1)], m_sc.at[hh], acc_sc.at[hh])

    def body(c, carry):
        chunk(c, False)
        return carry

    lax.fori_loop(0, qi, body, 0)
    chunk(qi, True)
    for hh in range(heads):
        o_ref[0, :, MLA_V_DIM * hh:MLA_V_DIM * (hh + 1)] = (
            acc_sc[hh, :, :MLA_V_DIM] / acc_sc[hh, :, MLA_V_DIM:]).astype(o_ref.dtype)


def _idx_select_kernel(q_ref, k_ref, wt_ref, pq_ref, pk_ref, o_ref, key_sc, *, tq, tk, nkc, top_k):
    qi = pl.program_id(1)
    nk = (qi * tq + tq + tk - 1) // tk
    wt = wt_ref[0]
    pq = pq_ref[0]

    def score_chunk(c, carry):
        kc = k_ref[0, pl.ds(pl.multiple_of(c * tk, tk), tk), :]
        acc = jnp.zeros((tk, tq), F32)
        for hh in range(IDX_HEADS):
            acc = acc + wt[hh:hh + 1, :] * jnp.maximum(_dot_nt(kc, q_ref[0, hh]), 0.0)
        bits = pltpu.bitcast(acc, I32)
        key = jnp.where(bits < 0, bits ^ jnp.int32(0x7FFFFFFF), bits)
        key = jnp.where(acc == 0.0, 0, key)
        key_sc[c] = jnp.where(pk_ref[0, c] <= pq, key, INT_MIN)
        return carry

    lax.fori_loop(0, nk, score_chunk, 0)

    def count(pred):
        def body(c, part):
            m = pred(key_sc[c], c).astype(I32)
            return part + jnp.sum(m.reshape(tk // 8, 8, tq), axis=0)
        part = lax.fori_loop(0, nk, body, jnp.zeros((8, tq), I32))
        return jnp.sum(part, axis=0, keepdims=True)

    def bit_step(i, base):
        cand = base ^ (jnp.int32(1) << (31 - i))
        cnt = count(lambda key, c: key >= cand)
        return jnp.where(cnt >= top_k, cand, base)

    thr = lax.fori_loop(0, 32, bit_step, jnp.full((1, tq), INT_MIN, I32))
    n_gt = count(lambda key, c: key > thr)
    n_eq = count(lambda key, c: key == thr)
    need = top_k - n_gt
    tie = jnp.logical_and(n_eq > need, thr != INT_MIN)

    def key_index(c):
        return c * tk + lax.broadcasted_iota(I32, (tk, tq), 0)

    def tie_search():
        def step(i, p):
            cand = p | (jnp.int32(1) << (14 - i))
            cnt = count(lambda key, c: jnp.logical_and(key == thr, key_index(c) < cand))
            return jnp.where(cnt <= need, cand, p)
        return lax.fori_loop(0, 15, step, jnp.zeros((1, tq), I32))

    p_cut = lax.cond(jnp.max(tie.astype(I32)) > 0, tie_search,
                     lambda: jnp.full((1, tq), 2 ** 30, I32))

    def out_chunk(c, carry):
        key = key_sc[c]
        sel = jnp.logical_or(key > thr, jnp.logical_and(key == thr, key_index(c) < p_cut))
        sel = jnp.logical_and(sel, key != INT_MIN)
        o_ref[0, c] = jnp.where(sel, 0.0, NEG_BIG).T.astype(o_ref.dtype)
        return carry

    lax.fori_loop(0, nk, out_chunk, 0)

    def fill_chunk(c, carry):
        o_ref[0, c] = jnp.full((tq, tk), NEG_BIG, o_ref.dtype)
        return carry

    lax.fori_loop(nk, nkc, fill_chunk, 0)


def _dsa_attn_kernel(q_ref, k_ref, v_ref, b_ref, o_ref, m_sc, acc_sc, *, tq, tk):
    qi = pl.program_id(1)
    nk = (qi * tq + tq + tk - 1) // tk
    m_sc[...] = jnp.full(m_sc.shape, NEG_BIG, F32)
    acc_sc[...] = jnp.zeros(acc_sc.shape, F32)

    def chunk(c, carry):
        bias = b_ref[0, c].astype(F32)
        ks = pl.ds(pl.multiple_of(c * tk, tk), tk)
        for hh in range(DSA_HEADS):
            hs = slice(DSA_HEAD_DIM * hh, DSA_HEAD_DIM * (hh + 1))
            vs = slice(2 * DSA_HEAD_DIM * hh, 2 * DSA_HEAD_DIM * (hh + 1))
            s = _dot_nt(q_ref[0, :, hs], k_ref[0, ks, hs]) + bias
            _flash_update(s, v_ref[0, ks, vs], m_sc.at[hh], acc_sc.at[hh])
        return carry

    lax.fori_loop(0, nk, chunk, 0)
    for hh in range(DSA_HEADS):
        hs = slice(DSA_HEAD_DIM * hh, DSA_HEAD_DIM * (hh + 1))
        o_ref[0, :, hs] = (acc_sc[hh, :, :DSA_HEAD_DIM] / acc_sc[hh, :, DSA_HEAD_DIM:]).astype(o_ref.dtype)


def _merge_kernel(h_ref, oa_ref, ob_ref, wga_ref, wgb_ref, woa_ref, wob_ref, y_ref):
    h = h_ref[...]
    ya = jax.nn.sigmoid(_dot(h, wga_ref[...])) * _dot(oa_ref[...], woa_ref[...])
    yb = jax.nn.sigmoid(_dot(h, wgb_ref[...])) * _dot(ob_ref[...], wob_ref[...])
    y_ref[...] = (ya + yb).astype(y_ref.dtype)


def _out_ln_kernel(y_ref, h_ref, wout_ref, g_ref, b_ref, wr_ref, br_ref, o_ref, op_ref, lg_ref, *, alpha):
    sub = 128
    half = o_ref.shape[1] // 2
    for r in range(y_ref.shape[0] // sub):
        rows = slice(r * sub, (r + 1) * sub)
        mix = _dot(y_ref[rows, :], wout_ref[...])
        h1 = _layer_norm(alpha * h_ref[rows, :] + mix, g_ref[...], b_ref[...])
        o_ref[rows, :] = h1
        lo = pltpu.bitcast(h1[:, :half].astype(BF16).astype(F32), I32)
        hi = pltpu.bitcast(h1[:, half:].astype(BF16).astype(F32), I32)
        op_ref[rows, :] = lax.shift_right_logical(lo, 16) | hi
        lg_ref[rows, :] = jnp.dot(h1, wr_ref[...], preferred_element_type=F32,
                                  precision=lax.Precision.HIGHEST) + br_ref[...]


def _route_kernel(lg_ref, meta_ref, wgt_ref, cnt_ref, carry_sc, *, tr):
    @pl.when(pl.program_id(0) == 0)
    def _():
        carry_sc[...] = jnp.zeros(carry_sc.shape, F32)

    lg = lg_ref[...]
    lane = lax.broadcasted_iota(I32, (tr, LANES), 1)
    gl = jnp.where(lane < N_GROUPS, lg, -jnp.inf)
    gmax = jnp.max(gl, axis=1, keepdims=True)
    g_idx = jnp.min(jnp.where(gl == gmax, lane, LANES), axis=1, keepdims=True)
    g_p = 1.0 / jnp.sum(jnp.exp(gl - gmax), axis=1, keepdims=True)
    lo = N_GROUPS + EXPERTS_PER_GROUP * g_idx
    in_group = jnp.logical_and(lane >= lo, lane < lo + EXPERTS_PER_GROUP)
    el = jnp.where(in_group, lg, -jnp.inf)
    emax = jnp.max(el, axis=1, keepdims=True)
    e1 = jnp.min(jnp.where(el == emax, lane, LANES), axis=1, keepdims=True)
    den = jnp.sum(jnp.exp(el - emax), axis=1, keepdims=True)
    el2 = jnp.where(lane == e1, -jnp.inf, el)
    emax2 = jnp.max(el2, axis=1, keepdims=True)
    e2 = jnp.min(jnp.where(el2 == emax2, lane, LANES), axis=1, keepdims=True)
    p1 = 1.0 / den
    p2 = jnp.exp(emax2 - emax) / den
    w1 = g_p * (p1 / (p1 + p2))
    w2 = g_p * (p2 / (p1 + p2))

    is1 = lane == e1
    is2 = lane == e2
    onehot = jnp.logical_or(is1, is2).astype(BF16)
    r = lax.broadcasted_iota(I32, (tr, tr), 0)
    cidx = lax.broadcasted_iota(I32, (tr, tr), 1)
    lower = (cidx < r).astype(BF16)
    before = _dot(lower, onehot) + carry_sc[...]
    rank1 = jnp.sum(jnp.where(is1, before, 0.0), axis=1, keepdims=True)
    rank2 = jnp.sum(jnp.where(is2, before, 0.0), axis=1, keepdims=True)
    carry_sc[...] = carry_sc[...] + jnp.sum(onehot.astype(F32), axis=0, keepdims=True)

    meta = jnp.where(lane == 0, e1 - N_GROUPS,
                     jnp.where(lane == 1, e2 - N_GROUPS,
                               jnp.where(lane == 2, rank1.astype(I32),
                                         jnp.where(lane == 3, rank2.astype(I32), 0))))
    meta_ref[...] = meta
    wgt_ref[...] = jnp.where(lane == 0, w1, jnp.where(lane == 1, w2, 0.0))
    cnt_ref[...] = jnp.broadcast_to(carry_sc[...], cnt_ref.shape)


def _row_copy(src, src_row, dst, dst_row, sem):
    return pltpu.make_async_copy(src.at[pl.ds(src_row, 1)], dst.at[pl.ds(dst_row, 1)], sem)


def _dest_kernel(meta_ref, ps_ref, o_ref):
    meta = meta_ref[...]
    ps = ps_ref[...]
    lane = lax.broadcasted_iota(I32, meta.shape, 1)
    d1 = jnp.sum(jnp.where(lane == meta[:, 0:1], ps, 0), axis=1, keepdims=True) + meta[:, 2:3]
    d2 = jnp.sum(jnp.where(lane == meta[:, 1:2], ps, 0), axis=1, keepdims=True) + meta[:, 3:4]
    o_ref[...] = jnp.where(lane == 0, d1, jnp.where(lane == 1, d2, 0))


def _dispatch_kernel(dest_ref, zblk_ref, nz_ref, h_ref, xs_out, zero_sc, sem, zsem, *, tt):
    base = pl.program_id(0) * tt
    rb = zero_sc.shape[0]

    @pl.when(pl.program_id(0) == 0)
    def _():
        zero_sc[...] = jnp.zeros(zero_sc.shape, zero_sc.dtype)

        def fill(j, carry):
            @pl.when(zblk_ref[j] != 0)
            def _():
                pltpu.make_async_copy(zero_sc, xs_out.at[pl.ds(pl.multiple_of(j * rb, rb), rb)], zsem).start()
            return carry

        def drain(j, carry):
            pltpu.make_async_copy(zero_sc, xs_out.at[pl.ds(0, rb)], zsem).wait()
            return carry

        lax.fori_loop(0, zblk_ref.shape[0], fill, 0)
        lax.fori_loop(0, nz_ref[0], drain, 0)

    def issue(t, carry):
        tok = base + t
        _row_copy(h_ref, t, xs_out, dest_ref[2 * tok], sem).start()
        _row_copy(h_ref, t, xs_out, dest_ref[2 * tok + 1], sem).start()
        return carry

    lax.fori_loop(0, tt, issue, 0, unroll=8)
    for _ in range(2):
        pltpu.make_async_copy(h_ref, xs_out.at[pl.ds(0, tt)], sem).wait()


def _experts_kernel(be_ref, grp_ref, nxt_ref, nu_ref, xs_ref, wg_hbm, wu_hbm, wd_hbm, o_ref,
                    wg_buf, wu_buf, wd_buf, wg_sc, wu_sc, wd_sc, sem, *, layer):
    i = pl.program_id(0)
    e = be_ref[i]
    used = i < nu_ref[0]
    first = jnp.logical_and(used, jnp.logical_or(i == 0, e != be_ref[jnp.maximum(i - 1, 0)]))
    slot = grp_ref[i] % 2

    def copies(expert, s):
        out = []
        for j, (hbm, buf) in enumerate(((wg_hbm, wg_buf), (wu_hbm, wu_buf), (wd_hbm, wd_buf))):
            rows = buf.shape[1] // WEIGHT_DMA_CHUNKS
            for c in range(WEIGHT_DMA_CHUNKS):
                rs = pl.ds(c * rows, rows)
                out.append(pltpu.make_async_copy(hbm.at[layer, expert, rs], buf.at[s, rs], sem.at[s, j]))
        return out

    @pl.when(i == 0)
    def _():
        for cp in copies(e, slot):
            cp.start()

    @pl.when(first)
    def _():
        for cp in copies(e, slot):
            cp.wait()
        nxt = nxt_ref[i]

        @pl.when(nxt >= 0)
        def _():
            for cp in copies(nxt, 1 - slot):
                cp.start()

        wg_sc[...] = wg_buf[slot].astype(BF16)
        wu_sc[...] = wu_buf[slot].astype(BF16)
        wd_sc[...] = wd_buf[slot].astype(BF16)

    @pl.when(used)
    def _():
        xp = xs_ref[...]
        x = jnp.concatenate([pltpu.bitcast(xp << 16, F32).astype(BF16),
                             pltpu.bitcast(xp & jnp.int32(-65536), F32).astype(BF16)], axis=1)
        hid = jax.nn.silu(_dot(x, wg_sc[...])) * _dot(x, wu_sc[...])
        o_ref[...] = _dot(hid.astype(BF16), wd_sc[...])

    @pl.when(jnp.logical_not(used))
    def _():
        o_ref[...] = jnp.zeros(o_ref.shape, o_ref.dtype)


def _combine_kernel(dest_ref, h_ref, wgt_ref, g_ref, b_ref, ys_hbm, o_ref, ob_ref, ybuf, sem, *, tc, alpha):
    i = pl.program_id(0)

    def gather(tile, slot):
        def issue(t, carry):
            tok = tile * tc + t
            _row_copy(ys_hbm, dest_ref[2 * tok], ybuf.at[slot], t, sem.at[slot]).start()
            _row_copy(ys_hbm, dest_ref[2 * tok + 1], ybuf.at[slot], tc + t, sem.at[slot]).start()
            return carry
        lax.fori_loop(0, tc, issue, 0, unroll=8)

    @pl.when(i == 0)
    def _():
        gather(0, 0)

    @pl.when(i + 1 < pl.num_programs(0))
    def _():
        gather(i + 1, (i + 1) % 2)

    slot = i % 2
    pltpu.make_async_copy(ys_hbm.at[pl.ds(0, 2 * tc)], ybuf.at[slot], sem.at[slot]).wait()
    wgt = wgt_ref[...]
    ff = ybuf[slot, 0:tc, :] * wgt[:, 0:1] + ybuf[slot, tc:2 * tc, :] * wgt[:, 1:2]
    h2 = _layer_norm(alpha * h_ref[...] + ff, g_ref[...], b_ref[...])
    o_ref[...] = h2
    ob_ref[...] = h2.astype(BF16)


def _tiles(seq):
    big = seq >= 2048
    return dict(
        rows=256 if big else 128,
        mla=512 if big else 128,
        mla_heads=8,
        idx_q=256 if big else 64,
        dsa_q=512 if big else 128,
        key=512 if big else 128,
        route=512 if big else 128,
        moe_tok=512 if big else 128,
        comb=128,
    )


def _rope_tables(positions, rot_dim, head_dim):
    half = rot_dim // 2
    inv = ROPE_THETA ** (-jnp.arange(0, rot_dim, 2, dtype=F32) / rot_dim)
    ang = positions.astype(F32).reshape(-1, 1) * inv
    cos, sin = jnp.cos(ang), jnp.sin(ang)
    t = cos.shape[0]
    rest = head_dim - rot_dim
    ones, zeros, zh = jnp.ones((t, rest), F32), jnp.zeros((t, rest), F32), jnp.zeros((t, half), F32)
    rep = LANES // head_dim
    c = jnp.tile(jnp.concatenate([cos, cos, ones], axis=1), (1, rep))
    sa = jnp.tile(jnp.concatenate([-sin, zh, zeros], axis=1), (1, rep))
    sb = jnp.tile(jnp.concatenate([zh, sin, zeros], axis=1), (1, rep))
    return c, sa, sb


def _layer(layer, h, h_bf, positions, tabs, lw, alpha, B, S):
    T, D = h.shape
    tl = _tiles(S)
    tm = tl["rows"]
    nrow = T // tm
    spb = S // tm
    (c_m, sa_m, sb_m), (c_d, sa_d, sb_d), (c_i, sa_i, sb_i) = tabs
    row = lambda w: pl.BlockSpec((tm, w), lambda i: (i, 0))
    head_major = lambda nh, w: pl.BlockSpec((1, nh, tm, w), lambda i: (i // spb, 0, i % spb, 0))

    log2e = math.log2(math.e)
    q_a, k_a, v_a = pl.pallas_call(
        functools.partial(_mla_prep_kernel, q_scale=log2e * MLA_QK_DIM ** -0.5),
        grid=(nrow,),
        in_specs=[row(D), _resident_layer(lw["w_mla_in"], layer), _resident((1, MLA_Q_LORA)), _resident((1, MLA_KV_LORA)),
                  _resident(lw["w_q_b"].shape), _resident(lw["w_kv_b"].shape), row(LANES), row(LANES), row(LANES)],
        out_specs=[head_major(MLA_HEADS, MLA_QK_DIM), head_major(MLA_HEADS, MLA_QK_DIM), row(2 * MLA_HEADS * MLA_V_DIM)],
        out_shape=[jax.ShapeDtypeStruct((B, MLA_HEADS, S, MLA_QK_DIM), BF16),
                   jax.ShapeDtypeStruct((B, MLA_HEADS, S, MLA_QK_DIM), BF16),
                   jax.ShapeDtypeStruct((T, 2 * MLA_HEADS * MLA_V_DIM), BF16)],
        compiler_params=_params("parallel"),
    )(h_bf, lw["w_mla_in"], lw["g_q"], lw["g_kv"], lw["w_q_b"], lw["w_kv_b"], c_m, sa_m, sb_m)

    hd = DSA_HEADS * DSA_HEAD_DIM
    q_b, k_b, v_b, q_idx, k_idx, w_idx = pl.pallas_call(
        functools.partial(_dsa_prep_kernel, w_idx_scale=(IDX_HEADS ** -0.5) * (IDX_DIM ** -0.5),
                          q_scale=log2e * DSA_HEAD_DIM ** -0.5),
        grid=(nrow,),
        in_specs=[row(D), _resident_layer(lw["w_dsa_in"], layer)] + [row(LANES)] * 6,
        out_specs=[row(hd), row(hd), row(2 * hd), head_major(IDX_HEADS, IDX_DIM), row(IDX_DIM),
                   pl.BlockSpec((1, IDX_HEADS, tm), lambda i: (i // spb, 0, i % spb))],
        out_shape=[jax.ShapeDtypeStruct((T, hd), BF16)] * 2 + [jax.ShapeDtypeStruct((T, 2 * hd), BF16)] + [
            jax.ShapeDtypeStruct((B, IDX_HEADS, S, IDX_DIM), BF16),
            jax.ShapeDtypeStruct((T, IDX_DIM), BF16),
            jax.ShapeDtypeStruct((B, IDX_HEADS, S), F32)],
        compiler_params=_params("parallel"),
    )(h_bf, lw["w_dsa_in"], c_d, sa_d, sb_d, c_i, sa_i, sb_i)

    ta = tl["mla"]
    pos_q = positions.reshape(B, S, 1)
    pos_k_mla = positions.reshape(B, S // ta, 1, ta)
    hp = tl["mla_heads"]
    o_a = pl.pallas_call(
        functools.partial(_mla_attn_kernel, tile=ta, heads=hp),
        grid=(B, MLA_HEADS // hp, S // ta),
        in_specs=[pl.BlockSpec((1, hp, ta, MLA_QK_DIM), lambda b, g, i: (b, g, i, 0)),
                  pl.BlockSpec((1, hp, S, MLA_QK_DIM), lambda b, g, i: (b, g, 0, 0), pipeline_mode=pl.Buffered(1)),
                  pl.BlockSpec((1, S, 2 * MLA_V_DIM * hp), lambda b, g, i: (b, 0, g), pipeline_mode=pl.Buffered(1)),
                  pl.BlockSpec((1, ta, 1), lambda b, g, i: (b, i, 0)),
                  pl.BlockSpec((1, S // ta, 1, ta), lambda b, g, i: (b, 0, 0, 0))],
        out_specs=pl.BlockSpec((1, ta, MLA_V_DIM * hp), lambda b, g, i: (b, i, g)),
        out_shape=jax.ShapeDtypeStruct((B, S, MLA_HEADS * MLA_V_DIM), BF16),
        scratch_shapes=[pltpu.VMEM((hp, ta, LANES), F32), pltpu.VMEM((hp, ta, 2 * MLA_V_DIM), F32)],
        compiler_params=_params("parallel", "parallel", "arbitrary"),
    )(q_a, k_a, v_a.reshape(B, S, -1), pos_q, pos_k_mla)

    tk = tl["key"]
    nkc = S // tk
    tqi = tl["idx_q"]
    top_k = min(DSA_MAX_TOPK, S // 4)
    sel_bias = pl.pallas_call(
        functools.partial(_idx_select_kernel, tq=tqi, tk=tk, nkc=nkc, top_k=top_k),
        grid=(B, S // tqi),
        in_specs=[pl.BlockSpec((1, IDX_HEADS, tqi, IDX_DIM), lambda b, i: (b, 0, i, 0)),
                  pl.BlockSpec((1, S, IDX_DIM), lambda b, i: (b, 0, 0)),
                  pl.BlockSpec((1, IDX_HEADS, tqi), lambda b, i: (b, 0, i)),
                  pl.BlockSpec((1, 1, tqi), lambda b, i: (b, 0, i)),
                  pl.BlockSpec((1, nkc, tk, 1), lambda b, i: (b, 0, 0, 0))],
        out_specs=pl.BlockSpec((1, nkc, tqi, tk), lambda b, i: (b, 0, i, 0)),
        out_shape=jax.ShapeDtypeStruct((B, nkc, S, tk), BF16),
        scratch_shapes=[pltpu.VMEM((nkc, tk, tqi), I32)],
        compiler_params=_params("parallel", "arbitrary"),
    )(q_idx, k_idx.reshape(B, S, IDX_DIM), w_idx, positions.reshape(B, 1, S), positions.reshape(B, nkc, tk, 1))

    tqd = tl["dsa_q"]
    o_b = pl.pallas_call(
        functools.partial(_dsa_attn_kernel, tq=tqd, tk=tk),
        grid=(B, S // tqd),
        in_specs=[pl.BlockSpec((1, tqd, hd), lambda b, i: (b, i, 0)),
                  pl.BlockSpec((1, S, hd), lambda b, i: (b, 0, 0), pipeline_mode=pl.Buffered(1)),
                  pl.BlockSpec((1, S, 2 * hd), lambda b, i: (b, 0, 0), pipeline_mode=pl.Buffered(1)),
                  pl.BlockSpec((1, nkc, tqd, tk), lambda b, i: (b, 0, i, 0))],
        out_specs=pl.BlockSpec((1, tqd, hd), lambda b, i: (b, i, 0)),
        out_shape=jax.ShapeDtypeStruct((B, S, hd), BF16),
        scratch_shapes=[pltpu.VMEM((DSA_HEADS, tqd, LANES), F32), pltpu.VMEM((DSA_HEADS, tqd, 2 * DSA_HEAD_DIM), F32)],
        compiler_params=_params("parallel", "arbitrary"),
    )(q_b.reshape(B, S, hd), k_b.reshape(B, S, hd), v_b.reshape(B, S, 2 * hd), sel_bias)

    y = pl.pallas_call(
        _merge_kernel,
        grid=(nrow,),
        in_specs=[row(D), row(MLA_HEADS * MLA_V_DIM), row(hd),
                  _resident_layer(lw["w_gate_a"], layer), _resident_layer(lw["w_gate_b"], layer),
                  _resident((MLA_HEADS * MLA_V_DIM, D)), _resident((hd, D))],
        out_specs=row(D),
        out_shape=jax.ShapeDtypeStruct((T, D), BF16),
        compiler_params=_params("parallel"),
    )(h_bf, o_a.reshape(T, -1), o_b.reshape(T, hd), lw["w_gate_a"], lw["w_gate_b"], lw["w_o_a"], lw["w_o_b"])

    h1, h1_packed, logits = pl.pallas_call(
        functools.partial(_out_ln_kernel, alpha=alpha),
        grid=(nrow,),
        in_specs=[row(D), row(D), _resident((D, D)), _resident((1, D)), _resident((1, D)),
                  _resident((D, LANES)), _resident((1, LANES))],
        out_specs=[row(D), row(D // 2), row(LANES)],
        out_shape=[jax.ShapeDtypeStruct((T, D), F32), jax.ShapeDtypeStruct((T, D // 2), I32),
                   jax.ShapeDtypeStruct((T, LANES), F32)],
        compiler_params=_params("parallel"),
    )(y, h, lw["w_out"], lw["ln1_g"], lw["ln1_b"], lw["w_route"], lw["b_route"])

    tr = tl["route"]
    meta, wgt, cnt = pl.pallas_call(
        functools.partial(_route_kernel, tr=tr),
        grid=(T // tr,),
        in_specs=[pl.BlockSpec((tr, LANES), lambda i: (i, 0))],
        out_specs=[pl.BlockSpec((tr, LANES), lambda i: (i, 0)), pl.BlockSpec((tr, LANES), lambda i: (i, 0)),
                   pl.BlockSpec((8, LANES), lambda i: (0, 0))],
        out_shape=[jax.ShapeDtypeStruct((T, LANES), I32), jax.ShapeDtypeStruct((T, LANES), F32),
                   jax.ShapeDtypeStruct((8, LANES), F32)],
        scratch_shapes=[pltpu.VMEM((1, LANES), F32)],
        compiler_params=_params("arbitrary"),
    )(logits)

    rb = MOE_ROW_BLOCK
    counts = cnt[0, N_GROUPS:N_GROUPS + N_EXPERTS].astype(I32)
    padded = ((counts + rb - 1) // rb) * rb
    pends = jnp.cumsum(padded)
    pstarts = pends - padded
    n_blocks = -(-(2 * T) // rb) + N_EXPERTS
    P = n_blocks * rb
    blk = jnp.arange(n_blocks, dtype=I32)
    block_e = jnp.minimum(jnp.sum((pends[None, :] <= (blk * rb)[:, None]).astype(I32), axis=1), N_EXPERTS - 1)
    n_used = (pends[-1] // rb).astype(I32).reshape(1)
    ps_lanes = jnp.zeros((1, LANES), I32).at[0, :N_EXPERTS].set(pstarts.astype(I32))
    dest = pl.pallas_call(
        _dest_kernel,
        grid=(T // tr,),
        in_specs=[pl.BlockSpec((tr, LANES), lambda i: (i, 0)), pl.BlockSpec((1, LANES), lambda i: (0, 0))],
        out_specs=pl.BlockSpec((tr, LANES), lambda i: (i, 0)),
        out_shape=jax.ShapeDtypeStruct((T, LANES), I32),
        compiler_params=_params("parallel"),
    )(meta, ps_lanes)[:, 0:2].reshape(-1)

    last_of_expert = jnp.concatenate([block_e[1:] != block_e[:-1], jnp.ones((1,), bool)])
    needs_zero = jnp.logical_or(blk >= n_used[0], jnp.logical_or(last_of_expert, blk == n_used[0] - 1)).astype(I32)
    n_zero = jnp.sum(needs_zero).astype(I32).reshape(1)
    tt = tl["moe_tok"]
    xs = pl.pallas_call(
        functools.partial(_dispatch_kernel, tt=tt),
        grid_spec=pltpu.PrefetchScalarGridSpec(
            num_scalar_prefetch=3, grid=(T // tt,),
            in_specs=[pl.BlockSpec((tt, D // 2), lambda i, *_: (i, 0))],
            out_specs=pl.BlockSpec(memory_space=pl.ANY),
            scratch_shapes=[pltpu.VMEM((rb, D // 2), I32), pltpu.SemaphoreType.DMA(()), pltpu.SemaphoreType.DMA(())]),
        out_shape=jax.ShapeDtypeStruct((P, D // 2), I32),
        compiler_params=_params("arbitrary"),
    )(dest, needs_zero, n_zero, h1_packed)

    F = D_EXPERT
    grp = jnp.cumsum(jnp.concatenate([jnp.zeros((1,), I32), (block_e[1:] != block_e[:-1]).astype(I32)]))
    ids = jnp.where(counts > 0, jnp.arange(N_EXPERTS, dtype=I32), N_EXPERTS)
    at_or_after = lax.cummin(ids, axis=0, reverse=True)
    after = jnp.concatenate([at_or_after[1:], jnp.full((1,), N_EXPERTS, I32)])
    nxt = jnp.where(after < N_EXPERTS, after, -1)[block_e].astype(I32)
    any_spec = pl.BlockSpec(memory_space=pl.ANY)
    ys = pl.pallas_call(
        functools.partial(_experts_kernel, layer=layer),
        grid_spec=pltpu.PrefetchScalarGridSpec(
            num_scalar_prefetch=4, grid=(n_blocks,),
            in_specs=[pl.BlockSpec((rb, D // 2), lambda i, be, gr, nx, nu: (jnp.minimum(i, nu[0] - 1), 0)),
                      any_spec, any_spec, any_spec],
            out_specs=pl.BlockSpec((rb, D), lambda i, *_: (i, 0)),
            scratch_shapes=[pltpu.VMEM((2, D, F), F32), pltpu.VMEM((2, D, F), F32), pltpu.VMEM((2, F, D), F32),
                            pltpu.VMEM((D, F), BF16), pltpu.VMEM((D, F), BF16), pltpu.VMEM((F, D), BF16),
                            pltpu.SemaphoreType.DMA((2, 3))]),
        out_shape=jax.ShapeDtypeStruct((P, D), F32),
        compiler_params=_params("arbitrary"),
    )(block_e, grp.astype(I32), nxt, n_used, xs, lw["w_e_gate"], lw["w_e_up"], lw["w_e_down"])

    tc = tl["comb"]
    h2, h2_bf = pl.pallas_call(
        functools.partial(_combine_kernel, tc=tc, alpha=alpha),
        grid_spec=pltpu.PrefetchScalarGridSpec(
            num_scalar_prefetch=1, grid=(T // tc,),
            in_specs=[pl.BlockSpec((tc, D), lambda i, d: (i, 0)),
                      pl.BlockSpec((tc, LANES), lambda i, d: (i, 0)),
                      pl.BlockSpec((1, D), lambda i, d: (0, 0)),
                      pl.BlockSpec((1, D), lambda i, d: (0, 0)),
                      pl.BlockSpec(memory_space=pl.ANY)],
            out_specs=[pl.BlockSpec((tc, D), lambda i, d: (i, 0)), pl.BlockSpec((tc, D), lambda i, d: (i, 0))],
            scratch_shapes=[pltpu.VMEM((2, 2 * tc, D), F32), pltpu.SemaphoreType.DMA((2,))]),
        out_shape=[jax.ShapeDtypeStruct((T, D), F32), jax.ShapeDtypeStruct((T, D), BF16)],
        compiler_params=_params("arbitrary"),
    )(dest, h1, wgt, lw["ln2_g"], lw["ln2_b"], ys)
    return h2, h2_bf


def _layer_weights(l, regrouped, g_q_lora, w_q_b, g_kv_lora, w_kv_b, w_o_a, w_o_b, w_out, ln1_g, ln1_b,
                   w_group, b_group, w_router, b_router, w_e_gate, w_e_up, w_e_down, ln2_g, ln2_b):
    w_mla_in, w_dsa_in, w_gate_a, w_gate_b = regrouped
    D = w_gate_a.shape[1]
    wq = w_q_b[l].reshape(MLA_Q_LORA, MLA_HEADS, MLA_QK_DIM)
    wq = jnp.concatenate([wq[:, :, :MLA_NOPE_DIM].reshape(MLA_Q_LORA, -1),
                          wq[:, :, MLA_NOPE_DIM:].reshape(MLA_Q_LORA, -1)], axis=1).astype(BF16)
    w_route = jnp.concatenate([w_group[l], w_router[l],
                               jnp.zeros((D, LANES - N_GROUPS - N_EXPERTS), F32)], axis=1)
    b_route = jnp.concatenate([b_group[l], b_router[l],
                               jnp.zeros((LANES - N_GROUPS - N_EXPERTS,), F32)]).reshape(1, LANES)
    return dict(
        w_mla_in=w_mla_in, w_dsa_in=w_dsa_in, w_gate_a=w_gate_a, w_gate_b=w_gate_b,
        g_q=g_q_lora[l].reshape(1, -1), g_kv=g_kv_lora[l].reshape(1, -1),
        w_q_b=wq, w_kv_b=w_kv_b[l].astype(BF16),
        w_o_a=w_o_a[l].astype(BF16), w_o_b=w_o_b[l].astype(BF16), w_out=w_out[l].astype(BF16),
        ln1_g=ln1_g[l].reshape(1, -1), ln1_b=ln1_b[l].reshape(1, -1),
        w_route=w_route, b_route=b_route,
        w_e_gate=w_e_gate, w_e_up=w_e_up, w_e_down=w_e_down,
        ln2_g=ln2_g[l].reshape(1, -1), ln2_b=ln2_b[l].reshape(1, -1),
    )


def kernel(x, positions, w_in, g_q_lora, w_q_b, g_kv_lora, w_kv_b, w_o_a, w_o_b, w_out, ln1_g, ln1_b,
           w_group, b_group, w_router, b_router, w_e_gate, w_e_up, w_e_down, ln2_g, ln2_b):
    B, S, D = x.shape
    depth = w_in.shape[0]
    alpha = (2 * depth) ** 0.25
    tabs = (_rope_tables(positions, MLA_ROPE_DIM, MLA_ROPE_DIM),
            _rope_tables(positions, DSA_ROPE_DIM, DSA_HEAD_DIM),
            _rope_tables(positions, IDX_ROPE_DIM, IDX_DIM))
    h = x.reshape(B * S, D)
    h_bf = h.astype(BF16)
    regrouped = _regroup_w_in(w_in)
    for l in range(depth):
        lw = _layer_weights(l, regrouped, g_q_lora, w_q_b, g_kv_lora, w_kv_b, w_o_a, w_o_b, w_out, ln1_g, ln1_b,
                            w_group, b_group, w_router, b_router, w_e_gate, w_e_up, w_e_down, ln2_g, ln2_b)
        h, h_bf = _layer(l, h, h_bf, positions, tabs, lw, alpha, B, S)
    return h.reshape(B, S, D)
```

```python
import functools
import math

import jax
import jax.numpy as jnp
from jax import lax
from jax.experimental import pallas as pl
from jax.experimental.pallas import tpu as pltpu

F32 = jnp.float32
BF16 = jnp.bfloat16
I32 = jnp.int32

MLA_HEADS = 8
MLA_Q_LORA = 512
MLA_KV_LORA = 512
MLA_NOPE_DIM = 128
MLA_ROPE_DIM = 64
MLA_V_DIM = 128
MLA_QK_DIM = MLA_NOPE_DIM + MLA_ROPE_DIM
DSA_HEADS = 8
DSA_HEAD_DIM = 128
DSA_ROPE_DIM = DSA_HEAD_DIM // 4
IDX_HEADS = 16
IDX_DIM = 64
IDX_ROPE_DIM = IDX_DIM // 4
DSA_MAX_TOPK = 256
ROPE_THETA = 500000.0
N_GROUPS = 8
EXPERTS_PER_GROUP = 8
N_EXPERTS = N_GROUPS * EXPERTS_PER_GROUP
D_EXPERT = 512
MOE_ROW_BLOCK = 128
LN_EPS = 1e-5
RMS_EPS = 1e-6

LANES = 128
NEG_BIG = -1e30
INT_MIN = -(2 ** 31)
VMEM_LIMIT = 56 * 1024 * 1024
WEIGHT_DMA_CHUNKS = 4


def _params(*sem):
    return pltpu.CompilerParams(dimension_semantics=sem, vmem_limit_bytes=VMEM_LIMIT)


def _resident(shape):
    nd = len(shape)
    return pl.BlockSpec(shape, lambda *_: (0,) * nd, pipeline_mode=pl.Buffered(1))


def _resident_layer(stacked, layer):
    shape = stacked.shape[1:]
    nd = len(shape)
    return pl.BlockSpec((None,) + shape, lambda *_: (layer,) + (0,) * nd, pipeline_mode=pl.Buffered(1))


def _dot(a, b):
    return jnp.dot(a, b, preferred_element_type=F32)


def _dot_nt(a, b):
    return lax.dot_general(a, b, (((1,), (1,)), ((), ())), preferred_element_type=F32)


def _rope128(x, c, sa, sb, half):
    return x * c + pltpu.roll(x, LANES - half, 1) * sa + pltpu.roll(x, half, 1) * sb


def _rms(x, g):
    return x * lax.rsqrt(jnp.mean(x * x, axis=-1, keepdims=True) + RMS_EPS) * g


def _layer_norm(x, g, b):
    mu = jnp.mean(x, axis=-1, keepdims=True)
    xc = x - mu
    var = jnp.mean(xc * xc, axis=-1, keepdims=True)
    return xc * lax.rsqrt(var + LN_EPS) * g + b


_O_KPE = MLA_Q_LORA + MLA_KV_LORA
_O_DSA = _O_KPE + MLA_ROPE_DIM
_O_KIDX = _O_DSA + 3 * DSA_HEADS * DSA_HEAD_DIM + IDX_HEADS * IDX_DIM
_O_WIDX = _O_KIDX + IDX_DIM
_O_GATE = _O_WIDX + IDX_HEADS
_W_MLA_IN = _O_KPE + 2 * MLA_ROPE_DIM
_W_DSA_IN = (_O_KIDX - _O_DSA) + 2 * IDX_DIM + LANES


def _regroup_w_in_kernel(w_ref, mla_ref, dsa_ref, ga_ref, gb_ref, *, d_model):
    mla_ref[:, 0:_O_DSA] = w_ref[:, 0:_O_DSA].astype(BF16)
    mla_ref[:, _O_DSA:_W_MLA_IN] = w_ref[:, _O_KPE:_O_DSA].astype(BF16)
    main = _O_KIDX - _O_DSA
    dsa_ref[:, 0:main] = w_ref[:, _O_DSA:_O_KIDX].astype(BF16)
    kidx = w_ref[:, _O_KIDX:_O_WIDX].astype(BF16)
    dsa_ref[:, main:main + IDX_DIM] = kidx
    dsa_ref[:, main + IDX_DIM:main + 2 * IDX_DIM] = kidx
    dsa_ref[:, main + 2 * IDX_DIM:main + 2 * IDX_DIM + IDX_HEADS] = w_ref[:, _O_WIDX:_O_GATE].astype(BF16)
    dsa_ref[:, main + 2 * IDX_DIM + IDX_HEADS:] = jnp.zeros((w_ref.shape[0], LANES - IDX_HEADS), BF16)
    ga_ref[...] = w_ref[:, _O_GATE:_O_GATE + d_model].astype(BF16)
    gb_ref[...] = w_ref[:, _O_GATE + d_model:_O_GATE + 2 * d_model].astype(BF16)


def _regroup_w_in(w_in):
    depth, d_model, width = w_in.shape
    tr = 256
    out_w = (_W_MLA_IN, _W_DSA_IN, d_model, d_model)
    return pl.pallas_call(
        functools.partial(_regroup_w_in_kernel, d_model=d_model),
        grid=(depth, d_model // tr),
        in_specs=[pl.BlockSpec((tr, width), lambda l, i: (l * (d_model // tr) + i, 0))],
        out_specs=[pl.BlockSpec((None, tr, w), lambda l, i: (l, i, 0)) for w in out_w],
        out_shape=[jax.ShapeDtypeStruct((depth, d_model, w), BF16) for w in out_w],
        compiler_params=_params("parallel", "parallel"),
    )(w_in.reshape(depth * d_model, width))


def _mla_prep_kernel(h_ref, win_ref, gq_ref, gkv_ref, wqb_ref, wkvb_ref, c_ref, sa_ref, sb_ref,
                     q_ref, k_ref, v_ref, *, q_scale):
    h = h_ref[...]
    p = _dot(h, win_ref[...])
    qn = _rms(p[:, :MLA_Q_LORA], gq_ref[...]).astype(BF16)
    kvn = _rms(p[:, MLA_Q_LORA:MLA_Q_LORA + MLA_KV_LORA], gkv_ref[...]).astype(BF16)
    q = _dot(qn, wqb_ref[...]) * q_scale
    kv = _dot(kvn, wkvb_ref[...])
    c, sa, sb = c_ref[...], sa_ref[...], sb_ref[...]
    half = MLA_ROPE_DIM // 2
    kpe = _rope128(p[:, MLA_Q_LORA + MLA_KV_LORA:], c, sa, sb, half)[:, :MLA_ROPE_DIM].astype(BF16)
    pe0 = MLA_HEADS * MLA_NOPE_DIM
    for hh in range(MLA_HEADS):
        q_ref[0, hh, :, 0:MLA_NOPE_DIM] = q[:, 128 * hh:128 * hh + 128].astype(BF16)
        k_ref[0, hh, :, 0:MLA_NOPE_DIM] = kv[:, 256 * hh:256 * hh + 128].astype(BF16)
        k_ref[0, hh, :, MLA_NOPE_DIM:MLA_QK_DIM] = kpe
        v_ref[:, 256 * hh:256 * hh + 128] = kv[:, 256 * hh + 128:256 * hh + 256].astype(BF16)
        v_ref[:, 256 * hh + 128:256 * hh + 256] = jnp.ones((kv.shape[0], LANES), BF16)
    for s in range(MLA_HEADS // 2):
        slab = _rope128(q[:, pe0 + 128 * s:pe0 + 128 * s + 128], c, sa, sb, half).astype(BF16)
        q_ref[0, 2 * s, :, MLA_NOPE_DIM:MLA_QK_DIM] = slab[:, :MLA_ROPE_DIM]
        q_ref[0, 2 * s + 1, :, MLA_NOPE_DIM:MLA_QK_DIM] = slab[:, MLA_ROPE_DIM:]


def _dsa_prep_kernel(h_ref, w_ref, cd_ref, sad_ref, sbd_ref, ci_ref, sai_ref, sbi_ref,
                     qb_ref, kb_ref, vb_ref, qi_ref, ki_ref, wi_ref, *, w_idx_scale, q_scale):
    h = h_ref[...]
    hd = DSA_HEADS * DSA_HEAD_DIM
    cd, sad, sbd = cd_ref[...], sad_ref[...], sbd_ref[...]
    ci, sai, sbi = ci_ref[...], sai_ref[...], sbi_ref[...]
    hd_half, hi_half = DSA_ROPE_DIM // 2, IDX_ROPE_DIM // 2
    q = _dot(h, w_ref[:, 0:hd]) * q_scale
    for hh in range(DSA_HEADS):
        qb_ref[:, 128 * hh:128 * hh + 128] = _rope128(q[:, 128 * hh:128 * hh + 128], cd, sad, sbd, hd_half).astype(BF16)
    k = _dot(h, w_ref[:, hd:2 * hd])
    for hh in range(DSA_HEADS):
        kb_ref[:, 128 * hh:128 * hh + 128] = _rope128(k[:, 128 * hh:128 * hh + 128], cd, sad, sbd, hd_half).astype(BF16)
    v = _dot(h, w_ref[:, 2 * hd:3 * hd]).astype(BF16)
    for hh in range(DSA_HEADS):
        vb_ref[:, 256 * hh:256 * hh + 128] = v[:, 128 * hh:128 * hh + 128]
        vb_ref[:, 256 * hh + 128:256 * hh + 256] = jnp.ones((v.shape[0], LANES), BF16)
    qi = _dot(h, w_ref[:, 3 * hd:3 * hd + IDX_HEADS * IDX_DIM])
    for s in range(IDX_HEADS // 2):
        slab = _rope128(qi[:, 128 * s:128 * s + 128], ci, sai, sbi, hi_half).astype(BF16)
        qi_ref[0, 2 * s] = slab[:, :IDX_DIM]
        qi_ref[0, 2 * s + 1] = slab[:, IDX_DIM:]
    o = 3 * hd + IDX_HEADS * IDX_DIM
    last = _dot(h, w_ref[:, o:o + 2 * LANES])
    ki_ref[...] = _rope128(last[:, :LANES], ci, sai, sbi, hi_half)[:, :IDX_DIM].astype(BF16)
    wi_ref[0] = (last[:, LANES:] * w_idx_scale).T[:IDX_HEADS, :]


def _flash_update(s, v_ones, m_ref, acc_ref):
    tk = s.shape[1]
    m_prev = m_ref[...]
    m_next = jnp.maximum(m_prev, jnp.max(s, axis=1, keepdims=True))
    p = jnp.exp2(s - jnp.tile(m_next, (1, tk // LANES)))
    alpha = jnp.exp2(m_prev - m_next)
    acc_ref[...] = acc_ref[...] * jnp.tile(alpha, (1, 2)) + _dot(p.astype(BF16), v_ones)
    m_ref[...] = m_next


def _mla_attn_kernel(q_ref, k_ref, v_ref, pq_ref, pk_ref, o_ref, m_sc, acc_sc, *, tile, heads):
    qi = pl.program_id(2)
    m_sc[...] = jnp.full(m_sc.shape, -jnp.inf, F32)
    acc_sc[...] = jnp.zeros(acc_sc.shape, F32)

    def chunk(c, masked):
        ks = pl.ds(pl.multiple_of(c * tile, tile), tile)
        if masked:
            visible = pk_ref[0, c] <= pq_ref[0]
        for hh in range(heads):
            s = _dot_nt(q_ref[0, hh], k_ref[0, hh, ks, :])
            if masked:
                s = jnp.where(visible, s, NEG_BIG)
            _flash_update(s, v_ref[0, ks, 2 * MLA_V_DIM * hh:2 * MLA_V_DIM * (hh + 1)], m_sc.at[hh], acc_sc.at[hh])

    def body(c, carry):
        chunk(c, False)
        return carry

    lax.fori_loop(0, qi, body, 0)
    chunk(qi, True)
    for hh in range(heads):
        o_ref[0, :, MLA_V_DIM * hh:MLA_V_DIM * (hh + 1)] = (
            acc_sc[hh, :, :MLA_V_DIM] / acc_sc[hh, :, MLA_V_DIM:]).astype(o_ref.dtype)


def _idx_select_kernel(q_ref, k_ref, wt_ref, pq_ref, pk_ref, o_ref, key_sc, *, tq, tk, nkc, top_k):
    qi = pl.program_id(1)
    nk = (qi * tq + tq + tk - 1) // tk
    wt = wt_ref[0]
    pq = pq_ref[0]

    def score_chunk(c, carry):
        kc = k_ref[0, pl.ds(pl.multiple_of(c * tk, tk), tk), :]
        acc = jnp.zeros((tk, tq), F32)
        for hh in range(IDX_HEADS):
            acc = acc + wt[hh:hh + 1, :] * jnp.maximum(_dot_nt(kc, q_ref[0, hh]), 0.0)
        bits = pltpu.bitcast(acc, I32)
        key = jnp.where(bits < 0, bits ^ jnp.int32(0x7FFFFFFF), bits)
        key = jnp.where(acc == 0.0, 0, key)
        key_sc[c] = jnp.where(pk_ref[0, c] <= pq, key, INT_MIN)
        return carry

    lax.fori_loop(0, nk, score_chunk, 0)

    def count(pred):
        def body(c, part):
            m = pred(key_sc[c], c).astype(I32)
            return part + jnp.sum(m.reshape(tk // 8, 8, tq), axis=0)
        part = lax.fori_loop(0, nk, body, jnp.zeros((8, tq), I32))
        return jnp.sum(part, axis=0, keepdims=True)

    def bit_step(i, base):
        cand = base ^ (jnp.int32(1) << (31 - i))
        cnt = count(lambda key, c: key >= cand)
        return jnp.where(cnt >= top_k, cand, base)

    thr = lax.fori_loop(0, 32, bit_step, jnp.full((1, tq), INT_MIN, I32))
    n_gt = count(lambda key, c: key > thr)
    n_eq = count(lambda key, c: key == thr)
    need = top_k - n_gt
    tie = jnp.logical_and(n_eq > need, thr != INT_MIN)

    def key_index(c):
        return c * tk + lax.broadcasted_iota(I32, (tk, tq), 0)

    def tie_search():
        def step(i, p):
            cand = p | (jnp.int32(1) << (14 - i))
            cnt = count(lambda key, c: jnp.logical_and(key == thr, key_index(c) < cand))
            return jnp.where(cnt <= need, cand, p)
        return lax.fori_loop(0, 15, step, jnp.zeros((1, tq), I32))

    p_cut = lax.cond(jnp.max(tie.astype(I32)) > 0, tie_search,
                     lambda: jnp.full((1, tq), 2 ** 30, I32))

    def out_chunk(c, carry):
        key = key_sc[c]
        sel = jnp.logical_or(key > thr, jnp.logical_and(key == thr, key_index(c) < p_cut))
        sel = jnp.logical_and(sel, key != INT_MIN)
        o_ref[0, c] = jnp.where(sel, 0.0, NEG_BIG).T.astype(o_ref.dtype)
        return carry

    lax.fori_loop(0, nk, out_chunk, 0)

    def fill_chunk(c, carry):
        o_ref[0, c] = jnp.full((tq, tk), NEG_BIG, o_ref.dtype)
        return carry

    lax.fori_loop(nk, nkc, fill_chunk, 0)


def _dsa_attn_kernel(q_ref, k_ref, v_ref, b_ref, o_ref, m_sc, acc_sc, *, tq, tk):
    qi = pl.program_id(1)
    nk = (qi * tq + tq + tk - 1) // tk
    m_sc[...] = jnp.full(m_sc.shape, NEG_BIG, F32)
    acc_sc[...] = jnp.zeros(acc_sc.shape, F32)

    def chunk(c, carry):
        bias = b_ref[0, c].astype(F32)
        ks = pl.ds(pl.multiple_of(c * tk, tk), tk)
        for hh in range(DSA_HEADS):
            hs = slice(DSA_HEAD_DIM * hh, DSA_HEAD_DIM * (hh + 1))
            vs = slice(2 * DSA_HEAD_DIM * hh, 2 * DSA_HEAD_DIM * (hh + 1))
            s = _dot_nt(q_ref[0, :, hs], k_ref[0, ks, hs]) + bias
            _flash_update(s, v_ref[0, ks, vs], m_sc.at[hh], acc_sc.at[hh])
        return carry

    lax.fori_loop(0, nk, chunk, 0)
    for hh in range(DSA_HEADS):
        hs = slice(DSA_HEAD_DIM * hh, DSA_HEAD_DIM * (hh + 1))
        o_ref[0, :, hs] = (acc_sc[hh, :, :DSA_HEAD_DIM] / acc_sc[hh, :, DSA_HEAD_DIM:]).astype(o_ref.dtype)


def _merge_kernel(h_ref, oa_ref, ob_ref, wga_ref, wgb_ref, woa_ref, wob_ref, y_ref):
    h = h_ref[...]
    ya = jax.nn.sigmoid(_dot(h, wga_ref[...])) * _dot(oa_ref[...], woa_ref[...])
    yb = jax.nn.sigmoid(_dot(h, wgb_ref[...])) * _dot(ob_ref[...], wob_ref[...])
    y_ref[...] = (ya + yb).astype(y_ref.dtype)


def _out_ln_kernel(y_ref, h_ref, wout_ref, g_ref, b_ref, wr_ref, br_ref, o_ref, op_ref, lg_ref, *, alpha):
    sub = 128
    half = o_ref.shape[1] // 2
    for r in range(y_ref.shape[0] // sub):
        rows = slice(r * sub, (r + 1) * sub)
        mix = _dot(y_ref[rows, :], wout_ref[...])
        h1 = _layer_norm(alpha * h_ref[rows, :] + mix, g_ref[...], b_ref[...])
        o_ref[rows, :] = h1
        lo = pltpu.bitcast(h1[:, :half].astype(BF16).astype(F32), I32)
        hi = pltpu.bitcast(h1[:, half:].astype(BF16).astype(F32), I32)
        op_ref[rows, :] = lax.shift_right_logical(lo, 16) | hi
        lg_ref[rows, :] = jnp.dot(h1, wr_ref[...], preferred_element_type=F32,
                                  precision=lax.Precision.HIGHEST) + br_ref[...]


def _route_kernel(lg_ref, meta_ref, wgt_ref, cnt_ref, carry_sc, *, tr):
    @pl.when(pl.program_id(0) == 0)
    def _():
        carry_sc[...] = jnp.zeros(carry_sc.shape, F32)

    lg = lg_ref[...]
    lane = lax.broadcasted_iota(I32, (tr, LANES), 1)
    gl = jnp.where(lane < N_GROUPS, lg, -jnp.inf)
    gmax = jnp.max(gl, axis=1, keepdims=True)
    g_idx = jnp.min(jnp.where(gl == gmax, lane, LANES), axis=1, keepdims=True)
    g_p = 1.0 / jnp.sum(jnp.exp(gl - gmax), axis=1, keepdims=True)
    lo = N_GROUPS + EXPERTS_PER_GROUP * g_idx
    in_group = jnp.logical_and(lane >= lo, lane < lo + EXPERTS_PER_GROUP)
    el = jnp.where(in_group, lg, -jnp.inf)
    emax = jnp.max(el, axis=1, keepdims=True)
    e1 = jnp.min(jnp.where(el == emax, lane, LANES), axis=1, keepdims=True)
    den = jnp.sum(jnp.exp(el - emax), axis=1, keepdims=True)
    el2 = jnp.where(lane == e1, -jnp.inf, el)
    emax2 = jnp.max(el2, axis=1, keepdims=True)
    e2 = jnp.min(jnp.where(el2 == emax2, lane, LANES), axis=1, keepdims=True)
    p1 = 1.0 / den
    p2 = jnp.exp(emax2 - emax) / den
    w1 = g_p * (p1 / (p1 + p2))
    w2 = g_p * (p2 / (p1 + p2))

    is1 = lane == e1
    is2 = lane == e2
    onehot = jnp.logical_or(is1, is2).astype(BF16)
    r = lax.broadcasted_iota(I32, (tr, tr), 0)
    cidx = lax.broadcasted_iota(I32, (tr, tr), 1)
    lower = (cidx < r).astype(BF16)
    before = _dot(lower, onehot) + carry_sc[...]
    rank1 = jnp.sum(jnp.where(is1, before, 0.0), axis=1, keepdims=True)
    rank2 = jnp.sum(jnp.where(is2, before, 0.0), axis=1, keepdims=True)
    carry_sc[...] = carry_sc[...] + jnp.sum(onehot.astype(F32), axis=0, keepdims=True)

    meta = jnp.where(lane == 0, e1 - N_GROUPS,
                     jnp.where(lane == 1, e2 - N_GROUPS,
                               jnp.where(lane == 2, rank1.astype(I32),
                                         jnp.where(lane == 3, rank2.astype(I32), 0))))
    meta_ref[...] = meta
    wgt_ref[...] = jnp.where(lane == 0, w1, jnp.where(lane == 1, w2, 0.0))
    cnt_ref[...] = jnp.broadcast_to(carry_sc[...], cnt_ref.shape)


def _row_copy(src, src_row, dst, dst_row, sem):
    return pltpu.make_async_copy(src.at[pl.ds(src_row, 1)], dst.at[pl.ds(dst_row, 1)], sem)


def _dest_kernel(meta_ref, ps_ref, o_ref):
    meta = meta_ref[...]
    ps = ps_ref[...]
    lane = lax.broadcasted_iota(I32, meta.shape, 1)
    d1 = jnp.sum(jnp.where(lane == meta[:, 0:1], ps, 0), axis=1, keepdims=True) + meta[:, 2:3]
    d2 = jnp.sum(jnp.where(lane == meta[:, 1:2], ps, 0), axis=1, keepdims=True) + meta[:, 3:4]
    o_ref[...] = jnp.where(lane == 0, d1, jnp.where(lane == 1, d2, 0))


def _dispatch_kernel(dest_ref, zblk_ref, nz_ref, h_ref, xs_out, zero_sc, sem, zsem, *, tt):
    base = pl.program_id(0) * tt
    rb = zero_sc.shape[0]

    @pl.when(pl.program_id(0) == 0)
    def _():
        zero_sc[...] = jnp.zeros(zero_sc.shape, zero_sc.dtype)

        def fill(j, carry):
            @pl.when(zblk_ref[j] != 0)
            def _():
                pltpu.make_async_copy(zero_sc, xs_out.at[pl.ds(pl.multiple_of(j * rb, rb), rb)], zsem).start()
            return carry

        def drain(j, carry):
            pltpu.make_async_copy(zero_sc, xs_out.at[pl.ds(0, rb)], zsem).wait()
            return carry

        lax.fori_loop(0, zblk_ref.shape[0], fill, 0)
        lax.fori_loop(0, nz_ref[0], drain, 0)

    def issue(t, carry):
        tok = base + t
        _row_copy(h_ref, t, xs_out, dest_ref[2 * tok], sem).start()
        _row_copy(h_ref, t, xs_out, dest_ref[2 * tok + 1], sem).start()
        return carry

    lax.fori_loop(0, tt, issue, 0, unroll=8)
    for _ in range(2):
        pltpu.make_async_copy(h_ref, xs_out.at[pl.ds(0, tt)], sem).wait()


def _experts_kernel(be_ref, grp_ref, nxt_ref, nxt2_ref, nu_ref, xs_ref, wg_hbm, wu_hbm, wd_hbm, o_ref,
                    wg_buf, wu_buf, wd_buf, wg_sc, wu_sc, wd_sc, sem, *, layer):
    i = pl.program_id(0)
    e = be_ref[i]
    used = i < nu_ref[0]
    first = jnp.logical_and(used, jnp.logical_or(i == 0, e != be_ref[jnp.maximum(i - 1, 0)]))
    slot = grp_ref[i] % 2

    def copies(expert, s):
        out = []
        for j, (hbm, buf) in enumerate(((wg_hbm, wg_buf), (wu_hbm, wu_buf), (wd_hbm, wd_buf))):
            rows = buf.shape[1] // WEIGHT_DMA_CHUNKS
            for c in range(WEIGHT_DMA_CHUNKS):
                rs = pl.ds(c * rows, rows)
                out.append(pltpu.make_async_copy(hbm.at[layer, expert, rs], buf.at[s, rs], sem.at[s, j]))
        return out

    @pl.when(i == 0)
    def _():
        for cp in copies(e, slot):
            cp.start()

        @pl.when(nxt_ref[0] >= 0)
        def _():
            for cp in copies(nxt_ref[0], 1 - slot):
                cp.start()

    @pl.when(first)
    def _():
        for cp in copies(e, slot):
            cp.wait()
        wg_sc[...] = wg_buf[slot].astype(BF16)
        wu_sc[...] = wu_buf[slot].astype(BF16)
        wd_sc[...] = wd_buf[slot].astype(BF16)
        nxt2 = nxt2_ref[i]

        @pl.when(nxt2 >= 0)
        def _():
            for cp in copies(nxt2, slot):
                cp.start()

    @pl.when(used)
    def _():
        xp = xs_ref[...]
        x = jnp.concatenate([pltpu.bitcast(xp << 16, F32).astype(BF16),
                             pltpu.bitcast(xp & jnp.int32(-65536), F32).astype(BF16)], axis=1)
        hid = jax.nn.silu(_dot(x, wg_sc[...])) * _dot(x, wu_sc[...])
        o_ref[...] = _dot(hid.astype(BF16), wd_sc[...])

    @pl.when(jnp.logical_not(used))
    def _():
        o_ref[...] = jnp.zeros(o_ref.shape, o_ref.dtype)


def _combine_kernel(dest_ref, h_ref, wgt_ref, g_ref, b_ref, ys_hbm, o_ref, ob_ref, ybuf, sem, *, tc, alpha):
    i = pl.program_id(0)

    def gather(tile, slot):
        def issue(t, carry):
            tok = tile * tc + t
            _row_copy(ys_hbm, dest_ref[2 * tok], ybuf.at[slot], t, sem.at[slot]).start()
            _row_copy(ys_hbm, dest_ref[2 * tok + 1], ybuf.at[slot], tc + t, sem.at[slot]).start()
            return carry
        lax.fori_loop(0, tc, issue, 0, unroll=8)

    @pl.when(i == 0)
    def _():
        gather(0, 0)

    @pl.when(i + 1 < pl.num_programs(0))
    def _():
        gather(i + 1, (i + 1) % 2)

    slot = i % 2
    pltpu.make_async_copy(ys_hbm.at[pl.ds(0, 2 * tc)], ybuf.at[slot], sem.at[slot]).wait()
    wgt = wgt_ref[...]
    ff = ybuf[slot, 0:tc, :] * wgt[:, 0:1] + ybuf[slot, tc:2 * tc, :] * wgt[:, 1:2]
    h2 = _layer_norm(alpha * h_ref[...] + ff, g_ref[...], b_ref[...])
    o_ref[...] = h2
    ob_ref[...] = h2.astype(BF16)


def _tiles(seq):
    big = seq >= 2048
    return dict(
        rows=256 if big else 128,
        out_rows=256 if big else 128,
        mla=512 if big else 128,
        mla_heads=8,
        idx_q=256 if big else 64,
        dsa_q=512 if big else 128,
        key=512 if big else 128,
        route=512 if big else 128,
        moe_tok=512 if big else 128,
        comb=128,
    )


def _rope_tables(positions, rot_dim, head_dim):
    half = rot_dim // 2
    inv = ROPE_THETA ** (-jnp.arange(0, rot_dim, 2, dtype=F32) / rot_dim)
    ang = positions.astype(F32).reshape(-1, 1) * inv
    cos, sin = jnp.cos(ang), jnp.sin(ang)
    t = cos.shape[0]
    rest = head_dim - rot_dim
    ones, zeros, zh = jnp.ones((t, rest), F32), jnp.zeros((t, rest), F32), jnp.zeros((t, half), F32)
    rep = LANES // head_dim
    c = jnp.tile(jnp.concatenate([cos, cos, ones], axis=1), (1, rep))
    sa = jnp.tile(jnp.concatenate([-sin, zh, zeros], axis=1), (1, rep))
    sb = jnp.tile(jnp.concatenate([zh, sin, zeros], axis=1), (1, rep))
    return c, sa, sb


def _layer(layer, h, h_bf, positions, tabs, lw, alpha, B, S):
    T, D = h.shape
    tl = _tiles(S)
    tm = tl["rows"]
    nrow = T // tm
    spb = S // tm
    (c_m, sa_m, sb_m), (c_d, sa_d, sb_d), (c_i, sa_i, sb_i) = tabs
    row = lambda w: pl.BlockSpec((tm, w), lambda i: (i, 0))
    head_major = lambda nh, w: pl.BlockSpec((1, nh, tm, w), lambda i: (i // spb, 0, i % spb, 0))

    log2e = math.log2(math.e)
    q_a, k_a, v_a = pl.pallas_call(
        functools.partial(_mla_prep_kernel, q_scale=log2e * MLA_QK_DIM ** -0.5),
        grid=(nrow,),
        in_specs=[row(D), _resident_layer(lw["w_mla_in"], layer), _resident((1, MLA_Q_LORA)), _resident((1, MLA_KV_LORA)),
                  _resident(lw["w_q_b"].shape), _resident(lw["w_kv_b"].shape), row(LANES), row(LANES), row(LANES)],
        out_specs=[head_major(MLA_HEADS, MLA_QK_DIM), head_major(MLA_HEADS, MLA_QK_DIM), row(2 * MLA_HEADS * MLA_V_DIM)],
        out_shape=[jax.ShapeDtypeStruct((B, MLA_HEADS, S, MLA_QK_DIM), BF16),
                   jax.ShapeDtypeStruct((B, MLA_HEADS, S, MLA_QK_DIM), BF16),
                   jax.ShapeDtypeStruct((T, 2 * MLA_HEADS * MLA_V_DIM), BF16)],
        compiler_params=_params("parallel"),
    )(h_bf, lw["w_mla_in"], lw["g_q"], lw["g_kv"], lw["w_q_b"], lw["w_kv_b"], c_m, sa_m, sb_m)

    hd = DSA_HEADS * DSA_HEAD_DIM
    q_b, k_b, v_b, q_idx, k_idx, w_idx = pl.pallas_call(
        functools.partial(_dsa_prep_kernel, w_idx_scale=(IDX_HEADS ** -0.5) * (IDX_DIM ** -0.5),
                          q_scale=log2e * DSA_HEAD_DIM ** -0.5),
        grid=(nrow,),
        in_specs=[row(D), _resident_layer(lw["w_dsa_in"], layer)] + [row(LANES)] * 6,
        out_specs=[row(hd), row(hd), row(2 * hd), head_major(IDX_HEADS, IDX_DIM), row(IDX_DIM),
                   pl.BlockSpec((1, IDX_HEADS, tm), lambda i: (i // spb, 0, i % spb))],
        out_shape=[jax.ShapeDtypeStruct((T, hd), BF16)] * 2 + [jax.ShapeDtypeStruct((T, 2 * hd), BF16)] + [
            jax.ShapeDtypeStruct((B, IDX_HEADS, S, IDX_DIM), BF16),
            jax.ShapeDtypeStruct((T, IDX_DIM), BF16),
            jax.ShapeDtypeStruct((B, IDX_HEADS, S), F32)],
        compiler_params=_params("parallel"),
    )(h_bf, lw["w_dsa_in"], c_d, sa_d, sb_d, c_i, sa_i, sb_i)

    ta = tl["mla"]
    pos_q = positions.reshape(B, S, 1)
    pos_k_mla = positions.reshape(B, S // ta, 1, ta)
    hp = tl["mla_heads"]
    o_a = pl.pallas_call(
        functools.partial(_mla_attn_kernel, tile=ta, heads=hp),
        grid=(B, MLA_HEADS // hp, S // ta),
        in_specs=[pl.BlockSpec((1, hp, ta, MLA_QK_DIM), lambda b, g, i: (b, g, i, 0)),
                  pl.BlockSpec((1, hp, S, MLA_QK_DIM), lambda b, g, i: (b, g, 0, 0), pipeline_mode=pl.Buffered(1)),
                  pl.BlockSpec((1, S, 2 * MLA_V_DIM * hp), lambda b, g, i: (b, 0, g), pipeline_mode=pl.Buffered(1)),
                  pl.BlockSpec((1, ta, 1), lambda b, g, i: (b, i, 0)),
                  pl.BlockSpec((1, S // ta, 1, ta), lambda b, g, i: (b, 0, 0, 0))],
        out_specs=pl.BlockSpec((1, ta, MLA_V_DIM * hp), lambda b, g, i: (b, i, g)),
        out_shape=jax.ShapeDtypeStruct((B, S, MLA_HEADS * MLA_V_DIM), BF16),
        scratch_shapes=[pltpu.VMEM((hp, ta, LANES), F32), pltpu.VMEM((hp, ta, 2 * MLA_V_DIM), F32)],
        compiler_params=_params("parallel", "parallel", "arbitrary"),
    )(q_a, k_a, v_a.reshape(B, S, -1), pos_q, pos_k_mla)

    tk = tl["key"]
    nkc = S // tk
    tqi = tl["idx_q"]
    top_k = min(DSA_MAX_TOPK, S // 4)
    sel_bias = pl.pallas_call(
        functools.partial(_idx_select_kernel, tq=tqi, tk=tk, nkc=nkc, top_k=top_k),
        grid=(B, S // tqi),
        in_specs=[pl.BlockSpec((1, IDX_HEADS, tqi, IDX_DIM), lambda b, i: (b, 0, i, 0)),
                  pl.BlockSpec((1, S, IDX_DIM), lambda b, i: (b, 0, 0)),
                  pl.BlockSpec((1, IDX_HEADS, tqi), lambda b, i: (b, 0, i)),
                  pl.BlockSpec((1, 1, tqi), lambda b, i: (b, 0, i)),
                  pl.BlockSpec((1, nkc, tk, 1), lambda b, i: (b, 0, 0, 0))],
        out_specs=pl.BlockSpec((1, nkc, tqi, tk), lambda b, i: (b, 0, i, 0)),
        out_shape=jax.ShapeDtypeStruct((B, nkc, S, tk), BF16),
        scratch_shapes=[pltpu.VMEM((nkc, tk, tqi), I32)],
        compiler_params=_params("parallel", "arbitrary"),
    )(q_idx, k_idx.reshape(B, S, IDX_DIM), w_idx, positions.reshape(B, 1, S), positions.reshape(B, nkc, tk, 1))

    tqd = tl["dsa_q"]
    o_b = pl.pallas_call(
        functools.partial(_dsa_attn_kernel, tq=tqd, tk=tk),
        grid=(B, S // tqd),
        in_specs=[pl.BlockSpec((1, tqd, hd), lambda b, i: (b, i, 0)),
                  pl.BlockSpec((1, S, hd), lambda b, i: (b, 0, 0), pipeline_mode=pl.Buffered(1)),
                  pl.BlockSpec((1, S, 2 * hd), lambda b, i: (b, 0, 0), pipeline_mode=pl.Buffered(1)),
                  pl.BlockSpec((1, nkc, tqd, tk), lambda b, i: (b, 0, i, 0))],
        out_specs=pl.BlockSpec((1, tqd, hd), lambda b, i: (b, i, 0)),
        out_shape=jax.ShapeDtypeStruct((B, S, hd), BF16),
        scratch_shapes=[pltpu.VMEM((DSA_HEADS, tqd, LANES), F32), pltpu.VMEM((DSA_HEADS, tqd, 2 * DSA_HEAD_DIM), F32)],
        compiler_params=_params("parallel", "arbitrary"),
    )(q_b.reshape(B, S, hd), k_b.reshape(B, S, hd), v_b.reshape(B, S, 2 * hd), sel_bias)

    y = pl.pallas_call(
        _merge_kernel,
        grid=(nrow,),
        in_specs=[row(D), row(MLA_HEADS * MLA_V_DIM), row(hd),
                  _resident_layer(lw["w_gate_a"], layer), _resident_layer(lw["w_gate_b"], layer),
                  _resident((MLA_HEADS * MLA_V_DIM, D)), _resident((hd, D))],
        out_specs=row(D),
        out_shape=jax.ShapeDtypeStruct((T, D), BF16),
        compiler_params=_params("parallel"),
    )(h_bf, o_a.reshape(T, -1), o_b.reshape(T, hd), lw["w_gate_a"], lw["w_gate_b"], lw["w_o_a"], lw["w_o_b"])

    to = tl["out_rows"]
    orow = lambda w: pl.BlockSpec((to, w), lambda i: (i, 0))
    h1, h1_packed, logits = pl.pallas_call(
        functools.partial(_out_ln_kernel, alpha=alpha),
        grid=(T // to,),
        in_specs=[orow(D), orow(D), _resident((D, D)), _resident((1, D)), _resident((1, D)),
                  _resident((D, LANES)), _resident((1, LANES))],
        out_specs=[orow(D), orow(D // 2), orow(LANES)],
        out_shape=[jax.ShapeDtypeStruct((T, D), F32), jax.ShapeDtypeStruct((T, D // 2), I32),
                   jax.ShapeDtypeStruct((T, LANES), F32)],
        compiler_params=_params("parallel"),
    )(y, h, lw["w_out"], lw["ln1_g"], lw["ln1_b"], lw["w_route"], lw["b_route"])

    tr = tl["route"]
    meta, wgt, cnt = pl.pallas_call(
        functools.partial(_route_kernel, tr=tr),
        grid=(T // tr,),
        in_specs=[pl.BlockSpec((tr, LANES), lambda i: (i, 0))],
        out_specs=[pl.BlockSpec((tr, LANES), lambda i: (i, 0)), pl.BlockSpec((tr, LANES), lambda i: (i, 0)),
                   pl.BlockSpec((8, LANES), lambda i: (0, 0))],
        out_shape=[jax.ShapeDtypeStruct((T, LANES), I32), jax.ShapeDtypeStruct((T, LANES), F32),
                   jax.ShapeDtypeStruct((8, LANES), F32)],
        scratch_shapes=[pltpu.VMEM((1, LANES), F32)],
        compiler_params=_params("arbitrary"),
    )(logits)

    rb = MOE_ROW_BLOCK
    counts = cnt[0, N_GROUPS:N_GROUPS + N_EXPERTS].astype(I32)
    padded = ((counts + rb - 1) // rb) * rb
    pends = jnp.cumsum(padded)
    pstarts = pends - padded
    n_blocks = -(-(2 * T) // rb) + N_EXPERTS
    P = n_blocks * rb
    blk = jnp.arange(n_blocks, dtype=I32)
    block_e = jnp.minimum(jnp.sum((pends[None, :] <= (blk * rb)[:, None]).astype(I32), axis=1), N_EXPERTS - 1)
    n_used = (pends[-1] // rb).astype(I32).reshape(1)
    ps_lanes = jnp.zeros((1, LANES), I32).at[0, :N_EXPERTS].set(pstarts.astype(I32))
    dest = pl.pallas_call(
        _dest_kernel,
        grid=(T // tr,),
        in_specs=[pl.BlockSpec((tr, LANES), lambda i: (i, 0)), pl.BlockSpec((1, LANES), lambda i: (0, 0))],
        out_specs=pl.BlockSpec((tr, LANES), lambda i: (i, 0)),
        out_shape=jax.ShapeDtypeStruct((T, LANES), I32),
        compiler_params=_params("parallel"),
    )(meta, ps_lanes)[:, 0:2].reshape(-1)

    last_of_expert = jnp.concatenate([block_e[1:] != block_e[:-1], jnp.ones((1,), bool)])
    needs_zero = jnp.logical_or(blk >= n_used[0], jnp.logical_or(last_of_expert, blk == n_used[0] - 1)).astype(I32)
    n_zero = jnp.sum(needs_zero).astype(I32).reshape(1)
    tt = tl["moe_tok"]
    xs = pl.pallas_call(
        functools.partial(_dispatch_kernel, tt=tt),
        grid_spec=pltpu.PrefetchScalarGridSpec(
            num_scalar_prefetch=3, grid=(T // tt,),
            in_specs=[pl.BlockSpec((tt, D // 2), lambda i, *_: (i, 0))],
            out_specs=pl.BlockSpec(memory_space=pl.ANY),
            scratch_shapes=[pltpu.VMEM((rb, D // 2), I32), pltpu.SemaphoreType.DMA(()), pltpu.SemaphoreType.DMA(())]),
        out_shape=jax.ShapeDtypeStruct((P, D // 2), I32),
        compiler_params=_params("arbitrary"),
    )(dest, needs_zero, n_zero, h1_packed)

    F = D_EXPERT
    grp = jnp.cumsum(jnp.concatenate([jnp.zeros((1,), I32), (block_e[1:] != block_e[:-1]).astype(I32)]))
    ids = jnp.where(counts > 0, jnp.arange(N_EXPERTS, dtype=I32), N_EXPERTS)
    at_or_after = lax.cummin(ids, axis=0, reverse=True)
    after = jnp.concatenate([at_or_after[1:], jnp.full((1,), N_EXPERTS, I32)])
    nxt_of = jnp.where(after < N_EXPERTS, after, -1)
    nxt2_of = jnp.where(nxt_of >= 0, nxt_of[jnp.maximum(nxt_of, 0)], -1)
    nxt = nxt_of[block_e].astype(I32)
    nxt2 = nxt2_of[block_e].astype(I32)
    any_spec = pl.BlockSpec(memory_space=pl.ANY)
    ys = pl.pallas_call(
        functools.partial(_experts_kernel, layer=layer),
        grid_spec=pltpu.PrefetchScalarGridSpec(
            num_scalar_prefetch=5, grid=(n_blocks,),
            in_specs=[pl.BlockSpec((rb, D // 2), lambda i, be, gr, nx, nx2, nu: (jnp.minimum(i, nu[0] - 1), 0)),
                      any_spec, any_spec, any_spec],
            out_specs=pl.BlockSpec((rb, D), lambda i, *_: (i, 0)),
            scratch_shapes=[pltpu.VMEM((2, D, F), F32), pltpu.VMEM((2, D, F), F32), pltpu.VMEM((2, F, D), F32),
                            pltpu.VMEM((D, F), BF16), pltpu.VMEM((D, F), BF16), pltpu.VMEM((F, D), BF16),
                            pltpu.SemaphoreType.DMA((2, 3))]),
        out_shape=jax.ShapeDtypeStruct((P, D), F32),
        compiler_params=_params("arbitrary"),
    )(block_e, grp.astype(I32), nxt, nxt2, n_used, xs, lw["w_e_gate"], lw["w_e_up"], lw["w_e_down"])

    tc = tl["comb"]
    h2, h2_bf = pl.pallas_call(
        functools.partial(_combine_kernel, tc=tc, alpha=alpha),
        grid_spec=pltpu.PrefetchScalarGridSpec(
            num_scalar_prefetch=1, grid=(T // tc,),
            in_specs=[pl.BlockSpec((tc, D), lambda i, d: (i, 0)),
                      pl.BlockSpec((tc, LANES), lambda i, d: (i, 0)),
                      pl.BlockSpec((1, D), lambda i, d: (0, 0)),
                      pl.BlockSpec((1, D), lambda i, d: (0, 0)),
                      pl.BlockSpec(memory_space=pl.ANY)],
            out_specs=[pl.BlockSpec((tc, D), lambda i, d: (i, 0)), pl.BlockSpec((tc, D), lambda i, d: (i, 0))],
            scratch_shapes=[pltpu.VMEM((2, 2 * tc, D), F32), pltpu.SemaphoreType.DMA((2,))]),
        out_shape=[jax.ShapeDtypeStruct((T, D), F32), jax.ShapeDtypeStruct((T, D), BF16)],
        compiler_params=_params("arbitrary"),
    )(dest, h1, wgt, lw["ln2_g"], lw["ln2_b"], ys)
    return h2, h2_bf


def _layer_weights(l, regrouped, g_q_lora, w_q_b, g_kv_lora, w_kv_b, w_o_a, w_o_b, w_out, ln1_g, ln1_b,
                   w_group, b_group, w_router, b_router, w_e_gate, w_e_up, w_e_down, ln2_g, ln2_b):
    w_mla_in, w_dsa_in, w_gate_a, w_gate_b = regrouped
    D = w_gate_a.shape[1]
    wq = w_q_b[l].reshape(MLA_Q_LORA, MLA_HEADS, MLA_QK_DIM)
    wq = jnp.concatenate([wq[:, :, :MLA_NOPE_DIM].reshape(MLA_Q_LORA, -1),
                          wq[:, :, MLA_NOPE_DIM:].reshape(MLA_Q_LORA, -1)], axis=1).astype(BF16)
    w_route = jnp.concatenate([w_group[l], w_router[l],
                               jnp.zeros((D, LANES - N_GROUPS - N_EXPERTS), F32)], axis=1)
    b_route = jnp.concatenate([b_group[l], b_router[l],
                               jnp.zeros((LANES - N_GROUPS - N_EXPERTS,), F32)]).reshape(1, LANES)
    return dict(
        w_mla_in=w_mla_in, w_dsa_in=w_dsa_in, w_gate_a=w_gate_a, w_gate_b=w_gate_b,
        g_q=g_q_lora[l].reshape(1, -1), g_kv=g_kv_lora[l].reshape(1, -1),
        w_q_b=wq, w_kv_b=w_kv_b[l].astype(BF16),
        w_o_a=w_o_a[l].astype(BF16), w_o_b=w_o_b[l].astype(BF16), w_out=w_out[l].astype(BF16),
        ln1_g=ln1_g[l].reshape(1, -1), ln1_b=ln1_b[l].reshape(1, -1),
        w_route=w_route, b_route=b_route,
        w_e_gate=w_e_gate, w_e_up=w_e_up, w_e_down=w_e_down,
        ln2_g=ln2_g[l].reshape(1, -1), ln2_b=ln2_b[l].reshape(1, -1),
    )


def kernel(x, positions, w_in, g_q_lora, w_q_b, g_kv_lora, w_kv_b, w_o_a, w_o_b, w_out, ln1_g, ln1_b,
           w_group, b_group, w_router, b_router, w_e_gate, w_e_up, w_e_down, ln2_g, ln2_b):
    B, S, D = x.shape
    depth = w_in.shape[0]
    alpha = (2 * depth) ** 0.25
    tabs = (_rope_tables(positions, MLA_ROPE_DIM, MLA_ROPE_DIM),
            _rope_tables(positions, DSA_ROPE_DIM, DSA_HEAD_DIM),
            _rope_tables(positions, IDX_ROPE_DIM, IDX_DIM))
    h = x.reshape(B * S, D)
    h_bf = h.astype(BF16)
    regrouped = _regroup_w_in(w_in)
    for l in range(depth):
        lw = _layer_weights(l, regrouped, g_q_lora, w_q_b, g_kv_lora, w_kv_b, w_o_a, w_o_b, w_out, ln1_g, ln1_b,
                            w_group, b_group, w_router, b_router, w_e_gate, w_e_up, w_e_down, ln2_g, ln2_b)
        h, h_bf = _layer(l, h, h_bf, positions, tabs, lw, alpha, B, S)
    return h.reshape(B, S, D)
```

```python
import functools
import math

import jax
import jax.numpy as jnp
from jax import lax
from jax.experimental import pallas as pl
from jax.experimental.pallas import tpu as pltpu

F32 = jnp.float32
BF16 = jnp.bfloat16
I32 = jnp.int32

MLA_HEADS = 8
MLA_Q_LORA = 512
MLA_KV_LORA = 512
MLA_NOPE_DIM = 128
MLA_ROPE_DIM = 64
MLA_V_DIM = 128
MLA_QK_DIM = MLA_NOPE_DIM + MLA_ROPE_DIM
DSA_HEADS = 8
DSA_HEAD_DIM = 128
DSA_ROPE_DIM = DSA_HEAD_DIM // 4
IDX_HEADS = 16
IDX_DIM = 64
IDX_ROPE_DIM = IDX_DIM // 4
DSA_MAX_TOPK = 256
ROPE_THETA = 500000.0
N_GROUPS = 8
EXPERTS_PER_GROUP = 8
N_EXPERTS = N_GROUPS * EXPERTS_PER_GROUP
D_EXPERT = 512
MOE_ROW_BLOCK = 128
LN_EPS = 1e-5
RMS_EPS = 1e-6

LANES = 128
NEG_BIG = -1e30
INT_MIN = -(2 ** 31)
VMEM_LIMIT = 56 * 1024 * 1024
WEIGHT_DMA_CHUNKS = 4


def _params(*sem):
    return pltpu.CompilerParams(dimension_semantics=sem, vmem_limit_bytes=VMEM_LIMIT)


def _resident(shape):
    nd = len(shape)
    return pl.BlockSpec(shape, lambda *_: (0,) * nd, pipeline_mode=pl.Buffered(1))


def _resident_layer(stacked, layer):
    shape = stacked.shape[1:]
    nd = len(shape)
    return pl.BlockSpec((None,) + shape, lambda *_: (layer,) + (0,) * nd, pipeline_mode=pl.Buffered(1))


def _dot(a, b):
    return jnp.dot(a, b, preferred_element_type=F32)


def _dot_nt(a, b):
    return lax.dot_general(a, b, (((1,), (1,)), ((), ())), preferred_element_type=F32)


def _rope128(x, c, sa, sb, half):
    return x * c + pltpu.roll(x, LANES - half, 1) * sa + pltpu.roll(x, half, 1) * sb


def _rms(x, g):
    return x * lax.rsqrt(jnp.mean(x * x, axis=-1, keepdims=True) + RMS_EPS) * g


def _layer_norm(x, g, b):
    mu = jnp.mean(x, axis=-1, keepdims=True)
    xc = x - mu
    var = jnp.mean(xc * xc, axis=-1, keepdims=True)
    return xc * lax.rsqrt(var + LN_EPS) * g + b


_O_KPE = MLA_Q_LORA + MLA_KV_LORA
_O_DSA = _O_KPE + MLA_ROPE_DIM
_O_KIDX = _O_DSA + 3 * DSA_HEADS * DSA_HEAD_DIM + IDX_HEADS * IDX_DIM
_O_WIDX = _O_KIDX + IDX_DIM
_O_GATE = _O_WIDX + IDX_HEADS
_W_MLA_IN = _O_KPE + 2 * MLA_ROPE_DIM
_W_DSA_IN = (_O_KIDX - _O_DSA) + 2 * IDX_DIM + LANES


def _regroup_w_in_kernel(w_ref, mla_ref, dsa_ref, ga_ref, gb_ref, *, d_model):
    mla_ref[:, 0:_O_DSA] = w_ref[:, 0:_O_DSA].astype(BF16)
    mla_ref[:, _O_DSA:_W_MLA_IN] = w_ref[:, _O_KPE:_O_DSA].astype(BF16)
    main = _O_KIDX - _O_DSA
    dsa_ref[:, 0:main] = w_ref[:, _O_DSA:_O_KIDX].astype(BF16)
    kidx = w_ref[:, _O_KIDX:_O_WIDX].astype(BF16)
    dsa_ref[:, main:main + IDX_DIM] = kidx
    dsa_ref[:, main + IDX_DIM:main + 2 * IDX_DIM] = kidx
    dsa_ref[:, main + 2 * IDX_DIM:main + 2 * IDX_DIM + IDX_HEADS] = w_ref[:, _O_WIDX:_O_GATE].astype(BF16)
    dsa_ref[:, main + 2 * IDX_DIM + IDX_HEADS:] = jnp.zeros((w_ref.shape[0], LANES - IDX_HEADS), BF16)
    ga_ref[...] = w_ref[:, _O_GATE:_O_GATE + d_model].astype(BF16)
    gb_ref[...] = w_ref[:, _O_GATE + d_model:_O_GATE + 2 * d_model].astype(BF16)


def _regroup_w_in(w_in):
    depth, d_model, width = w_in.shape
    tr = 256
    out_w = (_W_MLA_IN, _W_DSA_IN, d_model, d_model)
    return pl.pallas_call(
        functools.partial(_regroup_w_in_kernel, d_model=d_model),
        grid=(depth, d_model // tr),
        in_specs=[pl.BlockSpec((tr, width), lambda l, i: (l * (d_model // tr) + i, 0))],
        out_specs=[pl.BlockSpec((None, tr, w), lambda l, i: (l, i, 0)) for w in out_w],
        out_shape=[jax.ShapeDtypeStruct((depth, d_model, w), BF16) for w in out_w],
        compiler_params=_params("parallel", "parallel"),
    )(w_in.astype(BF16).reshape(depth * d_model, width))


def _mla_prep_kernel(h_ref, win_ref, gq_ref, gkv_ref, wqb_ref, wkvb_ref, c_ref, sa_ref, sb_ref,
                     q_ref, k_ref, v_ref, *, q_scale):
    h = h_ref[...]
    p = _dot(h, win_ref[...])
    qn = _rms(p[:, :MLA_Q_LORA], gq_ref[...]).astype(BF16)
    kvn = _rms(p[:, MLA_Q_LORA:MLA_Q_LORA + MLA_KV_LORA], gkv_ref[...]).astype(BF16)
    q = _dot(qn, wqb_ref[...]) * q_scale
    kv = _dot(kvn, wkvb_ref[...])
    c, sa, sb = c_ref[...], sa_ref[...], sb_ref[...]
    half = MLA_ROPE_DIM // 2
    kpe = _rope128(p[:, MLA_Q_LORA + MLA_KV_LORA:], c, sa, sb, half)[:, :MLA_ROPE_DIM].astype(BF16)
    pe0 = MLA_HEADS * MLA_NOPE_DIM
    for hh in range(MLA_HEADS):
        q_ref[0, hh, :, 0:MLA_NOPE_DIM] = q[:, 128 * hh:128 * hh + 128].astype(BF16)
        k_ref[0, hh, :, 0:MLA_NOPE_DIM] = kv[:, 256 * hh:256 * hh + 128].astype(BF16)
        k_ref[0, hh, :, MLA_NOPE_DIM:MLA_QK_DIM] = kpe
        v_ref[:, 256 * hh:256 * hh + 128] = kv[:, 256 * hh + 128:256 * hh + 256].astype(BF16)
        v_ref[:, 256 * hh + 128:256 * hh + 256] = jnp.ones((kv.shape[0], LANES), BF16)
    for s in range(MLA_HEADS // 2):
        slab = _rope128(q[:, pe0 + 128 * s:pe0 + 128 * s + 128], c, sa, sb, half).astype(BF16)
        q_ref[0, 2 * s, :, MLA_NOPE_DIM:MLA_QK_DIM] = slab[:, :MLA_ROPE_DIM]
        q_ref[0, 2 * s + 1, :, MLA_NOPE_DIM:MLA_QK_DIM] = slab[:, MLA_ROPE_DIM:]


def _dsa_prep_kernel(h_ref, w_ref, cd_ref, sad_ref, sbd_ref, ci_ref, sai_ref, sbi_ref,
                     qb_ref, kb_ref, vb_ref, qi_ref, ki_ref, wi_ref, *, w_idx_scale, q_scale):
    h = h_ref[...]
    hd = DSA_HEADS * DSA_HEAD_DIM
    cd, sad, sbd = cd_ref[...], sad_ref[...], sbd_ref[...]
    ci, sai, sbi = ci_ref[...], sai_ref[...], sbi_ref[...]
    hd_half, hi_half = DSA_ROPE_DIM // 2, IDX_ROPE_DIM // 2
    q = _dot(h, w_ref[:, 0:hd]) * q_scale
    for hh in range(DSA_HEADS):
        qb_ref[:, 128 * hh:128 * hh + 128] = _rope128(q[:, 128 * hh:128 * hh + 128], cd, sad, sbd, hd_half).astype(BF16)
    k = _dot(h, w_ref[:, hd:2 * hd])
    for hh in range(DSA_HEADS):
        kb_ref[:, 128 * hh:128 * hh + 128] = _rope128(k[:, 128 * hh:128 * hh + 128], cd, sad, sbd, hd_half).astype(BF16)
    v = _dot(h, w_ref[:, 2 * hd:3 * hd]).astype(BF16)
    for hh in range(DSA_HEADS):
        vb_ref[:, 256 * hh:256 * hh + 128] = v[:, 128 * hh:128 * hh + 128]
        vb_ref[:, 256 * hh + 128:256 * hh + 256] = jnp.ones((v.shape[0], LANES), BF16)
    qi = _dot(h, w_ref[:, 3 * hd:3 * hd + IDX_HEADS * IDX_DIM])
    for s in range(IDX_HEADS // 2):
        slab = _rope128(qi[:, 128 * s:128 * s + 128], ci, sai, sbi, hi_half).astype(BF16)
        qi_ref[0, 2 * s] = slab[:, :IDX_DIM]
        qi_ref[0, 2 * s + 1] = slab[:, IDX_DIM:]
    o = 3 * hd + IDX_HEADS * IDX_DIM
    last = _dot(h, w_ref[:, o:o + 2 * LANES])
    ki_ref[...] = _rope128(last[:, :LANES], ci, sai, sbi, hi_half)[:, :IDX_DIM].astype(BF16)
    wi_ref[0] = (last[:, LANES:] * w_idx_scale).T[:IDX_HEADS, :]


def _flash_update(s, v_ones, m_ref, acc_ref):
    tk = s.shape[1]
    m_prev = m_ref[...]
    m_next = jnp.maximum(m_prev, jnp.max(s, axis=1, keepdims=True))
    p = jnp.exp2(s - jnp.tile(m_next, (1, tk // LANES)))
    alpha = jnp.exp2(m_prev - m_next)
    acc_ref[...] = acc_ref[...] * jnp.tile(alpha, (1, 2)) + _dot(p.astype(BF16), v_ones)
    m_ref[...] = m_next


def _mla_attn_kernel(q_ref, k_ref, v_ref, pq_ref, pk_ref, o_ref, m_sc, acc_sc, *, tile, heads):
    qi = pl.program_id(2)
    m_sc[...] = jnp.full(m_sc.shape, -jnp.inf, F32)
    acc_sc[...] = jnp.zeros(acc_sc.shape, F32)

    def chunk(c, masked):
        ks = pl.ds(pl.multiple_of(c * tile, tile), tile)
        if masked:
            visible = pk_ref[0, c] <= pq_ref[0]
        for hh in range(heads):
            s = _dot_nt(q_ref[0, hh], k_ref[0, hh, ks, :])
            if masked:
                s = jnp.where(visible, s, NEG_BIG)
            _flash_update(s, v_ref[0, ks, 2 * MLA_V_DIM * hh:2 * MLA_V_DIM * (hh + 1)], m_sc.at[hh], acc_sc.at[hh])

    def body(c, carry):
        chunk(c, False)
        return carry

    lax.fori_loop(0, qi, body, 0)
    chunk(qi, True)
    for hh in range(heads):
        o_ref[0, :, MLA_V_DIM * hh:MLA_V_DIM * (hh + 1)] = (
            acc_sc[hh, :, :MLA_V_DIM] / acc_sc[hh, :, MLA_V_DIM:]).astype(o_ref.dtype)


def _idx_select_kernel(q_ref, k_ref, wt_ref, pq_ref, pk_ref, o_ref, key_sc, *, tq, tk, nkc, top_k):
    qi = pl.program_id(1)
    nk = (qi * tq + tq + tk - 1) // tk
    wt = wt_ref[0]
    pq = pq_ref[0]

    def score_chunk(c, carry):
        kc = k_ref[0, pl.ds(pl.multiple_of(c * tk, tk), tk), :]
        acc = jnp.zeros((tk, tq), F32)
        for hh in range(IDX_HEADS):
            acc = acc + wt[hh:hh + 1, :] * jnp.maximum(_dot_nt(kc, q_ref[0, hh]), 0.0)
        bits = pltpu.bitcast(acc, I32)
        key = jnp.where(bits < 0, bits ^ jnp.int32(0x7FFFFFFF), bits)
        key = jnp.where(acc == 0.0, 0, key)
        key_sc[c] = jnp.where(pk_ref[0, c] <= pq, key, INT_MIN)
        return carry

    lax.fori_loop(0, nk, score_chunk, 0)

    def count(pred):
        def body(c, part):
            m = pred(key_sc[c], c).astype(I32)
            return part + jnp.sum(m.reshape(tk // 8, 8, tq), axis=0)
        part = lax.fori_loop(0, nk, body, jnp.zeros((8, tq), I32))
        return jnp.sum(part, axis=0, keepdims=True)

    def bit_step(i, base):
        cand = base ^ (jnp.int32(1) << (31 - i))
        cnt = count(lambda key, c: key >= cand)
        return jnp.where(cnt >= top_k, cand, base)

    thr = lax.fori_loop(0, 32, bit_step, jnp.full((1, tq), INT_MIN, I32))
    n_gt = count(lambda key, c: key > thr)
    n_eq = count(lambda key, c: key == thr)
    need = top_k - n_gt
    tie = jnp.logical_and(n_eq > need, thr != INT_MIN)

    def key_index(c):
        return c * tk + lax.broadcasted_iota(I32, (tk, tq), 0)

    def tie_search():
        def step(i, p):
            cand = p | (jnp.int32(1) << (14 - i))
            cnt = count(lambda key, c: jnp.logical_and(key == thr, key_index(c) < cand))
            return jnp.where(cnt <= need, cand, p)
        return lax.fori_loop(0, 15, step, jnp.zeros((1, tq), I32))

    p_cut = lax.cond(jnp.max(tie.astype(I32)) > 0, tie_search,
                     lambda: jnp.full((1, tq), 2 ** 30, I32))

    def out_chunk(c, carry):
        key = key_sc[c]
        sel = jnp.logical_or(key > thr, jnp.logical_and(key == thr, key_index(c) < p_cut))
        sel = jnp.logical_and(sel, key != INT_MIN)
        o_ref[0, c] = jnp.where(sel, 0.0, NEG_BIG).T.astype(o_ref.dtype)
        return carry

    lax.fori_loop(0, nk, out_chunk, 0)

    def fill_chunk(c, carry):
        o_ref[0, c] = jnp.full((tq, tk), NEG_BIG, o_ref.dtype)
        return carry

    lax.fori_loop(nk, nkc, fill_chunk, 0)


def _dsa_attn_kernel(q_ref, k_ref, v_ref, b_ref, o_ref, m_sc, acc_sc, *, tq, tk):
    qi = pl.program_id(1)
    nk = (qi * tq + tq + tk - 1) // tk
    m_sc[...] = jnp.full(m_sc.shape, NEG_BIG, F32)
    acc_sc[...] = jnp.zeros(acc_sc.shape, F32)

    def chunk(c, carry):
        bias = b_ref[0, c].astype(F32)
        ks = pl.ds(pl.multiple_of(c * tk, tk), tk)
        for hh in range(DSA_HEADS):
            hs = slice(DSA_HEAD_DIM * hh, DSA_HEAD_DIM * (hh + 1))
            vs = slice(2 * DSA_HEAD_DIM * hh, 2 * DSA_HEAD_DIM * (hh + 1))
            s = _dot_nt(q_ref[0, :, hs], k_ref[0, ks, hs]) + bias
            _flash_update(s, v_ref[0, ks, vs], m_sc.at[hh], acc_sc.at[hh])
        return carry

    lax.fori_loop(0, nk, chunk, 0)
    for hh in range(DSA_HEADS):
        hs = slice(DSA_HEAD_DIM * hh, DSA_HEAD_DIM * (hh + 1))
        o_ref[0, :, hs] = (acc_sc[hh, :, :DSA_HEAD_DIM] / acc_sc[hh, :, DSA_HEAD_DIM:]).astype(o_ref.dtype)


def _merge_kernel(h_ref, oa_ref, ob_ref, wga_ref, wgb_ref, woa_ref, wob_ref, y_ref):
    h = h_ref[...]
    ya = jax.nn.sigmoid(_dot(h, wga_ref[...])) * _dot(oa_ref[...], woa_ref[...])
    yb = jax.nn.sigmoid(_dot(h, wgb_ref[...])) * _dot(ob_ref[...], wob_ref[...])
    y_ref[...] = (ya + yb).astype(y_ref.dtype)


def _out_ln_kernel(y_ref, h_ref, wout_ref, g_ref, b_ref, wr_ref, br_ref, o_ref, op_ref, lg_ref, *, alpha):
    sub = 128
    half = o_ref.shape[1] // 2
    for r in range(y_ref.shape[0] // sub):
        rows = slice(r * sub, (r + 1) * sub)
        mix = _dot(y_ref[rows, :], wout_ref[...])
        h1 = _layer_norm(alpha * h_ref[rows, :] + mix, g_ref[...], b_ref[...])
        o_ref[rows, :] = h1
        lo = pltpu.bitcast(h1[:, :half].astype(BF16).astype(F32), I32)
        hi = pltpu.bitcast(h1[:, half:].astype(BF16).astype(F32), I32)
        op_ref[rows, :] = lax.shift_right_logical(lo, 16) | hi
        lg_ref[rows, :] = jnp.dot(h1, wr_ref[...], preferred_element_type=F32,
                                  precision=lax.Precision.HIGHEST) + br_ref[...]


def _route_kernel(lg_ref, meta_ref, wgt_ref, cnt_ref, carry_sc, *, tr):
    @pl.when(pl.program_id(0) == 0)
    def _():
        carry_sc[...] = jnp.zeros(carry_sc.shape, F32)

    lg = lg_ref[...]
    lane = lax.broadcasted_iota(I32, (tr, LANES), 1)
    gl = jnp.where(lane < N_GROUPS, lg, -jnp.inf)
    gmax = jnp.max(gl, axis=1, keepdims=True)
    g_idx = jnp.min(jnp.where(gl == gmax, lane, LANES), axis=1, keepdims=True)
    g_p = 1.0 / jnp.sum(jnp.exp(gl - gmax), axis=1, keepdims=True)
    lo = N_GROUPS + EXPERTS_PER_GROUP * g_idx
    in_group = jnp.logical_and(lane >= lo, lane < lo + EXPERTS_PER_GROUP)
    el = jnp.where(in_group, lg, -jnp.inf)
    emax = jnp.max(el, axis=1, keepdims=True)
    e1 = jnp.min(jnp.where(el == emax, lane, LANES), axis=1, keepdims=True)
    den = jnp.sum(jnp.exp(el - emax), axis=1, keepdims=True)
    el2 = jnp.where(lane == e1, -jnp.inf, el)
    emax2 = jnp.max(el2, axis=1, keepdims=True)
    e2 = jnp.min(jnp.where(el2 == emax2, lane, LANES), axis=1, keepdims=True)
    p1 = 1.0 / den
    p2 = jnp.exp(emax2 - emax) / den
    w1 = g_p * (p1 / (p1 + p2))
    w2 = g_p * (p2 / (p1 + p2))

    is1 = lane == e1
    is2 = lane == e2
    onehot = jnp.logical_or(is1, is2).astype(BF16)
    r = lax.broadcasted_iota(I32, (tr, tr), 0)
    cidx = lax.broadcasted_iota(I32, (tr, tr), 1)
    lower = (cidx < r).astype(BF16)
    before = _dot(lower, onehot) + carry_sc[...]
    rank1 = jnp.sum(jnp.where(is1, before, 0.0), axis=1, keepdims=True)
    rank2 = jnp.sum(jnp.where(is2, before, 0.0), axis=1, keepdims=True)
    carry_sc[...] = carry_sc[...] + jnp.sum(onehot.astype(F32), axis=0, keepdims=True)

    meta = jnp.where(lane == 0, e1 - N_GROUPS,
                     jnp.where(lane == 1, e2 - N_GROUPS,
                               jnp.where(lane == 2, rank1.astype(I32),
                                         jnp.where(lane == 3, rank2.astype(I32), 0))))
    meta_ref[...] = meta
    wgt_ref[...] = jnp.where(lane == 0, w1, jnp.where(lane == 1, w2, 0.0))
    cnt_ref[...] = jnp.broadcast_to(carry_sc[...], cnt_ref.shape)


def _row_copy(src, src_row, dst, dst_row, sem):
    return pltpu.make_async_copy(src.at[pl.ds(src_row, 1)], dst.at[pl.ds(dst_row, 1)], sem)


def _dest_kernel(meta_ref, ps_ref, o_ref):
    meta = meta_ref[...]
    ps = ps_ref[...]
    lane = lax.broadcasted_iota(I32, meta.shape, 1)
    d1 = jnp.sum(jnp.where(lane == meta[:, 0:1], ps, 0), axis=1, keepdims=True) + meta[:, 2:3]
    d2 = jnp.sum(jnp.where(lane == meta[:, 1:2], ps, 0), axis=1, keepdims=True) + meta[:, 3:4]
    o_ref[...] = jnp.where(lane == 0, d1, jnp.where(lane == 1, d2, 0))


def _dispatch_kernel(dest_ref, zblk_ref, nz_ref, h_ref, xs_out, zero_sc, sem, zsem, *, tt):
    base = pl.program_id(0) * tt
    rb = zero_sc.shape[0]

    @pl.when(pl.program_id(0) == 0)
    def _():
        zero_sc[...] = jnp.zeros(zero_sc.shape, zero_sc.dtype)

        def fill(j, carry):
            @pl.when(zblk_ref[j] != 0)
            def _():
                pltpu.make_async_copy(zero_sc, xs_out.at[pl.ds(pl.multiple_of(j * rb, rb), rb)], zsem).start()
            return carry

        def drain(j, carry):
            pltpu.make_async_copy(zero_sc, xs_out.at[pl.ds(0, rb)], zsem).wait()
            return carry

        lax.fori_loop(0, zblk_ref.shape[0], fill, 0)
        lax.fori_loop(0, nz_ref[0], drain, 0)

    def issue(t, carry):
        tok = base + t
        _row_copy(h_ref, t, xs_out, dest_ref[2 * tok], sem).start()
        _row_copy(h_ref, t, xs_out, dest_ref[2 * tok + 1], sem).start()
        return carry

    lax.fori_loop(0, tt, issue, 0, unroll=8)
    for _ in range(2):
        pltpu.make_async_copy(h_ref, xs_out.at[pl.ds(0, tt)], sem).wait()


def _experts_kernel(be_ref, grp_ref, nxt_ref, nxt2_ref, nu_ref, xs_ref, wg_hbm, wu_hbm, wd_hbm, o_ref,
                    wg_buf, wu_buf, wd_buf, wg_sc, wu_sc, wd_sc, sem, *, layer):
    i = pl.program_id(0)
    e = be_ref[i]
    used = i < nu_ref[0]
    first = jnp.logical_and(used, jnp.logical_or(i == 0, e != be_ref[jnp.maximum(i - 1, 0)]))
    slot = grp_ref[i] % 2

    def copies(expert, s):
        out = []
        for j, (hbm, buf) in enumerate(((wg_hbm, wg_buf), (wu_hbm, wu_buf), (wd_hbm, wd_buf))):
            rows = buf.shape[1] // WEIGHT_DMA_CHUNKS
            for c in range(WEIGHT_DMA_CHUNKS):
                rs = pl.ds(c * rows, rows)
                out.append(pltpu.make_async_copy(hbm.at[layer, expert, rs], buf.at[s, rs], sem.at[s, j]))
        return out

    @pl.when(i == 0)
    def _():
        for cp in copies(e, slot):
            cp.start()

        @pl.when(nxt_ref[0] >= 0)
        def _():
            for cp in copies(nxt_ref[0], 1 - slot):
                cp.start()

    @pl.when(first)
    def _():
        for cp in copies(e, slot):
            cp.wait()
        wg_sc[...] = wg_buf[slot].astype(BF16)
        wu_sc[...] = wu_buf[slot].astype(BF16)
        wd_sc[...] = wd_buf[slot].astype(BF16)
        nxt2 = nxt2_ref[i]

        @pl.when(nxt2 >= 0)
        def _():
            for cp in copies(nxt2, slot):
                cp.start()

    @pl.when(used)
    def _():
        xp = xs_ref[...]
        x = jnp.concatenate([pltpu.bitcast(xp << 16, F32).astype(BF16),
                             pltpu.bitcast(xp & jnp.int32(-65536), F32).astype(BF16)], axis=1)
        hid = jax.nn.silu(_dot(x, wg_sc[...])) * _dot(x, wu_sc[...])
        o_ref[...] = _dot(hid.astype(BF16), wd_sc[...])

    @pl.when(jnp.logical_not(used))
    def _():
        o_ref[...] = jnp.zeros(o_ref.shape, o_ref.dtype)


def _combine_kernel(dest_ref, h_ref, wgt_ref, g_ref, b_ref, ys_hbm, o_ref, ob_ref, ybuf, sem, *, tc, alpha):
    i = pl.program_id(0)

    def gather(tile, slot):
        def issue(t, carry):
            tok = tile * tc + t
            _row_copy(ys_hbm, dest_ref[2 * tok], ybuf.at[slot], t, sem.at[slot]).start()
            _row_copy(ys_hbm, dest_ref[2 * tok + 1], ybuf.at[slot], tc + t, sem.at[slot]).start()
            return carry
        lax.fori_loop(0, tc, issue, 0, unroll=8)

    @pl.when(i == 0)
    def _():
        gather(0, 0)

    @pl.when(i + 1 < pl.num_programs(0))
    def _():
        gather(i + 1, (i + 1) % 2)

    slot = i % 2
    pltpu.make_async_copy(ys_hbm.at[pl.ds(0, 2 * tc)], ybuf.at[slot], sem.at[slot]).wait()
    wgt = wgt_ref[...]
    ff = ybuf[slot, 0:tc, :] * wgt[:, 0:1] + ybuf[slot, tc:2 * tc, :] * wgt[:, 1:2]
    h2 = _layer_norm(alpha * h_ref[...] + ff, g_ref[...], b_ref[...])
    o_ref[...] = h2
    ob_ref[...] = h2.astype(BF16)


def _tiles(seq):
    big = seq >= 2048
    return dict(
        rows=256 if big else 128,
        out_rows=256 if big else 128,
        mla=512 if big else 128,
        mla_heads=8,
        idx_q=512 if big else 64,
        dsa_q=512 if big else 128,
        key=512 if big else 128,
        route=512 if big else 128,
        moe_tok=512 if big else 128,
        comb=128,
    )


def _rope_tables(positions, rot_dim, head_dim):
    half = rot_dim // 2
    inv = ROPE_THETA ** (-jnp.arange(0, rot_dim, 2, dtype=F32) / rot_dim)
    ang = positions.astype(F32).reshape(-1, 1) * inv
    cos, sin = jnp.cos(ang), jnp.sin(ang)
    t = cos.shape[0]
    rest = head_dim - rot_dim
    ones, zeros, zh = jnp.ones((t, rest), F32), jnp.zeros((t, rest), F32), jnp.zeros((t, half), F32)
    rep = LANES // head_dim
    c = jnp.tile(jnp.concatenate([cos, cos, ones], axis=1), (1, rep))
    sa = jnp.tile(jnp.concatenate([-sin, zh, zeros], axis=1), (1, rep))
    sb = jnp.tile(jnp.concatenate([zh, sin, zeros], axis=1), (1, rep))
    return c, sa, sb


def _layer(layer, h, h_bf, positions, tabs, lw, alpha, B, S):
    T, D = h.shape
    tl = _tiles(S)
    tm = tl["rows"]
    nrow = T // tm
    spb = S // tm
    (c_m, sa_m, sb_m), (c_d, sa_d, sb_d), (c_i, sa_i, sb_i) = tabs
    row = lambda w: pl.BlockSpec((tm, w), lambda i: (i, 0))
    head_major = lambda nh, w: pl.BlockSpec((1, nh, tm, w), lambda i: (i // spb, 0, i % spb, 0))

    log2e = math.log2(math.e)
    q_a, k_a, v_a = pl.pallas_call(
        functools.partial(_mla_prep_kernel, q_scale=log2e * MLA_QK_DIM ** -0.5),
        grid=(nrow,),
        in_specs=[row(D), _resident_layer(lw["w_mla_in"], layer), _resident((1, MLA_Q_LORA)), _resident((1, MLA_KV_LORA)),
                  _resident(lw["w_q_b"].shape), _resident(lw["w_kv_b"].shape), row(LANES), row(LANES), row(LANES)],
        out_specs=[head_major(MLA_HEADS, MLA_QK_DIM), head_major(MLA_HEADS, MLA_QK_DIM), row(2 * MLA_HEADS * MLA_V_DIM)],
        out_shape=[jax.ShapeDtypeStruct((B, MLA_HEADS, S, MLA_QK_DIM), BF16),
                   jax.ShapeDtypeStruct((B, MLA_HEADS, S, MLA_QK_DIM), BF16),
                   jax.ShapeDtypeStruct((T, 2 * MLA_HEADS * MLA_V_DIM), BF16)],
        compiler_params=_params("parallel"),
    )(h_bf, lw["w_mla_in"], lw["g_q"], lw["g_kv"], lw["w_q_b"], lw["w_kv_b"], c_m, sa_m, sb_m)

    hd = DSA_HEADS * DSA_HEAD_DIM
    q_b, k_b, v_b, q_idx, k_idx, w_idx = pl.pallas_call(
        functools.partial(_dsa_prep_kernel, w_idx_scale=(IDX_HEADS ** -0.5) * (IDX_DIM ** -0.5),
                          q_scale=log2e * DSA_HEAD_DIM ** -0.5),
        grid=(nrow,),
        in_specs=[row(D), _resident_layer(lw["w_dsa_in"], layer)] + [row(LANES)] * 6,
        out_specs=[row(hd), row(hd), row(2 * hd), head_major(IDX_HEADS, IDX_DIM), row(IDX_DIM),
                   pl.BlockSpec((1, IDX_HEADS, tm), lambda i: (i // spb, 0, i % spb))],
        out_shape=[jax.ShapeDtypeStruct((T, hd), BF16)] * 2 + [jax.ShapeDtypeStruct((T, 2 * hd), BF16)] + [
            jax.ShapeDtypeStruct((B, IDX_HEADS, S, IDX_DIM), BF16),
            jax.ShapeDtypeStruct((T, IDX_DIM), BF16),
            jax.ShapeDtypeStruct((B, IDX_HEADS, S), F32)],
        compiler_params=_params("parallel"),
    )(h_bf, lw["w_dsa_in"], c_d, sa_d, sb_d, c_i, sa_i, sb_i)

    ta = tl["mla"]
    pos_q = positions.reshape(B, S, 1)
    pos_k_mla = positions.reshape(B, S // ta, 1, ta)
    hp = tl["mla_heads"]
    o_a = pl.pallas_call(
        functools.partial(_mla_attn_kernel, tile=ta, heads=hp),
        grid=(B, MLA_HEADS // hp, S // ta),
        in_specs=[pl.BlockSpec((1, hp, ta, MLA_QK_DIM), lambda b, g, i: (b, g, i, 0)),
                  pl.BlockSpec((1, hp, S, MLA_QK_DIM), lambda b, g, i: (b, g, 0, 0), pipeline_mode=pl.Buffered(1)),
                  pl.BlockSpec((1, S, 2 * MLA_V_DIM * hp), lambda b, g, i: (b, 0, g), pipeline_mode=pl.Buffered(1)),
                  pl.BlockSpec((1, ta, 1), lambda b, g, i: (b, i, 0)),
                  pl.BlockSpec((1, S // ta, 1, ta), lambda b, g, i: (b, 0, 0, 0))],
        out_specs=pl.BlockSpec((1, ta, MLA_V_DIM * hp), lambda b, g, i: (b, i, g)),
        out_shape=jax.ShapeDtypeStruct((B, S, MLA_HEADS * MLA_V_DIM), BF16),
        scratch_shapes=[pltpu.VMEM((hp, ta, LANES), F32), pltpu.VMEM((hp, ta, 2 * MLA_V_DIM), F32)],
        compiler_params=_params("parallel", "parallel", "arbitrary"),
    )(q_a, k_a, v_a.reshape(B, S, -1), pos_q, pos_k_mla)

    tk = tl["key"]
    nkc = S // tk
    tqi = tl["idx_q"]
    top_k = min(DSA_MAX_TOPK, S // 4)
    sel_bias = pl.pallas_call(
        functools.partial(_idx_select_kernel, tq=tqi, tk=tk, nkc=nkc, top_k=top_k),
        grid=(B, S // tqi),
        in_specs=[pl.BlockSpec((1, IDX_HEADS, tqi, IDX_DIM), lambda b, i: (b, 0, i, 0)),
                  pl.BlockSpec((1, S, IDX_DIM), lambda b, i: (b, 0, 0)),
                  pl.BlockSpec((1, IDX_HEADS, tqi), lambda b, i: (b, 0, i)),
                  pl.BlockSpec((1, 1, tqi), lambda b, i: (b, 0, i)),
                  pl.BlockSpec((1, nkc, tk, 1), lambda b, i: (b, 0, 0, 0))],
        out_specs=pl.BlockSpec((1, nkc, tqi, tk), lambda b, i: (b, 0, i, 0)),
        out_shape=jax.ShapeDtypeStruct((B, nkc, S, tk), BF16),
        scratch_shapes=[pltpu.VMEM((nkc, tk, tqi), I32)],
        compiler_params=_params("parallel", "arbitrary"),
    )(q_idx, k_idx.reshape(B, S, IDX_DIM), w_idx, positions.reshape(B, 1, S), positions.reshape(B, nkc, tk, 1))

    tqd = tl["dsa_q"]
    o_b = pl.pallas_call(
        functools.partial(_dsa_attn_kernel, tq=tqd, tk=tk),
        grid=(B, S // tqd),
        in_specs=[pl.BlockSpec((1, tqd, hd), lambda b, i: (b, i, 0)),
                  pl.BlockSpec((1, S, hd), lambda b, i: (b, 0, 0), pipeline_mode=pl.Buffered(1)),
                  pl.BlockSpec((1, S, 2 * hd), lambda b, i: (b, 0, 0), pipeline_mode=pl.Buffered(1)),
                  pl.BlockSpec((1, nkc, tqd, tk), lambda b, i: (b, 0, i, 0))],
        out_specs=pl.BlockSpec((1, tqd, hd), lambda b, i: (b, i, 0)),
        out_shape=jax.ShapeDtypeStruct((B, S, hd), BF16),
        scratch_shapes=[pltpu.VMEM((DSA_HEADS, tqd, LANES), F32), pltpu.VMEM((DSA_HEADS, tqd, 2 * DSA_HEAD_DIM), F32)],
        compiler_params=_params("parallel", "arbitrary"),
    )(q_b.reshape(B, S, hd), k_b.reshape(B, S, hd), v_b.reshape(B, S, 2 * hd), sel_bias)

    y = pl.pallas_call(
        _merge_kernel,
        grid=(nrow,),
        in_specs=[row(D), row(MLA_HEADS * MLA_V_DIM), row(hd),
                  _resident_layer(lw["w_gate_a"], layer), _resident_layer(lw["w_gate_b"], layer),
                  _resident((MLA_HEADS * MLA_V_DIM, D)), _resident((hd, D))],
        out_specs=row(D),
        out_shape=jax.ShapeDtypeStruct((T, D), BF16),
        compiler_params=_params("parallel"),
    )(h_bf, o_a.reshape(T, -1), o_b.reshape(T, hd), lw["w_gate_a"], lw["w_gate_b"], lw["w_o_a"], lw["w_o_b"])

    to = tl["out_rows"]
    orow = lambda w: pl.BlockSpec((to, w), lambda i: (i, 0))
    h1, h1_packed, logits = pl.pallas_call(
        functools.partial(_out_ln_kernel, alpha=alpha),
        grid=(T // to,),
        in_specs=[orow(D), orow(D), _resident((D, D)), _resident((1, D)), _resident((1, D)),
                  _resident((D, LANES)), _resident((1, LANES))],
        out_specs=[orow(D), orow(D // 2), orow(LANES)],
        out_shape=[jax.ShapeDtypeStruct((T, D), F32), jax.ShapeDtypeStruct((T, D // 2), I32),
                   jax.ShapeDtypeStruct((T, LANES), F32)],
        compiler_params=_params("parallel"),
    )(y, h, lw["w_out"], lw["ln1_g"], lw["ln1_b"], lw["w_route"], lw["b_route"])

    tr = tl["route"]
    meta, wgt, cnt = pl.pallas_call(
        functools.partial(_route_kernel, tr=tr),
        grid=(T // tr,),
        in_specs=[pl.BlockSpec((tr, LANES), lambda i: (i, 0))],
        out_specs=[pl.BlockSpec((tr, LANES), lambda i: (i, 0)), pl.BlockSpec((tr, LANES), lambda i: (i, 0)),
                   pl.BlockSpec((8, LANES), lambda i: (0, 0))],
        out_shape=[jax.ShapeDtypeStruct((T, LANES), I32), jax.ShapeDtypeStruct((T, LANES), F32),
                   jax.ShapeDtypeStruct((8, LANES), F32)],
        scratch_shapes=[pltpu.VMEM((1, LANES), F32)],
        compiler_params=_params("arbitrary"),
    )(logits)

    rb = MOE_ROW_BLOCK
    counts = cnt[0, N_GROUPS:N_GROUPS + N_EXPERTS].astype(I32)
    padded = ((counts + rb - 1) // rb) * rb
    pends = jnp.cumsum(padded)
    pstarts = pends - padded
    n_blocks = -(-(2 * T) // rb) + N_EXPERTS
    P = n_blocks * rb
    blk = jnp.arange(n_blocks, dtype=I32)
    block_e = jnp.minimum(jnp.sum((pends[None, :] <= (blk * rb)[:, None]).astype(I32), axis=1), N_EXPERTS - 1)
    n_used = (pends[-1] // rb).astype(I32).reshape(1)
    ps_lanes = jnp.zeros((1, LANES), I32).at[0, :N_EXPERTS].set(pstarts.astype(I32))
    dest = pl.pallas_call(
        _dest_kernel,
        grid=(T // tr,),
        in_specs=[pl.BlockSpec((tr, LANES), lambda i: (i, 0)), pl.BlockSpec((1, LANES), lambda i: (0, 0))],
        out_specs=pl.BlockSpec((tr, LANES), lambda i: (i, 0)),
        out_shape=jax.ShapeDtypeStruct((T, LANES), I32),
        compiler_params=_params("parallel"),
    )(meta, ps_lanes)[:, 0:2].reshape(-1)

    last_of_expert = jnp.concatenate([block_e[1:] != block_e[:-1], jnp.ones((1,), bool)])
    needs_zero = jnp.logical_or(blk >= n_used[0], jnp.logical_or(last_of_expert, blk == n_used[0] - 1)).astype(I32)
    n_zero = jnp.sum(needs_zero).astype(I32).reshape(1)
    tt = tl["moe_tok"]
    xs = pl.pallas_call(
        functools.partial(_dispatch_kernel, tt=tt),
        grid_spec=pltpu.PrefetchScalarGridSpec(
            num_scalar_prefetch=3, grid=(T // tt,),
            in_specs=[pl.BlockSpec((tt, D // 2), lambda i, *_: (i, 0))],
            out_specs=pl.BlockSpec(memory_space=pl.ANY),
            scratch_shapes=[pltpu.VMEM((rb, D // 2), I32), pltpu.SemaphoreType.DMA(()), pltpu.SemaphoreType.DMA(())]),
        out_shape=jax.ShapeDtypeStruct((P, D // 2), I32),
        compiler_params=_params("arbitrary"),
    )(dest, needs_zero, n_zero, h1_packed)

    F = D_EXPERT
    grp = jnp.cumsum(jnp.concatenate([jnp.zeros((1,), I32), (block_e[1:] != block_e[:-1]).astype(I32)]))
    ids = jnp.where(counts > 0, jnp.arange(N_EXPERTS, dtype=I32), N_EXPERTS)
    at_or_after = lax.cummin(ids, axis=0, reverse=True)
    after = jnp.concatenate([at_or_after[1:], jnp.full((1,), N_EXPERTS, I32)])
    nxt_of = jnp.where(after < N_EXPERTS, after, -1)
    nxt2_of = jnp.where(nxt_of >= 0, nxt_of[jnp.maximum(nxt_of, 0)], -1)
    nxt = nxt_of[block_e].astype(I32)
    nxt2 = nxt2_of[block_e].astype(I32)
    any_spec = pl.BlockSpec(memory_space=pl.ANY)
    ys = pl.pallas_call(
        functools.partial(_experts_kernel, layer=layer),
        grid_spec=pltpu.PrefetchScalarGridSpec(
            num_scalar_prefetch=5, grid=(n_blocks,),
            in_specs=[pl.BlockSpec((rb, D // 2), lambda i, be, gr, nx, nx2, nu: (jnp.minimum(i, nu[0] - 1), 0)),
                      any_spec, any_spec, any_spec],
            out_specs=pl.BlockSpec((rb, D), lambda i, *_: (i, 0)),
            scratch_shapes=[pltpu.VMEM((2, D, F), F32), pltpu.VMEM((2, D, F), F32), pltpu.VMEM((2, F, D), F32),
                            pltpu.VMEM((D, F), BF16), pltpu.VMEM((D, F), BF16), pltpu.VMEM((F, D), BF16),
                            pltpu.SemaphoreType.DMA((2, 3))]),
        out_shape=jax.ShapeDtypeStruct((P, D), F32),
        compiler_params=_params("arbitrary"),
    )(block_e, grp.astype(I32), nxt, nxt2, n_used, xs, lw["w_e_gate"], lw["w_e_up"], lw["w_e_down"])

    tc = tl["comb"]
    h2, h2_bf = pl.pallas_call(
        functools.partial(_combine_kernel, tc=tc, alpha=alpha),
        grid_spec=pltpu.PrefetchScalarGridSpec(
            num_scalar_prefetch=1, grid=(T // tc,),
            in_specs=[pl.BlockSpec((tc, D), lambda i, d: (i, 0)),
                      pl.BlockSpec((tc, LANES), lambda i, d: (i, 0)),
                      pl.BlockSpec((1, D), lambda i, d: (0, 0)),
                      pl.BlockSpec((1, D), lambda i, d: (0, 0)),
                      pl.BlockSpec(memory_space=pl.ANY)],
            out_specs=[pl.BlockSpec((tc, D), lambda i, d: (i, 0)), pl.BlockSpec((tc, D), lambda i, d: (i, 0))],
            scratch_shapes=[pltpu.VMEM((2, 2 * tc, D), F32), pltpu.SemaphoreType.DMA((2,))]),
        out_shape=[jax.ShapeDtypeStruct((T, D), F32), jax.ShapeDtypeStruct((T, D), BF16)],
        compiler_params=_params("arbitrary"),
    )(dest, h1, wgt, lw["ln2_g"], lw["ln2_b"], ys)
    return h2, h2_bf


def _layer_weights(l, regrouped, g_q_lora, w_q_b, g_kv_lora, w_kv_b, w_o_a, w_o_b, w_out, ln1_g, ln1_b,
                   w_group, b_group, w_router, b_router, w_e_gate, w_e_up, w_e_down, ln2_g, ln2_b):
    w_mla_in, w_dsa_in, w_gate_a, w_gate_b = regrouped
    D = w_gate_a.shape[1]
    wq = w_q_b[l].reshape(MLA_Q_LORA, MLA_HEADS, MLA_QK_DIM)
    wq = jnp.concatenate([wq[:, :, :MLA_NOPE_DIM].reshape(MLA_Q_LORA, -1),
                          wq[:, :, MLA_NOPE_DIM:].reshape(MLA_Q_LORA, -1)], axis=1).astype(BF16)
    w_route = jnp.concatenate([w_group[l], w_router[l],
                               jnp.zeros((D, LANES - N_GROUPS - N_EXPERTS), F32)], axis=1)
    b_route = jnp.concatenate([b_group[l], b_router[l],
                               jnp.zeros((LANES - N_GROUPS - N_EXPERTS,), F32)]).reshape(1, LANES)
    return dict(
        w_mla_in=w_mla_in, w_dsa_in=w_dsa_in, w_gate_a=w_gate_a, w_gate_b=w_gate_b,
        g_q=g_q_lora[l].reshape(1, -1), g_kv=g_kv_lora[l].reshape(1, -1),
        w_q_b=wq, w_kv_b=w_kv_b[l].astype(BF16),
        w_o_a=w_o_a[l].astype(BF16), w_o_b=w_o_b[l].astype(BF16), w_out=w_out[l].astype(BF16),
        ln1_g=ln1_g[l].reshape(1, -1), ln1_b=ln1_b[l].reshape(1, -1),
        w_route=w_route, b_route=b_route,
        w_e_gate=w_e_gate, w_e_up=w_e_up, w_e_down=w_e_down,
        ln2_g=ln2_g[l].reshape(1, -1), ln2_b=ln2_b[l].reshape(1, -1),
    )


def kernel(x, positions, w_in, g_q_lora, w_q_b, g_kv_lora, w_kv_b, w_o_a, w_o_b, w_out, ln1_g, ln1_b,
           w_group, b_group, w_router, b_router, w_e_gate, w_e_up, w_e_down, ln2_g, ln2_b):
    B, S, D = x.shape
    depth = w_in.shape[0]
    alpha = (2 * depth) ** 0.25
    tabs = (_rope_tables(positions, MLA_ROPE_DIM, MLA_ROPE_DIM),
            _rope_tables(positions, DSA_ROPE_DIM, DSA_HEAD_DIM),
            _rope_tables(positions, IDX_ROPE_DIM, IDX_DIM))
    h = x.reshape(B * S, D)
    h_bf = h.astype(BF16)
    regrouped = _regroup_w_in(w_in)
    for l in range(depth):
        lw = _layer_weights(l, regrouped, g_q_lora, w_q_b, g_kv_lora, w_kv_b, w_o_a, w_o_b, w_out, ln1_g, ln1_b,
                            w_group, b_group, w_router, b_router, w_e_gate, w_e_up, w_e_down, ln2_g, ln2_b)
        h, h_bf = _layer(l, h, h_bf, positions, tabs, lw, alpha, B, S)
    return h.reshape(B, S, D)
```

```python
import functools
import math

import jax
import jax.numpy as jnp
from jax import lax
from jax.experimental import pallas as pl
from jax.experimental.pallas import tpu as pltpu

F32 = jnp.float32
BF16 = jnp.bfloat16
I32 = jnp.int32

MLA_HEADS = 8
MLA_Q_LORA = 512
MLA_KV_LORA = 512
MLA_NOPE_DIM = 128
MLA_ROPE_DIM = 64
MLA_V_DIM = 128
MLA_QK_DIM = MLA_NOPE_DIM + MLA_ROPE_DIM
DSA_HEADS = 8
DSA_HEAD_DIM = 128
DSA_ROPE_DIM = DSA_HEAD_DIM // 4
IDX_HEADS = 16
IDX_DIM = 64
IDX_ROPE_DIM = IDX_DIM // 4
DSA_MAX_TOPK = 256
ROPE_THETA = 500000.0
N_GROUPS = 8
EXPERTS_PER_GROUP = 8
N_EXPERTS = N_GROUPS * EXPERTS_PER_GROUP
D_EXPERT = 512
MOE_ROW_BLOCK = 128
LN_EPS = 1e-5
RMS_EPS = 1e-6

LANES = 128
NEG_BIG = -1e30
INT_MIN = -(2 ** 31)
VMEM_LIMIT = 56 * 1024 * 1024
WEIGHT_DMA_CHUNKS = 4


def _params(*sem):
    return pltpu.CompilerParams(dimension_semantics=sem, vmem_limit_bytes=VMEM_LIMIT)


def _resident(shape):
    nd = len(shape)
    return pl.BlockSpec(shape, lambda *_: (0,) * nd, pipeline_mode=pl.Buffered(1))


def _resident_layer(stacked, layer):
    shape = stacked.shape[1:]
    nd = len(shape)
    return pl.BlockSpec((None,) + shape, lambda *_: (layer,) + (0,) * nd, pipeline_mode=pl.Buffered(1))


def _dot(a, b):
    return jnp.dot(a, b, preferred_element_type=F32)


def _dot_nt(a, b):
    return lax.dot_general(a, b, (((1,), (1,)), ((), ())), preferred_element_type=F32)


def _rope128(x, c, sa, sb, half):
    return x * c + pltpu.roll(x, LANES - half, 1) * sa + pltpu.roll(x, half, 1) * sb


def _rms(x, g):
    return x * lax.rsqrt(jnp.mean(x * x, axis=-1, keepdims=True) + RMS_EPS) * g


def _layer_norm(x, g, b):
    mu = jnp.mean(x, axis=-1, keepdims=True)
    xc = x - mu
    var = jnp.mean(xc * xc, axis=-1, keepdims=True)
    return xc * lax.rsqrt(var + LN_EPS) * g + b


_O_KPE = MLA_Q_LORA + MLA_KV_LORA
_O_DSA = _O_KPE + MLA_ROPE_DIM
_O_KIDX = _O_DSA + 3 * DSA_HEADS * DSA_HEAD_DIM + IDX_HEADS * IDX_DIM
_O_WIDX = _O_KIDX + IDX_DIM
_O_GATE = _O_WIDX + IDX_HEADS
_W_MLA_IN = _O_KPE + 2 * MLA_ROPE_DIM
_W_DSA_IN = (_O_KIDX - _O_DSA) + 2 * IDX_DIM + LANES


def _regroup_w_in_kernel(w_ref, mla_ref, dsa_ref, ga_ref, gb_ref, *, d_model):
    mla_ref[:, 0:_O_DSA] = w_ref[:, 0:_O_DSA].astype(BF16)
    mla_ref[:, _O_DSA:_W_MLA_IN] = w_ref[:, _O_KPE:_O_DSA].astype(BF16)
    main = _O_KIDX - _O_DSA
    dsa_ref[:, 0:main] = w_ref[:, _O_DSA:_O_KIDX].astype(BF16)
    kidx = w_ref[:, _O_KIDX:_O_WIDX].astype(BF16)
    dsa_ref[:, main:main + IDX_DIM] = kidx
    dsa_ref[:, main + IDX_DIM:main + 2 * IDX_DIM] = kidx
    dsa_ref[:, main + 2 * IDX_DIM:main + 2 * IDX_DIM + IDX_HEADS] = w_ref[:, _O_WIDX:_O_GATE].astype(BF16)
    dsa_ref[:, main + 2 * IDX_DIM + IDX_HEADS:] = jnp.zeros((w_ref.shape[0], LANES - IDX_HEADS), BF16)
    ga_ref[...] = w_ref[:, _O_GATE:_O_GATE + d_model].astype(BF16)
    gb_ref[...] = w_ref[:, _O_GATE + d_model:_O_GATE + 2 * d_model].astype(BF16)


def _regroup_w_in(w_in):
    depth, d_model, width = w_in.shape
    tr = 256
    out_w = (_W_MLA_IN, _W_DSA_IN, d_model, d_model)
    return pl.pallas_call(
        functools.partial(_regroup_w_in_kernel, d_model=d_model),
        grid=(depth, d_model // tr),
        in_specs=[pl.BlockSpec((tr, width), lambda l, i: (l * (d_model // tr) + i, 0))],
        out_specs=[pl.BlockSpec((None, tr, w), lambda l, i: (l, i, 0)) for w in out_w],
        out_shape=[jax.ShapeDtypeStruct((depth, d_model, w), BF16) for w in out_w],
        compiler_params=_params("parallel", "parallel"),
    )(w_in.astype(BF16).reshape(depth * d_model, width))


def _mla_prep_kernel(h_ref, win_ref, gq_ref, gkv_ref, wqb_ref, wkvb_ref, c_ref, sa_ref, sb_ref,
                     q_ref, k_ref, v_ref, *, q_scale):
    h = h_ref[...]
    p = _dot(h, win_ref[...])
    qn = _rms(p[:, :MLA_Q_LORA], gq_ref[...]).astype(BF16)
    kvn = _rms(p[:, MLA_Q_LORA:MLA_Q_LORA + MLA_KV_LORA], gkv_ref[...]).astype(BF16)
    q = _dot(qn, wqb_ref[...]) * q_scale
    kv = _dot(kvn, wkvb_ref[...])
    c, sa, sb = c_ref[...], sa_ref[...], sb_ref[...]
    half = MLA_ROPE_DIM // 2
    kpe = _rope128(p[:, MLA_Q_LORA + MLA_KV_LORA:], c, sa, sb, half)[:, :MLA_ROPE_DIM].astype(BF16)
    pe0 = MLA_HEADS * MLA_NOPE_DIM
    for hh in range(MLA_HEADS):
        q_ref[0, hh, :, 0:MLA_NOPE_DIM] = q[:, 128 * hh:128 * hh + 128].astype(BF16)
        k_ref[0, hh, :, 0:MLA_NOPE_DIM] = kv[:, 256 * hh:256 * hh + 128].astype(BF16)
        k_ref[0, hh, :, MLA_NOPE_DIM:MLA_QK_DIM] = kpe
        v_ref[:, 256 * hh:256 * hh + 128] = kv[:, 256 * hh + 128:256 * hh + 256].astype(BF16)
        v_ref[:, 256 * hh + 128:256 * hh + 256] = jnp.ones((kv.shape[0], LANES), BF16)
    for s in range(MLA_HEADS // 2):
        slab = _rope128(q[:, pe0 + 128 * s:pe0 + 128 * s + 128], c, sa, sb, half).astype(BF16)
        q_ref[0, 2 * s, :, MLA_NOPE_DIM:MLA_QK_DIM] = slab[:, :MLA_ROPE_DIM]
        q_ref[0, 2 * s + 1, :, MLA_NOPE_DIM:MLA_QK_DIM] = slab[:, MLA_ROPE_DIM:]


def _dsa_prep_kernel(h_ref, w_ref, cd_ref, sad_ref, sbd_ref, ci_ref, sai_ref, sbi_ref,
                     qb_ref, kb_ref, vb_ref, qi_ref, ki_ref, wi_ref, *, w_idx_scale, q_scale):
    h = h_ref[...]
    hd = DSA_HEADS * DSA_HEAD_DIM
    cd, sad, sbd = cd_ref[...], sad_ref[...], sbd_ref[...]
    ci, sai, sbi = ci_ref[...], sai_ref[...], sbi_ref[...]
    hd_half, hi_half = DSA_ROPE_DIM // 2, IDX_ROPE_DIM // 2
    q = _dot(h, w_ref[:, 0:hd]) * q_scale
    for hh in range(DSA_HEADS):
        qb_ref[:, 128 * hh:128 * hh + 128] = _rope128(q[:, 128 * hh:128 * hh + 128], cd, sad, sbd, hd_half).astype(BF16)
    k = _dot(h, w_ref[:, hd:2 * hd])
    for hh in range(DSA_HEADS):
        kb_ref[:, 128 * hh:128 * hh + 128] = _rope128(k[:, 128 * hh:128 * hh + 128], cd, sad, sbd, hd_half).astype(BF16)
    v = _dot(h, w_ref[:, 2 * hd:3 * hd]).astype(BF16)
    for hh in range(DSA_HEADS):
        vb_ref[:, 256 * hh:256 * hh + 128] = v[:, 128 * hh:128 * hh + 128]
        vb_ref[:, 256 * hh + 128:256 * hh + 256] = jnp.ones((v.shape[0], LANES), BF16)
    qi = _dot(h, w_ref[:, 3 * hd:3 * hd + IDX_HEADS * IDX_DIM])
    for s in range(IDX_HEADS // 2):
        slab = _rope128(qi[:, 128 * s:128 * s + 128], ci, sai, sbi, hi_half).astype(BF16)
        qi_ref[0, 2 * s] = slab[:, :IDX_DIM]
        qi_ref[0, 2 * s + 1] = slab[:, IDX_DIM:]
    o = 3 * hd + IDX_HEADS * IDX_DIM
    last = _dot(h, w_ref[:, o:o + 2 * LANES])
    ki_ref[...] = _rope128(last[:, :LANES], ci, sai, sbi, hi_half)[:, :IDX_DIM].astype(BF16)
    wi_ref[0] = (last[:, LANES:] * w_idx_scale).T[:IDX_HEADS, :]


def _flash_update(s, v_ones, m_ref, acc_ref):
    tk = s.shape[1]
    m_prev = m_ref[...]
    m_next = jnp.maximum(m_prev, jnp.max(s, axis=1, keepdims=True))
    p = jnp.exp2(s - jnp.tile(m_next, (1, tk // LANES)))
    alpha = jnp.exp2(m_prev - m_next)
    acc_ref[...] = acc_ref[...] * jnp.tile(alpha, (1, 2)) + _dot(p.astype(BF16), v_ones)
    m_ref[...] = m_next


def _mla_attn_kernel(q_ref, k_ref, v_ref, pq_ref, pk_ref, o_ref, m_sc, acc_sc, *, tile, heads):
    qi = pl.program_id(2)
    m_sc[...] = jnp.full(m_sc.shape, -jnp.inf, F32)
    acc_sc[...] = jnp.zeros(acc_sc.shape, F32)

    def chunk(c, masked):
        ks = pl.ds(pl.multiple_of(c * tile, tile), tile)
        if masked:
            visible = pk_ref[0, c] <= pq_ref[0]
        for hh in range(heads):
            s = _dot_nt(q_ref[0, hh], k_ref[0, hh, ks, :])
            if masked:
                s = jnp.where(visible, s, NEG_BIG)
            _flash_update(s, v_ref[0, ks, 2 * MLA_V_DIM * hh:2 * MLA_V_DIM * (hh + 1)], m_sc.at[hh], acc_sc.at[hh])

    def body(c, carry):
        chunk(c, False)
        return carry

    lax.fori_loop(0, qi, body, 0)
    chunk(qi, True)
    for hh in range(heads):
        o_ref[0, :, MLA_V_DIM * hh:MLA_V_DIM * (hh + 1)] = (
            acc_sc[hh, :, :MLA_V_DIM] / acc_sc[hh, :, MLA_V_DIM:]).astype(o_ref.dtype)


def _idx_select_kernel(q_ref, k_ref, wt_ref, pq_ref, pk_ref, o_ref, key_sc, *, tq, tk, nkc, top_k):
    qi = pl.program_id(1)
    nk = (qi * tq + tq + tk - 1) // tk
    wt = wt_ref[0]
    pq = pq_ref[0]

    def score_chunk(c, carry):
        kc = k_ref[0, pl.ds(pl.multiple_of(c * tk, tk), tk), :]
        acc = jnp.zeros((tk, tq), F32)
        for hh in range(IDX_HEADS):
            acc = acc + wt[hh:hh + 1, :] * jnp.maximum(_dot_nt(kc, q_ref[0, hh]), 0.0)
        bits = pltpu.bitcast(acc, I32)
        key = jnp.where(bits < 0, bits ^ jnp.int32(0x7FFFFFFF), bits)
        key = jnp.where(acc == 0.0, 0, key)
        key_sc[c] = jnp.where(pk_ref[0, c] <= pq, key, INT_MIN)
        return carry

    lax.fori_loop(0, nk, score_chunk, 0)

    def count(pred):
        def body(c, part):
            m = pred(key_sc[c], c).astype(I32)
            return part + jnp.sum(m.reshape(tk // 8, 8, tq), axis=0)
        part = lax.fori_loop(0, nk, body, jnp.zeros((8, tq), I32))
        return jnp.sum(part, axis=0, keepdims=True)

    def bit_step(i, base):
        cand = base ^ (jnp.int32(1) << (31 - i))
        cnt = count(lambda key, c: key >= cand)
        return jnp.where(cnt >= top_k, cand, base)

    thr = lax.fori_loop(0, 32, bit_step, jnp.full((1, tq), INT_MIN, I32))
    n_gt = count(lambda key, c: key > thr)
    n_eq = count(lambda key, c: key == thr)
    need = top_k - n_gt
    tie = jnp.logical_and(n_eq > need, thr != INT_MIN)

    def key_index(c):
        return c * tk + lax.broadcasted_iota(I32, (tk, tq), 0)

    def tie_search():
        def step(i, p):
            cand = p | (jnp.int32(1) << (14 - i))
            cnt = count(lambda key, c: jnp.logical_and(key == thr, key_index(c) < cand))
            return jnp.where(cnt <= need, cand, p)
        return lax.fori_loop(0, 15, step, jnp.zeros((1, tq), I32))

    p_cut = lax.cond(jnp.max(tie.astype(I32)) > 0, tie_search,
                     lambda: jnp.full((1, tq), 2 ** 30, I32))

    def out_chunk(c, carry):
        key = key_sc[c]
        sel = jnp.logical_or(key > thr, jnp.logical_and(key == thr, key_index(c) < p_cut))
        sel = jnp.logical_and(sel, key != INT_MIN)
        o_ref[0, c] = jnp.where(sel, 0.0, NEG_BIG).T.astype(o_ref.dtype)
        return carry

    lax.fori_loop(0, nk, out_chunk, 0)

    def fill_chunk(c, carry):
        o_ref[0, c] = jnp.full((tq, tk), NEG_BIG, o_ref.dtype)
        return carry

    lax.fori_loop(nk, nkc, fill_chunk, 0)


def _dsa_attn_kernel(q_ref, k_ref, v_ref, b_ref, o_ref, m_sc, acc_sc, *, tq, tk):
    qi = pl.program_id(1)
    nk = (qi * tq + tq + tk - 1) // tk
    m_sc[...] = jnp.full(m_sc.shape, NEG_BIG, F32)
    acc_sc[...] = jnp.zeros(acc_sc.shape, F32)

    def chunk(c, carry):
        bias = b_ref[0, c].astype(F32)
        ks = pl.ds(pl.multiple_of(c * tk, tk), tk)
        for hh in range(DSA_HEADS):
            hs = slice(DSA_HEAD_DIM * hh, DSA_HEAD_DIM * (hh + 1))
            vs = slice(2 * DSA_HEAD_DIM * hh, 2 * DSA_HEAD_DIM * (hh + 1))
            s = _dot_nt(q_ref[0, :, hs], k_ref[0, ks, hs]) + bias
            _flash_update(s, v_ref[0, ks, vs], m_sc.at[hh], acc_sc.at[hh])
        return carry

    lax.fori_loop(0, nk, chunk, 0)
    for hh in range(DSA_HEADS):
        hs = slice(DSA_HEAD_DIM * hh, DSA_HEAD_DIM * (hh + 1))
        o_ref[0, :, hs] = (acc_sc[hh, :, :DSA_HEAD_DIM] / acc_sc[hh, :, DSA_HEAD_DIM:]).astype(o_ref.dtype)


def _merge_kernel(h_ref, oa_ref, ob_ref, wga_ref, wgb_ref, woa_ref, wob_ref, y_ref):
    h = h_ref[...]
    ya = jax.nn.sigmoid(_dot(h, wga_ref[...])) * _dot(oa_ref[...], woa_ref[...])
    yb = jax.nn.sigmoid(_dot(h, wgb_ref[...])) * _dot(ob_ref[...], wob_ref[...])
    y_ref[...] = (ya + yb).astype(y_ref.dtype)


def _out_ln_kernel(y_ref, h_ref, wout_ref, g_ref, b_ref, wr_ref, br_ref, o_ref, op_ref, lg_ref, *, alpha):
    sub = 128
    half = o_ref.shape[1] // 2
    for r in range(y_ref.shape[0] // sub):
        rows = slice(r * sub, (r + 1) * sub)
        mix = _dot(y_ref[rows, :], wout_ref[...])
        h1 = _layer_norm(alpha * h_ref[rows, :] + mix, g_ref[...], b_ref[...])
        o_ref[rows, :] = h1
        lo = pltpu.bitcast(h1[:, :half].astype(BF16).astype(F32), I32)
        hi = pltpu.bitcast(h1[:, half:].astype(BF16).astype(F32), I32)
        op_ref[rows, :] = lax.shift_right_logical(lo, 16) | hi
        lg_ref[rows, :] = jnp.dot(h1, wr_ref[...], preferred_element_type=F32,
                                  precision=lax.Precision.HIGHEST) + br_ref[...]


def _route_kernel(lg_ref, meta_ref, wgt_ref, cnt_ref, carry_sc, *, tr):
    @pl.when(pl.program_id(0) == 0)
    def _():
        carry_sc[...] = jnp.zeros(carry_sc.shape, F32)

    lg = lg_ref[...]
    lane = lax.broadcasted_iota(I32, (tr, LANES), 1)
    gl = jnp.where(lane < N_GROUPS, lg, -jnp.inf)
    gmax = jnp.max(gl, axis=1, keepdims=True)
    g_idx = jnp.min(jnp.where(gl == gmax, lane, LANES), axis=1, keepdims=True)
    g_p = 1.0 / jnp.sum(jnp.exp(gl - gmax), axis=1, keepdims=True)
    lo = N_GROUPS + EXPERTS_PER_GROUP * g_idx
    in_group = jnp.logical_and(lane >= lo, lane < lo + EXPERTS_PER_GROUP)
    el = jnp.where(in_group, lg, -jnp.inf)
    emax = jnp.max(el, axis=1, keepdims=True)
    e1 = jnp.min(jnp.where(el == emax, lane, LANES), axis=1, keepdims=True)
    den = jnp.sum(jnp.exp(el - emax), axis=1, keepdims=True)
    el2 = jnp.where(lane == e1, -jnp.inf, el)
    emax2 = jnp.max(el2, axis=1, keepdims=True)
    e2 = jnp.min(jnp.where(el2 == emax2, lane, LANES), axis=1, keepdims=True)
    p1 = 1.0 / den
    p2 = jnp.exp(emax2 - emax) / den
    w1 = g_p * (p1 / (p1 + p2))
    w2 = g_p * (p2 / (p1 + p2))

    is1 = lane == e1
    is2 = lane == e2
    onehot = jnp.logical_or(is1, is2).astype(BF16)
    r = lax.broadcasted_iota(I32, (tr, tr), 0)
    cidx = lax.broadcasted_iota(I32, (tr, tr), 1)
    lower = (cidx < r).astype(BF16)
    before = _dot(lower, onehot) + carry_sc[...]
    rank1 = jnp.sum(jnp.where(is1, before, 0.0), axis=1, keepdims=True)
    rank2 = jnp.sum(jnp.where(is2, before, 0.0), axis=1, keepdims=True)
    carry_sc[...] = carry_sc[...] + jnp.sum(onehot.astype(F32), axis=0, keepdims=True)

    meta = jnp.where(lane == 0, e1 - N_GROUPS,
                     jnp.where(lane == 1, e2 - N_GROUPS,
                               jnp.where(lane == 2, rank1.astype(I32),
                                         jnp.where(lane == 3, rank2.astype(I32), 0))))
    meta_ref[...] = meta
    wgt_ref[...] = jnp.where(lane == 0, w1, jnp.where(lane == 1, w2, 0.0))
    cnt_ref[...] = jnp.broadcast_to(carry_sc[...], cnt_ref.shape)


def _row_copy(src, src_row, dst, dst_row, sem):
    return pltpu.make_async_copy(src.at[pl.ds(src_row, 1)], dst.at[pl.ds(dst_row, 1)], sem)


def _dest_kernel(meta_ref, ps_ref, o_ref):
    meta = meta_ref[...]
    ps = ps_ref[...]
    lane = lax.broadcasted_iota(I32, meta.shape, 1)
    d1 = jnp.sum(jnp.where(lane == meta[:, 0:1], ps, 0), axis=1, keepdims=True) + meta[:, 2:3]
    d2 = jnp.sum(jnp.where(lane == meta[:, 1:2], ps, 0), axis=1, keepdims=True) + meta[:, 3:4]
    o_ref[...] = jnp.where(lane == 0, d1, jnp.where(lane == 1, d2, 0))


def _dispatch_kernel(dest_ref, zblk_ref, nz_ref, h_ref, xs_out, zero_sc, sem, zsem, *, tt):
    base = pl.program_id(0) * tt
    rb = zero_sc.shape[0]

    @pl.when(pl.program_id(0) == 0)
    def _():
        zero_sc[...] = jnp.zeros(zero_sc.shape, zero_sc.dtype)

        def fill(j, carry):
            @pl.when(zblk_ref[j] != 0)
            def _():
                pltpu.make_async_copy(zero_sc, xs_out.at[pl.ds(pl.multiple_of(j * rb, rb), rb)], zsem).start()
            return carry

        def drain(j, carry):
            pltpu.make_async_copy(zero_sc, xs_out.at[pl.ds(0, rb)], zsem).wait()
            return carry

        lax.fori_loop(0, zblk_ref.shape[0], fill, 0)
        lax.fori_loop(0, nz_ref[0], drain, 0)

    def issue(t, carry):
        tok = base + t
        _row_copy(h_ref, t, xs_out, dest_ref[2 * tok], sem).start()
        _row_copy(h_ref, t, xs_out, dest_ref[2 * tok + 1], sem).start()
        return carry

    lax.fori_loop(0, tt, issue, 0, unroll=8)
    for _ in range(2):
        pltpu.make_async_copy(h_ref, xs_out.at[pl.ds(0, tt)], sem).wait()


def _experts_kernel(be_ref, grp_ref, nxt_ref, nxt2_ref, nu_ref, xs_ref, wg_hbm, wu_hbm, wd_hbm, o_ref,
                    wg_buf, wu_buf, wd_buf, wg_sc, wu_sc, wd_sc, sem, *, layer):
    i = pl.program_id(0)
    e = be_ref[i]
    used = i < nu_ref[0]
    first = jnp.logical_and(used, jnp.logical_or(i == 0, e != be_ref[jnp.maximum(i - 1, 0)]))
    slot = grp_ref[i] % 2

    def copies(expert, s):
        out = []
        for j, (hbm, buf) in enumerate(((wg_hbm, wg_buf), (wu_hbm, wu_buf), (wd_hbm, wd_buf))):
            rows = buf.shape[1] // WEIGHT_DMA_CHUNKS
            for c in range(WEIGHT_DMA_CHUNKS):
                rs = pl.ds(c * rows, rows)
                out.append(pltpu.make_async_copy(hbm.at[layer, expert, rs], buf.at[s, rs], sem.at[s, j]))
        return out

    @pl.when(i == 0)
    def _():
        for cp in copies(e, slot):
            cp.start()

        @pl.when(nxt_ref[0] >= 0)
        def _():
            for cp in copies(nxt_ref[0], 1 - slot):
                cp.start()

    @pl.when(first)
    def _():
        for cp in copies(e, slot):
            cp.wait()
        wg_sc[...] = wg_buf[slot].astype(BF16)
        wu_sc[...] = wu_buf[slot].astype(BF16)
        wd_sc[...] = wd_buf[slot].astype(BF16)
        nxt2 = nxt2_ref[i]

        @pl.when(nxt2 >= 0)
        def _():
            for cp in copies(nxt2, slot):
                cp.start()

    @pl.when(used)
    def _():
        xp = xs_ref[...]
        x = jnp.concatenate([pltpu.bitcast(xp << 16, F32).astype(BF16),
                             pltpu.bitcast(xp & jnp.int32(-65536), F32).astype(BF16)], axis=1)
        hid = jax.nn.silu(_dot(x, wg_sc[...])) * _dot(x, wu_sc[...])
        o_ref[...] = _dot(hid.astype(BF16), wd_sc[...])

    @pl.when(jnp.logical_not(used))
    def _():
        o_ref[...] = jnp.zeros(o_ref.shape, o_ref.dtype)


def _combine_kernel(dest_ref, h_ref, wgt_ref, g_ref, b_ref, ys_hbm, o_ref, ob_ref, ybuf, sem, *, tc, alpha):
    i = pl.program_id(0)

    def gather(tile, slot):
        def issue(t, carry):
            tok = tile * tc + t
            _row_copy(ys_hbm, dest_ref[2 * tok], ybuf.at[slot], t, sem.at[slot]).start()
            _row_copy(ys_hbm, dest_ref[2 * tok + 1], ybuf.at[slot], tc + t, sem.at[slot]).start()
            return carry
        lax.fori_loop(0, tc, issue, 0, unroll=8)

    @pl.when(i == 0)
    def _():
        gather(0, 0)

    @pl.when(i + 1 < pl.num_programs(0))
    def _():
        gather(i + 1, (i + 1) % 2)

    slot = i % 2
    pltpu.make_async_copy(ys_hbm.at[pl.ds(0, 2 * tc)], ybuf.at[slot], sem.at[slot]).wait()
    wgt = wgt_ref[...]
    ff = ybuf[slot, 0:tc, :] * wgt[:, 0:1] + ybuf[slot, tc:2 * tc, :] * wgt[:, 1:2]
    h2 = _layer_norm(alpha * h_ref[...] + ff, g_ref[...], b_ref[...])
    o_ref[...] = h2
    ob_ref[...] = h2.astype(BF16)


def _tiles(seq):
    big = seq >= 2048
    return dict(
        rows=512 if big else 128,
        out_rows=256 if big else 128,
        mla=512 if big else 128,
        mla_heads=8,
        idx_q=512 if big else 64,
        dsa_q=512 if big else 128,
        key=512 if big else 128,
        route=512 if big else 128,
        moe_tok=512 if big else 128,
        comb=256 if big else 128,
    )


def _rope_tables(positions, rot_dim, head_dim):
    half = rot_dim // 2
    inv = ROPE_THETA ** (-jnp.arange(0, rot_dim, 2, dtype=F32) / rot_dim)
    ang = positions.astype(F32).reshape(-1, 1) * inv
    cos, sin = jnp.cos(ang), jnp.sin(ang)
    t = cos.shape[0]
    rest = head_dim - rot_dim
    ones, zeros, zh = jnp.ones((t, rest), F32), jnp.zeros((t, rest), F32), jnp.zeros((t, half), F32)
    rep = LANES // head_dim
    c = jnp.tile(jnp.concatenate([cos, cos, ones], axis=1), (1, rep))
    sa = jnp.tile(jnp.concatenate([-sin, zh, zeros], axis=1), (1, rep))
    sb = jnp.tile(jnp.concatenate([zh, sin, zeros], axis=1), (1, rep))
    return c, sa, sb


def _layer(layer, h, h_bf, positions, tabs, lw, alpha, B, S):
    T, D = h.shape
    tl = _tiles(S)
    tm = tl["rows"]
    nrow = T // tm
    spb = S // tm
    (c_m, sa_m, sb_m), (c_d, sa_d, sb_d), (c_i, sa_i, sb_i) = tabs
    row = lambda w: pl.BlockSpec((tm, w), lambda i: (i, 0))
    head_major = lambda nh, w: pl.BlockSpec((1, nh, tm, w), lambda i: (i // spb, 0, i % spb, 0))

    log2e = math.log2(math.e)
    q_a, k_a, v_a = pl.pallas_call(
        functools.partial(_mla_prep_kernel, q_scale=log2e * MLA_QK_DIM ** -0.5),
        grid=(nrow,),
        in_specs=[row(D), _resident_layer(lw["w_mla_in"], layer), _resident((1, MLA_Q_LORA)), _resident((1, MLA_KV_LORA)),
                  _resident(lw["w_q_b"].shape), _resident(lw["w_kv_b"].shape), row(LANES), row(LANES), row(LANES)],
        out_specs=[head_major(MLA_HEADS, MLA_QK_DIM), head_major(MLA_HEADS, MLA_QK_DIM), row(2 * MLA_HEADS * MLA_V_DIM)],
        out_shape=[jax.ShapeDtypeStruct((B, MLA_HEADS, S, MLA_QK_DIM), BF16),
                   jax.ShapeDtypeStruct((B, MLA_HEADS, S, MLA_QK_DIM), BF16),
                   jax.ShapeDtypeStruct((T, 2 * MLA_HEADS * MLA_V_DIM), BF16)],
        compiler_params=_params("parallel"),
    )(h_bf, lw["w_mla_in"], lw["g_q"], lw["g_kv"], lw["w_q_b"], lw["w_kv_b"], c_m, sa_m, sb_m)

    hd = DSA_HEADS * DSA_HEAD_DIM
    q_b, k_b, v_b, q_idx, k_idx, w_idx = pl.pallas_call(
        functools.partial(_dsa_prep_kernel, w_idx_scale=(IDX_HEADS ** -0.5) * (IDX_DIM ** -0.5),
                          q_scale=log2e * DSA_HEAD_DIM ** -0.5),
        grid=(nrow,),
        in_specs=[row(D), _resident_layer(lw["w_dsa_in"], layer)] + [row(LANES)] * 6,
        out_specs=[row(hd), row(hd), row(2 * hd), head_major(IDX_HEADS, IDX_DIM), row(IDX_DIM),
                   pl.BlockSpec((1, IDX_HEADS, tm), lambda i: (i // spb, 0, i % spb))],
        out_shape=[jax.ShapeDtypeStruct((T, hd), BF16)] * 2 + [jax.ShapeDtypeStruct((T, 2 * hd), BF16)] + [
            jax.ShapeDtypeStruct((B, IDX_HEADS, S, IDX_DIM), BF16),
            jax.ShapeDtypeStruct((T, IDX_DIM), BF16),
            jax.ShapeDtypeStruct((B, IDX_HEADS, S), F32)],
        compiler_params=_params("parallel"),
    )(h_bf, lw["w_dsa_in"], c_d, sa_d, sb_d, c_i, sa_i, sb_i)

    ta = tl["mla"]
    pos_q = positions.reshape(B, S, 1)
    pos_k_mla = positions.reshape(B, S // ta, 1, ta)
    hp = tl["mla_heads"]
    o_a = pl.pallas_call(
        functools.partial(_mla_attn_kernel, tile=ta, heads=hp),
        grid=(B, MLA_HEADS // hp, S // ta),
        in_specs=[pl.BlockSpec((1, hp, ta, MLA_QK_DIM), lambda b, g, i: (b, g, i, 0)),
                  pl.BlockSpec((1, hp, S, MLA_QK_DIM), lambda b, g, i: (b, g, 0, 0), pipeline_mode=pl.Buffered(1)),
                  pl.BlockSpec((1, S, 2 * MLA_V_DIM * hp), lambda b, g, i: (b, 0, g), pipeline_mode=pl.Buffered(1)),
                  pl.BlockSpec((1, ta, 1), lambda b, g, i: (b, i, 0)),
                  pl.BlockSpec((1, S // ta, 1, ta), lambda b, g, i: (b, 0, 0, 0))],
        out_specs=pl.BlockSpec((1, ta, MLA_V_DIM * hp), lambda b, g, i: (b, i, g)),
        out_shape=jax.ShapeDtypeStruct((B, S, MLA_HEADS * MLA_V_DIM), BF16),
        scratch_shapes=[pltpu.VMEM((hp, ta, LANES), F32), pltpu.VMEM((hp, ta, 2 * MLA_V_DIM), F32)],
        compiler_params=_params("parallel", "parallel", "arbitrary"),
    )(q_a, k_a, v_a.reshape(B, S, -1), pos_q, pos_k_mla)

    tk = tl["key"]
    nkc = S // tk
    tqi = tl["idx_q"]
    top_k = min(DSA_MAX_TOPK, S // 4)
    sel_bias = pl.pallas_call(
        functools.partial(_idx_select_kernel, tq=tqi, tk=tk, nkc=nkc, top_k=top_k),
        grid=(B, S // tqi),
        in_specs=[pl.BlockSpec((1, IDX_HEADS, tqi, IDX_DIM), lambda b, i: (b, 0, i, 0)),
                  pl.BlockSpec((1, S, IDX_DIM), lambda b, i: (b, 0, 0)),
                  pl.BlockSpec((1, IDX_HEADS, tqi), lambda b, i: (b, 0, i)),
                  pl.BlockSpec((1, 1, tqi), lambda b, i: (b, 0, i)),
                  pl.BlockSpec((1, nkc, tk, 1), lambda b, i: (b, 0, 0, 0))],
        out_specs=pl.BlockSpec((1, nkc, tqi, tk), lambda b, i: (b, 0, i, 0)),
        out_shape=jax.ShapeDtypeStruct((B, nkc, S, tk), BF16),
        scratch_shapes=[pltpu.VMEM((nkc, tk, tqi), I32)],
        compiler_params=_params("parallel", "arbitrary"),
    )(q_idx, k_idx.reshape(B, S, IDX_DIM), w_idx, positions.reshape(B, 1, S), positions.reshape(B, nkc, tk, 1))

    tqd = tl["dsa_q"]
    o_b = pl.pallas_call(
        functools.partial(_dsa_attn_kernel, tq=tqd, tk=tk),
        grid=(B, S // tqd),
        in_specs=[pl.BlockSpec((1, tqd, hd), lambda b, i: (b, i, 0)),
                  pl.BlockSpec((1, S, hd), lambda b, i: (b, 0, 0), pipeline_mode=pl.Buffered(1)),
                  pl.BlockSpec((1, S, 2 * hd), lambda b, i: (b, 0, 0), pipeline_mode=pl.Buffered(1)),
                  pl.BlockSpec((1, nkc, tqd, tk), lambda b, i: (b, 0, i, 0))],
        out_specs=pl.BlockSpec((1, tqd, hd), lambda b, i: (b, i, 0)),
        out_shape=jax.ShapeDtypeStruct((B, S, hd), BF16),
        scratch_shapes=[pltpu.VMEM((DSA_HEADS, tqd, LANES), F32), pltpu.VMEM((DSA_HEADS, tqd, 2 * DSA_HEAD_DIM), F32)],
        compiler_params=_params("parallel", "arbitrary"),
    )(q_b.reshape(B, S, hd), k_b.reshape(B, S, hd), v_b.reshape(B, S, 2 * hd), sel_bias)

    y = pl.pallas_call(
        _merge_kernel,
        grid=(nrow,),
        in_specs=[row(D), row(MLA_HEADS * MLA_V_DIM), row(hd),
                  _resident_layer(lw["w_gate_a"], layer), _resident_layer(lw["w_gate_b"], layer),
                  _resident((MLA_HEADS * MLA_V_DIM, D)), _resident((hd, D))],
        out_specs=row(D),
        out_shape=jax.ShapeDtypeStruct((T, D), BF16),
        compiler_params=_params("parallel"),
    )(h_bf, o_a.reshape(T, -1), o_b.reshape(T, hd), lw["w_gate_a"], lw["w_gate_b"], lw["w_o_a"], lw["w_o_b"])

    to = tl["out_rows"]
    orow = lambda w: pl.BlockSpec((to, w), lambda i: (i, 0))
    h1, h1_packed, logits = pl.pallas_call(
        functools.partial(_out_ln_kernel, alpha=alpha),
        grid=(T // to,),
        in_specs=[orow(D), orow(D), _resident((D, D)), _resident((1, D)), _resident((1, D)),
                  _resident((D, LANES)), _resident((1, LANES))],
        out_specs=[orow(D), orow(D // 2), orow(LANES)],
        out_shape=[jax.ShapeDtypeStruct((T, D), F32), jax.ShapeDtypeStruct((T, D // 2), I32),
                   jax.ShapeDtypeStruct((T, LANES), F32)],
        compiler_params=_params("parallel"),
    )(y, h, lw["w_out"], lw["ln1_g"], lw["ln1_b"], lw["w_route"], lw["b_route"])

    tr = tl["route"]
    meta, wgt, cnt = pl.pallas_call(
        functools.partial(_route_kernel, tr=tr),
        grid=(T // tr,),
        in_specs=[pl.BlockSpec((tr, LANES), lambda i: (i, 0))],
        out_specs=[pl.BlockSpec((tr, LANES), lambda i: (i, 0)), pl.BlockSpec((tr, LANES), lambda i: (i, 0)),
                   pl.BlockSpec((8, LANES), lambda i: (0, 0))],
        out_shape=[jax.ShapeDtypeStruct((T, LANES), I32), jax.ShapeDtypeStruct((T, LANES), F32),
                   jax.ShapeDtypeStruct((8, LANES), F32)],
        scratch_shapes=[pltpu.VMEM((1, LANES), F32)],
        compiler_params=_params("arbitrary"),
    )(logits)

    rb = MOE_ROW_BLOCK
    counts = cnt[0, N_GROUPS:N_GROUPS + N_EXPERTS].astype(I32)
    padded = ((counts + rb - 1) // rb) * rb
    pends = jnp.cumsum(padded)
    pstarts = pends - padded
    n_blocks = -(-(2 * T) // rb) + N_EXPERTS
    P = n_blocks * rb
    blk = jnp.arange(n_blocks, dtype=I32)
    block_e = jnp.minimum(jnp.sum((pends[None, :] <= (blk * rb)[:, None]).astype(I32), axis=1), N_EXPERTS - 1)
    n_used = (pends[-1] // rb).astype(I32).reshape(1)
    ps_lanes = jnp.zeros((1, LANES), I32).at[0, :N_EXPERTS].set(pstarts.astype(I32))
    dest = pl.pallas_call(
        _dest_kernel,
        grid=(T // tr,),
        in_specs=[pl.BlockSpec((tr, LANES), lambda i: (i, 0)), pl.BlockSpec((1, LANES), lambda i: (0, 0))],
        out_specs=pl.BlockSpec((tr, LANES), lambda i: (i, 0)),
        out_shape=jax.ShapeDtypeStruct((T, LANES), I32),
        compiler_params=_params("parallel"),
    )(meta, ps_lanes)[:, 0:2].reshape(-1)

    last_of_expert = jnp.concatenate([block_e[1:] != block_e[:-1], jnp.ones((1,), bool)])
    needs_zero = jnp.logical_or(blk >= n_used[0], jnp.logical_or(last_of_expert, blk == n_used[0] - 1)).astype(I32)
    n_zero = jnp.sum(needs_zero).astype(I32).reshape(1)
    tt = tl["moe_tok"]
    xs = pl.pallas_call(
        functools.partial(_dispatch_kernel, tt=tt),
        grid_spec=pltpu.PrefetchScalarGridSpec(
            num_scalar_prefetch=3, grid=(T // tt,),
            in_specs=[pl.BlockSpec((tt, D // 2), lambda i, *_: (i, 0))],
            out_specs=pl.BlockSpec(memory_space=pl.ANY),
            scratch_shapes=[pltpu.VMEM((rb, D // 2), I32), pltpu.SemaphoreType.DMA(()), pltpu.SemaphoreType.DMA(())]),
        out_shape=jax.ShapeDtypeStruct((P, D // 2), I32),
        compiler_params=_params("arbitrary"),
    )(dest, needs_zero, n_zero, h1_packed)

    F = D_EXPERT
    grp = jnp.cumsum(jnp.concatenate([jnp.zeros((1,), I32), (block_e[1:] != block_e[:-1]).astype(I32)]))
    ids = jnp.where(counts > 0, jnp.arange(N_EXPERTS, dtype=I32), N_EXPERTS)
    at_or_after = lax.cummin(ids, axis=0, reverse=True)
    after = jnp.concatenate([at_or_after[1:], jnp.full((1,), N_EXPERTS, I32)])
    nxt_of = jnp.where(after < N_EXPERTS, after, -1)
    nxt2_of = jnp.where(nxt_of >= 0, nxt_of[jnp.maximum(nxt_of, 0)], -1)
    nxt = nxt_of[block_e].astype(I32)
    nxt2 = nxt2_of[block_e].astype(I32)
    any_spec = pl.BlockSpec(memory_space=pl.ANY)
    ys = pl.pallas_call(
        functools.partial(_experts_kernel, layer=layer),
        grid_spec=pltpu.PrefetchScalarGridSpec(
            num_scalar_prefetch=5, grid=(n_blocks,),
            in_specs=[pl.BlockSpec((rb, D // 2), lambda i, be, gr, nx, nx2, nu: (jnp.minimum(i, nu[0] - 1), 0)),
                      any_spec, any_spec, any_spec],
            out_specs=pl.BlockSpec((rb, D), lambda i, *_: (i, 0)),
            scratch_shapes=[pltpu.VMEM((2, D, F), F32), pltpu.VMEM((2, D, F), F32), pltpu.VMEM((2, F, D), F32),
                            pltpu.VMEM((D, F), BF16), pltpu.VMEM((D, F), BF16), pltpu.VMEM((F, D), BF16),
                            pltpu.SemaphoreType.DMA((2, 3))]),
        out_shape=jax.ShapeDtypeStruct((P, D), F32),
        compiler_params=_params("arbitrary"),
    )(block_e, grp.astype(I32), nxt, nxt2, n_used, xs, lw["w_e_gate"], lw["w_e_up"], lw["w_e_down"])

    tc = tl["comb"]
    h2, h2_bf = pl.pallas_call(
        functools.partial(_combine_kernel, tc=tc, alpha=alpha),
        grid_spec=pltpu.PrefetchScalarGridSpec(
            num_scalar_prefetch=1, grid=(T // tc,),
            in_specs=[pl.BlockSpec((tc, D), lambda i, d: (i, 0)),
                      pl.BlockSpec((tc, LANES), lambda i, d: (i, 0)),
                      pl.BlockSpec((1, D), lambda i, d: (0, 0)),
                      pl.BlockSpec((1, D), lambda i, d: (0, 0)),
                      pl.BlockSpec(memory_space=pl.ANY)],
            out_specs=[pl.BlockSpec((tc, D), lambda i, d: (i, 0)), pl.BlockSpec((tc, D), lambda i, d: (i, 0))],
            scratch_shapes=[pltpu.VMEM((2, 2 * tc, D), F32), pltpu.SemaphoreType.DMA((2,))]),
        out_shape=[jax.ShapeDtypeStruct((T, D), F32), jax.ShapeDtypeStruct((T, D), BF16)],
        compiler_params=_params("arbitrary"),
    )(dest, h1, wgt, lw["ln2_g"], lw["ln2_b"], ys)
    return h2, h2_bf


def _layer_weights(l, regrouped, g_q_lora, w_q_b, g_kv_lora, w_kv_b, w_o_a, w_o_b, w_out, ln1_g, ln1_b,
                   w_group, b_group, w_router, b_router, w_e_gate, w_e_up, w_e_down, ln2_g, ln2_b):
    w_mla_in, w_dsa_in, w_gate_a, w_gate_b = regrouped
    D = w_gate_a.shape[1]
    wq = w_q_b[l].reshape(MLA_Q_LORA, MLA_HEADS, MLA_QK_DIM)
    wq = jnp.concatenate([wq[:, :, :MLA_NOPE_DIM].reshape(MLA_Q_LORA, -1),
                          wq[:, :, MLA_NOPE_DIM:].reshape(MLA_Q_LORA, -1)], axis=1).astype(BF16)
    w_route = jnp.concatenate([w_group[l], w_router[l],
                               jnp.zeros((D, LANES - N_GROUPS - N_EXPERTS), F32)], axis=1)
    b_route = jnp.concatenate([b_group[l], b_router[l],
                               jnp.zeros((LANES - N_GROUPS - N_EXPERTS,), F32)]).reshape(1, LANES)
    return dict(
        w_mla_in=w_mla_in, w_dsa_in=w_dsa_in, w_gate_a=w_gate_a, w_gate_b=w_gate_b,
        g_q=g_q_lora[l].reshape(1, -1), g_kv=g_kv_lora[l].reshape(1, -1),
        w_q_b=wq, w_kv_b=w_kv_b[l].astype(BF16),
        w_o_a=w_o_a[l].astype(BF16), w_o_b=w_o_b[l].astype(BF16), w_out=w_out[l].astype(BF16),
        ln1_g=ln1_g[l].reshape(1, -1), ln1_b=ln1_b[l].reshape(1, -1),
        w_route=w_route, b_route=b_route,
        w_e_gate=w_e_gate, w_e_up=w_e_up, w_e_down=w_e_down,
        ln2_g=ln2_g[l].reshape(1, -1), ln2_b=ln2_b[l].reshape(1, -1),
    )


def kernel(x, positions, w_in, g_q_lora, w_q_b, g_kv_lora, w_kv_b, w_o_a, w_o_b, w_out, ln1_g, ln1_b,
           w_group, b_group, w_router, b_router, w_e_gate, w_e_up, w_e_down, ln2_g, ln2_b):
    B, S, D = x.shape
    depth = w_in.shape[0]
    alpha = (2 * depth) ** 0.25
    tabs = (_rope_tables(positions, MLA_ROPE_DIM, MLA_ROPE_DIM),
            _rope_tables(positions, DSA_ROPE_DIM, DSA_HEAD_DIM),
            _rope_tables(positions, IDX_ROPE_DIM, IDX_DIM))
    h = x.reshape(B * S, D)
    h_bf = h.astype(BF16)
    regrouped = _regroup_w_in(w_in)
    for l in range(depth):
        lw = _layer_weights(l, regrouped, g_q_lora, w_q_b, g_kv_lora, w_kv_b, w_o_a, w_o_b, w_out, ln1_g, ln1_b,
                            w_group, b_group, w_router, b_router, w_e_gate, w_e_up, w_e_down, ln2_g, ln2_b)
        h, h_bf = _layer(l, h, h_bf, positions, tabs, lw, alpha, B, S)
    return h.reshape(B, S, D)
```

```python
import functools
import math

import jax
import jax.numpy as jnp
from jax import lax
from jax.experimental import pallas as pl
from jax.experimental.pallas import tpu as pltpu

F32 = jnp.float32
BF16 = jnp.bfloat16
I32 = jnp.int32

MLA_HEADS = 8
MLA_Q_LORA = 512
MLA_KV_LORA = 512
MLA_NOPE_DIM = 128
MLA_ROPE_DIM = 64
MLA_V_DIM = 128
MLA_QK_DIM = MLA_NOPE_DIM + MLA_ROPE_DIM
DSA_HEADS = 8
DSA_HEAD_DIM = 128
DSA_ROPE_DIM = DSA_HEAD_DIM // 4
IDX_HEADS = 16
IDX_DIM = 64
IDX_ROPE_DIM = IDX_DIM // 4
DSA_MAX_TOPK = 256
ROPE_THETA = 500000.0
N_GROUPS = 8
EXPERTS_PER_GROUP = 8
N_EXPERTS = N_GROUPS * EXPERTS_PER_GROUP
D_EXPERT = 512
MOE_ROW_BLOCK = 128
LN_EPS = 1e-5
RMS_EPS = 1e-6

LANES = 128
NEG_BIG = -1e30
INT_MIN = -(2 ** 31)
INT_MAX = 2 ** 31 - 1
VMEM_LIMIT = 56 * 1024 * 1024
WEIGHT_DMA_CHUNKS = 4


def _params(*sem):
    return pltpu.CompilerParams(dimension_semantics=sem, vmem_limit_bytes=VMEM_LIMIT)


def _resident(shape):
    nd = len(shape)
    return pl.BlockSpec(shape, lambda *_: (0,) * nd, pipeline_mode=pl.Buffered(1))


def _resident_layer(stacked, layer):
    shape = stacked.shape[1:]
    nd = len(shape)
    return pl.BlockSpec((None,) + shape, lambda *_: (layer,) + (0,) * nd, pipeline_mode=pl.Buffered(1))


def _dot(a, b):
    return jnp.dot(a, b, preferred_element_type=F32)


def _dot_nt(a, b):
    return lax.dot_general(a, b, (((1,), (1,)), ((), ())), preferred_element_type=F32)


def _rope128(x, c, sa, sb, half):
    return x * c + pltpu.roll(x, LANES - half, 1) * sa + pltpu.roll(x, half, 1) * sb


def _rms(x, g):
    return x * lax.rsqrt(jnp.mean(x * x, axis=-1, keepdims=True) + RMS_EPS) * g


def _layer_norm(x, g, b):
    mu = jnp.mean(x, axis=-1, keepdims=True)
    xc = x - mu
    var = jnp.mean(xc * xc, axis=-1, keepdims=True)
    return xc * lax.rsqrt(var + LN_EPS) * g + b


_O_KPE = MLA_Q_LORA + MLA_KV_LORA
_O_DSA = _O_KPE + MLA_ROPE_DIM
_O_KIDX = _O_DSA + 3 * DSA_HEADS * DSA_HEAD_DIM + IDX_HEADS * IDX_DIM
_O_WIDX = _O_KIDX + IDX_DIM
_O_GATE = _O_WIDX + IDX_HEADS
_W_MLA_IN = _O_KPE + 2 * MLA_ROPE_DIM
_W_DSA_IN = (_O_KIDX - _O_DSA) + 2 * IDX_DIM + LANES


def _regroup_w_in_kernel(w_ref, mla_ref, dsa_ref, ga_ref, gb_ref, *, d_model):
    mla_ref[:, 0:_O_DSA] = w_ref[:, 0:_O_DSA].astype(BF16)
    mla_ref[:, _O_DSA:_W_MLA_IN] = w_ref[:, _O_KPE:_O_DSA].astype(BF16)
    main = _O_KIDX - _O_DSA
    dsa_ref[:, 0:main] = w_ref[:, _O_DSA:_O_KIDX].astype(BF16)
    kidx = w_ref[:, _O_KIDX:_O_WIDX].astype(BF16)
    dsa_ref[:, main:main + IDX_DIM] = kidx
    dsa_ref[:, main + IDX_DIM:main + 2 * IDX_DIM] = kidx
    dsa_ref[:, main + 2 * IDX_DIM:main + 2 * IDX_DIM + IDX_HEADS] = w_ref[:, _O_WIDX:_O_GATE].astype(BF16)
    dsa_ref[:, main + 2 * IDX_DIM + IDX_HEADS:] = jnp.zeros((w_ref.shape[0], LANES - IDX_HEADS), BF16)
    ga_ref[...] = w_ref[:, _O_GATE:_O_GATE + d_model].astype(BF16)
    gb_ref[...] = w_ref[:, _O_GATE + d_model:_O_GATE + 2 * d_model].astype(BF16)


def _regroup_w_in(w_in):
    depth, d_model, width = w_in.shape
    tr = 256
    out_w = (_W_MLA_IN, _W_DSA_IN, d_model, d_model)
    return pl.pallas_call(
        functools.partial(_regroup_w_in_kernel, d_model=d_model),
        grid=(depth, d_model // tr),
        in_specs=[pl.BlockSpec((tr, width), lambda l, i: (l * (d_model // tr) + i, 0))],
        out_specs=[pl.BlockSpec((None, tr, w), lambda l, i: (l, i, 0)) for w in out_w],
        out_shape=[jax.ShapeDtypeStruct((depth, d_model, w), BF16) for w in out_w],
        compiler_params=_params("parallel", "parallel"),
    )(w_in.astype(BF16).reshape(depth * d_model, width))


def _mla_prep_kernel(h_ref, win_ref, gq_ref, gkv_ref, wqb_ref, wkvb_ref, c_ref, sa_ref, sb_ref,
                     q_ref, k_ref, v_ref, *, q_scale):
    h = h_ref[...]
    p = _dot(h, win_ref[...])
    qn = _rms(p[:, :MLA_Q_LORA], gq_ref[...]).astype(BF16)
    kvn = _rms(p[:, MLA_Q_LORA:MLA_Q_LORA + MLA_KV_LORA], gkv_ref[...]).astype(BF16)
    q = _dot(qn, wqb_ref[...]) * q_scale
    kv = _dot(kvn, wkvb_ref[...])
    c, sa, sb = c_ref[...], sa_ref[...], sb_ref[...]
    half = MLA_ROPE_DIM // 2
    kpe = _rope128(p[:, MLA_Q_LORA + MLA_KV_LORA:], c, sa, sb, half)[:, :MLA_ROPE_DIM].astype(BF16)
    pe0 = MLA_HEADS * MLA_NOPE_DIM
    for hh in range(MLA_HEADS):
        q_ref[0, hh, :, 0:MLA_NOPE_DIM] = q[:, 128 * hh:128 * hh + 128].astype(BF16)
        k_ref[0, hh, :, 0:MLA_NOPE_DIM] = kv[:, 256 * hh:256 * hh + 128].astype(BF16)
        k_ref[0, hh, :, MLA_NOPE_DIM:MLA_QK_DIM] = kpe
        v_ref[:, 256 * hh:256 * hh + 128] = kv[:, 256 * hh + 128:256 * hh + 256].astype(BF16)
        v_ref[:, 256 * hh + 128:256 * hh + 256] = jnp.ones((kv.shape[0], LANES), BF16)
    for s in range(MLA_HEADS // 2):
        slab = _rope128(q[:, pe0 + 128 * s:pe0 + 128 * s + 128], c, sa, sb, half).astype(BF16)
        q_ref[0, 2 * s, :, MLA_NOPE_DIM:MLA_QK_DIM] = slab[:, :MLA_ROPE_DIM]
        q_ref[0, 2 * s + 1, :, MLA_NOPE_DIM:MLA_QK_DIM] = slab[:, MLA_ROPE_DIM:]


def _dsa_prep_kernel(h_ref, w_ref, cd_ref, sad_ref, sbd_ref, ci_ref, sai_ref, sbi_ref,
                     qb_ref, kb_ref, vb_ref, qi_ref, ki_ref, wi_ref, *, w_idx_scale, q_scale):
    h = h_ref[...]
    hd = DSA_HEADS * DSA_HEAD_DIM
    cd, sad, sbd = cd_ref[...], sad_ref[...], sbd_ref[...]
    ci, sai, sbi = ci_ref[...], sai_ref[...], sbi_ref[...]
    hd_half, hi_half = DSA_ROPE_DIM // 2, IDX_ROPE_DIM // 2
    q = _dot(h, w_ref[:, 0:hd]) * q_scale
    for hh in range(DSA_HEADS):
        qb_ref[:, 128 * hh:128 * hh + 128] = _rope128(q[:, 128 * hh:128 * hh + 128], cd, sad, sbd, hd_half).astype(BF16)
    k = _dot(h, w_ref[:, hd:2 * hd])
    for hh in range(DSA_HEADS):
        kb_ref[:, 128 * hh:128 * hh + 128] = _rope128(k[:, 128 * hh:128 * hh + 128], cd, sad, sbd, hd_half).astype(BF16)
    v = _dot(h, w_ref[:, 2 * hd:3 * hd]).astype(BF16)
    for hh in range(DSA_HEADS):
        vb_ref[:, 256 * hh:256 * hh + 128] = v[:, 128 * hh:128 * hh + 128]
        vb_ref[:, 256 * hh + 128:256 * hh + 256] = jnp.ones((v.shape[0], LANES), BF16)
    qi = _dot(h, w_ref[:, 3 * hd:3 * hd + IDX_HEADS * IDX_DIM])
    for s in range(IDX_HEADS // 2):
        slab = _rope128(qi[:, 128 * s:128 * s + 128], ci, sai, sbi, hi_half).astype(BF16)
        qi_ref[0, 2 * s] = slab[:, :IDX_DIM]
        qi_ref[0, 2 * s + 1] = slab[:, IDX_DIM:]
    o = 3 * hd + IDX_HEADS * IDX_DIM
    last = _dot(h, w_ref[:, o:o + 2 * LANES])
    ki_ref[...] = _rope128(last[:, :LANES], ci, sai, sbi, hi_half)[:, :IDX_DIM].astype(BF16)
    wi_ref[0] = (last[:, LANES:] * w_idx_scale).T[:IDX_HEADS, :]


def _flash_update(s, v_ones, m_ref, acc_ref):
    tk = s.shape[1]
    m_prev = m_ref[...]
    m_next = jnp.maximum(m_prev, jnp.max(s, axis=1, keepdims=True))
    p = jnp.exp2(s - jnp.tile(m_next, (1, tk // LANES)))
    alpha = jnp.exp2(m_prev - m_next)
    acc_ref[...] = acc_ref[...] * jnp.tile(alpha, (1, 2)) + _dot(p.astype(BF16), v_ones)
    m_ref[...] = m_next


def _mla_attn_kernel(q_ref, k_ref, v_ref, pq_ref, pk_ref, o_ref, m_sc, acc_sc, *, tile, heads):
    qi = pl.program_id(2)
    m_sc[...] = jnp.full(m_sc.shape, -jnp.inf, F32)
    acc_sc[...] = jnp.zeros(acc_sc.shape, F32)

    def chunk(c, masked):
        ks = pl.ds(pl.multiple_of(c * tile, tile), tile)
        if masked:
            visible = pk_ref[0, c] <= pq_ref[0]
        for hh in range(heads):
            s = _dot_nt(q_ref[0, hh], k_ref[0, hh, ks, :])
            if masked:
                s = jnp.where(visible, s, NEG_BIG)
            _flash_update(s, v_ref[0, ks, 2 * MLA_V_DIM * hh:2 * MLA_V_DIM * (hh + 1)], m_sc.at[hh], acc_sc.at[hh])

    def body(c, carry):
        chunk(c, False)
        return carry

    lax.fori_loop(0, qi, body, 0)
    chunk(qi, True)
    for hh in range(heads):
        o_ref[0, :, MLA_V_DIM * hh:MLA_V_DIM * (hh + 1)] = (
            acc_sc[hh, :, :MLA_V_DIM] / acc_sc[hh, :, MLA_V_DIM:]).astype(o_ref.dtype)


def _idx_select_kernel(q_ref, k_ref, wt_ref, pq_ref, pk_ref, o_ref, key_sc, *, tq, tk, nkc, top_k):
    qi = pl.program_id(1)
    nk = (qi * tq + tq + tk - 1) // tk
    wt = wt_ref[0]
    pq = pq_ref[0]

    def score_chunk(c, carry):
        kc = k_ref[0, pl.ds(pl.multiple_of(c * tk, tk), tk), :]
        acc = jnp.zeros((tk, tq), F32)
        for hh in range(IDX_HEADS):
            acc = acc + wt[hh:hh + 1, :] * jnp.maximum(_dot_nt(kc, q_ref[0, hh]), 0.0)
        bits = pltpu.bitcast(acc, I32)
        key = jnp.where(bits < 0, bits ^ jnp.int32(0x7FFFFFFF), bits)
        key = jnp.where(acc == 0.0, 0, key)
        key_sc[c] = jnp.where(pk_ref[0, c] <= pq, key, INT_MIN)
        return carry

    lax.fori_loop(0, nk, score_chunk, 0)

    def count(pred):
        def body(c, part):
            m = pred(key_sc[c], c).astype(I32)
            return part + jnp.sum(m.reshape(tk // 8, 8, tq), axis=0)
        part = lax.fori_loop(0, nk, body, jnp.zeros((8, tq), I32))
        return jnp.sum(part, axis=0, keepdims=True)

    def min_max(c, carry):
        mn, mx = carry
        key = key_sc[c]
        live = jnp.where(key == INT_MIN, INT_MAX, key)
        return (jnp.minimum(mn, jnp.min(live.reshape(tk // 8, 8, tq), axis=0)),
                jnp.maximum(mx, jnp.max(key.reshape(tk // 8, 8, tq), axis=0)))

    n_causal = count(lambda key, c: key != INT_MIN)
    mn, mx = lax.fori_loop(0, nk, min_max, (jnp.full((8, tq), INT_MAX, I32), jnp.full((8, tq), INT_MIN, I32)))
    k_min = jnp.min(mn, axis=0, keepdims=True)
    k_max = jnp.max(mx, axis=0, keepdims=True)

    def settled_of(lo, hi, n_lo):
        return jnp.logical_or(n_lo <= top_k, lo + 1 >= hi)

    def search_step(state):
        it, lo, hi, n_lo, n_hi, _ = state
        settled = settled_of(lo, hi, n_lo)
        mid = (lo >> 1) + (hi >> 1) + (lo & hi & 1)
        width = hi.astype(F32) - lo.astype(F32)
        frac = (n_lo - top_k).astype(F32) / jnp.maximum(n_lo - n_hi, 1).astype(F32)
        guess = lo + jnp.minimum(width * frac, 2.0 ** 30).astype(I32)
        cand = jnp.where(jnp.logical_and((it & 1) == 0, width < 2.0 ** 30), guess, mid)
        cand = jnp.where(settled, lo, jnp.minimum(jnp.maximum(cand, lo + 1), hi - 1))
        cnt = count(lambda key, c: key >= cand)
        up = jnp.logical_and(cnt >= top_k, jnp.logical_not(settled))
        down = jnp.logical_and(cnt < top_k, jnp.logical_not(settled))
        lo, n_lo = jnp.where(up, cand, lo), jnp.where(up, cnt, n_lo)
        hi, n_hi = jnp.where(down, cand, hi), jnp.where(down, cnt, n_hi)
        return it + 1, lo, hi, n_lo, n_hi, jnp.min(settled_of(lo, hi, n_lo).astype(I32)) > 0

    lo0 = jnp.where(n_causal > top_k, k_min, INT_MIN)
    hi0 = jnp.where(k_max == INT_MAX, k_max, k_max + 1)
    state = (jnp.int32(0), lo0, hi0, n_causal, jnp.zeros((1, tq), I32),
             jnp.min(settled_of(lo0, hi0, n_causal).astype(I32)) > 0)
    state = lax.while_loop(lambda s: jnp.logical_and(s[0] < 64, jnp.logical_not(s[5])), search_step, state)
    thr = state[1]
    n_gt = count(lambda key, c: key > thr)
    n_eq = count(lambda key, c: key == thr)
    need = top_k - n_gt
    tie = jnp.logical_and(n_eq > need, thr != INT_MIN)

    def key_index(c):
        return c * tk + lax.broadcasted_iota(I32, (tk, tq), 0)

    def tie_search():
        def step(i, p):
            cand = p | (jnp.int32(1) << (14 - i))
            cnt = count(lambda key, c: jnp.logical_and(key == thr, key_index(c) < cand))
            return jnp.where(cnt <= need, cand, p)
        return lax.fori_loop(0, 15, step, jnp.zeros((1, tq), I32))

    p_cut = lax.cond(jnp.max(tie.astype(I32)) > 0, tie_search,
                     lambda: jnp.full((1, tq), 2 ** 30, I32))

    def out_chunk(c, carry):
        key = key_sc[c]
        sel = jnp.logical_or(key > thr, jnp.logical_and(key == thr, key_index(c) < p_cut))
        sel = jnp.logical_and(sel, key != INT_MIN)
        o_ref[0, c] = jnp.where(sel, 0.0, NEG_BIG).T.astype(o_ref.dtype)
        return carry

    lax.fori_loop(0, nk, out_chunk, 0)

    def fill_chunk(c, carry):
        o_ref[0, c] = jnp.full((tq, tk), NEG_BIG, o_ref.dtype)
        return carry

    lax.fori_loop(nk, nkc, fill_chunk, 0)


def _dsa_attn_kernel(q_ref, k_ref, v_ref, b_ref, o_ref, m_sc, acc_sc, *, tq, tk):
    qi = pl.program_id(1)
    nk = (qi * tq + tq + tk - 1) // tk
    m_sc[...] = jnp.full(m_sc.shape, NEG_BIG, F32)
    acc_sc[...] = jnp.zeros(acc_sc.shape, F32)

    def chunk(c, carry):
        bias = b_ref[0, c].astype(F32)
        ks = pl.ds(pl.multiple_of(c * tk, tk), tk)
        for hh in range(DSA_HEADS):
            hs = slice(DSA_HEAD_DIM * hh, DSA_HEAD_DIM * (hh + 1))
            vs = slice(2 * DSA_HEAD_DIM * hh, 2 * DSA_HEAD_DIM * (hh + 1))
            s = _dot_nt(q_ref[0, :, hs], k_ref[0, ks, hs]) + bias
            _flash_update(s, v_ref[0, ks, vs], m_sc.at[hh], acc_sc.at[hh])
        return carry

    lax.fori_loop(0, nk, chunk, 0)
    for hh in range(DSA_HEADS):
        hs = slice(DSA_HEAD_DIM * hh, DSA_HEAD_DIM * (hh + 1))
        o_ref[0, :, hs] = (acc_sc[hh, :, :DSA_HEAD_DIM] / acc_sc[hh, :, DSA_HEAD_DIM:]).astype(o_ref.dtype)


def _merge_kernel(h_ref, oa_ref, ob_ref, wga_ref, wgb_ref, woa_ref, wob_ref, y_ref):
    h = h_ref[...]
    ya = jax.nn.sigmoid(_dot(h, wga_ref[...])) * _dot(oa_ref[...], woa_ref[...])
    yb = jax.nn.sigmoid(_dot(h, wgb_ref[...])) * _dot(ob_ref[...], wob_ref[...])
    y_ref[...] = (ya + yb).astype(y_ref.dtype)


def _out_ln_kernel(y_ref, h_ref, wout_ref, g_ref, b_ref, wrh_ref, wrl_ref, br_ref, o_ref, op_ref, lg_ref, *, alpha):
    sub = 128
    half = o_ref.shape[1] // 2
    for r in range(y_ref.shape[0] // sub):
        rows = slice(r * sub, (r + 1) * sub)
        mix = _dot(y_ref[rows, :], wout_ref[...])
        h1 = _layer_norm(alpha * h_ref[rows, :] + mix, g_ref[...], b_ref[...])
        o_ref[rows, :] = h1
        lo = pltpu.bitcast(h1[:, :half].astype(BF16).astype(F32), I32)
        hi = pltpu.bitcast(h1[:, half:].astype(BF16).astype(F32), I32)
        op_ref[rows, :] = lax.shift_right_logical(lo, 16) | hi
        h_hi = h1.astype(BF16)
        h_lo = (h1 - h_hi.astype(F32)).astype(BF16)
        lg_ref[rows, :] = (_dot(h_hi, wrh_ref[...]) + (_dot(h_hi, wrl_ref[...]) + _dot(h_lo, wrh_ref[...]))
                           + br_ref[...])


def _route_kernel(lg_ref, meta_ref, wgt_ref, cnt_ref, carry_sc, *, tr):
    @pl.when(pl.program_id(0) == 0)
    def _():
        carry_sc[...] = jnp.zeros(carry_sc.shape, F32)

    lg = lg_ref[...]
    lane = lax.broadcasted_iota(I32, (tr, LANES), 1)
    gl = jnp.where(lane < N_GROUPS, lg, -jnp.inf)
    gmax = jnp.max(gl, axis=1, keepdims=True)
    g_idx = jnp.min(jnp.where(gl == gmax, lane, LANES), axis=1, keepdims=True)
    g_p = 1.0 / jnp.sum(jnp.exp(gl - gmax), axis=1, keepdims=True)
    lo = N_GROUPS + EXPERTS_PER_GROUP * g_idx
    in_group = jnp.logical_and(lane >= lo, lane < lo + EXPERTS_PER_GROUP)
    el = jnp.where(in_group, lg, -jnp.inf)
    emax = jnp.max(el, axis=1, keepdims=True)
    e1 = jnp.min(jnp.where(el == emax, lane, LANES), axis=1, keepdims=True)
    den = jnp.sum(jnp.exp(el - emax), axis=1, keepdims=True)
    el2 = jnp.where(lane == e1, -jnp.inf, el)
    emax2 = jnp.max(el2, axis=1, keepdims=True)
    e2 = jnp.min(jnp.where(el2 == emax2, lane, LANES), axis=1, keepdims=True)
    p1 = 1.0 / den
    p2 = jnp.exp(emax2 - emax) / den
    w1 = g_p * (p1 / (p1 + p2))
    w2 = g_p * (p2 / (p1 + p2))

    is1 = lane == e1
    is2 = lane == e2
    onehot = jnp.logical_or(is1, is2).astype(BF16)
    r = lax.broadcasted_iota(I32, (tr, tr), 0)
    cidx = lax.broadcasted_iota(I32, (tr, tr), 1)
    lower = (cidx < r).astype(BF16)
    before = _dot(lower, onehot) + carry_sc[...]
    rank1 = jnp.sum(jnp.where(is1, before, 0.0), axis=1, keepdims=True)
    rank2 = jnp.sum(jnp.where(is2, before, 0.0), axis=1, keepdims=True)
    carry_sc[...] = carry_sc[...] + jnp.sum(onehot.astype(F32), axis=0, keepdims=True)

    meta = jnp.where(lane == 0, e1 - N_GROUPS,
                     jnp.where(lane == 1, e2 - N_GROUPS,
                               jnp.where(lane == 2, rank1.astype(I32),
                                         jnp.where(lane == 3, rank2.astype(I32), 0))))
    meta_ref[...] = meta
    wgt_ref[...] = jnp.where(lane == 0, w1, jnp.where(lane == 1, w2, 0.0))
    cnt_ref[...] = jnp.broadcast_to(carry_sc[...], cnt_ref.shape)


def _row_copy(src, src_row, dst, dst_row, sem):
    return pltpu.make_async_copy(src.at[pl.ds(src_row, 1)], dst.at[pl.ds(dst_row, 1)], sem)


def _dest_kernel(meta_ref, ps_ref, o_ref):
    meta = meta_ref[...]
    ps = ps_ref[...]
    lane = lax.broadcasted_iota(I32, meta.shape, 1)
    d1 = jnp.sum(jnp.where(lane == meta[:, 0:1], ps, 0), axis=1, keepdims=True) + meta[:, 2:3]
    d2 = jnp.sum(jnp.where(lane == meta[:, 1:2], ps, 0), axis=1, keepdims=True) + meta[:, 3:4]
    o_ref[...] = jnp.where(lane == 0, d1, jnp.where(lane == 1, d2, 0))


def _dispatch_kernel(dest_ref, zblk_ref, nz_ref, h_ref, xs_out, zero_sc, sem, zsem, *, tt):
    base = pl.program_id(0) * tt
    rb = zero_sc.shape[0]

    @pl.when(pl.program_id(0) == 0)
    def _():
        zero_sc[...] = jnp.zeros(zero_sc.shape, zero_sc.dtype)

        def fill(j, carry):
            @pl.when(zblk_ref[j] != 0)
            def _():
                pltpu.make_async_copy(zero_sc, xs_out.at[pl.ds(pl.multiple_of(j * rb, rb), rb)], zsem).start()
            return carry

        def drain(j, carry):
            pltpu.make_async_copy(zero_sc, xs_out.at[pl.ds(0, rb)], zsem).wait()
            return carry

        lax.fori_loop(0, zblk_ref.shape[0], fill, 0)
        lax.fori_loop(0, nz_ref[0], drain, 0)

    def issue(t, carry):
        tok = base + t
        _row_copy(h_ref, t, xs_out, dest_ref[2 * tok], sem).start()
        _row_copy(h_ref, t, xs_out, dest_ref[2 * tok + 1], sem).start()
        return carry

    lax.fori_loop(0, tt, issue, 0, unroll=8)
    for _ in range(2):
        pltpu.make_async_copy(h_ref, xs_out.at[pl.ds(0, tt)], sem).wait()


def _experts_kernel(be_ref, grp_ref, nxt_ref, nxt2_ref, nu_ref, xs_ref, wg_hbm, wu_hbm, wd_hbm, o_ref,
                    wg_buf, wu_buf, wd_buf, wg_sc, wu_sc, wd_sc, sem, *, layer):
    i = pl.program_id(0)
    e = be_ref[i]
    used = i < nu_ref[0]
    first = jnp.logical_and(used, jnp.logical_or(i == 0, e != be_ref[jnp.maximum(i - 1, 0)]))
    slot = grp_ref[i] % 2

    def copies(expert, s):
        out = []
        for j, (hbm, buf) in enumerate(((wg_hbm, wg_buf), (wu_hbm, wu_buf), (wd_hbm, wd_buf))):
            rows = buf.shape[1] // WEIGHT_DMA_CHUNKS
            for c in range(WEIGHT_DMA_CHUNKS):
                rs = pl.ds(c * rows, rows)
                out.append(pltpu.make_async_copy(hbm.at[layer, expert, rs], buf.at[s, rs], sem.at[s, j]))
        return out

    @pl.when(i == 0)
    def _():
        for cp in copies(e, slot):
            cp.start()

        @pl.when(nxt_ref[0] >= 0)
        def _():
            for cp in copies(nxt_ref[0], 1 - slot):
                cp.start()

    @pl.when(first)
    def _():
        for cp in copies(e, slot):
            cp.wait()
        wg_sc[...] = wg_buf[slot].astype(BF16)
        wu_sc[...] = wu_buf[slot].astype(BF16)
        wd_sc[...] = wd_buf[slot].astype(BF16)
        nxt2 = nxt2_ref[i]

        @pl.when(nxt2 >= 0)
        def _():
            for cp in copies(nxt2, slot):
                cp.start()

    @pl.when(used)
    def _():
        xp = xs_ref[...]
        x = jnp.concatenate([pltpu.bitcast(xp << 16, F32).astype(BF16),
                             pltpu.bitcast(xp & jnp.int32(-65536), F32).astype(BF16)], axis=1)
        hid = jax.nn.silu(_dot(x, wg_sc[...])) * _dot(x, wu_sc[...])
        o_ref[...] = _dot(hid.astype(BF16), wd_sc[...])

    @pl.when(jnp.logical_not(used))
    def _():
        o_ref[...] = jnp.zeros(o_ref.shape, o_ref.dtype)


def _combine_kernel(dest_ref, h_ref, wgt_ref, g_ref, b_ref, ys_hbm, o_ref, ob_ref, ybuf, sem, *, tc, alpha):
    i = pl.program_id(0)

    def gather(tile, slot):
        def issue(t, carry):
            tok = tile * tc + t
            _row_copy(ys_hbm, dest_ref[2 * tok], ybuf.at[slot], t, sem.at[slot]).start()
            _row_copy(ys_hbm, dest_ref[2 * tok + 1], ybuf.at[slot], tc + t, sem.at[slot]).start()
            return carry
        lax.fori_loop(0, tc, issue, 0, unroll=8)

    @pl.when(i == 0)
    def _():
        gather(0, 0)

    @pl.when(i + 1 < pl.num_programs(0))
    def _():
        gather(i + 1, (i + 1) % 2)

    slot = i % 2
    pltpu.make_async_copy(ys_hbm.at[pl.ds(0, 2 * tc)], ybuf.at[slot], sem.at[slot]).wait()
    wgt = wgt_ref[...]
    ff = ybuf[slot, 0:tc, :] * wgt[:, 0:1] + ybuf[slot, tc:2 * tc, :] * wgt[:, 1:2]
    h2 = _layer_norm(alpha * h_ref[...] + ff, g_ref[...], b_ref[...])
    o_ref[...] = h2
    ob_ref[...] = h2.astype(BF16)


def _tiles(seq):
    big = seq >= 2048
    return dict(
        rows=512 if big else 128,
        out_rows=256 if big else 128,
        mla=512 if big else 128,
        mla_heads=8,
        idx_q=512 if big else 64,
        dsa_q=512 if big else 128,
        key=512 if big else 128,
        route=512 if big else 128,
        moe_tok=512 if big else 128,
        comb=256 if big else 128,
    )


def _rope_tables(positions, rot_dim, head_dim):
    half = rot_dim // 2
    inv = ROPE_THETA ** (-jnp.arange(0, rot_dim, 2, dtype=F32) / rot_dim)
    ang = positions.astype(F32).reshape(-1, 1) * inv
    cos, sin = jnp.cos(ang), jnp.sin(ang)
    t = cos.shape[0]
    rest = head_dim - rot_dim
    ones, zeros, zh = jnp.ones((t, rest), F32), jnp.zeros((t, rest), F32), jnp.zeros((t, half), F32)
    rep = LANES // head_dim
    c = jnp.tile(jnp.concatenate([cos, cos, ones], axis=1), (1, rep))
    sa = jnp.tile(jnp.concatenate([-sin, zh, zeros], axis=1), (1, rep))
    sb = jnp.tile(jnp.concatenate([zh, sin, zeros], axis=1), (1, rep))
    return c, sa, sb


def _layer(layer, h, h_bf, positions, tabs, lw, alpha, B, S):
    T, D = h.shape
    tl = _tiles(S)
    tm = tl["rows"]
    nrow = T // tm
    spb = S // tm
    (c_m, sa_m, sb_m), (c_d, sa_d, sb_d), (c_i, sa_i, sb_i) = tabs
    row = lambda w: pl.BlockSpec((tm, w), lambda i: (i, 0))
    head_major = lambda nh, w: pl.BlockSpec((1, nh, tm, w), lambda i: (i // spb, 0, i % spb, 0))

    log2e = math.log2(math.e)
    q_a, k_a, v_a = pl.pallas_call(
        functools.partial(_mla_prep_kernel, q_scale=log2e * MLA_QK_DIM ** -0.5),
        grid=(nrow,),
        in_specs=[row(D), _resident_layer(lw["w_mla_in"], layer), _resident((1, MLA_Q_LORA)), _resident((1, MLA_KV_LORA)),
                  _resident(lw["w_q_b"].shape), _resident(lw["w_kv_b"].shape), row(LANES), row(LANES), row(LANES)],
        out_specs=[head_major(MLA_HEADS, MLA_QK_DIM), head_major(MLA_HEADS, MLA_QK_DIM), row(2 * MLA_HEADS * MLA_V_DIM)],
        out_shape=[jax.ShapeDtypeStruct((B, MLA_HEADS, S, MLA_QK_DIM), BF16),
                   jax.ShapeDtypeStruct((B, MLA_HEADS, S, MLA_QK_DIM), BF16),
                   jax.ShapeDtypeStruct((T, 2 * MLA_HEADS * MLA_V_DIM), BF16)],
        compiler_params=_params("parallel"),
    )(h_bf, lw["w_mla_in"], lw["g_q"], lw["g_kv"], lw["w_q_b"], lw["w_kv_b"], c_m, sa_m, sb_m)

    hd = DSA_HEADS * DSA_HEAD_DIM
    q_b, k_b, v_b, q_idx, k_idx, w_idx = pl.pallas_call(
        functools.partial(_dsa_prep_kernel, w_idx_scale=(IDX_HEADS ** -0.5) * (IDX_DIM ** -0.5),
                          q_scale=log2e * DSA_HEAD_DIM ** -0.5),
        grid=(nrow,),
        in_specs=[row(D), _resident_layer(lw["w_dsa_in"], layer)] + [row(LANES)] * 6,
        out_specs=[row(hd), row(hd), row(2 * hd), head_major(IDX_HEADS, IDX_DIM), row(IDX_DIM),
                   pl.BlockSpec((1, IDX_HEADS, tm), lambda i: (i // spb, 0, i % spb))],
        out_shape=[jax.ShapeDtypeStruct((T, hd), BF16)] * 2 + [jax.ShapeDtypeStruct((T, 2 * hd), BF16)] + [
            jax.ShapeDtypeStruct((B, IDX_HEADS, S, IDX_DIM), BF16),
            jax.ShapeDtypeStruct((T, IDX_DIM), BF16),
            jax.ShapeDtypeStruct((B, IDX_HEADS, S), F32)],
        compiler_params=_params("parallel"),
    )(h_bf, lw["w_dsa_in"], c_d, sa_d, sb_d, c_i, sa_i, sb_i)

    ta = tl["mla"]
    pos_q = positions.reshape(B, S, 1)
    pos_k_mla = positions.reshape(B, S // ta, 1, ta)
    hp = tl["mla_heads"]
    o_a = pl.pallas_call(
        functools.partial(_mla_attn_kernel, tile=ta, heads=hp),
        grid=(B, MLA_HEADS // hp, S // ta),
        in_specs=[pl.BlockSpec((1, hp, ta, MLA_QK_DIM), lambda b, g, i: (b, g, i, 0)),
                  pl.BlockSpec((1, hp, S, MLA_QK_DIM), lambda b, g, i: (b, g, 0, 0), pipeline_mode=pl.Buffered(1)),
                  pl.BlockSpec((1, S, 2 * MLA_V_DIM * hp), lambda b, g, i: (b, 0, g), pipeline_mode=pl.Buffered(1)),
                  pl.BlockSpec((1, ta, 1), lambda b, g, i: (b, i, 0)),
                  pl.BlockSpec((1, S // ta, 1, ta), lambda b, g, i: (b, 0, 0, 0))],
        out_specs=pl.BlockSpec((1, ta, MLA_V_DIM * hp), lambda b, g, i: (b, i, g)),
        out_shape=jax.ShapeDtypeStruct((B, S, MLA_HEADS * MLA_V_DIM), BF16),
        scratch_shapes=[pltpu.VMEM((hp, ta, LANES), F32), pltpu.VMEM((hp, ta, 2 * MLA_V_DIM), F32)],
        compiler_params=_params("parallel", "parallel", "arbitrary"),
    )(q_a, k_a, v_a.reshape(B, S, -1), pos_q, pos_k_mla)

    tk = tl["key"]
    nkc = S // tk
    tqi = tl["idx_q"]
    top_k = min(DSA_MAX_TOPK, S // 4)
    sel_bias = pl.pallas_call(
        functools.partial(_idx_select_kernel, tq=tqi, tk=tk, nkc=nkc, top_k=top_k),
        grid=(B, S // tqi),
        in_specs=[pl.BlockSpec((1, IDX_HEADS, tqi, IDX_DIM), lambda b, i: (b, 0, i, 0)),
                  pl.BlockSpec((1, S, IDX_DIM), lambda b, i: (b, 0, 0)),
                  pl.BlockSpec((1, IDX_HEADS, tqi), lambda b, i: (b, 0, i)),
                  pl.BlockSpec((1, 1, tqi), lambda b, i: (b, 0, i)),
                  pl.BlockSpec((1, nkc, tk, 1), lambda b, i: (b, 0, 0, 0))],
        out_specs=pl.BlockSpec((1, nkc, tqi, tk), lambda b, i: (b, 0, i, 0)),
        out_shape=jax.ShapeDtypeStruct((B, nkc, S, tk), BF16),
        scratch_shapes=[pltpu.VMEM((nkc, tk, tqi), I32)],
        compiler_params=_params("parallel", "arbitrary"),
    )(q_idx, k_idx.reshape(B, S, IDX_DIM), w_idx, positions.reshape(B, 1, S), positions.reshape(B, nkc, tk, 1))

    tqd = tl["dsa_q"]
    o_b = pl.pallas_call(
        functools.partial(_dsa_attn_kernel, tq=tqd, tk=tk),
        grid=(B, S // tqd),
        in_specs=[pl.BlockSpec((1, tqd, hd), lambda b, i: (b, i, 0)),
                  pl.BlockSpec((1, S, hd), lambda b, i: (b, 0, 0), pipeline_mode=pl.Buffered(1)),
                  pl.BlockSpec((1, S, 2 * hd), lambda b, i: (b, 0, 0), pipeline_mode=pl.Buffered(1)),
                  pl.BlockSpec((1, nkc, tqd, tk), lambda b, i: (b, 0, i, 0))],
        out_specs=pl.BlockSpec((1, tqd, hd), lambda b, i: (b, i, 0)),
        out_shape=jax.ShapeDtypeStruct((B, S, hd), BF16),
        scratch_shapes=[pltpu.VMEM((DSA_HEADS, tqd, LANES), F32), pltpu.VMEM((DSA_HEADS, tqd, 2 * DSA_HEAD_DIM), F32)],
        compiler_params=_params("parallel", "arbitrary"),
    )(q_b.reshape(B, S, hd), k_b.reshape(B, S, hd), v_b.reshape(B, S, 2 * hd), sel_bias)

    y = pl.pallas_call(
        _merge_kernel,
        grid=(nrow,),
        in_specs=[row(D), row(MLA_HEADS * MLA_V_DIM), row(hd),
                  _resident_layer(lw["w_gate_a"], layer), _resident_layer(lw["w_gate_b"], layer),
                  _resident((MLA_HEADS * MLA_V_DIM, D)), _resident((hd, D))],
        out_specs=row(D),
        out_shape=jax.ShapeDtypeStruct((T, D), BF16),
        compiler_params=_params("parallel"),
    )(h_bf, o_a.reshape(T, -1), o_b.reshape(T, hd), lw["w_gate_a"], lw["w_gate_b"], lw["w_o_a"], lw["w_o_b"])

    to = tl["out_rows"]
    orow = lambda w: pl.BlockSpec((to, w), lambda i: (i, 0))
    h1, h1_packed, logits = pl.pallas_call(
        functools.partial(_out_ln_kernel, alpha=alpha),
        grid=(T // to,),
        in_specs=[orow(D), orow(D), _resident((D, D)), _resident((1, D)), _resident((1, D)),
                  _resident((D, LANES)), _resident((D, LANES)), _resident((1, LANES))],
        out_specs=[orow(D), orow(D // 2), orow(LANES)],
        out_shape=[jax.ShapeDtypeStruct((T, D), F32), jax.ShapeDtypeStruct((T, D // 2), I32),
                   jax.ShapeDtypeStruct((T, LANES), F32)],
        compiler_params=_params("parallel"),
    )(y, h, lw["w_out"], lw["ln1_g"], lw["ln1_b"], lw["w_route_hi"], lw["w_route_lo"], lw["b_route"])

    tr = tl["route"]
    meta, wgt, cnt = pl.pallas_call(
        functools.partial(_route_kernel, tr=tr),
        grid=(T // tr,),
        in_specs=[pl.BlockSpec((tr, LANES), lambda i: (i, 0))],
        out_specs=[pl.BlockSpec((tr, LANES), lambda i: (i, 0)), pl.BlockSpec((tr, LANES), lambda i: (i, 0)),
                   pl.BlockSpec((8, LANES), lambda i: (0, 0))],
        out_shape=[jax.ShapeDtypeStruct((T, LANES), I32), jax.ShapeDtypeStruct((T, LANES), F32),
                   jax.ShapeDtypeStruct((8, LANES), F32)],
        scratch_shapes=[pltpu.VMEM((1, LANES), F32)],
        compiler_params=_params("arbitrary"),
    )(logits)

    rb = MOE_ROW_BLOCK
    counts = cnt[0, N_GROUPS:N_GROUPS + N_EXPERTS].astype(I32)
    padded = ((counts + rb - 1) // rb) * rb
    pends = jnp.cumsum(padded)
    pstarts = pends - padded
    n_blocks = -(-(2 * T) // rb) + N_EXPERTS
    P = n_blocks * rb
    blk = jnp.arange(n_blocks, dtype=I32)
    block_e = jnp.minimum(jnp.sum((pends[None, :] <= (blk * rb)[:, None]).astype(I32), axis=1), N_EXPERTS - 1)
    n_used = (pends[-1] // rb).astype(I32).reshape(1)
    ps_lanes = jnp.zeros((1, LANES), I32).at[0, :N_EXPERTS].set(pstarts.astype(I32))
    dest = pl.pallas_call(
        _dest_kernel,
        grid=(T // tr,),
        in_specs=[pl.BlockSpec((tr, LANES), lambda i: (i, 0)), pl.BlockSpec((1, LANES), lambda i: (0, 0))],
        out_specs=pl.BlockSpec((tr, LANES), lambda i: (i, 0)),
        out_shape=jax.ShapeDtypeStruct((T, LANES), I32),
        compiler_params=_params("parallel"),
    )(meta, ps_lanes)[:, 0:2].reshape(-1)

    last_of_expert = jnp.concatenate([block_e[1:] != block_e[:-1], jnp.ones((1,), bool)])
    needs_zero = jnp.logical_or(blk >= n_used[0], jnp.logical_or(last_of_expert, blk == n_used[0] - 1)).astype(I32)
    n_zero = jnp.sum(needs_zero).astype(I32).reshape(1)
    tt = tl["moe_tok"]
    xs = pl.pallas_call(
        functools.partial(_dispatch_kernel, tt=tt),
        grid_spec=pltpu.PrefetchScalarGridSpec(
            num_scalar_prefetch=3, grid=(T // tt,),
            in_specs=[pl.BlockSpec((tt, D // 2), lambda i, *_: (i, 0))],
            out_specs=pl.BlockSpec(memory_space=pl.ANY),
            scratch_shapes=[pltpu.VMEM((rb, D // 2), I32), pltpu.SemaphoreType.DMA(()), pltpu.SemaphoreType.DMA(())]),
        out_shape=jax.ShapeDtypeStruct((P, D // 2), I32),
        compiler_params=_params("arbitrary"),
    )(dest, needs_zero, n_zero, h1_packed)

    F = D_EXPERT
    grp = jnp.cumsum(jnp.concatenate([jnp.zeros((1,), I32), (block_e[1:] != block_e[:-1]).astype(I32)]))
    ids = jnp.where(counts > 0, jnp.arange(N_EXPERTS, dtype=I32), N_EXPERTS)
    at_or_after = lax.cummin(ids, axis=0, reverse=True)
    after = jnp.concatenate([at_or_after[1:], jnp.full((1,), N_EXPERTS, I32)])
    nxt_of = jnp.where(after < N_EXPERTS, after, -1)
    nxt2_of = jnp.where(nxt_of >= 0, nxt_of[jnp.maximum(nxt_of, 0)], -1)
    nxt = nxt_of[block_e].astype(I32)
    nxt2 = nxt2_of[block_e].astype(I32)
    any_spec = pl.BlockSpec(memory_space=pl.ANY)
    ys = pl.pallas_call(
        functools.partial(_experts_kernel, layer=layer),
        grid_spec=pltpu.PrefetchScalarGridSpec(
            num_scalar_prefetch=5, grid=(n_blocks,),
            in_specs=[pl.BlockSpec((rb, D // 2), lambda i, be, gr, nx, nx2, nu: (jnp.minimum(i, nu[0] - 1), 0)),
                      any_spec, any_spec, any_spec],
            out_specs=pl.BlockSpec((rb, D), lambda i, *_: (i, 0)),
            scratch_shapes=[pltpu.VMEM((2, D, F), F32), pltpu.VMEM((2, D, F), F32), pltpu.VMEM((2, F, D), F32),
                            pltpu.VMEM((D, F), BF16), pltpu.VMEM((D, F), BF16), pltpu.VMEM((F, D), BF16),
                            pltpu.SemaphoreType.DMA((2, 3))]),
        out_shape=jax.ShapeDtypeStruct((P, D), F32),
        compiler_params=_params("arbitrary"),
    )(block_e, grp.astype(I32), nxt, nxt2, n_used, xs, lw["w_e_gate"], lw["w_e_up"], lw["w_e_down"])

    tc = tl["comb"]
    h2, h2_bf = pl.pallas_call(
        functools.partial(_combine_kernel, tc=tc, alpha=alpha),
        grid_spec=pltpu.PrefetchScalarGridSpec(
            num_scalar_prefetch=1, grid=(T // tc,),
            in_specs=[pl.BlockSpec((tc, D), lambda i, d: (i, 0)),
                      pl.BlockSpec((tc, LANES), lambda i, d: (i, 0)),
                      pl.BlockSpec((1, D), lambda i, d: (0, 0)),
                      pl.BlockSpec((1, D), lambda i, d: (0, 0)),
                      pl.BlockSpec(memory_space=pl.ANY)],
            out_specs=[pl.BlockSpec((tc, D), lambda i, d: (i, 0)), pl.BlockSpec((tc, D), lambda i, d: (i, 0))],
            scratch_shapes=[pltpu.VMEM((2, 2 * tc, D), F32), pltpu.SemaphoreType.DMA((2,))]),
        out_shape=[jax.ShapeDtypeStruct((T, D), F32), jax.ShapeDtypeStruct((T, D), BF16)],
        compiler_params=_params("arbitrary"),
    )(dest, h1, wgt, lw["ln2_g"], lw["ln2_b"], ys)
    return h2, h2_bf


def _layer_weights(l, regrouped, g_q_lora, w_q_b, g_kv_lora, w_kv_b, w_o_a, w_o_b, w_out, ln1_g, ln1_b,
                   w_group, b_group, w_router, b_router, w_e_gate, w_e_up, w_e_down, ln2_g, ln2_b):
    w_mla_in, w_dsa_in, w_gate_a, w_gate_b = regrouped
    D = w_gate_a.shape[1]
    wq = w_q_b[l].reshape(MLA_Q_LORA, MLA_HEADS, MLA_QK_DIM)
    wq = jnp.concatenate([wq[:, :, :MLA_NOPE_DIM].reshape(MLA_Q_LORA, -1),
                          wq[:, :, MLA_NOPE_DIM:].reshape(MLA_Q_LORA, -1)], axis=1).astype(BF16)
    w_route = jnp.concatenate([w_group[l], w_router[l],
                               jnp.zeros((D, LANES - N_GROUPS - N_EXPERTS), F32)], axis=1)
    b_route = jnp.concatenate([b_group[l], b_router[l],
                               jnp.zeros((LANES - N_GROUPS - N_EXPERTS,), F32)]).reshape(1, LANES)
    return dict(
        w_mla_in=w_mla_in, w_dsa_in=w_dsa_in, w_gate_a=w_gate_a, w_gate_b=w_gate_b,
        g_q=g_q_lora[l].reshape(1, -1), g_kv=g_kv_lora[l].reshape(1, -1),
        w_q_b=wq, w_kv_b=w_kv_b[l].astype(BF16),
        w_o_a=w_o_a[l].astype(BF16), w_o_b=w_o_b[l].astype(BF16), w_out=w_out[l].astype(BF16),
        ln1_g=ln1_g[l].reshape(1, -1), ln1_b=ln1_b[l].reshape(1, -1),
        w_route_hi=w_route.astype(BF16), w_route_lo=(w_route - w_route.astype(BF16).astype(F32)).astype(BF16),
        b_route=b_route,
        w_e_gate=w_e_gate, w_e_up=w_e_up, w_e_down=w_e_down,
        ln2_g=ln2_g[l].reshape(1, -1), ln2_b=ln2_b[l].reshape(1, -1),
    )


def kernel(x, positions, w_in, g_q_lora, w_q_b, g_kv_lora, w_kv_b, w_o_a, w_o_b, w_out, ln1_g, ln1_b,
           w_group, b_group, w_router, b_router, w_e_gate, w_e_up, w_e_down, ln2_g, ln2_b):
    B, S, D = x.shape
    depth = w_in.shape[0]
    alpha = (2 * depth) ** 0.25
    tabs = (_rope_tables(positions, MLA_ROPE_DIM, MLA_ROPE_DIM),
            _rope_tables(positions, DSA_ROPE_DIM, DSA_HEAD_DIM),
            _rope_tables(positions, IDX_ROPE_DIM, IDX_DIM))
    h = x.reshape(B * S, D)
    h_bf = h.astype(BF16)
    regrouped = _regroup_w_in(w_in)
    for l in range(depth):
        lw = _layer_weights(l, regrouped, g_q_lora, w_q_b, g_kv_lora, w_kv_b, w_o_a, w_o_b, w_out, ln1_g, ln1_b,
                            w_group, b_group, w_router, b_router, w_e_gate, w_e_up, w_e_down, ln2_g, ln2_b)
        h, h_bf = _layer(l, h, h_bf, positions, tabs, lw, alpha, B, S)
    return h.reshape(B, S, D)
```

```python
import functools
import math

import jax
import jax.numpy as jnp
from jax import lax
from jax.experimental import pallas as pl
from jax.experimental.pallas import tpu as pltpu

F32 = jnp.float32
BF16 = jnp.bfloat16
I32 = jnp.int32

MLA_HEADS = 8
MLA_Q_LORA = 512
MLA_KV_LORA = 512
MLA_NOPE_DIM = 128
MLA_ROPE_DIM = 64
MLA_V_DIM = 128
MLA_QK_DIM = MLA_NOPE_DIM + MLA_ROPE_DIM
DSA_HEADS = 8
DSA_HEAD_DIM = 128
DSA_ROPE_DIM = DSA_HEAD_DIM // 4
IDX_HEADS = 16
IDX_DIM = 64
IDX_ROPE_DIM = IDX_DIM // 4
DSA_MAX_TOPK = 256
ROPE_THETA = 500000.0
N_GROUPS = 8
EXPERTS_PER_GROUP = 8
N_EXPERTS = N_GROUPS * EXPERTS_PER_GROUP
D_EXPERT = 512
MOE_ROW_BLOCK = 128
LN_EPS = 1e-5
RMS_EPS = 1e-6

LANES = 128
NEG_BIG = -1e30
INT_MIN = -(2 ** 31)
VMEM_LIMIT = 56 * 1024 * 1024
WEIGHT_DMA_CHUNKS = 4


def _params(*sem):
    return pltpu.CompilerParams(dimension_semantics=sem, vmem_limit_bytes=VMEM_LIMIT)


def _resident(shape):
    nd = len(shape)
    return pl.BlockSpec(shape, lambda *_: (0,) * nd, pipeline_mode=pl.Buffered(1))


def _resident_layer(stacked, layer):
    shape = stacked.shape[1:]
    nd = len(shape)
    return pl.BlockSpec((None,) + shape, lambda *_: (layer,) + (0,) * nd, pipeline_mode=pl.Buffered(1))


def _dot(a, b):
    return jnp.dot(a, b, preferred_element_type=F32)


def _dot_nt(a, b):
    return lax.dot_general(a, b, (((1,), (1,)), ((), ())), preferred_element_type=F32)


def _rope128(x, c, sa, sb, half):
    return x * c + pltpu.roll(x, LANES - half, 1) * sa + pltpu.roll(x, half, 1) * sb


def _rms(x, g):
    return x * lax.rsqrt(jnp.mean(x * x, axis=-1, keepdims=True) + RMS_EPS) * g


def _layer_norm(x, g, b):
    mu = jnp.mean(x, axis=-1, keepdims=True)
    xc = x - mu
    var = jnp.mean(xc * xc, axis=-1, keepdims=True)
    return xc * lax.rsqrt(var + LN_EPS) * g + b


_O_KPE = MLA_Q_LORA + MLA_KV_LORA
_O_DSA = _O_KPE + MLA_ROPE_DIM
_O_KIDX = _O_DSA + 3 * DSA_HEADS * DSA_HEAD_DIM + IDX_HEADS * IDX_DIM
_O_WIDX = _O_KIDX + IDX_DIM
_O_GATE = _O_WIDX + IDX_HEADS
_W_MLA_IN = _O_KPE + 2 * MLA_ROPE_DIM
_W_DSA_IN = (_O_KIDX - _O_DSA) + 2 * IDX_DIM + LANES


def _regroup_w_in_kernel(w_ref, mla_ref, dsa_ref, ga_ref, gb_ref, *, d_model):
    mla_ref[:, 0:_O_DSA] = w_ref[:, 0:_O_DSA].astype(BF16)
    mla_ref[:, _O_DSA:_W_MLA_IN] = w_ref[:, _O_KPE:_O_DSA].astype(BF16)
    main = _O_KIDX - _O_DSA
    dsa_ref[:, 0:main] = w_ref[:, _O_DSA:_O_KIDX].astype(BF16)
    kidx = w_ref[:, _O_KIDX:_O_WIDX].astype(BF16)
    dsa_ref[:, main:main + IDX_DIM] = kidx
    dsa_ref[:, main + IDX_DIM:main + 2 * IDX_DIM] = kidx
    dsa_ref[:, main + 2 * IDX_DIM:main + 2 * IDX_DIM + IDX_HEADS] = w_ref[:, _O_WIDX:_O_GATE].astype(BF16)
    dsa_ref[:, main + 2 * IDX_DIM + IDX_HEADS:] = jnp.zeros((w_ref.shape[0], LANES - IDX_HEADS), BF16)
    ga_ref[...] = w_ref[:, _O_GATE:_O_GATE + d_model].astype(BF16)
    gb_ref[...] = w_ref[:, _O_GATE + d_model:_O_GATE + 2 * d_model].astype(BF16)


def _regroup_w_in(w_in):
    depth, d_model, width = w_in.shape
    tr = 256
    out_w = (_W_MLA_IN, _W_DSA_IN, d_model, d_model)
    return pl.pallas_call(
        functools.partial(_regroup_w_in_kernel, d_model=d_model),
        grid=(depth, d_model // tr),
        in_specs=[pl.BlockSpec((tr, width), lambda l, i: (l * (d_model // tr) + i, 0))],
        out_specs=[pl.BlockSpec((None, tr, w), lambda l, i: (l, i, 0)) for w in out_w],
        out_shape=[jax.ShapeDtypeStruct((depth, d_model, w), BF16) for w in out_w],
        compiler_params=_params("parallel", "parallel"),
    )(w_in.astype(BF16).reshape(depth * d_model, width))


def _mla_prep_kernel(h_ref, win_ref, gq_ref, gkv_ref, wqb_ref, wkvb_ref, c_ref, sa_ref, sb_ref,
                     q_ref, k_ref, v_ref, *, q_scale):
    h = h_ref[...]
    p = _dot(h, win_ref[...])
    qn = _rms(p[:, :MLA_Q_LORA], gq_ref[...]).astype(BF16)
    kvn = _rms(p[:, MLA_Q_LORA:MLA_Q_LORA + MLA_KV_LORA], gkv_ref[...]).astype(BF16)
    q = _dot(qn, wqb_ref[...]) * q_scale
    kv = _dot(kvn, wkvb_ref[...])
    c, sa, sb = c_ref[...], sa_ref[...], sb_ref[...]
    half = MLA_ROPE_DIM // 2
    kpe = _rope128(p[:, MLA_Q_LORA + MLA_KV_LORA:], c, sa, sb, half)[:, :MLA_ROPE_DIM].astype(BF16)
    pe0 = MLA_HEADS * MLA_NOPE_DIM
    for hh in range(MLA_HEADS):
        q_ref[0, hh, :, 0:MLA_NOPE_DIM] = q[:, 128 * hh:128 * hh + 128].astype(BF16)
        k_ref[0, hh, :, 0:MLA_NOPE_DIM] = kv[:, 256 * hh:256 * hh + 128].astype(BF16)
        k_ref[0, hh, :, MLA_NOPE_DIM:MLA_QK_DIM] = kpe
        v_ref[:, 256 * hh:256 * hh + 128] = kv[:, 256 * hh + 128:256 * hh + 256].astype(BF16)
        v_ref[:, 256 * hh + 128:256 * hh + 256] = jnp.ones((kv.shape[0], LANES), BF16)
    for s in range(MLA_HEADS // 2):
        slab = _rope128(q[:, pe0 + 128 * s:pe0 + 128 * s + 128], c, sa, sb, half).astype(BF16)
        q_ref[0, 2 * s, :, MLA_NOPE_DIM:MLA_QK_DIM] = slab[:, :MLA_ROPE_DIM]
        q_ref[0, 2 * s + 1, :, MLA_NOPE_DIM:MLA_QK_DIM] = slab[:, MLA_ROPE_DIM:]


def _dsa_prep_kernel(h_ref, w_ref, cd_ref, sad_ref, sbd_ref, ci_ref, sai_ref, sbi_ref,
                     qb_ref, kb_ref, vb_ref, qi_ref, ki_ref, wi_ref, *, w_idx_scale, q_scale):
    h = h_ref[...]
    hd = DSA_HEADS * DSA_HEAD_DIM
    cd, sad, sbd = cd_ref[...], sad_ref[...], sbd_ref[...]
    ci, sai, sbi = ci_ref[...], sai_ref[...], sbi_ref[...]
    hd_half, hi_half = DSA_ROPE_DIM // 2, IDX_ROPE_DIM // 2
    q = _dot(h, w_ref[:, 0:hd]) * q_scale
    for hh in range(DSA_HEADS):
        qb_ref[:, 128 * hh:128 * hh + 128] = _rope128(q[:, 128 * hh:128 * hh + 128], cd, sad, sbd, hd_half).astype(BF16)
    k = _dot(h, w_ref[:, hd:2 * hd])
    for hh in range(DSA_HEADS):
        kb_ref[:, 128 * hh:128 * hh + 128] = _rope128(k[:, 128 * hh:128 * hh + 128], cd, sad, sbd, hd_half).astype(BF16)
    v = _dot(h, w_ref[:, 2 * hd:3 * hd]).astype(BF16)
    for hh in range(DSA_HEADS):
        vb_ref[:, 256 * hh:256 * hh + 128] = v[:, 128 * hh:128 * hh + 128]
        vb_ref[:, 256 * hh + 128:256 * hh + 256] = jnp.ones((v.shape[0], LANES), BF16)
    qi = _dot(h, w_ref[:, 3 * hd:3 * hd + IDX_HEADS * IDX_DIM])
    for s in range(IDX_HEADS // 2):
        slab = _rope128(qi[:, 128 * s:128 * s + 128], ci, sai, sbi, hi_half).astype(BF16)
        qi_ref[0, 2 * s] = slab[:, :IDX_DIM]
        qi_ref[0, 2 * s + 1] = slab[:, IDX_DIM:]
    o = 3 * hd + IDX_HEADS * IDX_DIM
    last = _dot(h, w_ref[:, o:o + 2 * LANES])
    ki_ref[...] = _rope128(last[:, :LANES], ci, sai, sbi, hi_half)[:, :IDX_DIM].astype(BF16)
    wi_ref[0] = (last[:, LANES:] * w_idx_scale).T[:IDX_HEADS, :]


def _flash_update(s, v_ones, m_ref, acc_ref):
    tk = s.shape[1]
    m_prev = m_ref[...]
    m_next = jnp.maximum(m_prev, jnp.max(s, axis=1, keepdims=True))
    p = jnp.exp2(s - jnp.tile(m_next, (1, tk // LANES)))
    alpha = jnp.exp2(m_prev - m_next)
    acc_ref[...] = acc_ref[...] * jnp.tile(alpha, (1, 2)) + _dot(p.astype(BF16), v_ones)
    m_ref[...] = m_next


def _mla_attn_kernel(q_ref, k_ref, v_ref, pq_ref, pk_ref, o_ref, m_sc, acc_sc, *, tile, heads):
    qi = pl.program_id(2)
    m_sc[...] = jnp.full(m_sc.shape, -jnp.inf, F32)
    acc_sc[...] = jnp.zeros(acc_sc.shape, F32)

    def chunk(c, masked):
        ks = pl.ds(pl.multiple_of(c * tile, tile), tile)
        if masked:
            visible = pk_ref[0, c] <= pq_ref[0]
        for hh in range(heads):
            s = _dot_nt(q_ref[0, hh], k_ref[0, hh, ks, :])
            if masked:
                s = jnp.where(visible, s, NEG_BIG)
            _flash_update(s, v_ref[0, ks, 2 * MLA_V_DIM * hh:2 * MLA_V_DIM * (hh + 1)], m_sc.at[hh], acc_sc.at[hh])

    def body(c, carry):
        chunk(c, False)
        return carry

    lax.fori_loop(0, qi, body, 0)
    chunk(qi, True)
    for hh in range(heads):
        o_ref[0, :, MLA_V_DIM * hh:MLA_V_DIM * (hh + 1)] = (
            acc_sc[hh, :, :MLA_V_DIM] / acc_sc[hh, :, MLA_V_DIM:]).astype(o_ref.dtype)


def _idx_select_kernel(q_ref, k_ref, wt_ref, pq_ref, pk_ref, o_ref, key_sc, *, tq, tk, nkc, top_k):
    qi = pl.program_id(1)
    nk = (qi * tq + tq + tk - 1) // tk
    wt = wt_ref[0]
    pq = pq_ref[0]

    def score_chunk(c, carry):
        kc = k_ref[0, pl.ds(pl.multiple_of(c * tk, tk), tk), :]
        acc = jnp.zeros((tk, tq), F32)
        for hh in range(IDX_HEADS):
            acc = acc + wt[hh:hh + 1, :] * jnp.maximum(_dot_nt(kc, q_ref[0, hh]), 0.0)
        bits = pltpu.bitcast(acc, I32)
        key = jnp.where(bits < 0, bits ^ jnp.int32(0x7FFFFFFF), bits)
        key = jnp.where(acc == 0.0, 0, key)
        key_sc[c] = jnp.where(pk_ref[0, c] <= pq, key, INT_MIN)
        return carry

    lax.fori_loop(0, nk, score_chunk, 0)

    def count(pred):
        def body(c, part):
            m = pred(key_sc[c], c).astype(I32)
            return part + jnp.sum(m.reshape(tk // 8, 8, tq), axis=0)
        part = lax.fori_loop(0, nk, body, jnp.zeros((8, tq), I32))
        return jnp.sum(part, axis=0, keepdims=True)

    def bit_step(i, carry):
        base, at_base = carry
        cand = base ^ (jnp.int32(1) << (31 - i))
        cnt = count(lambda key, c: key >= cand)
        take = cnt >= top_k
        return jnp.where(take, cand, base), jnp.where(take, cnt, at_base)

    def refine(lo, hi, carry):
        settled = jnp.min((carry[1] <= top_k).astype(I32)) > 0
        return lax.cond(settled, lambda c: c, lambda c: lax.fori_loop(lo, hi, bit_step, c), carry)

    carry = lax.fori_loop(0, 24, bit_step, (jnp.full((1, tq), INT_MIN, I32), jnp.full((1, tq), 2 ** 30, I32)))
    carry = refine(24, 28, carry)
    thr, at_thr = refine(28, 32, carry)
    tie = jnp.logical_and(at_thr > top_k, thr != INT_MIN)

    def key_index(c):
        return c * tk + lax.broadcasted_iota(I32, (tk, tq), 0)

    def tie_search():
        n_eq = count(lambda key, c: key == thr)
        need = top_k - (at_thr - n_eq)

        def step(i, p):
            cand = p | (jnp.int32(1) << (14 - i))
            cnt = count(lambda key, c: jnp.logical_and(key == thr, key_index(c) < cand))
            return jnp.where(cnt <= need, cand, p)
        return lax.fori_loop(0, 15, step, jnp.zeros((1, tq), I32))

    p_cut = lax.cond(jnp.max(tie.astype(I32)) > 0, tie_search,
                     lambda: jnp.full((1, tq), 2 ** 30, I32))

    def out_chunk(c, carry):
        key = key_sc[c]
        sel = jnp.logical_or(key > thr, jnp.logical_and(key == thr, key_index(c) < p_cut))
        sel = jnp.logical_and(sel, key != INT_MIN)
        o_ref[0, c] = jnp.where(sel, 0.0, NEG_BIG).T.astype(o_ref.dtype)
        return carry

    lax.fori_loop(0, nk, out_chunk, 0)

    def fill_chunk(c, carry):
        o_ref[0, c] = jnp.full((tq, tk), NEG_BIG, o_ref.dtype)
        return carry

    lax.fori_loop(nk, nkc, fill_chunk, 0)


def _dsa_attn_kernel(q_ref, k_ref, v_ref, b_ref, o_ref, m_sc, acc_sc, *, tq, tk):
    qi = pl.program_id(1)
    nk = (qi * tq + tq + tk - 1) // tk
    m_sc[...] = jnp.full(m_sc.shape, NEG_BIG, F32)
    acc_sc[...] = jnp.zeros(acc_sc.shape, F32)

    def chunk(c, carry):
        bias = b_ref[0, c].astype(F32)
        ks = pl.ds(pl.multiple_of(c * tk, tk), tk)
        for hh in range(DSA_HEADS):
            hs = slice(DSA_HEAD_DIM * hh, DSA_HEAD_DIM * (hh + 1))
            vs = slice(2 * DSA_HEAD_DIM * hh, 2 * DSA_HEAD_DIM * (hh + 1))
            s = _dot_nt(q_ref[0, :, hs], k_ref[0, ks, hs]) + bias
            _flash_update(s, v_ref[0, ks, vs], m_sc.at[hh], acc_sc.at[hh])
        return carry

    lax.fori_loop(0, nk, chunk, 0)
    for hh in range(DSA_HEADS):
        hs = slice(DSA_HEAD_DIM * hh, DSA_HEAD_DIM * (hh + 1))
        o_ref[0, :, hs] = (acc_sc[hh, :, :DSA_HEAD_DIM] / acc_sc[hh, :, DSA_HEAD_DIM:]).astype(o_ref.dtype)


def _merge_kernel(h_ref, oa_ref, ob_ref, wga_ref, wgb_ref, woa_ref, wob_ref, y_ref):
    h = h_ref[...]
    ya = jax.nn.sigmoid(_dot(h, wga_ref[...])) * _dot(oa_ref[...], woa_ref[...])
    yb = jax.nn.sigmoid(_dot(h, wgb_ref[...])) * _dot(ob_ref[...], wob_ref[...])
    y_ref[...] = (ya + yb).astype(y_ref.dtype)


def _out_ln_kernel(y_ref, h_ref, wout_ref, g_ref, b_ref, wrh_ref, wrl_ref, br_ref, o_ref, op_ref, lg_ref, *, alpha):
    sub = 128
    half = o_ref.shape[1] // 2
    for r in range(y_ref.shape[0] // sub):
        rows = slice(r * sub, (r + 1) * sub)
        mix = _dot(y_ref[rows, :], wout_ref[...])
        h1 = _layer_norm(alpha * h_ref[rows, :] + mix, g_ref[...], b_ref[...])
        o_ref[rows, :] = h1
        lo = pltpu.bitcast(h1[:, :half].astype(BF16).astype(F32), I32)
        hi = pltpu.bitcast(h1[:, half:].astype(BF16).astype(F32), I32)
        op_ref[rows, :] = lax.shift_right_logical(lo, 16) | hi
        h_hi = h1.astype(BF16)
        h_lo = (h1 - h_hi.astype(F32)).astype(BF16)
        lg_ref[rows, :] = (_dot(h_hi, wrh_ref[...]) + (_dot(h_hi, wrl_ref[...]) + _dot(h_lo, wrh_ref[...]))
                           + br_ref[...])


def _route_kernel(lg_ref, meta_ref, wgt_ref, cnt_ref, carry_sc, *, tr):
    @pl.when(pl.program_id(0) == 0)
    def _():
        carry_sc[...] = jnp.zeros(carry_sc.shape, F32)

    lg = lg_ref[...]
    lane = lax.broadcasted_iota(I32, (tr, LANES), 1)
    gl = jnp.where(lane < N_GROUPS, lg, -jnp.inf)
    gmax = jnp.max(gl, axis=1, keepdims=True)
    g_idx = jnp.min(jnp.where(gl == gmax, lane, LANES), axis=1, keepdims=True)
    g_p = 1.0 / jnp.sum(jnp.exp(gl - gmax), axis=1, keepdims=True)
    lo = N_GROUPS + EXPERTS_PER_GROUP * g_idx
    in_group = jnp.logical_and(lane >= lo, lane < lo + EXPERTS_PER_GROUP)
    el = jnp.where(in_group, lg, -jnp.inf)
    emax = jnp.max(el, axis=1, keepdims=True)
    e1 = jnp.min(jnp.where(el == emax, lane, LANES), axis=1, keepdims=True)
    den = jnp.sum(jnp.exp(el - emax), axis=1, keepdims=True)
    el2 = jnp.where(lane == e1, -jnp.inf, el)
    emax2 = jnp.max(el2, axis=1, keepdims=True)
    e2 = jnp.min(jnp.where(el2 == emax2, lane, LANES), axis=1, keepdims=True)
    p1 = 1.0 / den
    p2 = jnp.exp(emax2 - emax) / den
    w1 = g_p * (p1 / (p1 + p2))
    w2 = g_p * (p2 / (p1 + p2))

    is1 = lane == e1
    is2 = lane == e2
    onehot = jnp.logical_or(is1, is2).astype(BF16)
    r = lax.broadcasted_iota(I32, (tr, tr), 0)
    cidx = lax.broadcasted_iota(I32, (tr, tr), 1)
    lower = (cidx < r).astype(BF16)
    before = _dot(lower, onehot) + carry_sc[...]
    rank1 = jnp.sum(jnp.where(is1, before, 0.0), axis=1, keepdims=True)
    rank2 = jnp.sum(jnp.where(is2, before, 0.0), axis=1, keepdims=True)
    carry_sc[...] = carry_sc[...] + jnp.sum(onehot.astype(F32), axis=0, keepdims=True)

    meta = jnp.where(lane == 0, e1 - N_GROUPS,
                     jnp.where(lane == 1, e2 - N_GROUPS,
                               jnp.where(lane == 2, rank1.astype(I32),
                                         jnp.where(lane == 3, rank2.astype(I32), 0))))
    meta_ref[...] = meta
    wgt_ref[...] = jnp.where(lane == 0, w1, jnp.where(lane == 1, w2, 0.0))
    cnt_ref[...] = jnp.broadcast_to(carry_sc[...], cnt_ref.shape)


def _row_copy(src, src_row, dst, dst_row, sem):
    return pltpu.make_async_copy(src.at[pl.ds(src_row, 1)], dst.at[pl.ds(dst_row, 1)], sem)


def _dest_kernel(meta_ref, ps_ref, o_ref):
    meta = meta_ref[...]
    ps = ps_ref[...]
    lane = lax.broadcasted_iota(I32, meta.shape, 1)
    d1 = jnp.sum(jnp.where(lane == meta[:, 0:1], ps, 0), axis=1, keepdims=True) + meta[:, 2:3]
    d2 = jnp.sum(jnp.where(lane == meta[:, 1:2], ps, 0), axis=1, keepdims=True) + meta[:, 3:4]
    o_ref[...] = jnp.where(lane == 0, d1, jnp.where(lane == 1, d2, 0))


def _dispatch_kernel(dest_ref, zblk_ref, nz_ref, h_ref, xs_out, zero_sc, sem, zsem, *, tt):
    base = pl.program_id(0) * tt
    rb = zero_sc.shape[0]

    @pl.when(pl.program_id(0) == 0)
    def _():
        zero_sc[...] = jnp.zeros(zero_sc.shape, zero_sc.dtype)

        def fill(j, carry):
            @pl.when(zblk_ref[j] != 0)
            def _():
                pltpu.make_async_copy(zero_sc, xs_out.at[pl.ds(pl.multiple_of(j * rb, rb), rb)], zsem).start()
            return carry

        def drain(j, carry):
            pltpu.make_async_copy(zero_sc, xs_out.at[pl.ds(0, rb)], zsem).wait()
            return carry

        lax.fori_loop(0, zblk_ref.shape[0], fill, 0)
        lax.fori_loop(0, nz_ref[0], drain, 0)

    def issue(t, carry):
        tok = base + t
        _row_copy(h_ref, t, xs_out, dest_ref[2 * tok], sem).start()
        _row_copy(h_ref, t, xs_out, dest_ref[2 * tok + 1], sem).start()
        return carry

    lax.fori_loop(0, tt, issue, 0, unroll=8)
    for _ in range(2):
        pltpu.make_async_copy(h_ref, xs_out.at[pl.ds(0, tt)], sem).wait()


def _experts_kernel(be_ref, grp_ref, nxt_ref, nxt2_ref, nu_ref, xs_ref, wg_hbm, wu_hbm, wd_hbm, o_ref,
                    wg_buf, wu_buf, wd_buf, wg_sc, wu_sc, wd_sc, sem, *, layer):
    i = pl.program_id(0)
    e = be_ref[i]
    used = i < nu_ref[0]
    first = jnp.logical_and(used, jnp.logical_or(i == 0, e != be_ref[jnp.maximum(i - 1, 0)]))
    slot = grp_ref[i] % 2

    def copies(expert, s):
        out = []
        for j, (hbm, buf) in enumerate(((wg_hbm, wg_buf), (wu_hbm, wu_buf), (wd_hbm, wd_buf))):
            rows = buf.shape[1] // WEIGHT_DMA_CHUNKS
            for c in range(WEIGHT_DMA_CHUNKS):
                rs = pl.ds(c * rows, rows)
                out.append(pltpu.make_async_copy(hbm.at[layer, expert, rs], buf.at[s, rs], sem.at[s, j]))
        return out

    @pl.when(i == 0)
    def _():
        for cp in copies(e, slot):
            cp.start()

        @pl.when(nxt_ref[0] >= 0)
        def _():
            for cp in copies(nxt_ref[0], 1 - slot):
                cp.start()

    @pl.when(first)
    def _():
        for cp in copies(e, slot):
            cp.wait()
        wg_sc[...] = wg_buf[slot].astype(BF16)
        wu_sc[...] = wu_buf[slot].astype(BF16)
        wd_sc[...] = wd_buf[slot].astype(BF16)
        nxt2 = nxt2_ref[i]

        @pl.when(nxt2 >= 0)
        def _():
            for cp in copies(nxt2, slot):
                cp.start()

    @pl.when(used)
    def _():
        xp = xs_ref[...]
        x = jnp.concatenate([pltpu.bitcast(xp << 16, F32).astype(BF16),
                             pltpu.bitcast(xp & jnp.int32(-65536), F32).astype(BF16)], axis=1)
        hid = jax.nn.silu(_dot(x, wg_sc[...])) * _dot(x, wu_sc[...])
        o_ref[...] = _dot(hid.astype(BF16), wd_sc[...])

    @pl.when(jnp.logical_not(used))
    def _():
        o_ref[...] = jnp.zeros(o_ref.shape, o_ref.dtype)


def _combine_kernel(dest_ref, h_ref, wgt_ref, g_ref, b_ref, ys_hbm, o_ref, ob_ref, ybuf, sem, *, tc, alpha):
    i = pl.program_id(0)

    def gather(tile, slot):
        def issue(t, carry):
            tok = tile * tc + t
            _row_copy(ys_hbm, dest_ref[2 * tok], ybuf.at[slot], t, sem.at[slot]).start()
            _row_copy(ys_hbm, dest_ref[2 * tok + 1], ybuf.at[slot], tc + t, sem.at[slot]).start()
            return carry
        lax.fori_loop(0, tc, issue, 0, unroll=8)

    @pl.when(i == 0)
    def _():
        gather(0, 0)

    @pl.when(i + 1 < pl.num_programs(0))
    def _():
        gather(i + 1, (i + 1) % 2)

    slot = i % 2
    pltpu.make_async_copy(ys_hbm.at[pl.ds(0, 2 * tc)], ybuf.at[slot], sem.at[slot]).wait()
    wgt = wgt_ref[...]
    ff = ybuf[slot, 0:tc, :] * wgt[:, 0:1] + ybuf[slot, tc:2 * tc, :] * wgt[:, 1:2]
    h2 = _layer_norm(alpha * h_ref[...] + ff, g_ref[...], b_ref[...])
    o_ref[...] = h2
    ob_ref[...] = h2.astype(BF16)


def _tiles(seq):
    big = seq >= 2048
    return dict(
        rows=512 if big else 128,
        out_rows=256 if big else 128,
        mla=512 if big else 128,
        mla_heads=8,
        idx_q=512 if big else 64,
        dsa_q=512 if big else 128,
        key=512 if big else 128,
        route=512 if big else 128,
        moe_tok=512 if big else 128,
        comb=256 if big else 128,
    )


def _rope_tables(positions, rot_dim, head_dim):
    half = rot_dim // 2
    inv = ROPE_THETA ** (-jnp.arange(0, rot_dim, 2, dtype=F32) / rot_dim)
    ang = positions.astype(F32).reshape(-1, 1) * inv
    cos, sin = jnp.cos(ang), jnp.sin(ang)
    t = cos.shape[0]
    rest = head_dim - rot_dim
    ones, zeros, zh = jnp.ones((t, rest), F32), jnp.zeros((t, rest), F32), jnp.zeros((t, half), F32)
    rep = LANES // head_dim
    c = jnp.tile(jnp.concatenate([cos, cos, ones], axis=1), (1, rep))
    sa = jnp.tile(jnp.concatenate([-sin, zh, zeros], axis=1), (1, rep))
    sb = jnp.tile(jnp.concatenate([zh, sin, zeros], axis=1), (1, rep))
    return c, sa, sb


def _layer(layer, h, h_bf, positions, tabs, lw, alpha, B, S):
    T, D = h.shape
    tl = _tiles(S)
    tm = tl["rows"]
    nrow = T // tm
    spb = S // tm
    (c_m, sa_m, sb_m), (c_d, sa_d, sb_d), (c_i, sa_i, sb_i) = tabs
    row = lambda w: pl.BlockSpec((tm, w), lambda i: (i, 0))
    head_major = lambda nh, w: pl.BlockSpec((1, nh, tm, w), lambda i: (i // spb, 0, i % spb, 0))

    log2e = math.log2(math.e)
    q_a, k_a, v_a = pl.pallas_call(
        functools.partial(_mla_prep_kernel, q_scale=log2e * MLA_QK_DIM ** -0.5),
        grid=(nrow,),
        in_specs=[row(D), _resident_layer(lw["w_mla_in"], layer), _resident((1, MLA_Q_LORA)), _resident((1, MLA_KV_LORA)),
                  _resident(lw["w_q_b"].shape), _resident(lw["w_kv_b"].shape), row(LANES), row(LANES), row(LANES)],
        out_specs=[head_major(MLA_HEADS, MLA_QK_DIM), head_major(MLA_HEADS, MLA_QK_DIM), row(2 * MLA_HEADS * MLA_V_DIM)],
        out_shape=[jax.ShapeDtypeStruct((B, MLA_HEADS, S, MLA_QK_DIM), BF16),
                   jax.ShapeDtypeStruct((B, MLA_HEADS, S, MLA_QK_DIM), BF16),
                   jax.ShapeDtypeStruct((T, 2 * MLA_HEADS * MLA_V_DIM), BF16)],
        compiler_params=_params("parallel"),
    )(h_bf, lw["w_mla_in"], lw["g_q"], lw["g_kv"], lw["w_q_b"], lw["w_kv_b"], c_m, sa_m, sb_m)

    hd = DSA_HEADS * DSA_HEAD_DIM
    q_b, k_b, v_b, q_idx, k_idx, w_idx = pl.pallas_call(
        functools.partial(_dsa_prep_kernel, w_idx_scale=(IDX_HEADS ** -0.5) * (IDX_DIM ** -0.5),
                          q_scale=log2e * DSA_HEAD_DIM ** -0.5),
        grid=(nrow,),
        in_specs=[row(D), _resident_layer(lw["w_dsa_in"], layer)] + [row(LANES)] * 6,
        out_specs=[row(hd), row(hd), row(2 * hd), head_major(IDX_HEADS, IDX_DIM), row(IDX_DIM),
                   pl.BlockSpec((1, IDX_HEADS, tm), lambda i: (i // spb, 0, i % spb))],
        out_shape=[jax.ShapeDtypeStruct((T, hd), BF16)] * 2 + [jax.ShapeDtypeStruct((T, 2 * hd), BF16)] + [
            jax.ShapeDtypeStruct((B, IDX_HEADS, S, IDX_DIM), BF16),
            jax.ShapeDtypeStruct((T, IDX_DIM), BF16),
            jax.ShapeDtypeStruct((B, IDX_HEADS, S), F32)],
        compiler_params=_params("parallel"),
    )(h_bf, lw["w_dsa_in"], c_d, sa_d, sb_d, c_i, sa_i, sb_i)

    ta = tl["mla"]
    pos_q = positions.reshape(B, S, 1)
    pos_k_mla = positions.reshape(B, S // ta, 1, ta)
    hp = tl["mla_heads"]
    o_a = pl.pallas_call(
        functools.partial(_mla_attn_kernel, tile=ta, heads=hp),
        grid=(B, MLA_HEADS // hp, S // ta),
        in_specs=[pl.BlockSpec((1, hp, ta, MLA_QK_DIM), lambda b, g, i: (b, g, i, 0)),
                  pl.BlockSpec((1, hp, S, MLA_QK_DIM), lambda b, g, i: (b, g, 0, 0), pipeline_mode=pl.Buffered(1)),
                  pl.BlockSpec((1, S, 2 * MLA_V_DIM * hp), lambda b, g, i: (b, 0, g), pipeline_mode=pl.Buffered(1)),
                  pl.BlockSpec((1, ta, 1), lambda b, g, i: (b, i, 0)),
                  pl.BlockSpec((1, S // ta, 1, ta), lambda b, g, i: (b, 0, 0, 0))],
        out_specs=pl.BlockSpec((1, ta, MLA_V_DIM * hp), lambda b, g, i: (b, i, g)),
        out_shape=jax.ShapeDtypeStruct((B, S, MLA_HEADS * MLA_V_DIM), BF16),
        scratch_shapes=[pltpu.VMEM((hp, ta, LANES), F32), pltpu.VMEM((hp, ta, 2 * MLA_V_DIM), F32)],
        compiler_params=_params("parallel", "parallel", "arbitrary"),
    )(q_a, k_a, v_a.reshape(B, S, -1), pos_q, pos_k_mla)

    tk = tl["key"]
    nkc = S // tk
    tqi = tl["idx_q"]
    top_k = min(DSA_MAX_TOPK, S // 4)
    sel_bias = pl.pallas_call(
        functools.partial(_idx_select_kernel, tq=tqi, tk=tk, nkc=nkc, top_k=top_k),
        grid=(B, S // tqi),
        in_specs=[pl.BlockSpec((1, IDX_HEADS, tqi, IDX_DIM), lambda b, i: (b, 0, i, 0)),
                  pl.BlockSpec((1, S, IDX_DIM), lambda b, i: (b, 0, 0)),
                  pl.BlockSpec((1, IDX_HEADS, tqi), lambda b, i: (b, 0, i)),
                  pl.BlockSpec((1, 1, tqi), lambda b, i: (b, 0, i)),
                  pl.BlockSpec((1, nkc, tk, 1), lambda b, i: (b, 0, 0, 0))],
        out_specs=pl.BlockSpec((1, nkc, tqi, tk), lambda b, i: (b, 0, i, 0)),
        out_shape=jax.ShapeDtypeStruct((B, nkc, S, tk), BF16),
        scratch_shapes=[pltpu.VMEM((nkc, tk, tqi), I32)],
        compiler_params=_params("parallel", "arbitrary"),
    )(q_idx, k_idx.reshape(B, S, IDX_DIM), w_idx, positions.reshape(B, 1, S), positions.reshape(B, nkc, tk, 1))

    tqd = tl["dsa_q"]
    o_b = pl.pallas_call(
        functools.partial(_dsa_attn_kernel, tq=tqd, tk=tk),
        grid=(B, S // tqd),
        in_specs=[pl.BlockSpec((1, tqd, hd), lambda b, i: (b, i, 0)),
                  pl.BlockSpec((1, S, hd), lambda b, i: (b, 0, 0), pipeline_mode=pl.Buffered(1)),
                  pl.BlockSpec((1, S, 2 * hd), lambda b, i: (b, 0, 0), pipeline_mode=pl.Buffered(1)),
                  pl.BlockSpec((1, nkc, tqd, tk), lambda b, i: (b, 0, i, 0))],
        out_specs=pl.BlockSpec((1, tqd, hd), lambda b, i: (b, i, 0)),
        out_shape=jax.ShapeDtypeStruct((B, S, hd), BF16),
        scratch_shapes=[pltpu.VMEM((DSA_HEADS, tqd, LANES), F32), pltpu.VMEM((DSA_HEADS, tqd, 2 * DSA_HEAD_DIM), F32)],
        compiler_params=_params("parallel", "arbitrary"),
    )(q_b.reshape(B, S, hd), k_b.reshape(B, S, hd), v_b.reshape(B, S, 2 * hd), sel_bias)

    y = pl.pallas_call(
        _merge_kernel,
        grid=(nrow,),
        in_specs=[row(D), row(MLA_HEADS * MLA_V_DIM), row(hd),
                  _resident_layer(lw["w_gate_a"], layer), _resident_layer(lw["w_gate_b"], layer),
                  _resident((MLA_HEADS * MLA_V_DIM, D)), _resident((hd, D))],
        out_specs=row(D),
        out_shape=jax.ShapeDtypeStruct((T, D), BF16),
        compiler_params=_params("parallel"),
    )(h_bf, o_a.reshape(T, -1), o_b.reshape(T, hd), lw["w_gate_a"], lw["w_gate_b"], lw["w_o_a"], lw["w_o_b"])

    to = tl["out_rows"]
    orow = lambda w: pl.BlockSpec((to, w), lambda i: (i, 0))
    h1, h1_packed, logits = pl.pallas_call(
        functools.partial(_out_ln_kernel, alpha=alpha),
        grid=(T // to,),
        in_specs=[orow(D), orow(D), _resident((D, D)), _resident((1, D)), _resident((1, D)),
                  _resident((D, LANES)), _resident((D, LANES)), _resident((1, LANES))],
        out_specs=[orow(D), orow(D // 2), orow(LANES)],
        out_shape=[jax.ShapeDtypeStruct((T, D), F32), jax.ShapeDtypeStruct((T, D // 2), I32),
                   jax.ShapeDtypeStruct((T, LANES), F32)],
        compiler_params=_params("parallel"),
    )(y, h, lw["w_out"], lw["ln1_g"], lw["ln1_b"], lw["w_route_hi"], lw["w_route_lo"], lw["b_route"])

    tr = tl["route"]
    meta, wgt, cnt = pl.pallas_call(
        functools.partial(_route_kernel, tr=tr),
        grid=(T // tr,),
        in_specs=[pl.BlockSpec((tr, LANES), lambda i: (i, 0))],
        out_specs=[pl.BlockSpec((tr, LANES), lambda i: (i, 0)), pl.BlockSpec((tr, LANES), lambda i: (i, 0)),
                   pl.BlockSpec((8, LANES), lambda i: (0, 0))],
        out_shape=[jax.ShapeDtypeStruct((T, LANES), I32), jax.ShapeDtypeStruct((T, LANES), F32),
                   jax.ShapeDtypeStruct((8, LANES), F32)],
        scratch_shapes=[pltpu.VMEM((1, LANES), F32)],
        compiler_params=_params("arbitrary"),
    )(logits)

    rb = MOE_ROW_BLOCK
    counts = cnt[0, N_GROUPS:N_GROUPS + N_EXPERTS].astype(I32)
    padded = ((counts + rb - 1) // rb) * rb
    pends = jnp.cumsum(padded)
    pstarts = pends - padded
    n_blocks = -(-(2 * T) // rb) + N_EXPERTS
    P = n_blocks * rb
    blk = jnp.arange(n_blocks, dtype=I32)
    block_e = jnp.minimum(jnp.sum((pends[None, :] <= (blk * rb)[:, None]).astype(I32), axis=1), N_EXPERTS - 1)
    n_used = (pends[-1] // rb).astype(I32).reshape(1)
    ps_lanes = jnp.zeros((1, LANES), I32).at[0, :N_EXPERTS].set(pstarts.astype(I32))
    dest = pl.pallas_call(
        _dest_kernel,
        grid=(T // tr,),
        in_specs=[pl.BlockSpec((tr, LANES), lambda i: (i, 0)), pl.BlockSpec((1, LANES), lambda i: (0, 0))],
        out_specs=pl.BlockSpec((tr, LANES), lambda i: (i, 0)),
        out_shape=jax.ShapeDtypeStruct((T, LANES), I32),
        compiler_params=_params("parallel"),
    )(meta, ps_lanes)[:, 0:2].reshape(-1)

    last_of_expert = jnp.concatenate([block_e[1:] != block_e[:-1], jnp.ones((1,), bool)])
    needs_zero = jnp.logical_or(blk >= n_used[0], jnp.logical_or(last_of_expert, blk == n_used[0] - 1)).astype(I32)
    n_zero = jnp.sum(needs_zero).astype(I32).reshape(1)
    tt = tl["moe_tok"]
    xs = pl.pallas_call(
        functools.partial(_dispatch_kernel, tt=tt),
        grid_spec=pltpu.PrefetchScalarGridSpec(
            num_scalar_prefetch=3, grid=(T // tt,),
            in_specs=[pl.BlockSpec((tt, D // 2), lambda i, *_: (i, 0))],
            out_specs=pl.BlockSpec(memory_space=pl.ANY),
            scratch_shapes=[pltpu.VMEM((rb, D // 2), I32), pltpu.SemaphoreType.DMA(()), pltpu.SemaphoreType.DMA(())]),
        out_shape=jax.ShapeDtypeStruct((P, D // 2), I32),
        compiler_params=_params("arbitrary"),
    )(dest, needs_zero, n_zero, h1_packed)

    F = D_EXPERT
    grp = jnp.cumsum(jnp.concatenate([jnp.zeros((1,), I32), (block_e[1:] != block_e[:-1]).astype(I32)]))
    ids = jnp.where(counts > 0, jnp.arange(N_EXPERTS, dtype=I32), N_EXPERTS)
    at_or_after = lax.cummin(ids, axis=0, reverse=True)
    after = jnp.concatenate([at_or_after[1:], jnp.full((1,), N_EXPERTS, I32)])
    nxt_of = jnp.where(after < N_EXPERTS, after, -1)
    nxt2_of = jnp.where(nxt_of >= 0, nxt_of[jnp.maximum(nxt_of, 0)], -1)
    nxt = nxt_of[block_e].astype(I32)
    nxt2 = nxt2_of[block_e].astype(I32)
    any_spec = pl.BlockSpec(memory_space=pl.ANY)
    ys = pl.pallas_call(
        functools.partial(_experts_kernel, layer=layer),
        grid_spec=pltpu.PrefetchScalarGridSpec(
            num_scalar_prefetch=5, grid=(n_blocks,),
            in_specs=[pl.BlockSpec((rb, D // 2), lambda i, be, gr, nx, nx2, nu: (jnp.minimum(i, nu[0] - 1), 0)),
                      any_spec, any_spec, any_spec],
            out_specs=pl.BlockSpec((rb, D), lambda i, *_: (i, 0)),
            scratch_shapes=[pltpu.VMEM((2, D, F), F32), pltpu.VMEM((2, D, F), F32), pltpu.VMEM((2, F, D), F32),
                            pltpu.VMEM((D, F), BF16), pltpu.VMEM((D, F), BF16), pltpu.VMEM((F, D), BF16),
                            pltpu.SemaphoreType.DMA((2, 3))]),
        out_shape=jax.ShapeDtypeStruct((P, D), F32),
        compiler_params=_params("arbitrary"),
    )(block_e, grp.astype(I32), nxt, nxt2, n_used, xs, lw["w_e_gate"], lw["w_e_up"], lw["w_e_down"])

    tc = tl["comb"]
    h2, h2_bf = pl.pallas_call(
        functools.partial(_combine_kernel, tc=tc, alpha=alpha),
        grid_spec=pltpu.PrefetchScalarGridSpec(
            num_scalar_prefetch=1, grid=(T // tc,),
            in_specs=[pl.BlockSpec((tc, D), lambda i, d: (i, 0)),
                      pl.BlockSpec((tc, LANES), lambda i, d: (i, 0)),
                      pl.BlockSpec((1, D), lambda i, d: (0, 0)),
                      pl.BlockSpec((1, D), lambda i, d: (0, 0)),
                      pl.BlockSpec(memory_space=pl.ANY)],
            out_specs=[pl.BlockSpec((tc, D), lambda i, d: (i, 0)), pl.BlockSpec((tc, D), lambda i, d: (i, 0))],
            scratch_shapes=[pltpu.VMEM((2, 2 * tc, D), F32), pltpu.SemaphoreType.DMA((2,))]),
        out_shape=[jax.ShapeDtypeStruct((T, D), F32), jax.ShapeDtypeStruct((T, D), BF16)],
        compiler_params=_params("arbitrary"),
    )(dest, h1, wgt, lw["ln2_g"], lw["ln2_b"], ys)
    return h2, h2_bf


def _layer_weights(l, regrouped, g_q_lora, w_q_b, g_kv_lora, w_kv_b, w_o_a, w_o_b, w_out, ln1_g, ln1_b,
                   w_group, b_group, w_router, b_router, w_e_gate, w_e_up, w_e_down, ln2_g, ln2_b):
    w_mla_in, w_dsa_in, w_gate_a, w_gate_b = regrouped
    D = w_gate_a.shape[1]
    wq = w_q_b[l].reshape(MLA_Q_LORA, MLA_HEADS, MLA_QK_DIM)
    wq = jnp.concatenate([wq[:, :, :MLA_NOPE_DIM].reshape(MLA_Q_LORA, -1),
                          wq[:, :, MLA_NOPE_DIM:].reshape(MLA_Q_LORA, -1)], axis=1).astype(BF16)
    w_route = jnp.concatenate([w_group[l], w_router[l],
                               jnp.zeros((D, LANES - N_GROUPS - N_EXPERTS), F32)], axis=1)
    b_route = jnp.concatenate([b_group[l], b_router[l],
                               jnp.zeros((LANES - N_GROUPS - N_EXPERTS,), F32)]).reshape(1, LANES)
    return dict(
        w_mla_in=w_mla_in, w_dsa_in=w_dsa_in, w_gate_a=w_gate_a, w_gate_b=w_gate_b,
        g_q=g_q_lora[l].reshape(1, -1), g_kv=g_kv_lora[l].reshape(1, -1),
        w_q_b=wq, w_kv_b=w_kv_b[l].astype(BF16),
        w_o_a=w_o_a[l].astype(BF16), w_o_b=w_o_b[l].astype(BF16), w_out=w_out[l].astype(BF16),
        ln1_g=ln1_g[l].reshape(1, -1), ln1_b=ln1_b[l].reshape(1, -1),
        w_route_hi=w_route.astype(BF16), w_route_lo=(w_route - w_route.astype(BF16).astype(F32)).astype(BF16),
        b_route=b_route,
        w_e_gate=w_e_gate, w_e_up=w_e_up, w_e_down=w_e_down,
        ln2_g=ln2_g[l].reshape(1, -1), ln2_b=ln2_b[l].reshape(1, -1),
    )


def kernel(x, positions, w_in, g_q_lora, w_q_b, g_kv_lora, w_kv_b, w_o_a, w_o_b, w_out, ln1_g, ln1_b,
           w_group, b_group, w_router, b_router, w_e_gate, w_e_up, w_e_down, ln2_g, ln2_b):
    B, S, D = x.shape
    depth = w_in.shape[0]
    alpha = (2 * depth) ** 0.25
    tabs = (_rope_tables(positions, MLA_ROPE_DIM, MLA_ROPE_DIM),
            _rope_tables(positions, DSA_ROPE_DIM, DSA_HEAD_DIM),
            _rope_tables(positions, IDX_ROPE_DIM, IDX_DIM))
    h = x.reshape(B * S, D)
    h_bf = h.astype(BF16)
    regrouped = _regroup_w_in(w_in)
    for l in range(depth):
        lw = _layer_weights(l, regrouped, g_q_lora, w_q_b, g_kv_lora, w_kv_b, w_o_a, w_o_b, w_out, ln1_g, ln1_b,
                            w_group, b_group, w_router, b_router, w_e_gate, w_e_up, w_e_down, ln2_g, ln2_b)
        h, h_bf = _layer(l, h, h_bf, positions, tabs, lw, alpha, B, S)
    return h.reshape(B, S, D)
```

```python
import functools
import math

import jax
import jax.numpy as jnp
from jax import lax
from jax.experimental import pallas as pl
from jax.experimental.pallas import tpu as pltpu

F32 = jnp.float32
BF16 = jnp.bfloat16
I32 = jnp.int32

MLA_HEADS = 8
MLA_Q_LORA = 512
MLA_KV_LORA = 512
MLA_NOPE_DIM = 128
MLA_ROPE_DIM = 64
MLA_V_DIM = 128
MLA_QK_DIM = MLA_NOPE_DIM + MLA_ROPE_DIM
DSA_HEADS = 8
DSA_HEAD_DIM = 128
DSA_ROPE_DIM = DSA_HEAD_DIM // 4
IDX_HEADS = 16
IDX_DIM = 64
IDX_ROPE_DIM = IDX_DIM // 4
DSA_MAX_TOPK = 256
ROPE_THETA = 500000.0
N_GROUPS = 8
EXPERTS_PER_GROUP = 8
N_EXPERTS = N_GROUPS * EXPERTS_PER_GROUP
D_EXPERT = 512
MOE_ROW_BLOCK = 128
LN_EPS = 1e-5
RMS_EPS = 1e-6

LANES = 128
NEG_BIG = -1e30
INT_MIN = -(2 ** 31)
VMEM_LIMIT = 56 * 1024 * 1024
WEIGHT_DMA_CHUNKS = 4


def _params(*sem):
    return pltpu.CompilerParams(dimension_semantics=sem, vmem_limit_bytes=VMEM_LIMIT)


def _resident(shape):
    nd = len(shape)
    return pl.BlockSpec(shape, lambda *_: (0,) * nd, pipeline_mode=pl.Buffered(1))


def _resident_layer(stacked, layer):
    shape = stacked.shape[1:]
    nd = len(shape)
    return pl.BlockSpec((None,) + shape, lambda *_: (layer,) + (0,) * nd, pipeline_mode=pl.Buffered(1))


def _dot(a, b):
    return jnp.dot(a, b, preferred_element_type=F32)


def _dot_nt(a, b):
    return lax.dot_general(a, b, (((1,), (1,)), ((), ())), preferred_element_type=F32)


def _rope128(x, c, sa, sb, half):
    return x * c + pltpu.roll(x, LANES - half, 1) * sa + pltpu.roll(x, half, 1) * sb


def _rms(x, g):
    return x * lax.rsqrt(jnp.mean(x * x, axis=-1, keepdims=True) + RMS_EPS) * g


def _layer_norm(x, g, b):
    mu = jnp.mean(x, axis=-1, keepdims=True)
    xc = x - mu
    var = jnp.mean(xc * xc, axis=-1, keepdims=True)
    return xc * lax.rsqrt(var + LN_EPS) * g + b


_O_KPE = MLA_Q_LORA + MLA_KV_LORA
_O_DSA = _O_KPE + MLA_ROPE_DIM
_O_KIDX = _O_DSA + 3 * DSA_HEADS * DSA_HEAD_DIM + IDX_HEADS * IDX_DIM
_O_WIDX = _O_KIDX + IDX_DIM
_O_GATE = _O_WIDX + IDX_HEADS
_W_MLA_IN = _O_KPE + 2 * MLA_ROPE_DIM
_W_DSA_IN = (_O_KIDX - _O_DSA) + 2 * IDX_DIM + LANES


def _regroup_w_in_kernel(w_ref, mla_ref, dsa_ref, ga_ref, gb_ref, *, d_model):
    mla_ref[:, 0:_O_DSA] = w_ref[:, 0:_O_DSA].astype(BF16)
    mla_ref[:, _O_DSA:_W_MLA_IN] = w_ref[:, _O_KPE:_O_DSA].astype(BF16)
    main = _O_KIDX - _O_DSA
    dsa_ref[:, 0:main] = w_ref[:, _O_DSA:_O_KIDX].astype(BF16)
    kidx = w_ref[:, _O_KIDX:_O_WIDX].astype(BF16)
    dsa_ref[:, main:main + IDX_DIM] = kidx
    dsa_ref[:, main + IDX_DIM:main + 2 * IDX_DIM] = kidx
    dsa_ref[:, main + 2 * IDX_DIM:main + 2 * IDX_DIM + IDX_HEADS] = w_ref[:, _O_WIDX:_O_GATE].astype(BF16)
    dsa_ref[:, main + 2 * IDX_DIM + IDX_HEADS:] = jnp.zeros((w_ref.shape[0], LANES - IDX_HEADS), BF16)
    ga_ref[...] = w_ref[:, _O_GATE:_O_GATE + d_model].astype(BF16)
    gb_ref[...] = w_ref[:, _O_GATE + d_model:_O_GATE + 2 * d_model].astype(BF16)


def _regroup_w_in(w_in):
    depth, d_model, width = w_in.shape
    tr = 256
    out_w = (_W_MLA_IN, _W_DSA_IN, d_model, d_model)
    return pl.pallas_call(
        functools.partial(_regroup_w_in_kernel, d_model=d_model),
        grid=(depth, d_model // tr),
        in_specs=[pl.BlockSpec((tr, width), lambda l, i: (l * (d_model // tr) + i, 0))],
        out_specs=[pl.BlockSpec((None, tr, w), lambda l, i: (l, i, 0)) for w in out_w],
        out_shape=[jax.ShapeDtypeStruct((depth, d_model, w), BF16) for w in out_w],
        compiler_params=_params("parallel", "parallel"),
    )(w_in.astype(BF16).reshape(depth * d_model, width))


def _mla_prep_kernel(h_ref, win_ref, gq_ref, gkv_ref, wqb_ref, wkvb_ref, c_ref, sa_ref, sb_ref,
                     q_ref, k_ref, v_ref, *, q_scale):
    h = h_ref[...]
    p = _dot(h, win_ref[...])
    qn = _rms(p[:, :MLA_Q_LORA], gq_ref[...]).astype(BF16)
    kvn = _rms(p[:, MLA_Q_LORA:MLA_Q_LORA + MLA_KV_LORA], gkv_ref[...]).astype(BF16)
    q = _dot(qn, wqb_ref[...]) * q_scale
    kv = _dot(kvn, wkvb_ref[...])
    c, sa, sb = c_ref[...], sa_ref[...], sb_ref[...]
    half = MLA_ROPE_DIM // 2
    kpe = _rope128(p[:, MLA_Q_LORA + MLA_KV_LORA:], c, sa, sb, half)[:, :MLA_ROPE_DIM].astype(BF16)
    pe0 = MLA_HEADS * MLA_NOPE_DIM
    for hh in range(MLA_HEADS):
        q_ref[0, hh, :, 0:MLA_NOPE_DIM] = q[:, 128 * hh:128 * hh + 128].astype(BF16)
        k_ref[0, hh, :, 0:MLA_NOPE_DIM] = kv[:, 256 * hh:256 * hh + 128].astype(BF16)
        k_ref[0, hh, :, MLA_NOPE_DIM:MLA_QK_DIM] = kpe
        v_ref[:, 256 * hh:256 * hh + 128] = kv[:, 256 * hh + 128:256 * hh + 256].astype(BF16)
        v_ref[:, 256 * hh + 128:256 * hh + 256] = jnp.ones((kv.shape[0], LANES), BF16)
    for s in range(MLA_HEADS // 2):
        slab = _rope128(q[:, pe0 + 128 * s:pe0 + 128 * s + 128], c, sa, sb, half).astype(BF16)
        q_ref[0, 2 * s, :, MLA_NOPE_DIM:MLA_QK_DIM] = slab[:, :MLA_ROPE_DIM]
        q_ref[0, 2 * s + 1, :, MLA_NOPE_DIM:MLA_QK_DIM] = slab[:, MLA_ROPE_DIM:]


def _dsa_prep_kernel(h_ref, w_ref, cd_ref, sad_ref, sbd_ref, ci_ref, sai_ref, sbi_ref,
                     qb_ref, kb_ref, vb_ref, qi_ref, ki_ref, wi_ref, *, w_idx_scale, q_scale):
    h = h_ref[...]
    hd = DSA_HEADS * DSA_HEAD_DIM
    cd, sad, sbd = cd_ref[...], sad_ref[...], sbd_ref[...]
    ci, sai, sbi = ci_ref[...], sai_ref[...], sbi_ref[...]
    hd_half, hi_half = DSA_ROPE_DIM // 2, IDX_ROPE_DIM // 2
    q = _dot(h, w_ref[:, 0:hd]) * q_scale
    for hh in range(DSA_HEADS):
        qb_ref[:, 128 * hh:128 * hh + 128] = _rope128(q[:, 128 * hh:128 * hh + 128], cd, sad, sbd, hd_half).astype(BF16)
    k = _dot(h, w_ref[:, hd:2 * hd])
    for hh in range(DSA_HEADS):
        kb_ref[:, 128 * hh:128 * hh + 128] = _rope128(k[:, 128 * hh:128 * hh + 128], cd, sad, sbd, hd_half).astype(BF16)
    v = _dot(h, w_ref[:, 2 * hd:3 * hd]).astype(BF16)
    for hh in range(DSA_HEADS):
        vb_ref[:, 256 * hh:256 * hh + 128] = v[:, 128 * hh:128 * hh + 128]
        vb_ref[:, 256 * hh + 128:256 * hh + 256] = jnp.ones((v.shape[0], LANES), BF16)
    qi = _dot(h, w_ref[:, 3 * hd:3 * hd + IDX_HEADS * IDX_DIM])
    for s in range(IDX_HEADS // 2):
        slab = _rope128(qi[:, 128 * s:128 * s + 128], ci, sai, sbi, hi_half).astype(BF16)
        qi_ref[0, 2 * s] = slab[:, :IDX_DIM]
        qi_ref[0, 2 * s + 1] = slab[:, IDX_DIM:]
    o = 3 * hd + IDX_HEADS * IDX_DIM
    last = _dot(h, w_ref[:, o:o + 2 * LANES])
    ki_ref[...] = _rope128(last[:, :LANES], ci, sai, sbi, hi_half)[:, :IDX_DIM].astype(BF16)
    wi_ref[0] = (last[:, LANES:] * w_idx_scale).T[:IDX_HEADS, :]


def _flash_update(s, v_ones, m_ref, acc_ref):
    tk = s.shape[1]
    m_prev = m_ref[...]
    m_next = jnp.maximum(m_prev, jnp.max(s, axis=1, keepdims=True))
    p = jnp.exp2(s - jnp.tile(m_next, (1, tk // LANES)))
    alpha = jnp.exp2(m_prev - m_next)
    acc_ref[...] = acc_ref[...] * jnp.tile(alpha, (1, 2)) + _dot(p.astype(BF16), v_ones)
    m_ref[...] = m_next


def _mla_attn_kernel(q_ref, k_ref, v_ref, pq_ref, pk_ref, o_ref, m_sc, acc_sc, *, tile, heads):
    qi = pl.program_id(2)
    m_sc[...] = jnp.full(m_sc.shape, -jnp.inf, F32)
    acc_sc[...] = jnp.zeros(acc_sc.shape, F32)

    def chunk(c, masked):
        ks = pl.ds(pl.multiple_of(c * tile, tile), tile)
        if masked:
            visible = pk_ref[0, c] <= pq_ref[0]
        for hh in range(heads):
            s = _dot_nt(q_ref[0, hh], k_ref[0, hh, ks, :])
            if masked:
                s = jnp.where(visible, s, NEG_BIG)
            _flash_update(s, v_ref[0, ks, 2 * MLA_V_DIM * hh:2 * MLA_V_DIM * (hh + 1)], m_sc.at[hh], acc_sc.at[hh])

    def body(c, carry):
        chunk(c, False)
        return carry

    lax.fori_loop(0, qi, body, 0)
    chunk(qi, True)
    for hh in range(heads):
        o_ref[0, :, MLA_V_DIM * hh:MLA_V_DIM * (hh + 1)] = (
            acc_sc[hh, :, :MLA_V_DIM] / acc_sc[hh, :, MLA_V_DIM:]).astype(o_ref.dtype)


def _idx_select_kernel(q_ref, k_ref, wt_ref, pq_ref, pk_ref, o_ref, key_sc, *, tq, tk, nkc, top_k):
    qi = pl.program_id(1)
    nk = (qi * tq + tq + tk - 1) // tk
    wt = wt_ref[0]
    pq = pq_ref[0]

    def score_chunk(c, carry):
        kc = k_ref[0, pl.ds(pl.multiple_of(c * tk, tk), tk), :]
        acc = jnp.zeros((tk, tq), F32)
        for hh in range(IDX_HEADS):
            acc = acc + wt[hh:hh + 1, :] * jnp.maximum(_dot_nt(kc, q_ref[0, hh]), 0.0)
        bits = pltpu.bitcast(acc, I32)
        key = jnp.where(bits < 0, bits ^ jnp.int32(0x7FFFFFFF), bits)
        key = jnp.where(acc == 0.0, 0, key)
        key_sc[c] = jnp.where(pk_ref[0, c] <= pq, key, INT_MIN)
        return carry

    lax.fori_loop(0, nk, score_chunk, 0)

    def count(pred):
        def body(c, part):
            m = pred(key_sc[c], c).astype(I32)
            return part + jnp.sum(m.reshape(tk // 8, 8, tq), axis=0)
        part = lax.fori_loop(0, nk, body, jnp.zeros((8, tq), I32))
        return jnp.sum(part, axis=0, keepdims=True)

    def bit_step(i, carry):
        base, at_base = carry
        cand = base ^ (jnp.int32(1) << (31 - i))
        cnt = count(lambda key, c: key >= cand)
        take = cnt >= top_k
        return jnp.where(take, cand, base), jnp.where(take, cnt, at_base)

    def refine(lo, hi, carry):
        settled = jnp.min((carry[1] <= top_k).astype(I32)) > 0
        return lax.cond(settled, lambda c: c, lambda c: lax.fori_loop(lo, hi, bit_step, c), carry)

    carry = lax.fori_loop(0, 24, bit_step, (jnp.full((1, tq), INT_MIN, I32), jnp.full((1, tq), 2 ** 30, I32)))
    carry = refine(24, 28, carry)
    thr, at_thr = refine(28, 32, carry)
    tie = jnp.logical_and(at_thr > top_k, thr != INT_MIN)

    def key_index(c):
        return c * tk + lax.broadcasted_iota(I32, (tk, tq), 0)

    def tie_search():
        n_eq = count(lambda key, c: key == thr)
        need = top_k - (at_thr - n_eq)

        def step(i, p):
            cand = p | (jnp.int32(1) << (14 - i))
            cnt = count(lambda key, c: jnp.logical_and(key == thr, key_index(c) < cand))
            return jnp.where(cnt <= need, cand, p)
        return lax.fori_loop(0, 15, step, jnp.zeros((1, tq), I32))

    p_cut = lax.cond(jnp.max(tie.astype(I32)) > 0, tie_search,
                     lambda: jnp.full((1, tq), 2 ** 30, I32))

    def out_chunk(c, carry):
        key = key_sc[c]
        sel = jnp.logical_or(key > thr, jnp.logical_and(key == thr, key_index(c) < p_cut))
        sel = jnp.logical_and(sel, key != INT_MIN)
        o_ref[0, c] = jnp.where(sel, 0.0, NEG_BIG).T.astype(o_ref.dtype)
        return carry

    lax.fori_loop(0, nk, out_chunk, 0)

    def fill_chunk(c, carry):
        o_ref[0, c] = jnp.full((tq, tk), NEG_BIG, o_ref.dtype)
        return carry

    lax.fori_loop(nk, nkc, fill_chunk, 0)


def _dsa_attn_kernel(q_ref, k_ref, v_ref, b_ref, o_ref, m_sc, acc_sc, *, tq, tk):
    qi = pl.program_id(1)
    nk = (qi * tq + tq + tk - 1) // tk
    m_sc[...] = jnp.full(m_sc.shape, NEG_BIG, F32)
    acc_sc[...] = jnp.zeros(acc_sc.shape, F32)

    def chunk(c, carry):
        bias = b_ref[0, c].astype(F32)
        ks = pl.ds(pl.multiple_of(c * tk, tk), tk)
        for hh in range(DSA_HEADS):
            hs = slice(DSA_HEAD_DIM * hh, DSA_HEAD_DIM * (hh + 1))
            vs = slice(2 * DSA_HEAD_DIM * hh, 2 * DSA_HEAD_DIM * (hh + 1))
            s = _dot_nt(q_ref[0, :, hs], k_ref[0, ks, hs]) + bias
            _flash_update(s, v_ref[0, ks, vs], m_sc.at[hh], acc_sc.at[hh])
        return carry

    lax.fori_loop(0, nk, chunk, 0)
    for hh in range(DSA_HEADS):
        hs = slice(DSA_HEAD_DIM * hh, DSA_HEAD_DIM * (hh + 1))
        o_ref[0, :, hs] = (acc_sc[hh, :, :DSA_HEAD_DIM] / acc_sc[hh, :, DSA_HEAD_DIM:]).astype(o_ref.dtype)


def _merge_kernel(h_ref, oa_ref, ob_ref, wga_ref, wgb_ref, woa_ref, wob_ref, y_ref):
    h = h_ref[...]
    ya = jax.nn.sigmoid(_dot(h, wga_ref[...])) * _dot(oa_ref[...], woa_ref[...])
    yb = jax.nn.sigmoid(_dot(h, wgb_ref[...])) * _dot(ob_ref[...], wob_ref[...])
    y_ref[...] = (ya + yb).astype(y_ref.dtype)


def _out_ln_kernel(y_ref, h_ref, wout_ref, g_ref, b_ref, wrh_ref, wrl_ref, br_ref, o_ref, op_ref, lg_ref, *, alpha):
    sub = 128
    half = o_ref.shape[1] // 2
    for r in range(y_ref.shape[0] // sub):
        rows = slice(r * sub, (r + 1) * sub)
        mix = _dot(y_ref[rows, :], wout_ref[...])
        h1 = _layer_norm(alpha * h_ref[rows, :] + mix, g_ref[...], b_ref[...])
        o_ref[rows, :] = h1
        lo = pltpu.bitcast(h1[:, :half].astype(BF16).astype(F32), I32)
        hi = pltpu.bitcast(h1[:, half:].astype(BF16).astype(F32), I32)
        op_ref[rows, :] = lax.shift_right_logical(lo, 16) | hi
        h_hi = h1.astype(BF16)
        h_lo = (h1 - h_hi.astype(F32)).astype(BF16)
        lg_ref[rows, :] = (_dot(h_hi, wrh_ref[...]) + (_dot(h_hi, wrl_ref[...]) + _dot(h_lo, wrh_ref[...]))
                           + br_ref[...])


def _route_kernel(lg_ref, meta_ref, wgt_ref, cnt_ref, carry_sc, *, tr):
    @pl.when(pl.program_id(0) == 0)
    def _():
        carry_sc[...] = jnp.zeros(carry_sc.shape, F32)

    lg = lg_ref[...]
    lane = lax.broadcasted_iota(I32, (tr, LANES), 1)
    gl = jnp.where(lane < N_GROUPS, lg, -jnp.inf)
    gmax = jnp.max(gl, axis=1, keepdims=True)
    g_idx = jnp.min(jnp.where(gl == gmax, lane, LANES), axis=1, keepdims=True)
    g_p = 1.0 / jnp.sum(jnp.exp(gl - gmax), axis=1, keepdims=True)
    lo = N_GROUPS + EXPERTS_PER_GROUP * g_idx
    in_group = jnp.logical_and(lane >= lo, lane < lo + EXPERTS_PER_GROUP)
    el = jnp.where(in_group, lg, -jnp.inf)
    emax = jnp.max(el, axis=1, keepdims=True)
    e1 = jnp.min(jnp.where(el == emax, lane, LANES), axis=1, keepdims=True)
    den = jnp.sum(jnp.exp(el - emax), axis=1, keepdims=True)
    el2 = jnp.where(lane == e1, -jnp.inf, el)
    emax2 = jnp.max(el2, axis=1, keepdims=True)
    e2 = jnp.min(jnp.where(el2 == emax2, lane, LANES), axis=1, keepdims=True)
    p1 = 1.0 / den
    p2 = jnp.exp(emax2 - emax) / den
    w1 = g_p * (p1 / (p1 + p2))
    w2 = g_p * (p2 / (p1 + p2))

    is1 = lane == e1
    is2 = lane == e2
    onehot = jnp.logical_or(is1, is2).astype(BF16)
    r = lax.broadcasted_iota(I32, (tr, tr), 0)
    cidx = lax.broadcasted_iota(I32, (tr, tr), 1)
    lower = (cidx < r).astype(BF16)
    before = _dot(lower, onehot) + carry_sc[...]
    rank1 = jnp.sum(jnp.where(is1, before, 0.0), axis=1, keepdims=True)
    rank2 = jnp.sum(jnp.where(is2, before, 0.0), axis=1, keepdims=True)
    carry_sc[...] = carry_sc[...] + jnp.sum(onehot.astype(F32), axis=0, keepdims=True)

    meta = jnp.where(lane == 0, e1 - N_GROUPS,
                     jnp.where(lane == 1, e2 - N_GROUPS,
                               jnp.where(lane == 2, rank1.astype(I32),
                                         jnp.where(lane == 3, rank2.astype(I32), 0))))
    meta_ref[...] = meta
    wgt_ref[...] = jnp.where(lane == 0, w1, jnp.where(lane == 1, w2, 0.0))
    cnt_ref[...] = jnp.broadcast_to(carry_sc[...], cnt_ref.shape)


def _row_copy(src, src_row, dst, dst_row, sem):
    return pltpu.make_async_copy(src.at[pl.ds(src_row, 1)], dst.at[pl.ds(dst_row, 1)], sem)


def _dest_kernel(meta_ref, ps_ref, o_ref):
    meta = meta_ref[...]
    ps = ps_ref[...]
    lane = lax.broadcasted_iota(I32, meta.shape, 1)
    d1 = jnp.sum(jnp.where(lane == meta[:, 0:1], ps, 0), axis=1, keepdims=True) + meta[:, 2:3]
    d2 = jnp.sum(jnp.where(lane == meta[:, 1:2], ps, 0), axis=1, keepdims=True) + meta[:, 3:4]
    o_ref[...] = jnp.where(lane == 0, d1, jnp.where(lane == 1, d2, 0))


def _dispatch_kernel(dest_ref, zblk_ref, nz_ref, h_ref, xs_out, zero_sc, sem, zsem, *, tt):
    base = pl.program_id(0) * tt
    rb = zero_sc.shape[0]

    @pl.when(pl.program_id(0) == 0)
    def _():
        zero_sc[...] = jnp.zeros(zero_sc.shape, zero_sc.dtype)

        def fill(j, carry):
            @pl.when(zblk_ref[j] != 0)
            def _():
                pltpu.make_async_copy(zero_sc, xs_out.at[pl.ds(pl.multiple_of(j * rb, rb), rb)], zsem).start()
            return carry

        def drain(j, carry):
            pltpu.make_async_copy(zero_sc, xs_out.at[pl.ds(0, rb)], zsem).wait()
            return carry

        lax.fori_loop(0, zblk_ref.shape[0], fill, 0)
        lax.fori_loop(0, nz_ref[0], drain, 0)

    def issue(t, carry):
        tok = base + t
        _row_copy(h_ref, t, xs_out, dest_ref[2 * tok], sem).start()
        _row_copy(h_ref, t, xs_out, dest_ref[2 * tok + 1], sem).start()
        return carry

    lax.fori_loop(0, tt, issue, 0, unroll=8)
    for _ in range(2):
        pltpu.make_async_copy(h_ref, xs_out.at[pl.ds(0, tt)], sem).wait()


def _experts_kernel(be_ref, grp_ref, nxt_ref, nxt2_ref, nu_ref, xs_ref, wg_hbm, wu_hbm, wd_hbm, o_ref,
                    wg_buf, wu_buf, wd_buf, wg_sc, wu_sc, wd_sc, sem, *, layer):
    i = pl.program_id(0)
    e = be_ref[i]
    used = i < nu_ref[0]
    first = jnp.logical_and(used, jnp.logical_or(i == 0, e != be_ref[jnp.maximum(i - 1, 0)]))
    slot = grp_ref[i] % 2

    def copies(expert, s):
        out = []
        for j, (hbm, buf) in enumerate(((wg_hbm, wg_buf), (wu_hbm, wu_buf), (wd_hbm, wd_buf))):
            rows = buf.shape[1] // WEIGHT_DMA_CHUNKS
            for c in range(WEIGHT_DMA_CHUNKS):
                rs = pl.ds(c * rows, rows)
                out.append(pltpu.make_async_copy(hbm.at[layer, expert, rs], buf.at[s, rs], sem.at[s, j]))
        return out

    @pl.when(i == 0)
    def _():
        for cp in copies(e, slot):
            cp.start()

        @pl.when(nxt_ref[0] >= 0)
        def _():
            for cp in copies(nxt_ref[0], 1 - slot):
                cp.start()

    @pl.when(first)
    def _():
        for cp in copies(e, slot):
            cp.wait()
        wg_sc[...] = wg_buf[slot].astype(BF16)
        wu_sc[...] = wu_buf[slot].astype(BF16)
        wd_sc[...] = wd_buf[slot].astype(BF16)
        nxt2 = nxt2_ref[i]

        @pl.when(nxt2 >= 0)
        def _():
            for cp in copies(nxt2, slot):
                cp.start()

    @pl.when(used)
    def _():
        xp = xs_ref[...]
        x = jnp.concatenate([pltpu.bitcast(xp << 16, F32).astype(BF16),
                             pltpu.bitcast(xp & jnp.int32(-65536), F32).astype(BF16)], axis=1)
        hid = jax.nn.silu(_dot(x, wg_sc[...])) * _dot(x, wu_sc[...])
        o_ref[...] = _dot(hid.astype(BF16), wd_sc[...])

    @pl.when(jnp.logical_not(used))
    def _():
        o_ref[...] = jnp.zeros(o_ref.shape, o_ref.dtype)


def _combine_kernel(dest_ref, h_ref, wgt_ref, g_ref, b_ref, ys_hbm, o_ref, ob_ref, ybuf, sem, *, tc, alpha):
    i = pl.program_id(0)

    def gather(tile, slot):
        def issue(t, carry):
            tok = tile * tc + t
            _row_copy(ys_hbm, dest_ref[2 * tok], ybuf.at[slot], t, sem.at[slot]).start()
            _row_copy(ys_hbm, dest_ref[2 * tok + 1], ybuf.at[slot], tc + t, sem.at[slot]).start()
            return carry
        lax.fori_loop(0, tc, issue, 0, unroll=8)

    @pl.when(i == 0)
    def _():
        gather(0, 0)

    @pl.when(i + 1 < pl.num_programs(0))
    def _():
        gather(i + 1, (i + 1) % 2)

    slot = i % 2
    pltpu.make_async_copy(ys_hbm.at[pl.ds(0, 2 * tc)], ybuf.at[slot], sem.at[slot]).wait()
    wgt = wgt_ref[...]
    ff = ybuf[slot, 0:tc, :] * wgt[:, 0:1] + ybuf[slot, tc:2 * tc, :] * wgt[:, 1:2]
    h2 = _layer_norm(alpha * h_ref[...] + ff, g_ref[...], b_ref[...])
    o_ref[...] = h2
    ob_ref[...] = h2.astype(BF16)


def _tiles(seq):
    big = seq >= 2048
    return dict(
        rows=512 if big else 128,
        out_rows=256 if big else 128,
        mla=512 if big else 128,
        mla_heads=8,
        idx_q=512 if big else 64,
        dsa_q=512 if big else 128,
        key=512 if big else 128,
        route=512 if big else 128,
        moe_tok=1024 if big else 128,
        comb=256 if big else 128,
    )


def _rope_tables(positions, rot_dim, head_dim):
    half = rot_dim // 2
    inv = ROPE_THETA ** (-jnp.arange(0, rot_dim, 2, dtype=F32) / rot_dim)
    ang = positions.astype(F32).reshape(-1, 1) * inv
    cos, sin = jnp.cos(ang), jnp.sin(ang)
    t = cos.shape[0]
    rest = head_dim - rot_dim
    ones, zeros, zh = jnp.ones((t, rest), F32), jnp.zeros((t, rest), F32), jnp.zeros((t, half), F32)
    rep = LANES // head_dim
    c = jnp.tile(jnp.concatenate([cos, cos, ones], axis=1), (1, rep))
    sa = jnp.tile(jnp.concatenate([-sin, zh, zeros], axis=1), (1, rep))
    sb = jnp.tile(jnp.concatenate([zh, sin, zeros], axis=1), (1, rep))
    return c, sa, sb


def _layer(layer, h, h_bf, positions, tabs, lw, alpha, B, S):
    T, D = h.shape
    tl = _tiles(S)
    tm = tl["rows"]
    nrow = T // tm
    spb = S // tm
    (c_m, sa_m, sb_m), (c_d, sa_d, sb_d), (c_i, sa_i, sb_i) = tabs
    row = lambda w: pl.BlockSpec((tm, w), lambda i: (i, 0))
    head_major = lambda nh, w: pl.BlockSpec((1, nh, tm, w), lambda i: (i // spb, 0, i % spb, 0))

    log2e = math.log2(math.e)
    q_a, k_a, v_a = pl.pallas_call(
        functools.partial(_mla_prep_kernel, q_scale=log2e * MLA_QK_DIM ** -0.5),
        grid=(nrow,),
        in_specs=[row(D), _resident_layer(lw["w_mla_in"], layer), _resident((1, MLA_Q_LORA)), _resident((1, MLA_KV_LORA)),
                  _resident(lw["w_q_b"].shape), _resident(lw["w_kv_b"].shape), row(LANES), row(LANES), row(LANES)],
        out_specs=[head_major(MLA_HEADS, MLA_QK_DIM), head_major(MLA_HEADS, MLA_QK_DIM), row(2 * MLA_HEADS * MLA_V_DIM)],
        out_shape=[jax.ShapeDtypeStruct((B, MLA_HEADS, S, MLA_QK_DIM), BF16),
                   jax.ShapeDtypeStruct((B, MLA_HEADS, S, MLA_QK_DIM), BF16),
                   jax.ShapeDtypeStruct((T, 2 * MLA_HEADS * MLA_V_DIM), BF16)],
        compiler_params=_params("parallel"),
    )(h_bf, lw["w_mla_in"], lw["g_q"], lw["g_kv"], lw["w_q_b"], lw["w_kv_b"], c_m, sa_m, sb_m)

    hd = DSA_HEADS * DSA_HEAD_DIM
    q_b, k_b, v_b, q_idx, k_idx, w_idx = pl.pallas_call(
        functools.partial(_dsa_prep_kernel, w_idx_scale=(IDX_HEADS ** -0.5) * (IDX_DIM ** -0.5),
                          q_scale=log2e * DSA_HEAD_DIM ** -0.5),
        grid=(nrow,),
        in_specs=[row(D), _resident_layer(lw["w_dsa_in"], layer)] + [row(LANES)] * 6,
        out_specs=[row(hd), row(hd), row(2 * hd), head_major(IDX_HEADS, IDX_DIM), row(IDX_DIM),
                   pl.BlockSpec((1, IDX_HEADS, tm), lambda i: (i // spb, 0, i % spb))],
        out_shape=[jax.ShapeDtypeStruct((T, hd), BF16)] * 2 + [jax.ShapeDtypeStruct((T, 2 * hd), BF16)] + [
            jax.ShapeDtypeStruct((B, IDX_HEADS, S, IDX_DIM), BF16),
            jax.ShapeDtypeStruct((T, IDX_DIM), BF16),
            jax.ShapeDtypeStruct((B, IDX_HEADS, S), F32)],
        compiler_params=_params("parallel"),
    )(h_bf, lw["w_dsa_in"], c_d, sa_d, sb_d, c_i, sa_i, sb_i)

    ta = tl["mla"]
    pos_q = positions.reshape(B, S, 1)
    pos_k_mla = positions.reshape(B, S // ta, 1, ta)
    hp = tl["mla_heads"]
    o_a = pl.pallas_call(
        functools.partial(_mla_attn_kernel, tile=ta, heads=hp),
        grid=(B, MLA_HEADS // hp, S // ta),
        in_specs=[pl.BlockSpec((1, hp, ta, MLA_QK_DIM), lambda b, g, i: (b, g, i, 0)),
                  pl.BlockSpec((1, hp, S, MLA_QK_DIM), lambda b, g, i: (b, g, 0, 0), pipeline_mode=pl.Buffered(1)),
                  pl.BlockSpec((1, S, 2 * MLA_V_DIM * hp), lambda b, g, i: (b, 0, g), pipeline_mode=pl.Buffered(1)),
                  pl.BlockSpec((1, ta, 1), lambda b, g, i: (b, i, 0)),
                  pl.BlockSpec((1, S // ta, 1, ta), lambda b, g, i: (b, 0, 0, 0))],
        out_specs=pl.BlockSpec((1, ta, MLA_V_DIM * hp), lambda b, g, i: (b, i, g)),
        out_shape=jax.ShapeDtypeStruct((B, S, MLA_HEADS * MLA_V_DIM), BF16),
        scratch_shapes=[pltpu.VMEM((hp, ta, LANES), F32), pltpu.VMEM((hp, ta, 2 * MLA_V_DIM), F32)],
        compiler_params=_params("parallel", "parallel", "arbitrary"),
    )(q_a, k_a, v_a.reshape(B, S, -1), pos_q, pos_k_mla)

    tk = tl["key"]
    nkc = S // tk
    tqi = tl["idx_q"]
    top_k = min(DSA_MAX_TOPK, S // 4)
    sel_bias = pl.pallas_call(
        functools.partial(_idx_select_kernel, tq=tqi, tk=tk, nkc=nkc, top_k=top_k),
        grid=(B, S // tqi),
        in_specs=[pl.BlockSpec((1, IDX_HEADS, tqi, IDX_DIM), lambda b, i: (b, 0, i, 0)),
                  pl.BlockSpec((1, S, IDX_DIM), lambda b, i: (b, 0, 0)),
                  pl.BlockSpec((1, IDX_HEADS, tqi), lambda b, i: (b, 0, i)),
                  pl.BlockSpec((1, 1, tqi), lambda b, i: (b, 0, i)),
                  pl.BlockSpec((1, nkc, tk, 1), lambda b, i: (b, 0, 0, 0))],
        out_specs=pl.BlockSpec((1, nkc, tqi, tk), lambda b, i: (b, 0, i, 0)),
        out_shape=jax.ShapeDtypeStruct((B, nkc, S, tk), BF16),
        scratch_shapes=[pltpu.VMEM((nkc, tk, tqi), I32)],
        compiler_params=_params("parallel", "arbitrary"),
    )(q_idx, k_idx.reshape(B, S, IDX_DIM), w_idx, positions.reshape(B, 1, S), positions.reshape(B, nkc, tk, 1))

    tqd = tl["dsa_q"]
    o_b = pl.pallas_call(
        functools.partial(_dsa_attn_kernel, tq=tqd, tk=tk),
        grid=(B, S // tqd),
        in_specs=[pl.BlockSpec((1, tqd, hd), lambda b, i: (b, i, 0)),
                  pl.BlockSpec((1, S, hd), lambda b, i: (b, 0, 0), pipeline_mode=pl.Buffered(1)),
                  pl.BlockSpec((1, S, 2 * hd), lambda b, i: (b, 0, 0), pipeline_mode=pl.Buffered(1)),
                  pl.BlockSpec((1, nkc, tqd, tk), lambda b, i: (b, 0, i, 0))],
        out_specs=pl.BlockSpec((1, tqd, hd), lambda b, i: (b, i, 0)),
        out_shape=jax.ShapeDtypeStruct((B, S, hd), BF16),
        scratch_shapes=[pltpu.VMEM((DSA_HEADS, tqd, LANES), F32), pltpu.VMEM((DSA_HEADS, tqd, 2 * DSA_HEAD_DIM), F32)],
        compiler_params=_params("parallel", "arbitrary"),
    )(q_b.reshape(B, S, hd), k_b.reshape(B, S, hd), v_b.reshape(B, S, 2 * hd), sel_bias)

    y = pl.pallas_call(
        _merge_kernel,
        grid=(nrow,),
        in_specs=[row(D), row(MLA_HEADS * MLA_V_DIM), row(hd),
                  _resident_layer(lw["w_gate_a"], layer), _resident_layer(lw["w_gate_b"], layer),
                  _resident((MLA_HEADS * MLA_V_DIM, D)), _resident((hd, D))],
        out_specs=row(D),
        out_shape=jax.ShapeDtypeStruct((T, D), BF16),
        compiler_params=_params("parallel"),
    )(h_bf, o_a.reshape(T, -1), o_b.reshape(T, hd), lw["w_gate_a"], lw["w_gate_b"], lw["w_o_a"], lw["w_o_b"])

    to = tl["out_rows"]
    orow = lambda w: pl.BlockSpec((to, w), lambda i: (i, 0))
    h1, h1_packed, logits = pl.pallas_call(
        functools.partial(_out_ln_kernel, alpha=alpha),
        grid=(T // to,),
        in_specs=[orow(D), orow(D), _resident((D, D)), _resident((1, D)), _resident((1, D)),
                  _resident((D, LANES)), _resident((D, LANES)), _resident((1, LANES))],
        out_specs=[orow(D), orow(D // 2), orow(LANES)],
        out_shape=[jax.ShapeDtypeStruct((T, D), F32), jax.ShapeDtypeStruct((T, D // 2), I32),
                   jax.ShapeDtypeStruct((T, LANES), F32)],
        compiler_params=_params("parallel"),
    )(y, h, lw["w_out"], lw["ln1_g"], lw["ln1_b"], lw["w_route_hi"], lw["w_route_lo"], lw["b_route"])

    tr = tl["route"]
    meta, wgt, cnt = pl.pallas_call(
        functools.partial(_route_kernel, tr=tr),
        grid=(T // tr,),
        in_specs=[pl.BlockSpec((tr, LANES), lambda i: (i, 0))],
        out_specs=[pl.BlockSpec((tr, LANES), lambda i: (i, 0)), pl.BlockSpec((tr, LANES), lambda i: (i, 0)),
                   pl.BlockSpec((8, LANES), lambda i: (0, 0))],
        out_shape=[jax.ShapeDtypeStruct((T, LANES), I32), jax.ShapeDtypeStruct((T, LANES), F32),
                   jax.ShapeDtypeStruct((8, LANES), F32)],
        scratch_shapes=[pltpu.VMEM((1, LANES), F32)],
        compiler_params=_params("arbitrary"),
    )(logits)

    rb = MOE_ROW_BLOCK
    counts = cnt[0, N_GROUPS:N_GROUPS + N_EXPERTS].astype(I32)
    padded = ((counts + rb - 1) // rb) * rb
    pends = jnp.cumsum(padded)
    pstarts = pends - padded
    n_blocks = -(-(2 * T) // rb) + N_EXPERTS
    P = n_blocks * rb
    blk = jnp.arange(n_blocks, dtype=I32)
    block_e = jnp.minimum(jnp.sum((pends[None, :] <= (blk * rb)[:, None]).astype(I32), axis=1), N_EXPERTS - 1)
    n_used = (pends[-1] // rb).astype(I32).reshape(1)
    ps_lanes = jnp.zeros((1, LANES), I32).at[0, :N_EXPERTS].set(pstarts.astype(I32))
    dest = pl.pallas_call(
        _dest_kernel,
        grid=(T // tr,),
        in_specs=[pl.BlockSpec((tr, LANES), lambda i: (i, 0)), pl.BlockSpec((1, LANES), lambda i: (0, 0))],
        out_specs=pl.BlockSpec((tr, LANES), lambda i: (i, 0)),
        out_shape=jax.ShapeDtypeStruct((T, LANES), I32),
        compiler_params=_params("parallel"),
    )(meta, ps_lanes)[:, 0:2].reshape(-1)

    last_of_expert = jnp.concatenate([block_e[1:] != block_e[:-1], jnp.ones((1,), bool)])
    needs_zero = jnp.logical_or(blk >= n_used[0], jnp.logical_or(last_of_expert, blk == n_used[0] - 1)).astype(I32)
    n_zero = jnp.sum(needs_zero).astype(I32).reshape(1)
    tt = tl["moe_tok"]
    xs = pl.pallas_call(
        functools.partial(_dispatch_kernel, tt=tt),
        grid_spec=pltpu.PrefetchScalarGridSpec(
            num_scalar_prefetch=3, grid=(T // tt,),
            in_specs=[pl.BlockSpec((tt, D // 2), lambda i, *_: (i, 0))],
            out_specs=pl.BlockSpec(memory_space=pl.ANY),
            scratch_shapes=[pltpu.VMEM((rb, D // 2), I32), pltpu.SemaphoreType.DMA(()), pltpu.SemaphoreType.DMA(())]),
        out_shape=jax.ShapeDtypeStruct((P, D // 2), I32),
        compiler_params=_params("arbitrary"),
    )(dest, needs_zero, n_zero, h1_packed)

    F = D_EXPERT
    grp = jnp.cumsum(jnp.concatenate([jnp.zeros((1,), I32), (block_e[1:] != block_e[:-1]).astype(I32)]))
    ids = jnp.where(counts > 0, jnp.arange(N_EXPERTS, dtype=I32), N_EXPERTS)
    at_or_after = lax.cummin(ids, axis=0, reverse=True)
    after = jnp.concatenate([at_or_after[1:], jnp.full((1,), N_EXPERTS, I32)])
    nxt_of = jnp.where(after < N_EXPERTS, after, -1)
    nxt2_of = jnp.where(nxt_of >= 0, nxt_of[jnp.maximum(nxt_of, 0)], -1)
    nxt = nxt_of[block_e].astype(I32)
    nxt2 = nxt2_of[block_e].astype(I32)
    any_spec = pl.BlockSpec(memory_space=pl.ANY)
    ys = pl.pallas_call(
        functools.partial(_experts_kernel, layer=layer),
        grid_spec=pltpu.PrefetchScalarGridSpec(
            num_scalar_prefetch=5, grid=(n_blocks,),
            in_specs=[pl.BlockSpec((rb, D // 2), lambda i, be, gr, nx, nx2, nu: (jnp.minimum(i, nu[0] - 1), 0)),
                      any_spec, any_spec, any_spec],
            out_specs=pl.BlockSpec((rb, D), lambda i, *_: (i, 0)),
            scratch_shapes=[pltpu.VMEM((2, D, F), F32), pltpu.VMEM((2, D, F), F32), pltpu.VMEM((2, F, D), F32),
                            pltpu.VMEM((D, F), BF16), pltpu.VMEM((D, F), BF16), pltpu.VMEM((F, D), BF16),
                            pltpu.SemaphoreType.DMA((2, 3))]),
        out_shape=jax.ShapeDtypeStruct((P, D), F32),
        compiler_params=_params("arbitrary"),
    )(block_e, grp.astype(I32), nxt, nxt2, n_used, xs, lw["w_e_gate"], lw["w_e_up"], lw["w_e_down"])

    tc = tl["comb"]
    h2, h2_bf = pl.pallas_call(
        functools.partial(_combine_kernel, tc=tc, alpha=alpha),
        grid_spec=pltpu.PrefetchScalarGridSpec(
            num_scalar_prefetch=1, grid=(T // tc,),
            in_specs=[pl.BlockSpec((tc, D), lambda i, d: (i, 0)),
                      pl.BlockSpec((tc, LANES), lambda i, d: (i, 0)),
                      pl.BlockSpec((1, D), lambda i, d: (0, 0)),
                      pl.BlockSpec((1, D), lambda i, d: (0, 0)),
                      pl.BlockSpec(memory_space=pl.ANY)],
            out_specs=[pl.BlockSpec((tc, D), lambda i, d: (i, 0)), pl.BlockSpec((tc, D), lambda i, d: (i, 0))],
            scratch_shapes=[pltpu.VMEM((2, 2 * tc, D), F32), pltpu.SemaphoreType.DMA((2,))]),
        out_shape=[jax.ShapeDtypeStruct((T, D), F32), jax.ShapeDtypeStruct((T, D), BF16)],
        compiler_params=_params("arbitrary"),
    )(dest, h1, wgt, lw["ln2_g"], lw["ln2_b"], ys)
    return h2, h2_bf


def _layer_weights(l, regrouped, g_q_lora, w_q_b, g_kv_lora, w_kv_b, w_o_a, w_o_b, w_out, ln1_g, ln1_b,
                   w_group, b_group, w_router, b_router, w_e_gate, w_e_up, w_e_down, ln2_g, ln2_b):
    w_mla_in, w_dsa_in, w_gate_a, w_gate_b = regrouped
    D = w_gate_a.shape[1]
    wq = w_q_b[l].reshape(MLA_Q_LORA, MLA_HEADS, MLA_QK_DIM)
    wq = jnp.concatenate([wq[:, :, :MLA_NOPE_DIM].reshape(MLA_Q_LORA, -1),
                          wq[:, :, MLA_NOPE_DIM:].reshape(MLA_Q_LORA, -1)], axis=1).astype(BF16)
    w_route = jnp.concatenate([w_group[l], w_router[l],
                               jnp.zeros((D, LANES - N_GROUPS - N_EXPERTS), F32)], axis=1)
    b_route = jnp.concatenate([b_group[l], b_router[l],
                               jnp.zeros((LANES - N_GROUPS - N_EXPERTS,), F32)]).reshape(1, LANES)
    return dict(
        w_mla_in=w_mla_in, w_dsa_in=w_dsa_in, w_gate_a=w_gate_a, w_gate_b=w_gate_b,
        g_q=g_q_lora[l].reshape(1, -1), g_kv=g_kv_lora[l].reshape(1, -1),
        w_q_b=wq, w_kv_b=w_kv_b[l].astype(BF16),
        w_o_a=w_o_a[l].astype(BF16), w_o_b=w_o_b[l].astype(BF16), w_out=w_out[l].astype(BF16),
        ln1_g=ln1_g[l].reshape(1, -1), ln1_b=ln1_b[l].reshape(1, -1),
        w_route_hi=w_route.astype(BF16), w_route_lo=(w_route - w_route.astype(BF16).astype(F32)).astype(BF16),
        b_route=b_route,
        w_e_gate=w_e_gate, w_e_up=w_e_up, w_e_down=w_e_down,
        ln2_g=ln2_g[l].reshape(1, -1), ln2_b=ln2_b[l].reshape(1, -1),
    )


def kernel(x, positions, w_in, g_q_lora, w_q_b, g_kv_lora, w_kv_b, w_o_a, w_o_b, w_out, ln1_g, ln1_b,
           w_group, b_group, w_router, b_router, w_e_gate, w_e_up, w_e_down, ln2_g, ln2_b):
    B, S, D = x.shape
    depth = w_in.shape[0]
    alpha = (2 * depth) ** 0.25
    tabs = (_rope_tables(positions, MLA_ROPE_DIM, MLA_ROPE_DIM),
            _rope_tables(positions, DSA_ROPE_DIM, DSA_HEAD_DIM),
            _rope_tables(positions, IDX_ROPE_DIM, IDX_DIM))
    h = x.reshape(B * S, D)
    h_bf = h.astype(BF16)
    regrouped = _regroup_w_in(w_in)
    for l in range(depth):
        lw = _layer_weights(l, regrouped, g_q_lora, w_q_b, g_kv_lora, w_kv_b, w_o_a, w_o_b, w_out, ln1_g, ln1_b,
                            w_group, b_group, w_router, b_router, w_e_gate, w_e_up, w_e_down, ln2_g, ln2_b)
        h, h_bf = _layer(l, h, h_bf, positions, tabs, lw, alpha, B, S)
    return h.reshape(B, S, D)
```

```python
import functools
import math

import jax
import jax.numpy as jnp
from jax import lax
from jax.experimental import pallas as pl
from jax.experimental.pallas import tpu as pltpu

F32 = jnp.float32
BF16 = jnp.bfloat16
I32 = jnp.int32

MLA_HEADS = 8
MLA_Q_LORA = 512
MLA_KV_LORA = 512
MLA_NOPE_DIM = 128
MLA_ROPE_DIM = 64
MLA_V_DIM = 128
MLA_QK_DIM = MLA_NOPE_DIM + MLA_ROPE_DIM
DSA_HEADS = 8
DSA_HEAD_DIM = 128
DSA_ROPE_DIM = DSA_HEAD_DIM // 4
IDX_HEADS = 16
IDX_DIM = 64
IDX_ROPE_DIM = IDX_DIM // 4
DSA_MAX_TOPK = 256
ROPE_THETA = 500000.0
N_GROUPS = 8
EXPERTS_PER_GROUP = 8
N_EXPERTS = N_GROUPS * EXPERTS_PER_GROUP
D_EXPERT = 512
MOE_ROW_BLOCK = 128
LN_EPS = 1e-5
RMS_EPS = 1e-6

LANES = 128
NEG_BIG = -1e30
INT_MIN = -(2 ** 31)
VMEM_LIMIT = 56 * 1024 * 1024
WEIGHT_DMA_CHUNKS = 4


def _params(*sem):
    return pltpu.CompilerParams(dimension_semantics=sem, vmem_limit_bytes=VMEM_LIMIT)


def _resident(shape):
    nd = len(shape)
    return pl.BlockSpec(shape, lambda *_: (0,) * nd, pipeline_mode=pl.Buffered(1))


def _resident_layer(stacked, layer):
    shape = stacked.shape[1:]
    nd = len(shape)
    return pl.BlockSpec((None,) + shape, lambda *_: (layer,) + (0,) * nd, pipeline_mode=pl.Buffered(1))


def _dot(a, b):
    return jnp.dot(a, b, preferred_element_type=F32)


def _dot_nt(a, b):
    return lax.dot_general(a, b, (((1,), (1,)), ((), ())), preferred_element_type=F32)


def _rope128(x, c, sa, sb, half):
    return x * c + pltpu.roll(x, LANES - half, 1) * sa + pltpu.roll(x, half, 1) * sb


def _rms(x, g):
    return x * lax.rsqrt(jnp.mean(x * x, axis=-1, keepdims=True) + RMS_EPS) * g


def _layer_norm(x, g, b):
    mu = jnp.mean(x, axis=-1, keepdims=True)
    xc = x - mu
    var = jnp.mean(xc * xc, axis=-1, keepdims=True)
    return xc * lax.rsqrt(var + LN_EPS) * g + b


_O_KPE = MLA_Q_LORA + MLA_KV_LORA
_O_DSA = _O_KPE + MLA_ROPE_DIM
_O_KIDX = _O_DSA + 3 * DSA_HEADS * DSA_HEAD_DIM + IDX_HEADS * IDX_DIM
_O_WIDX = _O_KIDX + IDX_DIM
_O_GATE = _O_WIDX + IDX_HEADS
_W_MLA_IN = _O_KPE + 2 * MLA_ROPE_DIM
_W_DSA_IN = (_O_KIDX - _O_DSA) + 2 * IDX_DIM + LANES


def _regroup_w_in_kernel(w_ref, mla_ref, dsa_ref, ga_ref, gb_ref, *, d_model):
    mla_ref[:, 0:_O_DSA] = w_ref[:, 0:_O_DSA].astype(BF16)
    mla_ref[:, _O_DSA:_W_MLA_IN] = w_ref[:, _O_KPE:_O_DSA].astype(BF16)
    main = _O_KIDX - _O_DSA
    dsa_ref[:, 0:main] = w_ref[:, _O_DSA:_O_KIDX].astype(BF16)
    kidx = w_ref[:, _O_KIDX:_O_WIDX].astype(BF16)
    dsa_ref[:, main:main + IDX_DIM] = kidx
    dsa_ref[:, main + IDX_DIM:main + 2 * IDX_DIM] = kidx
    dsa_ref[:, main + 2 * IDX_DIM:main + 2 * IDX_DIM + IDX_HEADS] = w_ref[:, _O_WIDX:_O_GATE].astype(BF16)
    dsa_ref[:, main + 2 * IDX_DIM + IDX_HEADS:] = jnp.zeros((w_ref.shape[0], LANES - IDX_HEADS), BF16)
    ga_ref[...] = w_ref[:, _O_GATE:_O_GATE + d_model].astype(BF16)
    gb_ref[...] = w_ref[:, _O_GATE + d_model:_O_GATE + 2 * d_model].astype(BF16)


def _regroup_w_in(w_in):
    depth, d_model, width = w_in.shape
    tr = 256
    out_w = (_W_MLA_IN, _W_DSA_IN, d_model, d_model)
    return pl.pallas_call(
        functools.partial(_regroup_w_in_kernel, d_model=d_model),
        grid=(depth, d_model // tr),
        in_specs=[pl.BlockSpec((tr, width), lambda l, i: (l * (d_model // tr) + i, 0))],
        out_specs=[pl.BlockSpec((None, tr, w), lambda l, i: (l, i, 0)) for w in out_w],
        out_shape=[jax.ShapeDtypeStruct((depth, d_model, w), BF16) for w in out_w],
        compiler_params=_params("parallel", "parallel"),
    )(w_in.astype(BF16).reshape(depth * d_model, width))


def _mla_prep_kernel(h_ref, win_ref, gq_ref, gkv_ref, wqb_ref, wkvb_ref, c_ref, sa_ref, sb_ref,
                     q_ref, k_ref, v_ref, *, q_scale):
    h = h_ref[...]
    p = _dot(h, win_ref[...])
    qn = _rms(p[:, :MLA_Q_LORA], gq_ref[...]).astype(BF16)
    kvn = _rms(p[:, MLA_Q_LORA:MLA_Q_LORA + MLA_KV_LORA], gkv_ref[...]).astype(BF16)
    q = _dot(qn, wqb_ref[...]) * q_scale
    kv = _dot(kvn, wkvb_ref[...])
    c, sa, sb = c_ref[...], sa_ref[...], sb_ref[...]
    half = MLA_ROPE_DIM // 2
    kpe = _rope128(p[:, MLA_Q_LORA + MLA_KV_LORA:], c, sa, sb, half)[:, :MLA_ROPE_DIM].astype(BF16)
    pe0 = MLA_HEADS * MLA_NOPE_DIM
    for hh in range(MLA_HEADS):
        q_ref[0, hh, :, 0:MLA_NOPE_DIM] = q[:, 128 * hh:128 * hh + 128].astype(BF16)
        k_ref[0, hh, :, 0:MLA_NOPE_DIM] = kv[:, 256 * hh:256 * hh + 128].astype(BF16)
        k_ref[0, hh, :, MLA_NOPE_DIM:MLA_QK_DIM] = kpe
        v_ref[:, 256 * hh:256 * hh + 128] = kv[:, 256 * hh + 128:256 * hh + 256].astype(BF16)
        v_ref[:, 256 * hh + 128:256 * hh + 256] = jnp.ones((kv.shape[0], LANES), BF16)
    for s in range(MLA_HEADS // 2):
        slab = _rope128(q[:, pe0 + 128 * s:pe0 + 128 * s + 128], c, sa, sb, half).astype(BF16)
        q_ref[0, 2 * s, :, MLA_NOPE_DIM:MLA_QK_DIM] = slab[:, :MLA_ROPE_DIM]
        q_ref[0, 2 * s + 1, :, MLA_NOPE_DIM:MLA_QK_DIM] = slab[:, MLA_ROPE_DIM:]


def _dsa_prep_kernel(h_ref, w_ref, cd_ref, sad_ref, sbd_ref, ci_ref, sai_ref, sbi_ref,
                     qb_ref, kb_ref, vb_ref, qi_ref, ki_ref, wi_ref, *, w_idx_scale, q_scale):
    h = h_ref[...]
    hd = DSA_HEADS * DSA_HEAD_DIM
    cd, sad, sbd = cd_ref[...], sad_ref[...], sbd_ref[...]
    ci, sai, sbi = ci_ref[...], sai_ref[...], sbi_ref[...]
    hd_half, hi_half = DSA_ROPE_DIM // 2, IDX_ROPE_DIM // 2
    q = _dot(h, w_ref[:, 0:hd]) * q_scale
    for hh in range(DSA_HEADS):
        qb_ref[:, 128 * hh:128 * hh + 128] = _rope128(q[:, 128 * hh:128 * hh + 128], cd, sad, sbd, hd_half).astype(BF16)
    k = _dot(h, w_ref[:, hd:2 * hd])
    for hh in range(DSA_HEADS):
        kb_ref[:, 128 * hh:128 * hh + 128] = _rope128(k[:, 128 * hh:128 * hh + 128], cd, sad, sbd, hd_half).astype(BF16)
    v = _dot(h, w_ref[:, 2 * hd:3 * hd]).astype(BF16)
    for hh in range(DSA_HEADS):
        vb_ref[:, 256 * hh:256 * hh + 128] = v[:, 128 * hh:128 * hh + 128]
        vb_ref[:, 256 * hh + 128:256 * hh + 256] = jnp.ones((v.shape[0], LANES), BF16)
    qi = _dot(h, w_ref[:, 3 * hd:3 * hd + IDX_HEADS * IDX_DIM])
    for s in range(IDX_HEADS // 2):
        slab = _rope128(qi[:, 128 * s:128 * s + 128], ci, sai, sbi, hi_half).astype(BF16)
        qi_ref[0, 2 * s] = slab[:, :IDX_DIM]
        qi_ref[0, 2 * s + 1] = slab[:, IDX_DIM:]
    o = 3 * hd + IDX_HEADS * IDX_DIM
    last = _dot(h, w_ref[:, o:o + 2 * LANES])
    ki_ref[...] = _rope128(last[:, :LANES], ci, sai, sbi, hi_half)[:, :IDX_DIM].astype(BF16)
    wi_ref[0] = (last[:, LANES:] * w_idx_scale).T[:IDX_HEADS, :]


def _flash_update(s, v_ones, m_ref, acc_ref):
    tk = s.shape[1]
    m_prev = m_ref[...]
    m_next = jnp.maximum(m_prev, jnp.max(s, axis=1, keepdims=True))
    p = jnp.exp2(s - jnp.tile(m_next, (1, tk // LANES)))
    alpha = jnp.exp2(m_prev - m_next)
    acc_ref[...] = acc_ref[...] * jnp.tile(alpha, (1, 2)) + _dot(p.astype(BF16), v_ones)
    m_ref[...] = m_next


def _mla_attn_kernel(q_ref, k_ref, v_ref, pq_ref, pk_ref, o_ref, m_sc, acc_sc, *, tile, heads):
    qi = pl.program_id(2)
    m_sc[...] = jnp.full(m_sc.shape, -jnp.inf, F32)
    acc_sc[...] = jnp.zeros(acc_sc.shape, F32)

    def chunk(c, masked):
        ks = pl.ds(pl.multiple_of(c * tile, tile), tile)
        if masked:
            visible = pk_ref[0, c] <= pq_ref[0]
        for hh in range(heads):
            s = _dot_nt(q_ref[0, hh], k_ref[0, hh, ks, :])
            if masked:
                s = jnp.where(visible, s, NEG_BIG)
            _flash_update(s, v_ref[0, ks, 2 * MLA_V_DIM * hh:2 * MLA_V_DIM * (hh + 1)], m_sc.at[hh], acc_sc.at[hh])

    def body(c, carry):
        chunk(c, False)
        return carry

    lax.fori_loop(0, qi, body, 0)
    chunk(qi, True)
    for hh in range(heads):
        o_ref[0, :, MLA_V_DIM * hh:MLA_V_DIM * (hh + 1)] = (
            acc_sc[hh, :, :MLA_V_DIM] / acc_sc[hh, :, MLA_V_DIM:]).astype(o_ref.dtype)


def _idx_select_kernel(q_ref, k_ref, wt_ref, pq_ref, pk_ref, o_ref, key_sc, *, tq, tk, nkc, top_k):
    qi = pl.program_id(1)
    nk = (qi * tq + tq + tk - 1) // tk
    wt = wt_ref[0]
    pq = pq_ref[0]

    def score_chunk(c, carry):
        kc = k_ref[0, pl.ds(pl.multiple_of(c * tk, tk), tk), :]
        acc = jnp.zeros((tk, tq), F32)
        for hh in range(IDX_HEADS):
            acc = acc + wt[hh:hh + 1, :] * jnp.maximum(_dot_nt(kc, q_ref[0, hh]), 0.0)
        bits = pltpu.bitcast(acc, I32)
        key = jnp.where(bits < 0, bits ^ jnp.int32(0x7FFFFFFF), bits)
        key = jnp.where(acc == 0.0, 0, key)
        key_sc[c] = jnp.where(pk_ref[0, c] <= pq, key, INT_MIN)
        return carry

    lax.fori_loop(0, nk, score_chunk, 0)

    def count(pred):
        def body(c, part):
            m = pred(key_sc[c], c).astype(I32)
            return part + jnp.sum(m.reshape(tk // 8, 8, tq), axis=0)
        part = lax.fori_loop(0, nk, body, jnp.zeros((8, tq), I32))
        return jnp.sum(part, axis=0, keepdims=True)

    def bit_step(i, carry):
        base, at_base = carry
        cand = base ^ (jnp.int32(1) << (31 - i))
        cnt = count(lambda key, c: key >= cand)
        take = cnt >= top_k
        return jnp.where(take, cand, base), jnp.where(take, cnt, at_base)

    def refine(lo, hi, carry):
        settled = jnp.min((carry[1] <= top_k).astype(I32)) > 0
        return lax.cond(settled, lambda c: c, lambda c: lax.fori_loop(lo, hi, bit_step, c), carry)

    carry = lax.fori_loop(0, 24, bit_step, (jnp.full((1, tq), INT_MIN, I32), jnp.full((1, tq), 2 ** 30, I32)))
    carry = refine(24, 28, carry)
    thr, at_thr = refine(28, 32, carry)
    tie = jnp.logical_and(at_thr > top_k, thr != INT_MIN)

    def key_index(c):
        return c * tk + lax.broadcasted_iota(I32, (tk, tq), 0)

    def tie_search():
        n_eq = count(lambda key, c: key == thr)
        need = top_k - (at_thr - n_eq)

        def step(i, p):
            cand = p | (jnp.int32(1) << (14 - i))
            cnt = count(lambda key, c: jnp.logical_and(key == thr, key_index(c) < cand))
            return jnp.where(cnt <= need, cand, p)
        return lax.fori_loop(0, 15, step, jnp.zeros((1, tq), I32))

    p_cut = lax.cond(jnp.max(tie.astype(I32)) > 0, tie_search,
                     lambda: jnp.full((1, tq), 2 ** 30, I32))

    def out_chunk(c, carry):
        key = key_sc[c]
        sel = jnp.logical_or(key > thr, jnp.logical_and(key == thr, key_index(c) < p_cut))
        sel = jnp.logical_and(sel, key != INT_MIN)
        o_ref[0, c] = jnp.where(sel, 0.0, NEG_BIG).T.astype(o_ref.dtype)
        return carry

    lax.fori_loop(0, nk, out_chunk, 0)

    def fill_chunk(c, carry):
        o_ref[0, c] = jnp.full((tq, tk), NEG_BIG, o_ref.dtype)
        return carry

    lax.fori_loop(nk, nkc, fill_chunk, 0)


def _dsa_attn_kernel(q_ref, k_ref, v_ref, b_ref, o_ref, m_sc, acc_sc, *, tq, tk):
    qi = pl.program_id(1)
    nk = (qi * tq + tq + tk - 1) // tk
    m_sc[...] = jnp.full(m_sc.shape, NEG_BIG, F32)
    acc_sc[...] = jnp.zeros(acc_sc.shape, F32)

    def chunk(c, carry):
        bias = b_ref[0, c].astype(F32)
        ks = pl.ds(pl.multiple_of(c * tk, tk), tk)
        for hh in range(DSA_HEADS):
            hs = slice(DSA_HEAD_DIM * hh, DSA_HEAD_DIM * (hh + 1))
            vs = slice(2 * DSA_HEAD_DIM * hh, 2 * DSA_HEAD_DIM * (hh + 1))
            s = _dot_nt(q_ref[0, :, hs], k_ref[0, ks, hs]) + bias
            _flash_update(s, v_ref[0, ks, vs], m_sc.at[hh], acc_sc.at[hh])
        return carry

    lax.fori_loop(0, nk, chunk, 0)
    for hh in range(DSA_HEADS):
        hs = slice(DSA_HEAD_DIM * hh, DSA_HEAD_DIM * (hh + 1))
        o_ref[0, :, hs] = (acc_sc[hh, :, :DSA_HEAD_DIM] / acc_sc[hh, :, DSA_HEAD_DIM:]).astype(o_ref.dtype)


def _merge_kernel(h_ref, oa_ref, ob_ref, wga_ref, wgb_ref, woa_ref, wob_ref, y_ref):
    h = h_ref[...]
    ya = jax.nn.sigmoid(_dot(h, wga_ref[...])) * _dot(oa_ref[...], woa_ref[...])
    yb = jax.nn.sigmoid(_dot(h, wgb_ref[...])) * _dot(ob_ref[...], wob_ref[...])
    y_ref[...] = (ya + yb).astype(y_ref.dtype)


def _out_ln_kernel(y_ref, h_ref, wout_ref, g_ref, b_ref, wrh_ref, wrl_ref, br_ref, o_ref, op_ref, lg_ref, *, alpha):
    sub = 128
    half = o_ref.shape[1] // 2
    for r in range(y_ref.shape[0] // sub):
        rows = slice(r * sub, (r + 1) * sub)
        mix = _dot(y_ref[rows, :], wout_ref[...])
        h1 = _layer_norm(alpha * h_ref[rows, :] + mix, g_ref[...], b_ref[...])
        o_ref[rows, :] = h1
        lo = pltpu.bitcast(h1[:, :half].astype(BF16).astype(F32), I32)
        hi = pltpu.bitcast(h1[:, half:].astype(BF16).astype(F32), I32)
        op_ref[rows, :] = lax.shift_right_logical(lo, 16) | hi
        h_hi = h1.astype(BF16)
        h_lo = (h1 - h_hi.astype(F32)).astype(BF16)
        lg_ref[rows, :] = (_dot(h_hi, wrh_ref[...]) + (_dot(h_hi, wrl_ref[...]) + _dot(h_lo, wrh_ref[...]))
                           + br_ref[...])


def _route_kernel(lg_ref, meta_ref, wgt_ref, cnt_ref, carry_sc, *, tr):
    @pl.when(pl.program_id(0) == 0)
    def _():
        carry_sc[...] = jnp.zeros(carry_sc.shape, F32)

    lg = lg_ref[...]
    lane = lax.broadcasted_iota(I32, (tr, LANES), 1)
    gl = jnp.where(lane < N_GROUPS, lg, -jnp.inf)
    gmax = jnp.max(gl, axis=1, keepdims=True)
    g_idx = jnp.min(jnp.where(gl == gmax, lane, LANES), axis=1, keepdims=True)
    g_p = 1.0 / jnp.sum(jnp.exp(gl - gmax), axis=1, keepdims=True)
    lo = N_GROUPS + EXPERTS_PER_GROUP * g_idx
    in_group = jnp.logical_and(lane >= lo, lane < lo + EXPERTS_PER_GROUP)
    el = jnp.where(in_group, lg, -jnp.inf)
    emax = jnp.max(el, axis=1, keepdims=True)
    e1 = jnp.min(jnp.where(el == emax, lane, LANES), axis=1, keepdims=True)
    den = jnp.sum(jnp.exp(el - emax), axis=1, keepdims=True)
    el2 = jnp.where(lane == e1, -jnp.inf, el)
    emax2 = jnp.max(el2, axis=1, keepdims=True)
    e2 = jnp.min(jnp.where(el2 == emax2, lane, LANES), axis=1, keepdims=True)
    p1 = 1.0 / den
    p2 = jnp.exp(emax2 - emax) / den
    w1 = g_p * (p1 / (p1 + p2))
    w2 = g_p * (p2 / (p1 + p2))

    is1 = lane == e1
    is2 = lane == e2
    onehot = jnp.logical_or(is1, is2).astype(BF16)
    r = lax.broadcasted_iota(I32, (tr, tr), 0)
    cidx = lax.broadcasted_iota(I32, (tr, tr), 1)
    lower = (cidx < r).astype(BF16)
    before = _dot(lower, onehot) + carry_sc[...]
    rank1 = jnp.sum(jnp.where(is1, before, 0.0), axis=1, keepdims=True)
    rank2 = jnp.sum(jnp.where(is2, before, 0.0), axis=1, keepdims=True)
    carry_sc[...] = carry_sc[...] + jnp.sum(onehot.astype(F32), axis=0, keepdims=True)

    meta = jnp.where(lane == 0, e1 - N_GROUPS,
                     jnp.where(lane == 1, e2 - N_GROUPS,
                               jnp.where(lane == 2, rank1.astype(I32),
                                         jnp.where(lane == 3, rank2.astype(I32), 0))))
    meta_ref[...] = meta
    wgt_ref[...] = jnp.where(lane == 0, w1, jnp.where(lane == 1, w2, 0.0))
    cnt_ref[...] = jnp.broadcast_to(carry_sc[...], cnt_ref.shape)


def _row_copy(src, src_row, dst, dst_row, sem):
    return pltpu.make_async_copy(src.at[pl.ds(src_row, 1)], dst.at[pl.ds(dst_row, 1)], sem)


def _dest_kernel(meta_ref, ps_ref, o_ref):
    meta = meta_ref[...]
    ps = ps_ref[...]
    lane = lax.broadcasted_iota(I32, meta.shape, 1)
    d1 = jnp.sum(jnp.where(lane == meta[:, 0:1], ps, 0), axis=1, keepdims=True) + meta[:, 2:3]
    d2 = jnp.sum(jnp.where(lane == meta[:, 1:2], ps, 0), axis=1, keepdims=True) + meta[:, 3:4]
    o_ref[...] = jnp.where(lane == 0, d1, jnp.where(lane == 1, d2, 0))


def _dispatch_kernel(dest_ref, zblk_ref, nz_ref, h_ref, xs_out, zero_sc, sem, zsem, *, tt):
    base = pl.program_id(0) * tt
    rb = zero_sc.shape[0]

    @pl.when(pl.program_id(0) == 0)
    def _():
        zero_sc[...] = jnp.zeros(zero_sc.shape, zero_sc.dtype)

        def fill(j, carry):
            @pl.when(zblk_ref[j] != 0)
            def _():
                pltpu.make_async_copy(zero_sc, xs_out.at[pl.ds(pl.multiple_of(j * rb, rb), rb)], zsem).start()
            return carry

        def drain(j, carry):
            pltpu.make_async_copy(zero_sc, xs_out.at[pl.ds(0, rb)], zsem).wait()
            return carry

        lax.fori_loop(0, zblk_ref.shape[0], fill, 0)
        lax.fori_loop(0, nz_ref[0], drain, 0)

    def issue(t, carry):
        tok = base + t
        _row_copy(h_ref, t, xs_out, dest_ref[2 * tok], sem).start()
        _row_copy(h_ref, t, xs_out, dest_ref[2 * tok + 1], sem).start()
        return carry

    lax.fori_loop(0, tt, issue, 0, unroll=8)
    for _ in range(2):
        pltpu.make_async_copy(h_ref, xs_out.at[pl.ds(0, tt)], sem).wait()


def _experts_kernel(be_ref, grp_ref, nxt_ref, nxt2_ref, nu_ref, xs_ref, wg_hbm, wu_hbm, wd_hbm, o_ref,
                    wg_buf, wu_buf, wd_buf, wg_sc, wu_sc, wd_sc, sem, *, layer):
    i = pl.program_id(0)
    e = be_ref[i]
    used = i < nu_ref[0]
    first = jnp.logical_and(used, jnp.logical_or(i == 0, e != be_ref[jnp.maximum(i - 1, 0)]))
    slot = grp_ref[i] % 2

    def copies(expert, s):
        out = []
        for j, (hbm, buf) in enumerate(((wg_hbm, wg_buf), (wu_hbm, wu_buf), (wd_hbm, wd_buf))):
            rows = buf.shape[1] // WEIGHT_DMA_CHUNKS
            for c in range(WEIGHT_DMA_CHUNKS):
                rs = pl.ds(c * rows, rows)
                out.append(pltpu.make_async_copy(hbm.at[layer, expert, rs], buf.at[s, rs], sem.at[s, j]))
        return out

    @pl.when(i == 0)
    def _():
        for cp in copies(e, slot):
            cp.start()

        @pl.when(nxt_ref[0] >= 0)
        def _():
            for cp in copies(nxt_ref[0], 1 - slot):
                cp.start()

    @pl.when(first)
    def _():
        for cp in copies(e, slot):
            cp.wait()
        wg_sc[...] = wg_buf[slot].astype(BF16)
        wu_sc[...] = wu_buf[slot].astype(BF16)
        wd_sc[...] = wd_buf[slot].astype(BF16)
        nxt2 = nxt2_ref[i]

        @pl.when(nxt2 >= 0)
        def _():
            for cp in copies(nxt2, slot):
                cp.start()

    @pl.when(used)
    def _():
        xp = xs_ref[...]
        x = jnp.concatenate([pltpu.bitcast(xp << 16, F32).astype(BF16),
                             pltpu.bitcast(xp & jnp.int32(-65536), F32).astype(BF16)], axis=1)
        hid = jax.nn.silu(_dot(x, wg_sc[...])) * _dot(x, wu_sc[...])
        o_ref[...] = _dot(hid.astype(BF16), wd_sc[...])

    @pl.when(jnp.logical_not(used))
    def _():
        o_ref[...] = jnp.zeros(o_ref.shape, o_ref.dtype)


def _combine_kernel(dest_ref, h_ref, wgt_ref, g_ref, b_ref, ys_hbm, o_ref, ob_ref, ybuf, sem, *, tc, alpha):
    i = pl.program_id(0)
    last = pl.num_programs(0) - 1
    slot = i % 2

    def issue(tile, s, t):
        tok = tile * tc + t
        _row_copy(ys_hbm, dest_ref[2 * tok], ybuf.at[s], t, sem.at[s]).start()
        _row_copy(ys_hbm, dest_ref[2 * tok + 1], ybuf.at[s], tc + t, sem.at[s]).start()

    def wait(s):
        pltpu.make_async_copy(ys_hbm.at[pl.ds(0, 2 * tc)], ybuf.at[s], sem.at[s]).wait()

    @pl.when(i == 0)
    def _():
        def body(t, carry):
            issue(0, 0, t)
            return carry
        lax.fori_loop(0, tc, body, 0, unroll=8)

    wait(slot)
    nxt = jnp.minimum(i + 1, last)
    for t in range(tc):
        issue(nxt, 1 - slot, t)
    wgt = wgt_ref[...]
    ff = ybuf[slot, 0:tc, :] * wgt[:, 0:1] + ybuf[slot, tc:2 * tc, :] * wgt[:, 1:2]
    h2 = _layer_norm(alpha * h_ref[...] + ff, g_ref[...], b_ref[...])
    o_ref[...] = h2
    ob_ref[...] = h2.astype(BF16)

    @pl.when(i == last)
    def _():
        wait(1 - slot)


def _tiles(seq):
    big = seq >= 2048
    return dict(
        rows=512 if big else 128,
        out_rows=256 if big else 128,
        mla=512 if big else 128,
        mla_heads=8,
        idx_q=512 if big else 64,
        dsa_q=512 if big else 128,
        key=512 if big else 128,
        route=512 if big else 128,
        moe_tok=1024 if big else 128,
        comb=256 if big else 128,
    )


def _rope_tables(positions, rot_dim, head_dim):
    half = rot_dim // 2
    inv = ROPE_THETA ** (-jnp.arange(0, rot_dim, 2, dtype=F32) / rot_dim)
    ang = positions.astype(F32).reshape(-1, 1) * inv
    cos, sin = jnp.cos(ang), jnp.sin(ang)
    t = cos.shape[0]
    rest = head_dim - rot_dim
    ones, zeros, zh = jnp.ones((t, rest), F32), jnp.zeros((t, rest), F32), jnp.zeros((t, half), F32)
    rep = LANES // head_dim
    c = jnp.tile(jnp.concatenate([cos, cos, ones], axis=1), (1, rep))
    sa = jnp.tile(jnp.concatenate([-sin, zh, zeros], axis=1), (1, rep))
    sb = jnp.tile(jnp.concatenate([zh, sin, zeros], axis=1), (1, rep))
    return c, sa, sb


def _layer(layer, h, h_bf, positions, tabs, lw, alpha, B, S):
    T, D = h.shape
    tl = _tiles(S)
    tm = tl["rows"]
    nrow = T // tm
    spb = S // tm
    (c_m, sa_m, sb_m), (c_d, sa_d, sb_d), (c_i, sa_i, sb_i) = tabs
    row = lambda w: pl.BlockSpec((tm, w), lambda i: (i, 0))
    head_major = lambda nh, w: pl.BlockSpec((1, nh, tm, w), lambda i: (i // spb, 0, i % spb, 0))

    log2e = math.log2(math.e)
    q_a, k_a, v_a = pl.pallas_call(
        functools.partial(_mla_prep_kernel, q_scale=log2e * MLA_QK_DIM ** -0.5),
        grid=(nrow,),
        in_specs=[row(D), _resident_layer(lw["w_mla_in"], layer), _resident((1, MLA_Q_LORA)), _resident((1, MLA_KV_LORA)),
                  _resident(lw["w_q_b"].shape), _resident(lw["w_kv_b"].shape), row(LANES), row(LANES), row(LANES)],
        out_specs=[head_major(MLA_HEADS, MLA_QK_DIM), head_major(MLA_HEADS, MLA_QK_DIM), row(2 * MLA_HEADS * MLA_V_DIM)],
        out_shape=[jax.ShapeDtypeStruct((B, MLA_HEADS, S, MLA_QK_DIM), BF16),
                   jax.ShapeDtypeStruct((B, MLA_HEADS, S, MLA_QK_DIM), BF16),
                   jax.ShapeDtypeStruct((T, 2 * MLA_HEADS * MLA_V_DIM), BF16)],
        compiler_params=_params("parallel"),
    )(h_bf, lw["w_mla_in"], lw["g_q"], lw["g_kv"], lw["w_q_b"], lw["w_kv_b"], c_m, sa_m, sb_m)

    hd = DSA_HEADS * DSA_HEAD_DIM
    q_b, k_b, v_b, q_idx, k_idx, w_idx = pl.pallas_call(
        functools.partial(_dsa_prep_kernel, w_idx_scale=(IDX_HEADS ** -0.5) * (IDX_DIM ** -0.5),
                          q_scale=log2e * DSA_HEAD_DIM ** -0.5),
        grid=(nrow,),
        in_specs=[row(D), _resident_layer(lw["w_dsa_in"], layer)] + [row(LANES)] * 6,
        out_specs=[row(hd), row(hd), row(2 * hd), head_major(IDX_HEADS, IDX_DIM), row(IDX_DIM),
                   pl.BlockSpec((1, IDX_HEADS, tm), lambda i: (i // spb, 0, i % spb))],
        out_shape=[jax.ShapeDtypeStruct((T, hd), BF16)] * 2 + [jax.ShapeDtypeStruct((T, 2 * hd), BF16)] + [
            jax.ShapeDtypeStruct((B, IDX_HEADS, S, IDX_DIM), BF16),
            jax.ShapeDtypeStruct((T, IDX_DIM), BF16),
            jax.ShapeDtypeStruct((B, IDX_HEADS, S), F32)],
        compiler_params=_params("parallel"),
    )(h_bf, lw["w_dsa_in"], c_d, sa_d, sb_d, c_i, sa_i, sb_i)

    ta = tl["mla"]
    pos_q = positions.reshape(B, S, 1)
    pos_k_mla = positions.reshape(B, S // ta, 1, ta)
    hp = tl["mla_heads"]
    o_a = pl.pallas_call(
        functools.partial(_mla_attn_kernel, tile=ta, heads=hp),
        grid=(B, MLA_HEADS // hp, S // ta),
        in_specs=[pl.BlockSpec((1, hp, ta, MLA_QK_DIM), lambda b, g, i: (b, g, i, 0)),
                  pl.BlockSpec((1, hp, S, MLA_QK_DIM), lambda b, g, i: (b, g, 0, 0), pipeline_mode=pl.Buffered(1)),
                  pl.BlockSpec((1, S, 2 * MLA_V_DIM * hp), lambda b, g, i: (b, 0, g), pipeline_mode=pl.Buffered(1)),
                  pl.BlockSpec((1, ta, 1), lambda b, g, i: (b, i, 0)),
                  pl.BlockSpec((1, S // ta, 1, ta), lambda b, g, i: (b, 0, 0, 0))],
        out_specs=pl.BlockSpec((1, ta, MLA_V_DIM * hp), lambda b, g, i: (b, i, g)),
        out_shape=jax.ShapeDtypeStruct((B, S, MLA_HEADS * MLA_V_DIM), BF16),
        scratch_shapes=[pltpu.VMEM((hp, ta, LANES), F32), pltpu.VMEM((hp, ta, 2 * MLA_V_DIM), F32)],
        compiler_params=_params("parallel", "parallel", "arbitrary"),
    )(q_a, k_a, v_a.reshape(B, S, -1), pos_q, pos_k_mla)

    tk = tl["key"]
    nkc = S // tk
    tqi = tl["idx_q"]
    top_k = min(DSA_MAX_TOPK, S // 4)
    sel_bias = pl.pallas_call(
        functools.partial(_idx_select_kernel, tq=tqi, tk=tk, nkc=nkc, top_k=top_k),
        grid=(B, S // tqi),
        in_specs=[pl.BlockSpec((1, IDX_HEADS, tqi, IDX_DIM), lambda b, i: (b, 0, i, 0)),
                  pl.BlockSpec((1, S, IDX_DIM), lambda b, i: (b, 0, 0)),
                  pl.BlockSpec((1, IDX_HEADS, tqi), lambda b, i: (b, 0, i)),
                  pl.BlockSpec((1, 1, tqi), lambda b, i: (b, 0, i)),
                  pl.BlockSpec((1, nkc, tk, 1), lambda b, i: (b, 0, 0, 0))],
        out_specs=pl.BlockSpec((1, nkc, tqi, tk), lambda b, i: (b, 0, i, 0)),
        out_shape=jax.ShapeDtypeStruct((B, nkc, S, tk), BF16),
        scratch_shapes=[pltpu.VMEM((nkc, tk, tqi), I32)],
        compiler_params=_params("parallel", "arbitrary"),
    )(q_idx, k_idx.reshape(B, S, IDX_DIM), w_idx, positions.reshape(B, 1, S), positions.reshape(B, nkc, tk, 1))

    tqd = tl["dsa_q"]
    o_b = pl.pallas_call(
        functools.partial(_dsa_attn_kernel, tq=tqd, tk=tk),
        grid=(B, S // tqd),
        in_specs=[pl.BlockSpec((1, tqd, hd), lambda b, i: (b, i, 0)),
                  pl.BlockSpec((1, S, hd), lambda b, i: (b, 0, 0), pipeline_mode=pl.Buffered(1)),
                  pl.BlockSpec((1, S, 2 * hd), lambda b, i: (b, 0, 0), pipeline_mode=pl.Buffered(1)),
                  pl.BlockSpec((1, nkc, tqd, tk), lambda b, i: (b, 0, i, 0))],
        out_specs=pl.BlockSpec((1, tqd, hd), lambda b, i: (b, i, 0)),
        out_shape=jax.ShapeDtypeStruct((B, S, hd), BF16),
        scratch_shapes=[pltpu.VMEM((DSA_HEADS, tqd, LANES), F32), pltpu.VMEM((DSA_HEADS, tqd, 2 * DSA_HEAD_DIM), F32)],
        compiler_params=_params("parallel", "arbitrary"),
    )(q_b.reshape(B, S, hd), k_b.reshape(B, S, hd), v_b.reshape(B, S, 2 * hd), sel_bias)

    y = pl.pallas_call(
        _merge_kernel,
        grid=(nrow,),
        in_specs=[row(D), row(MLA_HEADS * MLA_V_DIM), row(hd),
                  _resident_layer(lw["w_gate_a"], layer), _resident_layer(lw["w_gate_b"], layer),
                  _resident((MLA_HEADS * MLA_V_DIM, D)), _resident((hd, D))],
        out_specs=row(D),
        out_shape=jax.ShapeDtypeStruct((T, D), BF16),
        compiler_params=_params("parallel"),
    )(h_bf, o_a.reshape(T, -1), o_b.reshape(T, hd), lw["w_gate_a"], lw["w_gate_b"], lw["w_o_a"], lw["w_o_b"])

    to = tl["out_rows"]
    orow = lambda w: pl.BlockSpec((to, w), lambda i: (i, 0))
    h1, h1_packed, logits = pl.pallas_call(
        functools.partial(_out_ln_kernel, alpha=alpha),
        grid=(T // to,),
        in_specs=[orow(D), orow(D), _resident((D, D)), _resident((1, D)), _resident((1, D)),
                  _resident((D, LANES)), _resident((D, LANES)), _resident((1, LANES))],
        out_specs=[orow(D), orow(D // 2), orow(LANES)],
        out_shape=[jax.ShapeDtypeStruct((T, D), F32), jax.ShapeDtypeStruct((T, D // 2), I32),
                   jax.ShapeDtypeStruct((T, LANES), F32)],
        compiler_params=_params("parallel"),
    )(y, h, lw["w_out"], lw["ln1_g"], lw["ln1_b"], lw["w_route_hi"], lw["w_route_lo"], lw["b_route"])

    tr = tl["route"]
    meta, wgt, cnt = pl.pallas_call(
        functools.partial(_route_kernel, tr=tr),
        grid=(T // tr,),
        in_specs=[pl.BlockSpec((tr, LANES), lambda i: (i, 0))],
        out_specs=[pl.BlockSpec((tr, LANES), lambda i: (i, 0)), pl.BlockSpec((tr, LANES), lambda i: (i, 0)),
                   pl.BlockSpec((8, LANES), lambda i: (0, 0))],
        out_shape=[jax.ShapeDtypeStruct((T, LANES), I32), jax.ShapeDtypeStruct((T, LANES), F32),
                   jax.ShapeDtypeStruct((8, LANES), F32)],
        scratch_shapes=[pltpu.VMEM((1, LANES), F32)],
        compiler_params=_params("arbitrary"),
    )(logits)

    rb = MOE_ROW_BLOCK
    counts = cnt[0, N_GROUPS:N_GROUPS + N_EXPERTS].astype(I32)
    padded = ((counts + rb - 1) // rb) * rb
    pends = jnp.cumsum(padded)
    pstarts = pends - padded
    n_blocks = -(-(2 * T) // rb) + N_EXPERTS
    P = n_blocks * rb
    blk = jnp.arange(n_blocks, dtype=I32)
    block_e = jnp.minimum(jnp.sum((pends[None, :] <= (blk * rb)[:, None]).astype(I32), axis=1), N_EXPERTS - 1)
    n_used = (pends[-1] // rb).astype(I32).reshape(1)
    ps_lanes = jnp.zeros((1, LANES), I32).at[0, :N_EXPERTS].set(pstarts.astype(I32))
    dest = pl.pallas_call(
        _dest_kernel,
        grid=(T // tr,),
        in_specs=[pl.BlockSpec((tr, LANES), lambda i: (i, 0)), pl.BlockSpec((1, LANES), lambda i: (0, 0))],
        out_specs=pl.BlockSpec((tr, LANES), lambda i: (i, 0)),
        out_shape=jax.ShapeDtypeStruct((T, LANES), I32),
        compiler_params=_params("parallel"),
    )(meta, ps_lanes)[:, 0:2].reshape(-1)

    last_of_expert = jnp.concatenate([block_e[1:] != block_e[:-1], jnp.ones((1,), bool)])
    needs_zero = jnp.logical_or(blk >= n_used[0], jnp.logical_or(last_of_expert, blk == n_used[0] - 1)).astype(I32)
    n_zero = jnp.sum(needs_zero).astype(I32).reshape(1)
    tt = tl["moe_tok"]
    xs = pl.pallas_call(
        functools.partial(_dispatch_kernel, tt=tt),
        grid_spec=pltpu.PrefetchScalarGridSpec(
            num_scalar_prefetch=3, grid=(T // tt,),
            in_specs=[pl.BlockSpec((tt, D // 2), lambda i, *_: (i, 0))],
            out_specs=pl.BlockSpec(memory_space=pl.ANY),
            scratch_shapes=[pltpu.VMEM((rb, D // 2), I32), pltpu.SemaphoreType.DMA(()), pltpu.SemaphoreType.DMA(())]),
        out_shape=jax.ShapeDtypeStruct((P, D // 2), I32),
        compiler_params=_params("arbitrary"),
    )(dest, needs_zero, n_zero, h1_packed)

    F = D_EXPERT
    grp = jnp.cumsum(jnp.concatenate([jnp.zeros((1,), I32), (block_e[1:] != block_e[:-1]).astype(I32)]))
    ids = jnp.where(counts > 0, jnp.arange(N_EXPERTS, dtype=I32), N_EXPERTS)
    at_or_after = lax.cummin(ids, axis=0, reverse=True)
    after = jnp.concatenate([at_or_after[1:], jnp.full((1,), N_EXPERTS, I32)])
    nxt_of = jnp.where(after < N_EXPERTS, after, -1)
    nxt2_of = jnp.where(nxt_of >= 0, nxt_of[jnp.maximum(nxt_of, 0)], -1)
    nxt = nxt_of[block_e].astype(I32)
    nxt2 = nxt2_of[block_e].astype(I32)
    any_spec = pl.BlockSpec(memory_space=pl.ANY)
    ys = pl.pallas_call(
        functools.partial(_experts_kernel, layer=layer),
        grid_spec=pltpu.PrefetchScalarGridSpec(
            num_scalar_prefetch=5, grid=(n_blocks,),
            in_specs=[pl.BlockSpec((rb, D // 2), lambda i, be, gr, nx, nx2, nu: (jnp.minimum(i, nu[0] - 1), 0)),
                      any_spec, any_spec, any_spec],
            out_specs=pl.BlockSpec((rb, D), lambda i, *_: (i, 0)),
            scratch_shapes=[pltpu.VMEM((2, D, F), F32), pltpu.VMEM((2, D, F), F32), pltpu.VMEM((2, F, D), F32),
                            pltpu.VMEM((D, F), BF16), pltpu.VMEM((D, F), BF16), pltpu.VMEM((F, D), BF16),
                            pltpu.SemaphoreType.DMA((2, 3))]),
        out_shape=jax.ShapeDtypeStruct((P, D), F32),
        compiler_params=_params("arbitrary"),
    )(block_e, grp.astype(I32), nxt, nxt2, n_used, xs, lw["w_e_gate"], lw["w_e_up"], lw["w_e_down"])

    tc = tl["comb"]
    h2, h2_bf = pl.pallas_call(
        functools.partial(_combine_kernel, tc=tc, alpha=alpha),
        grid_spec=pltpu.PrefetchScalarGridSpec(
            num_scalar_prefetch=1, grid=(T // tc,),
            in_specs=[pl.BlockSpec((tc, D), lambda i, d: (i, 0)),
                      pl.BlockSpec((tc, LANES), lambda i, d: (i, 0)),
                      pl.BlockSpec((1, D), lambda i, d: (0, 0)),
                      pl.BlockSpec((1, D), lambda i, d: (0, 0)),
                      pl.BlockSpec(memory_space=pl.ANY)],
            out_specs=[pl.BlockSpec((tc, D), lambda i, d: (i, 0)), pl.BlockSpec((tc, D), lambda i, d: (i, 0))],
            scratch_shapes=[pltpu.VMEM((2, 2 * tc, D), F32), pltpu.SemaphoreType.DMA((2,))]),
        out_shape=[jax.ShapeDtypeStruct((T, D), F32), jax.ShapeDtypeStruct((T, D), BF16)],
        compiler_params=_params("arbitrary"),
    )(dest, h1, wgt, lw["ln2_g"], lw["ln2_b"], ys)
    return h2, h2_bf


def _layer_weights(l, regrouped, g_q_lora, w_q_b, g_kv_lora, w_kv_b, w_o_a, w_o_b, w_out, ln1_g, ln1_b,
                   w_group, b_group, w_router, b_router, w_e_gate, w_e_up, w_e_down, ln2_g, ln2_b):
    w_mla_in, w_dsa_in, w_gate_a, w_gate_b = regrouped
    D = w_gate_a.shape[1]
    wq = w_q_b[l].reshape(MLA_Q_LORA, MLA_HEADS, MLA_QK_DIM)
    wq = jnp.concatenate([wq[:, :, :MLA_NOPE_DIM].reshape(MLA_Q_LORA, -1),
                          wq[:, :, MLA_NOPE_DIM:].reshape(MLA_Q_LORA, -1)], axis=1).astype(BF16)
    w_route = jnp.concatenate([w_group[l], w_router[l],
                               jnp.zeros((D, LANES - N_GROUPS - N_EXPERTS), F32)], axis=1)
    b_route = jnp.concatenate([b_group[l], b_router[l],
                               jnp.zeros((LANES - N_GROUPS - N_EXPERTS,), F32)]).reshape(1, LANES)
    return dict(
        w_mla_in=w_mla_in, w_dsa_in=w_dsa_in, w_gate_a=w_gate_a, w_gate_b=w_gate_b,
        g_q=g_q_lora[l].reshape(1, -1), g_kv=g_kv_lora[l].reshape(1, -1),
        w_q_b=wq, w_kv_b=w_kv_b[l].astype(BF16),
        w_o_a=w_o_a[l].astype(BF16), w_o_b=w_o_b[l].astype(BF16), w_out=w_out[l].astype(BF16),
        ln1_g=ln1_g[l].reshape(1, -1), ln1_b=ln1_b[l].reshape(1, -1),
        w_route_hi=w_route.astype(BF16), w_route_lo=(w_route - w_route.astype(BF16).astype(F32)).astype(BF16),
        b_route=b_route,
        w_e_gate=w_e_gate, w_e_up=w_e_up, w_e_down=w_e_down,
        ln2_g=ln2_g[l].reshape(1, -1), ln2_b=ln2_b[l].reshape(1, -1),
    )


def kernel(x, positions, w_in, g_q_lora, w_q_b, g_kv_lora, w_kv_b, w_o_a, w_o_b, w_out, ln1_g, ln1_b,
           w_group, b_group, w_router, b_router, w_e_gate, w_e_up, w_e_down, ln2_g, ln2_b):
    B, S, D = x.shape
    depth = w_in.shape[0]
    alpha = (2 * depth) ** 0.25
    tabs = (_rope_tables(positions, MLA_ROPE_DIM, MLA_ROPE_DIM),
            _rope_tables(positions, DSA_ROPE_DIM, DSA_HEAD_DIM),
            _rope_tables(positions, IDX_ROPE_DIM, IDX_DIM))
    h = x.reshape(B * S, D)
    h_bf = h.astype(BF16)
    regrouped = _regroup_w_in(w_in)
    for l in range(depth):
        lw = _layer_weights(l, regrouped, g_q_lora, w_q_b, g_kv_lora, w_kv_b, w_o_a, w_o_b, w_out, ln1_g, ln1_b,
                            w_group, b_group, w_router, b_router, w_e_gate, w_e_up, w_e_down, ln2_g, ln2_b)
        h, h_bf = _layer(l, h, h_bf, positions, tabs, lw, alpha, B, S)
    return h.reshape(B, S, D)
```

```python
import functools
import math

import jax
import jax.numpy as jnp
from jax import lax
from jax.experimental import pallas as pl
from jax.experimental.pallas import tpu as pltpu

F32 = jnp.float32
BF16 = jnp.bfloat16
I32 = jnp.int32

MLA_HEADS = 8
MLA_Q_LORA = 512
MLA_KV_LORA = 512
MLA_NOPE_DIM = 128
MLA_ROPE_DIM = 64
MLA_V_DIM = 128
MLA_QK_DIM = MLA_NOPE_DIM + MLA_ROPE_DIM
DSA_HEADS = 8
DSA_HEAD_DIM = 128
DSA_ROPE_DIM = DSA_HEAD_DIM // 4
IDX_HEADS = 16
IDX_DIM = 64
IDX_ROPE_DIM = IDX_DIM // 4
DSA_MAX_TOPK = 256
ROPE_THETA = 500000.0
N_GROUPS = 8
EXPERTS_PER_GROUP = 8
N_EXPERTS = N_GROUPS * EXPERTS_PER_GROUP
D_EXPERT = 512
MOE_ROW_BLOCK = 128
LN_EPS = 1e-5
RMS_EPS = 1e-6

LANES = 128
NEG_BIG = -1e30
INT_MIN = -(2 ** 31)
VMEM_LIMIT = 56 * 1024 * 1024
WEIGHT_DMA_CHUNKS = 4


def _params(*sem):
    return pltpu.CompilerParams(dimension_semantics=sem, vmem_limit_bytes=VMEM_LIMIT)


def _resident(shape):
    nd = len(shape)
    return pl.BlockSpec(shape, lambda *_: (0,) * nd, pipeline_mode=pl.Buffered(1))


def _resident_layer(stacked, layer):
    shape = stacked.shape[1:]
    nd = len(shape)
    return pl.BlockSpec((None,) + shape, lambda *_: (layer,) + (0,) * nd, pipeline_mode=pl.Buffered(1))


def _dot(a, b):
    return jnp.dot(a, b, preferred_element_type=F32)


def _dot_nt(a, b):
    return lax.dot_general(a, b, (((1,), (1,)), ((), ())), preferred_element_type=F32)


def _rope128(x, c, sa, sb, half):
    return x * c + pltpu.roll(x, LANES - half, 1) * sa + pltpu.roll(x, half, 1) * sb


def _rms(x, g):
    return x * lax.rsqrt(jnp.mean(x * x, axis=-1, keepdims=True) + RMS_EPS) * g


def _layer_norm(x, g, b):
    mu = jnp.mean(x, axis=-1, keepdims=True)
    xc = x - mu
    var = jnp.mean(xc * xc, axis=-1, keepdims=True)
    return xc * lax.rsqrt(var + LN_EPS) * g + b


_O_KPE = MLA_Q_LORA + MLA_KV_LORA
_O_DSA = _O_KPE + MLA_ROPE_DIM
_O_KIDX = _O_DSA + 3 * DSA_HEADS * DSA_HEAD_DIM + IDX_HEADS * IDX_DIM
_O_WIDX = _O_KIDX + IDX_DIM
_O_GATE = _O_WIDX + IDX_HEADS
_W_MLA_IN = _O_KPE + 2 * MLA_ROPE_DIM
_W_DSA_IN = (_O_KIDX - _O_DSA) + 2 * IDX_DIM + LANES


def _regroup_w_in_kernel(w_ref, mla_ref, dsa_ref, ga_ref, gb_ref, *, d_model):
    mla_ref[:, 0:_O_DSA] = w_ref[:, 0:_O_DSA].astype(BF16)
    mla_ref[:, _O_DSA:_W_MLA_IN] = w_ref[:, _O_KPE:_O_DSA].astype(BF16)
    main = _O_KIDX - _O_DSA
    dsa_ref[:, 0:main] = w_ref[:, _O_DSA:_O_KIDX].astype(BF16)
    kidx = w_ref[:, _O_KIDX:_O_WIDX].astype(BF16)
    dsa_ref[:, main:main + IDX_DIM] = kidx
    dsa_ref[:, main + IDX_DIM:main + 2 * IDX_DIM] = kidx
    dsa_ref[:, main + 2 * IDX_DIM:main + 2 * IDX_DIM + IDX_HEADS] = w_ref[:, _O_WIDX:_O_GATE].astype(BF16)
    dsa_ref[:, main + 2 * IDX_DIM + IDX_HEADS:] = jnp.zeros((w_ref.shape[0], LANES - IDX_HEADS), BF16)
    ga_ref[...] = w_ref[:, _O_GATE:_O_GATE + d_model].astype(BF16)
    gb_ref[...] = w_ref[:, _O_GATE + d_model:_O_GATE + 2 * d_model].astype(BF16)


def _regroup_w_in(w_in):
    depth, d_model, width = w_in.shape
    tr = 256
    out_w = (_W_MLA_IN, _W_DSA_IN, d_model, d_model)
    return pl.pallas_call(
        functools.partial(_regroup_w_in_kernel, d_model=d_model),
        grid=(depth, d_model // tr),
        in_specs=[pl.BlockSpec((tr, width), lambda l, i: (l * (d_model // tr) + i, 0))],
        out_specs=[pl.BlockSpec((None, tr, w), lambda l, i: (l, i, 0)) for w in out_w],
        out_shape=[jax.ShapeDtypeStruct((depth, d_model, w), BF16) for w in out_w],
        compiler_params=_params("parallel", "parallel"),
    )(w_in.astype(BF16).reshape(depth * d_model, width))


def _mla_prep_kernel(h_ref, win_ref, gq_ref, gkv_ref, wqb_ref, wkvb_ref, c_ref, sa_ref, sb_ref,
                     q_ref, k_ref, v_ref, *, q_scale):
    h = h_ref[...]
    p = _dot(h, win_ref[...])
    qn = _rms(p[:, :MLA_Q_LORA], gq_ref[...]).astype(BF16)
    kvn = _rms(p[:, MLA_Q_LORA:MLA_Q_LORA + MLA_KV_LORA], gkv_ref[...]).astype(BF16)
    q = _dot(qn, wqb_ref[...]) * q_scale
    kv = _dot(kvn, wkvb_ref[...])
    c, sa, sb = c_ref[...], sa_ref[...], sb_ref[...]
    half = MLA_ROPE_DIM // 2
    kpe = _rope128(p[:, MLA_Q_LORA + MLA_KV_LORA:], c, sa, sb, half)[:, :MLA_ROPE_DIM].astype(BF16)
    pe0 = MLA_HEADS * MLA_NOPE_DIM
    for hh in range(MLA_HEADS):
        q_ref[0, hh, :, 0:MLA_NOPE_DIM] = q[:, 128 * hh:128 * hh + 128].astype(BF16)
        k_ref[0, hh, :, 0:MLA_NOPE_DIM] = kv[:, 256 * hh:256 * hh + 128].astype(BF16)
        k_ref[0, hh, :, MLA_NOPE_DIM:MLA_QK_DIM] = kpe
        v_ref[:, 256 * hh:256 * hh + 128] = kv[:, 256 * hh + 128:256 * hh + 256].astype(BF16)
        v_ref[:, 256 * hh + 128:256 * hh + 256] = jnp.ones((kv.shape[0], LANES), BF16)
    for s in range(MLA_HEADS // 2):
        slab = _rope128(q[:, pe0 + 128 * s:pe0 + 128 * s + 128], c, sa, sb, half).astype(BF16)
        q_ref[0, 2 * s, :, MLA_NOPE_DIM:MLA_QK_DIM] = slab[:, :MLA_ROPE_DIM]
        q_ref[0, 2 * s + 1, :, MLA_NOPE_DIM:MLA_QK_DIM] = slab[:, MLA_ROPE_DIM:]


def _dsa_prep_kernel(h_ref, w_ref, cd_ref, sad_ref, sbd_ref, ci_ref, sai_ref, sbi_ref,
                     qb_ref, kb_ref, vb_ref, qi_ref, ki_ref, wi_ref, *, w_idx_scale, q_scale):
    h = h_ref[...]
    hd = DSA_HEADS * DSA_HEAD_DIM
    cd, sad, sbd = cd_ref[...], sad_ref[...], sbd_ref[...]
    ci, sai, sbi = ci_ref[...], sai_ref[...], sbi_ref[...]
    hd_half, hi_half = DSA_ROPE_DIM // 2, IDX_ROPE_DIM // 2
    q = _dot(h, w_ref[:, 0:hd]) * q_scale
    for hh in range(DSA_HEADS):
        qb_ref[:, 128 * hh:128 * hh + 128] = _rope128(q[:, 128 * hh:128 * hh + 128], cd, sad, sbd, hd_half).astype(BF16)
    k = _dot(h, w_ref[:, hd:2 * hd])
    for hh in range(DSA_HEADS):
        kb_ref[:, 128 * hh:128 * hh + 128] = _rope128(k[:, 128 * hh:128 * hh + 128], cd, sad, sbd, hd_half).astype(BF16)
    v = _dot(h, w_ref[:, 2 * hd:3 * hd]).astype(BF16)
    for hh in range(DSA_HEADS):
        vb_ref[:, 256 * hh:256 * hh + 128] = v[:, 128 * hh:128 * hh + 128]
        vb_ref[:, 256 * hh + 128:256 * hh + 256] = jnp.ones((v.shape[0], LANES), BF16)
    qi = _dot(h, w_ref[:, 3 * hd:3 * hd + IDX_HEADS * IDX_DIM])
    for s in range(IDX_HEADS // 2):
        slab = _rope128(qi[:, 128 * s:128 * s + 128], ci, sai, sbi, hi_half).astype(BF16)
        qi_ref[0, 2 * s] = slab[:, :IDX_DIM]
        qi_ref[0, 2 * s + 1] = slab[:, IDX_DIM:]
    o = 3 * hd + IDX_HEADS * IDX_DIM
    last = _dot(h, w_ref[:, o:o + 2 * LANES])
    ki_ref[...] = _rope128(last[:, :LANES], ci, sai, sbi, hi_half)[:, :IDX_DIM].astype(BF16)
    wi_ref[0] = (last[:, LANES:] * w_idx_scale).T[:IDX_HEADS, :]


def _flash_update(s, v_ones, m_ref, acc_ref):
    tk = s.shape[1]
    m_prev = m_ref[...]
    m_next = jnp.maximum(m_prev, jnp.max(s, axis=1, keepdims=True))
    p = jnp.exp2(s - jnp.tile(m_next, (1, tk // LANES)))
    alpha = jnp.exp2(m_prev - m_next)
    acc_ref[...] = acc_ref[...] * jnp.tile(alpha, (1, 2)) + _dot(p.astype(BF16), v_ones)
    m_ref[...] = m_next


def _mla_attn_kernel(q_ref, k_ref, v_ref, pq_ref, pk_ref, o_ref, m_sc, acc_sc, *, tile, heads):
    qi = pl.program_id(2)
    m_sc[...] = jnp.full(m_sc.shape, -jnp.inf, F32)
    acc_sc[...] = jnp.zeros(acc_sc.shape, F32)

    def chunk(c, masked):
        ks = pl.ds(pl.multiple_of(c * tile, tile), tile)
        if masked:
            visible = pk_ref[0, c] <= pq_ref[0]
        for hh in range(heads):
            s = _dot_nt(q_ref[0, hh], k_ref[0, hh, ks, :])
            if masked:
                s = jnp.where(visible, s, NEG_BIG)
            _flash_update(s, v_ref[0, ks, 2 * MLA_V_DIM * hh:2 * MLA_V_DIM * (hh + 1)], m_sc.at[hh], acc_sc.at[hh])

    def body(c, carry):
        chunk(c, False)
        return carry

    lax.fori_loop(0, qi, body, 0)
    chunk(qi, True)
    for hh in range(heads):
        o_ref[0, :, MLA_V_DIM * hh:MLA_V_DIM * (hh + 1)] = (
            acc_sc[hh, :, :MLA_V_DIM] / acc_sc[hh, :, MLA_V_DIM:]).astype(o_ref.dtype)


def _idx_select_kernel(q_ref, k_ref, wt_ref, pq_ref, pk_ref, o_ref, key_sc, *, tq, tk, nkc, top_k):
    qi = pl.program_id(1)
    nk = (qi * tq + tq + tk - 1) // tk
    wt = wt_ref[0]
    pq = pq_ref[0]

    def score_chunk(c, carry):
        kc = k_ref[0, pl.ds(pl.multiple_of(c * tk, tk), tk), :]
        acc = jnp.zeros((tk, tq), F32)
        for hh in range(IDX_HEADS):
            acc = acc + wt[hh:hh + 1, :] * jnp.maximum(_dot_nt(kc, q_ref[0, hh]), 0.0)
        bits = pltpu.bitcast(acc, I32)
        key = jnp.where(bits < 0, bits ^ jnp.int32(0x7FFFFFFF), bits)
        key = jnp.where(acc == 0.0, 0, key)
        key_sc[c] = jnp.where(pk_ref[0, c] <= pq, key, INT_MIN)
        return carry

    lax.fori_loop(0, nk, score_chunk, 0)

    def count(pred):
        def body(c, part):
            m = pred(key_sc[c], c).astype(I32)
            return part + jnp.sum(m.reshape(tk // 8, 8, tq), axis=0)
        part = lax.fori_loop(0, nk, body, jnp.zeros((8, tq), I32))
        return jnp.sum(part, axis=0, keepdims=True)

    def bit_step(i, carry):
        base, at_base = carry
        cand = base ^ (jnp.int32(1) << (31 - i))
        cnt = count(lambda key, c: key >= cand)
        take = cnt >= top_k
        return jnp.where(take, cand, base), jnp.where(take, cnt, at_base)

    def refine(lo, hi, carry):
        settled = jnp.min((carry[1] <= top_k).astype(I32)) > 0
        return lax.cond(settled, lambda c: c, lambda c: lax.fori_loop(lo, hi, bit_step, c), carry)

    carry = lax.fori_loop(0, 24, bit_step, (jnp.full((1, tq), INT_MIN, I32), jnp.full((1, tq), 2 ** 30, I32)))
    carry = refine(24, 28, carry)
    thr, at_thr = refine(28, 32, carry)
    tie = jnp.logical_and(at_thr > top_k, thr != INT_MIN)

    def key_index(c):
        return c * tk + lax.broadcasted_iota(I32, (tk, tq), 0)

    def tie_search():
        n_eq = count(lambda key, c: key == thr)
        need = top_k - (at_thr - n_eq)

        def step(i, p):
            cand = p | (jnp.int32(1) << (14 - i))
            cnt = count(lambda key, c: jnp.logical_and(key == thr, key_index(c) < cand))
            return jnp.where(cnt <= need, cand, p)
        return lax.fori_loop(0, 15, step, jnp.zeros((1, tq), I32))

    p_cut = lax.cond(jnp.max(tie.astype(I32)) > 0, tie_search,
                     lambda: jnp.full((1, tq), 2 ** 30, I32))

    def out_chunk(c, carry):
        key = key_sc[c]
        sel = jnp.logical_or(key > thr, jnp.logical_and(key == thr, key_index(c) < p_cut))
        sel = jnp.logical_and(sel, key != INT_MIN)
        o_ref[0, c] = jnp.where(sel, 0.0, NEG_BIG).T.astype(o_ref.dtype)
        return carry

    lax.fori_loop(0, nk, out_chunk, 0)

    def fill_chunk(c, carry):
        o_ref[0, c] = jnp.full((tq, tk), NEG_BIG, o_ref.dtype)
        return carry

    lax.fori_loop(nk, nkc, fill_chunk, 0)


def _dsa_attn_kernel(q_ref, k_ref, v_ref, b_ref, o_ref, m_sc, acc_sc, *, tq, tk):
    qi = pl.program_id(1)
    nk = (qi * tq + tq + tk - 1) // tk
    m_sc[...] = jnp.full(m_sc.shape, NEG_BIG, F32)
    acc_sc[...] = jnp.zeros(acc_sc.shape, F32)

    def chunk(c, carry):
        bias = b_ref[0, c].astype(F32)
        ks = pl.ds(pl.multiple_of(c * tk, tk), tk)
        for hh in range(DSA_HEADS):
            hs = slice(DSA_HEAD_DIM * hh, DSA_HEAD_DIM * (hh + 1))
            vs = slice(2 * DSA_HEAD_DIM * hh, 2 * DSA_HEAD_DIM * (hh + 1))
            s = _dot_nt(q_ref[0, :, hs], k_ref[0, ks, hs]) + bias
            _flash_update(s, v_ref[0, ks, vs], m_sc.at[hh], acc_sc.at[hh])
        return carry

    lax.fori_loop(0, nk, chunk, 0)
    for hh in range(DSA_HEADS):
        hs = slice(DSA_HEAD_DIM * hh, DSA_HEAD_DIM * (hh + 1))
        o_ref[0, :, hs] = (acc_sc[hh, :, :DSA_HEAD_DIM] / acc_sc[hh, :, DSA_HEAD_DIM:]).astype(o_ref.dtype)


def _merge_kernel(h_ref, oa_ref, ob_ref, wga_ref, wgb_ref, woa_ref, wob_ref, y_ref):
    h = h_ref[...]
    ya = jax.nn.sigmoid(_dot(h, wga_ref[...])) * _dot(oa_ref[...], woa_ref[...])
    yb = jax.nn.sigmoid(_dot(h, wgb_ref[...])) * _dot(ob_ref[...], wob_ref[...])
    y_ref[...] = (ya + yb).astype(y_ref.dtype)


def _out_ln_kernel(y_ref, h_ref, wout_ref, g_ref, b_ref, wrh_ref, wrl_ref, br_ref, o_ref, op_ref, lg_ref, *, alpha):
    sub = 128
    half = o_ref.shape[1] // 2
    for r in range(y_ref.shape[0] // sub):
        rows = slice(r * sub, (r + 1) * sub)
        mix = _dot(y_ref[rows, :], wout_ref[...])
        h1 = _layer_norm(alpha * h_ref[rows, :] + mix, g_ref[...], b_ref[...])
        o_ref[rows, :] = h1
        lo = pltpu.bitcast(h1[:, :half].astype(BF16).astype(F32), I32)
        hi = pltpu.bitcast(h1[:, half:].astype(BF16).astype(F32), I32)
        op_ref[rows, :] = lax.shift_right_logical(lo, 16) | hi
        h_hi = h1.astype(BF16)
        h_lo = (h1 - h_hi.astype(F32)).astype(BF16)
        lg_ref[rows, :] = (_dot(h_hi, wrh_ref[...]) + (_dot(h_hi, wrl_ref[...]) + _dot(h_lo, wrh_ref[...]))
                           + br_ref[...])


def _route_kernel(lg_ref, meta_ref, wgt_ref, cnt_ref, carry_sc, *, tr):
    @pl.when(pl.program_id(0) == 0)
    def _():
        carry_sc[...] = jnp.zeros(carry_sc.shape, F32)

    lg = lg_ref[...]
    lane = lax.broadcasted_iota(I32, (tr, LANES), 1)
    gl = jnp.where(lane < N_GROUPS, lg, -jnp.inf)
    gmax = jnp.max(gl, axis=1, keepdims=True)
    g_idx = jnp.min(jnp.where(gl == gmax, lane, LANES), axis=1, keepdims=True)
    g_p = 1.0 / jnp.sum(jnp.exp(gl - gmax), axis=1, keepdims=True)
    lo = N_GROUPS + EXPERTS_PER_GROUP * g_idx
    in_group = jnp.logical_and(lane >= lo, lane < lo + EXPERTS_PER_GROUP)
    el = jnp.where(in_group, lg, -jnp.inf)
    emax = jnp.max(el, axis=1, keepdims=True)
    e1 = jnp.min(jnp.where(el == emax, lane, LANES), axis=1, keepdims=True)
    den = jnp.sum(jnp.exp(el - emax), axis=1, keepdims=True)
    el2 = jnp.where(lane == e1, -jnp.inf, el)
    emax2 = jnp.max(el2, axis=1, keepdims=True)
    e2 = jnp.min(jnp.where(el2 == emax2, lane, LANES), axis=1, keepdims=True)
    p1 = 1.0 / den
    p2 = jnp.exp(emax2 - emax) / den
    w1 = g_p * (p1 / (p1 + p2))
    w2 = g_p * (p2 / (p1 + p2))

    is1 = lane == e1
    is2 = lane == e2
    onehot = jnp.logical_or(is1, is2).astype(BF16)
    r = lax.broadcasted_iota(I32, (tr, tr), 0)
    cidx = lax.broadcasted_iota(I32, (tr, tr), 1)
    lower = (cidx < r).astype(BF16)
    before = _dot(lower, onehot) + carry_sc[...]
    rank1 = jnp.sum(jnp.where(is1, before, 0.0), axis=1, keepdims=True)
    rank2 = jnp.sum(jnp.where(is2, before, 0.0), axis=1, keepdims=True)
    carry_sc[...] = carry_sc[...] + jnp.sum(onehot.astype(F32), axis=0, keepdims=True)

    meta = jnp.where(lane == 0, e1 - N_GROUPS,
                     jnp.where(lane == 1, e2 - N_GROUPS,
                               jnp.where(lane == 2, rank1.astype(I32),
                                         jnp.where(lane == 3, rank2.astype(I32), 0))))
    meta_ref[...] = meta
    wgt_ref[...] = jnp.where(lane == 0, w1, jnp.where(lane == 1, w2, 0.0))
    cnt_ref[...] = jnp.broadcast_to(carry_sc[...], cnt_ref.shape)


def _row_copy(src, src_row, dst, dst_row, sem):
    return pltpu.make_async_copy(src.at[pl.ds(src_row, 1)], dst.at[pl.ds(dst_row, 1)], sem)


def _dest_kernel(meta_ref, ps_ref, o_ref):
    meta = meta_ref[...]
    ps = ps_ref[...]
    lane = lax.broadcasted_iota(I32, meta.shape, 1)
    d1 = jnp.sum(jnp.where(lane == meta[:, 0:1], ps, 0), axis=1, keepdims=True) + meta[:, 2:3]
    d2 = jnp.sum(jnp.where(lane == meta[:, 1:2], ps, 0), axis=1, keepdims=True) + meta[:, 3:4]
    o_ref[...] = jnp.where(lane == 0, d1, jnp.where(lane == 1, d2, 0))


def _dispatch_kernel(dest_ref, zblk_ref, nz_ref, h_ref, xs_out, zero_sc, sem, zsem, *, tt):
    base = pl.program_id(0) * tt
    rb = zero_sc.shape[0]

    @pl.when(pl.program_id(0) == 0)
    def _():
        zero_sc[...] = jnp.zeros(zero_sc.shape, zero_sc.dtype)

        def fill(j, carry):
            @pl.when(zblk_ref[j] != 0)
            def _():
                pltpu.make_async_copy(zero_sc, xs_out.at[pl.ds(pl.multiple_of(j * rb, rb), rb)], zsem).start()
            return carry

        def drain(j, carry):
            pltpu.make_async_copy(zero_sc, xs_out.at[pl.ds(0, rb)], zsem).wait()
            return carry

        lax.fori_loop(0, zblk_ref.shape[0], fill, 0)
        lax.fori_loop(0, nz_ref[0], drain, 0)

    def issue(t, carry):
        tok = base + t
        _row_copy(h_ref, t, xs_out, dest_ref[2 * tok], sem).start(priority=0)
        _row_copy(h_ref, t, xs_out, dest_ref[2 * tok + 1], sem).start(priority=1)
        return carry

    lax.fori_loop(0, tt, issue, 0, unroll=8)
    for _ in range(2):
        pltpu.make_async_copy(h_ref, xs_out.at[pl.ds(0, tt)], sem).wait()


def _experts_kernel(be_ref, grp_ref, nxt_ref, nxt2_ref, nu_ref, xs_ref, wg_hbm, wu_hbm, wd_hbm, o_ref,
                    wg_buf, wu_buf, wd_buf, wg_sc, wu_sc, wd_sc, sem, *, layer):
    i = pl.program_id(0)
    e = be_ref[i]
    used = i < nu_ref[0]
    first = jnp.logical_and(used, jnp.logical_or(i == 0, e != be_ref[jnp.maximum(i - 1, 0)]))
    slot = grp_ref[i] % 2

    def copies(expert, s):
        out = []
        for j, (hbm, buf) in enumerate(((wg_hbm, wg_buf), (wu_hbm, wu_buf), (wd_hbm, wd_buf))):
            rows = buf.shape[1] // WEIGHT_DMA_CHUNKS
            for c in range(WEIGHT_DMA_CHUNKS):
                rs = pl.ds(c * rows, rows)
                out.append(pltpu.make_async_copy(hbm.at[layer, expert, rs], buf.at[s, rs], sem.at[s, j]))
        return out

    @pl.when(i == 0)
    def _():
        for cp in copies(e, slot):
            cp.start()

        @pl.when(nxt_ref[0] >= 0)
        def _():
            for cp in copies(nxt_ref[0], 1 - slot):
                cp.start()

    @pl.when(first)
    def _():
        for cp in copies(e, slot):
            cp.wait()
        wg_sc[...] = wg_buf[slot].astype(BF16)
        wu_sc[...] = wu_buf[slot].astype(BF16)
        wd_sc[...] = wd_buf[slot].astype(BF16)
        nxt2 = nxt2_ref[i]

        @pl.when(nxt2 >= 0)
        def _():
            for cp in copies(nxt2, slot):
                cp.start()

    @pl.when(used)
    def _():
        xp = xs_ref[...]
        x = jnp.concatenate([pltpu.bitcast(xp << 16, F32).astype(BF16),
                             pltpu.bitcast(xp & jnp.int32(-65536), F32).astype(BF16)], axis=1)
        hid = jax.nn.silu(_dot(x, wg_sc[...])) * _dot(x, wu_sc[...])
        o_ref[...] = _dot(hid.astype(BF16), wd_sc[...])

    @pl.when(jnp.logical_not(used))
    def _():
        o_ref[...] = jnp.zeros(o_ref.shape, o_ref.dtype)


def _combine_kernel(dest_ref, h_ref, wgt_ref, g_ref, b_ref, ys_hbm, o_ref, ob_ref, ybuf, sem, *, tc, alpha):
    i = pl.program_id(0)
    last = pl.num_programs(0) - 1
    slot = i % 2

    def issue(tile, s, t):
        tok = tile * tc + t
        _row_copy(ys_hbm, dest_ref[2 * tok], ybuf.at[s], t, sem.at[s]).start(priority=0)
        _row_copy(ys_hbm, dest_ref[2 * tok + 1], ybuf.at[s], tc + t, sem.at[s]).start(priority=1)

    def wait(s):
        pltpu.make_async_copy(ys_hbm.at[pl.ds(0, 2 * tc)], ybuf.at[s], sem.at[s]).wait()

    @pl.when(i == 0)
    def _():
        def body(t, carry):
            issue(0, 0, t)
            return carry
        lax.fori_loop(0, tc, body, 0, unroll=8)

    wait(slot)
    nxt = jnp.minimum(i + 1, last)
    for t in range(tc):
        issue(nxt, 1 - slot, t)
    wgt = wgt_ref[...]
    ff = ybuf[slot, 0:tc, :] * wgt[:, 0:1] + ybuf[slot, tc:2 * tc, :] * wgt[:, 1:2]
    h2 = _layer_norm(alpha * h_ref[...] + ff, g_ref[...], b_ref[...])
    o_ref[...] = h2
    ob_ref[...] = h2.astype(BF16)

    @pl.when(i == last)
    def _():
        wait(1 - slot)


def _tiles(seq):
    big = seq >= 2048
    return dict(
        rows=512 if big else 128,
        out_rows=256 if big else 128,
        mla=512 if big else 128,
        mla_heads=8,
        idx_q=512 if big else 64,
        dsa_q=512 if big else 128,
        key=512 if big else 128,
        route=512 if big else 128,
        moe_tok=1024 if big else 128,
        comb=256 if big else 128,
    )


def _rope_tables(positions, rot_dim, head_dim):
    half = rot_dim // 2
    inv = ROPE_THETA ** (-jnp.arange(0, rot_dim, 2, dtype=F32) / rot_dim)
    ang = positions.astype(F32).reshape(-1, 1) * inv
    cos, sin = jnp.cos(ang), jnp.sin(ang)
    t = cos.shape[0]
    rest = head_dim - rot_dim
    ones, zeros, zh = jnp.ones((t, rest), F32), jnp.zeros((t, rest), F32), jnp.zeros((t, half), F32)
    rep = LANES // head_dim
    c = jnp.tile(jnp.concatenate([cos, cos, ones], axis=1), (1, rep))
    sa = jnp.tile(jnp.concatenate([-sin, zh, zeros], axis=1), (1, rep))
    sb = jnp.tile(jnp.concatenate([zh, sin, zeros], axis=1), (1, rep))
    return c, sa, sb


def _layer(layer, h, h_bf, positions, tabs, lw, alpha, B, S):
    T, D = h.shape
    tl = _tiles(S)
    tm = tl["rows"]
    nrow = T // tm
    spb = S // tm
    (c_m, sa_m, sb_m), (c_d, sa_d, sb_d), (c_i, sa_i, sb_i) = tabs
    row = lambda w: pl.BlockSpec((tm, w), lambda i: (i, 0))
    head_major = lambda nh, w: pl.BlockSpec((1, nh, tm, w), lambda i: (i // spb, 0, i % spb, 0))

    log2e = math.log2(math.e)
    q_a, k_a, v_a = pl.pallas_call(
        functools.partial(_mla_prep_kernel, q_scale=log2e * MLA_QK_DIM ** -0.5),
        grid=(nrow,),
        in_specs=[row(D), _resident_layer(lw["w_mla_in"], layer), _resident((1, MLA_Q_LORA)), _resident((1, MLA_KV_LORA)),
                  _resident(lw["w_q_b"].shape), _resident(lw["w_kv_b"].shape), row(LANES), row(LANES), row(LANES)],
        out_specs=[head_major(MLA_HEADS, MLA_QK_DIM), head_major(MLA_HEADS, MLA_QK_DIM), row(2 * MLA_HEADS * MLA_V_DIM)],
        out_shape=[jax.ShapeDtypeStruct((B, MLA_HEADS, S, MLA_QK_DIM), BF16),
                   jax.ShapeDtypeStruct((B, MLA_HEADS, S, MLA_QK_DIM), BF16),
                   jax.ShapeDtypeStruct((T, 2 * MLA_HEADS * MLA_V_DIM), BF16)],
        compiler_params=_params("parallel"),
    )(h_bf, lw["w_mla_in"], lw["g_q"], lw["g_kv"], lw["w_q_b"], lw["w_kv_b"], c_m, sa_m, sb_m)

    hd = DSA_HEADS * DSA_HEAD_DIM
    q_b, k_b, v_b, q_idx, k_idx, w_idx = pl.pallas_call(
        functools.partial(_dsa_prep_kernel, w_idx_scale=(IDX_HEADS ** -0.5) * (IDX_DIM ** -0.5),
                          q_scale=log2e * DSA_HEAD_DIM ** -0.5),
        grid=(nrow,),
        in_specs=[row(D), _resident_layer(lw["w_dsa_in"], layer)] + [row(LANES)] * 6,
        out_specs=[row(hd), row(hd), row(2 * hd), head_major(IDX_HEADS, IDX_DIM), row(IDX_DIM),
                   pl.BlockSpec((1, IDX_HEADS, tm), lambda i: (i // spb, 0, i % spb))],
        out_shape=[jax.ShapeDtypeStruct((T, hd), BF16)] * 2 + [jax.ShapeDtypeStruct((T, 2 * hd), BF16)] + [
            jax.ShapeDtypeStruct((B, IDX_HEADS, S, IDX_DIM), BF16),
            jax.ShapeDtypeStruct((T, IDX_DIM), BF16),
            jax.ShapeDtypeStruct((B, IDX_HEADS, S), F32)],
        compiler_params=_params("parallel"),
    )(h_bf, lw["w_dsa_in"], c_d, sa_d, sb_d, c_i, sa_i, sb_i)

    ta = tl["mla"]
    pos_q = positions.reshape(B, S, 1)
    pos_k_mla = positions.reshape(B, S // ta, 1, ta)
    hp = tl["mla_heads"]
    o_a = pl.pallas_call(
        functools.partial(_mla_attn_kernel, tile=ta, heads=hp),
        grid=(B, MLA_HEADS // hp, S // ta),
        in_specs=[pl.BlockSpec((1, hp, ta, MLA_QK_DIM), lambda b, g, i: (b, g, i, 0)),
                  pl.BlockSpec((1, hp, S, MLA_QK_DIM), lambda b, g, i: (b, g, 0, 0), pipeline_mode=pl.Buffered(1)),
                  pl.BlockSpec((1, S, 2 * MLA_V_DIM * hp), lambda b, g, i: (b, 0, g), pipeline_mode=pl.Buffered(1)),
                  pl.BlockSpec((1, ta, 1), lambda b, g, i: (b, i, 0)),
                  pl.BlockSpec((1, S // ta, 1, ta), lambda b, g, i: (b, 0, 0, 0))],
        out_specs=pl.BlockSpec((1, ta, MLA_V_DIM * hp), lambda b, g, i: (b, i, g)),
        out_shape=jax.ShapeDtypeStruct((B, S, MLA_HEADS * MLA_V_DIM), BF16),
        scratch_shapes=[pltpu.VMEM((hp, ta, LANES), F32), pltpu.VMEM((hp, ta, 2 * MLA_V_DIM), F32)],
        compiler_params=_params("parallel", "parallel", "arbitrary"),
    )(q_a, k_a, v_a.reshape(B, S, -1), pos_q, pos_k_mla)

    tk = tl["key"]
    nkc = S // tk
    tqi = tl["idx_q"]
    top_k = min(DSA_MAX_TOPK, S // 4)
    sel_bias = pl.pallas_call(
        functools.partial(_idx_select_kernel, tq=tqi, tk=tk, nkc=nkc, top_k=top_k),
        grid=(B, S // tqi),
        in_specs=[pl.BlockSpec((1, IDX_HEADS, tqi, IDX_DIM), lambda b, i: (b, 0, i, 0)),
                  pl.BlockSpec((1, S, IDX_DIM), lambda b, i: (b, 0, 0)),
                  pl.BlockSpec((1, IDX_HEADS, tqi), lambda b, i: (b, 0, i)),
                  pl.BlockSpec((1, 1, tqi), lambda b, i: (b, 0, i)),
                  pl.BlockSpec((1, nkc, tk, 1), lambda b, i: (b, 0, 0, 0))],
        out_specs=pl.BlockSpec((1, nkc, tqi, tk), lambda b, i: (b, 0, i, 0)),
        out_shape=jax.ShapeDtypeStruct((B, nkc, S, tk), BF16),
        scratch_shapes=[pltpu.VMEM((nkc, tk, tqi), I32)],
        compiler_params=_params("parallel", "arbitrary"),
    )(q_idx, k_idx.reshape(B, S, IDX_DIM), w_idx, positions.reshape(B, 1, S), positions.reshape(B, nkc, tk, 1))

    tqd = tl["dsa_q"]
    o_b = pl.pallas_call(
        functools.partial(_dsa_attn_kernel, tq=tqd, tk=tk),
        grid=(B, S // tqd),
        in_specs=[pl.BlockSpec((1, tqd, hd), lambda b, i: (b, i, 0)),
                  pl.BlockSpec((1, S, hd), lambda b, i: (b, 0, 0), pipeline_mode=pl.Buffered(1)),
                  pl.BlockSpec((1, S, 2 * hd), lambda b, i: (b, 0, 0), pipeline_mode=pl.Buffered(1)),
                  pl.BlockSpec((1, nkc, tqd, tk), lambda b, i: (b, 0, i, 0))],
        out_specs=pl.BlockSpec((1, tqd, hd), lambda b, i: (b, i, 0)),
        out_shape=jax.ShapeDtypeStruct((B, S, hd), BF16),
        scratch_shapes=[pltpu.VMEM((DSA_HEADS, tqd, LANES), F32), pltpu.VMEM((DSA_HEADS, tqd, 2 * DSA_HEAD_DIM), F32)],
        compiler_params=_params("parallel", "arbitrary"),
    )(q_b.reshape(B, S, hd), k_b.reshape(B, S, hd), v_b.reshape(B, S, 2 * hd), sel_bias)

    y = pl.pallas_call(
        _merge_kernel,
        grid=(nrow,),
        in_specs=[row(D), row(MLA_HEADS * MLA_V_DIM), row(hd),
                  _resident_layer(lw["w_gate_a"], layer), _resident_layer(lw["w_gate_b"], layer),
                  _resident((MLA_HEADS * MLA_V_DIM, D)), _resident((hd, D))],
        out_specs=row(D),
        out_shape=jax.ShapeDtypeStruct((T, D), BF16),
        compiler_params=_params("parallel"),
    )(h_bf, o_a.reshape(T, -1), o_b.reshape(T, hd), lw["w_gate_a"], lw["w_gate_b"], lw["w_o_a"], lw["w_o_b"])

    to = tl["out_rows"]
    orow = lambda w: pl.BlockSpec((to, w), lambda i: (i, 0))
    h1, h1_packed, logits = pl.pallas_call(
        functools.partial(_out_ln_kernel, alpha=alpha),
        grid=(T // to,),
        in_specs=[orow(D), orow(D), _resident((D, D)), _resident((1, D)), _resident((1, D)),
                  _resident((D, LANES)), _resident((D, LANES)), _resident((1, LANES))],
        out_specs=[orow(D), orow(D // 2), orow(LANES)],
        out_shape=[jax.ShapeDtypeStruct((T, D), F32), jax.ShapeDtypeStruct((T, D // 2), I32),
                   jax.ShapeDtypeStruct((T, LANES), F32)],
        compiler_params=_params("parallel"),
    )(y, h, lw["w_out"], lw["ln1_g"], lw["ln1_b"], lw["w_route_hi"], lw["w_route_lo"], lw["b_route"])

    tr = tl["route"]
    meta, wgt, cnt = pl.pallas_call(
        functools.partial(_route_kernel, tr=tr),
        grid=(T // tr,),
        in_specs=[pl.BlockSpec((tr, LANES), lambda i: (i, 0))],
        out_specs=[pl.BlockSpec((tr, LANES), lambda i: (i, 0)), pl.BlockSpec((tr, LANES), lambda i: (i, 0)),
                   pl.BlockSpec((8, LANES), lambda i: (0, 0))],
        out_shape=[jax.ShapeDtypeStruct((T, LANES), I32), jax.ShapeDtypeStruct((T, LANES), F32),
                   jax.ShapeDtypeStruct((8, LANES), F32)],
        scratch_shapes=[pltpu.VMEM((1, LANES), F32)],
        compiler_params=_params("arbitrary"),
    )(logits)

    rb = MOE_ROW_BLOCK
    counts = cnt[0, N_GROUPS:N_GROUPS + N_EXPERTS].astype(I32)
    padded = ((counts + rb - 1) // rb) * rb
    pends = jnp.cumsum(padded)
    pstarts = pends - padded
    n_blocks = -(-(2 * T) // rb) + N_EXPERTS
    P = n_blocks * rb
    blk = jnp.arange(n_blocks, dtype=I32)
    block_e = jnp.minimum(jnp.sum((pends[None, :] <= (blk * rb)[:, None]).astype(I32), axis=1), N_EXPERTS - 1)
    n_used = (pends[-1] // rb).astype(I32).reshape(1)
    ps_lanes = jnp.zeros((1, LANES), I32).at[0, :N_EXPERTS].set(pstarts.astype(I32))
    dest = pl.pallas_call(
        _dest_kernel,
        grid=(T // tr,),
        in_specs=[pl.BlockSpec((tr, LANES), lambda i: (i, 0)), pl.BlockSpec((1, LANES), lambda i: (0, 0))],
        out_specs=pl.BlockSpec((tr, LANES), lambda i: (i, 0)),
        out_shape=jax.ShapeDtypeStruct((T, LANES), I32),
        compiler_params=_params("parallel"),
    )(meta, ps_lanes)[:, 0:2].reshape(-1)

    last_of_expert = jnp.concatenate([block_e[1:] != block_e[:-1], jnp.ones((1,), bool)])
    needs_zero = jnp.logical_or(blk >= n_used[0], jnp.logical_or(last_of_expert, blk == n_used[0] - 1)).astype(I32)
    n_zero = jnp.sum(needs_zero).astype(I32).reshape(1)
    tt = tl["moe_tok"]
    xs = pl.pallas_call(
        functools.partial(_dispatch_kernel, tt=tt),
        grid_spec=pltpu.PrefetchScalarGridSpec(
            num_scalar_prefetch=3, grid=(T // tt,),
            in_specs=[pl.BlockSpec((tt, D // 2), lambda i, *_: (i, 0))],
            out_specs=pl.BlockSpec(memory_space=pl.ANY),
            scratch_shapes=[pltpu.VMEM((rb, D // 2), I32), pltpu.SemaphoreType.DMA(()), pltpu.SemaphoreType.DMA(())]),
        out_shape=jax.ShapeDtypeStruct((P, D // 2), I32),
        compiler_params=_params("arbitrary"),
    )(dest, needs_zero, n_zero, h1_packed)

    F = D_EXPERT
    grp = jnp.cumsum(jnp.concatenate([jnp.zeros((1,), I32), (block_e[1:] != block_e[:-1]).astype(I32)]))
    ids = jnp.where(counts > 0, jnp.arange(N_EXPERTS, dtype=I32), N_EXPERTS)
    at_or_after = lax.cummin(ids, axis=0, reverse=True)
    after = jnp.concatenate([at_or_after[1:], jnp.full((1,), N_EXPERTS, I32)])
    nxt_of = jnp.where(after < N_EXPERTS, after, -1)
    nxt2_of = jnp.where(nxt_of >= 0, nxt_of[jnp.maximum(nxt_of, 0)], -1)
    nxt = nxt_of[block_e].astype(I32)
    nxt2 = nxt2_of[block_e].astype(I32)
    any_spec = pl.BlockSpec(memory_space=pl.ANY)
    ys = pl.pallas_call(
        functools.partial(_experts_kernel, layer=layer),
        grid_spec=pltpu.PrefetchScalarGridSpec(
            num_scalar_prefetch=5, grid=(n_blocks,),
            in_specs=[pl.BlockSpec((rb, D // 2), lambda i, be, gr, nx, nx2, nu: (jnp.minimum(i, nu[0] - 1), 0)),
                      any_spec, any_spec, any_spec],
            out_specs=pl.BlockSpec((rb, D), lambda i, *_: (i, 0)),
            scratch_shapes=[pltpu.VMEM((2, D, F), F32), pltpu.VMEM((2, D, F), F32), pltpu.VMEM((2, F, D), F32),
                            pltpu.VMEM((D, F), BF16), pltpu.VMEM((D, F), BF16), pltpu.VMEM((F, D), BF16),
                            pltpu.SemaphoreType.DMA((2, 3))]),
        out_shape=jax.ShapeDtypeStruct((P, D), F32),
        compiler_params=_params("arbitrary"),
    )(block_e, grp.astype(I32), nxt, nxt2, n_used, xs, lw["w_e_gate"], lw["w_e_up"], lw["w_e_down"])

    tc = tl["comb"]
    h2, h2_bf = pl.pallas_call(
        functools.partial(_combine_kernel, tc=tc, alpha=alpha),
        grid_spec=pltpu.PrefetchScalarGridSpec(
            num_scalar_prefetch=1, grid=(T // tc,),
            in_specs=[pl.BlockSpec((tc, D), lambda i, d: (i, 0)),
                      pl.BlockSpec((tc, LANES), lambda i, d: (i, 0)),
                      pl.BlockSpec((1, D), lambda i, d: (0, 0)),
                      pl.BlockSpec((1, D), lambda i, d: (0, 0)),
                      pl.BlockSpec(memory_space=pl.ANY)],
            out_specs=[pl.BlockSpec((tc, D), lambda i, d: (i, 0)), pl.BlockSpec((tc, D), lambda i, d: (i, 0))],
            scratch_shapes=[pltpu.VMEM((2, 2 * tc, D), F32), pltpu.SemaphoreType.DMA((2,))]),
        out_shape=[jax.ShapeDtypeStruct((T, D), F32), jax.ShapeDtypeStruct((T, D), BF16)],
        compiler_params=_params("arbitrary"),
    )(dest, h1, wgt, lw["ln2_g"], lw["ln2_b"], ys)
    return h2, h2_bf


def _layer_weights(l, regrouped, g_q_lora, w_q_b, g_kv_lora, w_kv_b, w_o_a, w_o_b, w_out, ln1_g, ln1_b,
                   w_group, b_group, w_router, b_router, w_e_gate, w_e_up, w_e_down, ln2_g, ln2_b):
    w_mla_in, w_dsa_in, w_gate_a, w_gate_b = regrouped
    D = w_gate_a.shape[1]
    wq = w_q_b[l].reshape(MLA_Q_LORA, MLA_HEADS, MLA_QK_DIM)
    wq = jnp.concatenate([wq[:, :, :MLA_NOPE_DIM].reshape(MLA_Q_LORA, -1),
                          wq[:, :, MLA_NOPE_DIM:].reshape(MLA_Q_LORA, -1)], axis=1).astype(BF16)
    w_route = jnp.concatenate([w_group[l], w_router[l],
                               jnp.zeros((D, LANES - N_GROUPS - N_EXPERTS), F32)], axis=1)
    b_route = jnp.concatenate([b_group[l], b_router[l],
                               jnp.zeros((LANES - N_GROUPS - N_EXPERTS,), F32)]).reshape(1, LANES)
    return dict(
        w_mla_in=w_mla_in, w_dsa_in=w_dsa_in, w_gate_a=w_gate_a, w_gate_b=w_gate_b,
        g_q=g_q_lora[l].reshape(1, -1), g_kv=g_kv_lora[l].reshape(1, -1),
        w_q_b=wq, w_kv_b=w_kv_b[l].astype(BF16),
        w_o_a=w_o_a[l].astype(BF16), w_o_b=w_o_b[l].astype(BF16), w_out=w_out[l].astype(BF16),
        ln1_g=ln1_g[l].reshape(1, -1), ln1_b=ln1_b[l].reshape(1, -1),
        w_route_hi=w_route.astype(BF16), w_route_lo=(w_route - w_route.astype(BF16).astype(F32)).astype(BF16),
        b_route=b_route,
        w_e_gate=w_e_gate, w_e_up=w_e_up, w_e_down=w_e_down,
        ln2_g=ln2_g[l].reshape(1, -1), ln2_b=ln2_b[l].reshape(1, -1),
    )


def kernel(x, positions, w_in, g_q_lora, w_q_b, g_kv_lora, w_kv_b, w_o_a, w_o_b, w_out, ln1_g, ln1_b,
           w_group, b_group, w_router, b_router, w_e_gate, w_e_up, w_e_down, ln2_g, ln2_b):
    B, S, D = x.shape
    depth = w_in.shape[0]
    alpha = (2 * depth) ** 0.25
    tabs = (_rope_tables(positions, MLA_ROPE_DIM, MLA_ROPE_DIM),
            _rope_tables(positions, DSA_ROPE_DIM, DSA_HEAD_DIM),
            _rope_tables(positions, IDX_ROPE_DIM, IDX_DIM))
    h = x.reshape(B * S, D)
    h_bf = h.astype(BF16)
    regrouped = _regroup_w_in(w_in)
    for l in range(depth):
        lw = _layer_weights(l, regrouped, g_q_lora, w_q_b, g_kv_lora, w_kv_b, w_o_a, w_o_b, w_out, ln1_g, ln1_b,
                            w_group, b_group, w_router, b_router, w_e_gate, w_e_up, w_e_down, ln2_g, ln2_b)
        h, h_bf = _layer(l, h, h_bf, positions, tabs, lw, alpha, B, S)
    return h.reshape(B, S, D)
```
